```python
import jax, jax.numpy as jnp
from jax import lax
import numpy as np

D_MODEL = 1024
BATCH = 8
SEQ = 2048
DEPTH = 2
DEC_BATCH = 128
DEC_SEQ = 1
PAST_LEN = 16384
PAGE_SIZE = 128

N_AB = (DEPTH + 1) // 2
N_C = DEPTH // 2
D_A = D_MODEL // 2
A_GROUPS = 8
CONV_W = 3
HEAD_B = 64
D_B = D_MODEL // 2
H_B = D_B // HEAD_B
R_DECAY = 64
R_AAA = 64
R_GATE = 128
P_B = 3 * D_B + R_DECAY + R_AAA + R_GATE
P_AB = 3 * D_A + P_B
GN_EPS = 64e-5
D_C = D_MODEL
CHUNK = 128
C_GROUPS = 8
C_GROUP_CH = D_C // C_GROUPS
N_EXPERTS = 64
TOP_K = 8
N_GROUPS = 8
TOPK_GROUPS = 4
EXPERTS_PER_GROUP = N_EXPERTS // N_GROUPS
D_EXPERT = 256
D_SHARED = 256
ROUTE_SCALE = 2.5
EPS = 1e-6

kernel_name = 'hybrid_conv_rwkv7_gmlp_moe_adaln_step'


def rmsnorm(x, g):
    xf = x.astype(jnp.float32)
    y = xf * lax.rsqrt(jnp.mean(xf * xf, axis=-1, keepdims=True) + EPS)
    return (y * g.astype(jnp.float32)).astype(x.dtype)


def layernorm(x, g, b):
    xf = x.astype(jnp.float32)
    mu = jnp.mean(xf, axis=-1, keepdims=True)
    var = jnp.mean(jnp.square(xf - mu), axis=-1, keepdims=True)
    return ((xf - mu) * lax.rsqrt(var + EPS) * g.astype(jnp.float32) + b.astype(jnp.float32)).astype(x.dtype)


def modulation(c, w_mod, b_mod):
    m = jax.nn.silu(c) @ w_mod + b_mod
    return jnp.split(m[:, None, :], 6, axis=-1)


def short_conv_mixer(a_h, a_b, a_c, conv_prev, conv_w):
    L = a_h.shape[1]
    gated = a_c * a_h
    full = jnp.concatenate([conv_prev.astype(gated.dtype), gated], axis=1)
    conv = full[:, 0:L] * conv_w[0]
    for j in range(1, CONV_W):
        conv = conv + full[:, j:j + L] * conv_w[j]
    return a_b * conv, full[:, L:]


def _wkv_step(S, inp):
    r, w, k, v, kk, a = inp
    s_kk = jnp.einsum('bhij,bhj->bhi', S, kk)
    S = S * w[:, :, None, :] - s_kk[..., None] * (kk * a)[:, :, None, :] + v[..., None] * k[:, :, None, :]
    out = jnp.einsum('bhij,bhj->bhi', S, r)
    return S, out


def rwkv7_mixer(pb, shift_prev, wkv_prev, mu, w0, w_decay_up, a0, a_up, g_up, k_k, k_a, r_k, gn_g, gn_b):
    Bn, L, _ = pb.shape
    f32 = jnp.float32
    prev = jnp.concatenate([shift_prev[:, None, :].astype(pb.dtype), pb[:, :-1]], axis=1)
    xm = pb + (prev - pb) * mu
    r, k, v, dw, da, dg = jnp.split(xm, [D_B, 2 * D_B, 3 * D_B, 3 * D_B + R_DECAY, 3 * D_B + R_DECAY + R_AAA], axis=-1)
    w_log = -jax.nn.softplus(-(w0 + jnp.tanh(dw) @ w_decay_up).astype(f32)) - 0.5
    decay = jnp.exp(-jnp.exp(w_log))
    a = jax.nn.sigmoid((a0 + da @ a_up).astype(f32))
    g = jax.nn.sigmoid(dg) @ g_up
    heads = lambda t: t.astype(f32).reshape(Bn, L, H_B, HEAD_B)
    r_h, k_h, v_h, a_h, w_h = heads(r), heads(k), heads(v), heads(a), heads(decay)
    kk = k_h * k_k.astype(f32).reshape(H_B, HEAD_B)
    kk = kk / jnp.maximum(jnp.sqrt(jnp.sum(kk * kk, axis=-1, keepdims=True)), 1e-12)
    k_h = k_h * (1.0 + (a_h - 1.0) * k_a.astype(f32).reshape(H_B, HEAD_B))
    tm = lambda t: jnp.swapaxes(t, 0, 1)
    S_final, out = lax.scan(_wkv_step, wkv_prev.astype(f32), (tm(r_h), tm(w_h), tm(k_h), tm(v_h), tm(kk), tm(a_h)))
    out = jnp.swapaxes(out, 0, 1)
    mean = jnp.mean(out, axis=-1, keepdims=True)
    var = jnp.mean(jnp.square(out - mean), axis=-1, keepdims=True)
    out = (out - mean) * lax.rsqrt(var + GN_EPS) * gn_g.astype(f32).reshape(H_B, HEAD_B) + gn_b.astype(f32).reshape(H_B, HEAD_B)
    out = out + jnp.sum(r_h * k_h * r_k.astype(f32), axis=-1, keepdims=True) * v_h
    y = out.reshape(Bn, L, D_B).astype(pb.dtype) * g
    return y, pb[:, -1], S_final


def mixer_ab(h, conv_prev, shift_prev, wkv_prev, w, i):
    proj = h @ w['ab_w_in'][i]
    a_h, a_b, a_c, pb = jnp.split(proj, [D_A, 2 * D_A, 3 * D_A], axis=-1)
    y_a, new_conv = short_conv_mixer(a_h, a_b, a_c, conv_prev, w['ab_conv_w'][i])
    y_b, new_shift, new_wkv = rwkv7_mixer(pb, shift_prev, wkv_prev, w['ab_mu'][i], w['ab_w0'][i], w['ab_w_decay_up'][i],
                                          w['ab_a0'][i], w['ab_a_up'][i], w['ab_g_up'][i], w['ab_k_k'][i], w['ab_k_a'][i],
                                          w['ab_r_k'][i], w['ab_gn_g'][i], w['ab_gn_b'][i])
    y = jnp.concatenate([y_a, y_b], axis=-1) @ w['ab_w_out'][i]
    return y, new_conv, new_shift, new_wkv


def gmlp_mixer(h, w_in, b_in, ln_g, ln_b, w_s, b_s, w_out):
    Bn, L, _ = h.shape
    z = jax.nn.gelu(h @ w_in + b_in)
    u, v = jnp.split(z, 2, axis=-1)
    v = layernorm(v, ln_g, ln_b)
    n_chunks = -(-L // CHUNK)
    vp = jnp.pad(v, ((0, 0), (0, n_chunks * CHUNK - L), (0, 0))).reshape(Bn, n_chunks, CHUNK, C_GROUPS, C_GROUP_CH)
    mask = jnp.tril(jnp.ones((CHUNK, CHUNK), dtype=bool))
    ws = jnp.where(mask[None], w_s, jnp.zeros_like(w_s)).astype(v.dtype)
    s = jnp.einsum('gij,bnjgc->bnigc', ws, vp) + jnp.transpose(b_s)[:, :, None].astype(v.dtype)
    s = s.reshape(Bn, n_chunks * CHUNK, D_C)[:, :L]
    return (u * s) @ w_out, v


def moe(h, w_router, b_router, w_gate, w_up, w_down, ws_gate, ws_up, ws_down):
    Bn, L, D = h.shape
    x = h.reshape(-1, D)
    T = x.shape[0]
    scores = jax.nn.sigmoid((x @ w_router).astype(jnp.float32))
    sel = scores + b_router.astype(jnp.float32)
    grp_score = jnp.sum(lax.top_k(sel.reshape(T, N_GROUPS, EXPERTS_PER_GROUP), 2)[0], axis=-1)
    _, gidx = lax.top_k(grp_score, TOPK_GROUPS)
    gmask = jnp.any(gidx[:, :, None] == jnp.arange(N_GROUPS)[None, None, :], axis=1)
    emask = jnp.repeat(gmask, EXPERTS_PER_GROUP, axis=1)
    _, eidx = lax.top_k(jnp.where(emask, sel, -jnp.inf), TOP_K)
    wts = jnp.take_along_axis(scores, eidx, axis=1)
    wts = wts / jnp.sum(wts, axis=-1, keepdims=True) * ROUTE_SCALE
    combine = jnp.zeros((T, N_EXPERTS), jnp.float32).at[jnp.arange(T)[:, None], eidx].set(wts).astype(x.dtype)
    routed = jnp.zeros_like(x)
    for gi in range(N_GROUPS):
        sl = slice(gi * EXPERTS_PER_GROUP, (gi + 1) * EXPERTS_PER_GROUP)
        hg = jnp.einsum('td,edf->tef', x, w_gate[sl])
        hu = jnp.einsum('td,edf->tef', x, w_up[sl])
        hw = jax.nn.silu(hg) * hu * combine[:, sl, None]
        routed = routed + jnp.einsum('tef,efd->td', hw, w_down[sl])
    shared = (jax.nn.silu(x @ ws_gate) * (x @ ws_up)) @ ws_down
    return (routed + shared).reshape(Bn, L, D)


def trunk(x, c, conv_st, shift_st, wkv_st, w):
    new_conv, new_shift, new_wkv, chunk_v = [], [], [], []
    for layer in range(DEPTH):
        sh1, sc1, g1, sh2, sc2, g2 = modulation(c, w['w_mod'][layer], w['b_mod'][layer])
        ng = w['norm_g'][layer]
        h = rmsnorm(x, ng[0]) * (1.0 + sc1) + sh1
        if layer % 2 == 0:
            i = layer // 2
            y, nc, ns, nw = mixer_ab(h, conv_st[i], shift_st[i], wkv_st[i], w, i)
            new_conv.append(nc)
            new_shift.append(ns)
            new_wkv.append(nw)
        else:
            j = layer // 2
            y, v = gmlp_mixer(h, w['gm_w_in'][j], w['gm_b_in'][j], w['gm_ln_g'][j], w['gm_ln_b'][j],
                              w['gm_w_s'][j], w['gm_b_s'][j], w['gm_w_out'][j])
            chunk_v.append(v)
        x = x + g1 * rmsnorm(y, ng[1])
        h = rmsnorm(x, ng[2]) * (1.0 + sc2) + sh2
        f = moe(h, w['moe_w_router'][layer], w['moe_b_router'][layer], w['moe_w_gate'][layer], w['moe_w_up'][layer],
                w['moe_w_down'][layer], w['moe_ws_gate'][layer], w['moe_ws_up'][layer], w['moe_ws_down'][layer])
        x = x + g2 * rmsnorm(f, ng[3])
    return x, jnp.stack(new_conv), jnp.stack(new_shift), jnp.stack(new_wkv), jnp.stack(chunk_v)


def setup_inputs(seed: int = 0) -> dict:
    key = jax.random.key(seed)
    ks = iter(jax.random.split(key, 64))
    f32 = jnp.float32

    def nrm(shape, scale):
        return jax.random.normal(next(ks), shape, f32) * scale

    D = D_MODEL
    return {
        'x_prompt': nrm((BATCH, SEQ, D), 1.0),
        'x_sample': nrm((DEC_BATCH, DEC_SEQ, D), 1.0),
        'c_prompt': nrm((BATCH, D), 1.0),
        'c_sample': nrm((DEC_BATCH, D), 1.0),
        'state_conv': nrm((N_AB, DEC_BATCH, CONV_W - 1, D_A), 1.0),
        'state_shift': nrm((N_AB, DEC_BATCH, P_B), 1.0),
        'state_wkv': nrm((N_AB, DEC_BATCH, H_B, HEAD_B, HEAD_B), 0.5),
        'w_mod': nrm((DEPTH, D, 6 * D), 0.3 * D ** -0.5),
        'b_mod': nrm((DEPTH, 6 * D), 0.02),
        'norm_g': 1.0 + nrm((DEPTH, 4, D), 0.05),
        'ab_w_in': nrm((N_AB, D, P_AB), D ** -0.5),
        'ab_conv_w': nrm((N_AB, CONV_W, D_A), CONV_W ** -0.5),
        'ab_mu': jax.random.uniform(next(ks), (N_AB, P_B), f32),
        'ab_w0': nrm((N_AB, D_B), 0.5),
        'ab_w_decay_up': nrm((N_AB, R_DECAY, D_B), 0.3 * R_DECAY ** -0.5),
        'ab_a0': nrm((N_AB, D_B), 0.1),
        'ab_a_up': nrm((N_AB, R_AAA, D_B), 0.3 * R_AAA ** -0.5),
        'ab_g_up': nrm((N_AB, R_GATE, D_B), R_GATE ** -0.5),
        'ab_k_k': 0.85 + nrm((N_AB, D_B), 0.05),
        'ab_k_a': 1.0 + nrm((N_AB, D_B), 0.05),
        'ab_r_k': nrm((N_AB, H_B, HEAD_B), 0.1),
        'ab_gn_g': 1.0 + nrm((N_AB, D_B), 0.05),
        'ab_gn_b': nrm((N_AB, D_B), 0.02),
        'ab_w_out': nrm((N_AB, D_A + D_B, D), (D_A + D_B) ** -0.5),
        'gm_w_in': nrm((N_C, D, 2 * D_C), D ** -0.5),
        'gm_b_in': nrm((N_C, 2 * D_C), 0.02),
        'gm_ln_g': 1.0 + nrm((N_C, D_C), 0.05),
        'gm_ln_b': nrm((N_C, D_C), 0.02),
        'gm_w_s': nrm((N_C, C_GROUPS, CHUNK, CHUNK), 0.5 * CHUNK ** -0.5),
        'gm_b_s': 1.0 + nrm((N_C, C_GROUPS, CHUNK), 0.1),
        'gm_w_out': nrm((N_C, D_C, D), D_C ** -0.5),
        'moe_w_router': nrm((DEPTH, D, N_EXPERTS), D ** -0.5),
        'moe_b_router': nrm((DEPTH, N_EXPERTS), 0.01),
        'moe_w_gate': nrm((DEPTH, N_EXPERTS, D, D_EXPERT), D ** -0.5),
        'moe_w_up': nrm((DEPTH, N_EXPERTS, D, D_EXPERT), D ** -0.5),
        'moe_w_down': nrm((DEPTH, N_EXPERTS, D_EXPERT, D), D_EXPERT ** -0.5),
        'moe_ws_gate': nrm((DEPTH, D, D_SHARED), D ** -0.5),
        'moe_ws_up': nrm((DEPTH, D, D_SHARED), D ** -0.5),
        'moe_ws_down': nrm((DEPTH, D_SHARED, D), D_SHARED ** -0.5),
    }


def reference(x_prompt, x_sample, c_prompt, c_sample, state_conv, state_shift, state_wkv,
              w_mod, b_mod, norm_g, ab_w_in, ab_conv_w, ab_mu, ab_w0, ab_w_decay_up, ab_a0, ab_a_up, ab_g_up,
              ab_k_k, ab_k_a, ab_r_k, ab_gn_g, ab_gn_b, ab_w_out, gm_w_in, gm_b_in, gm_ln_g, gm_ln_b, gm_w_s,
              gm_b_s, gm_w_out, moe_w_router, moe_b_router, moe_w_gate, moe_w_up, moe_w_down, moe_ws_gate,
              moe_ws_up, moe_ws_down):
    w = dict(w_mod=w_mod, b_mod=b_mod, norm_g=norm_g, ab_w_in=ab_w_in, ab_conv_w=ab_conv_w, ab_mu=ab_mu,
             ab_w0=ab_w0, ab_w_decay_up=ab_w_decay_up, ab_a0=ab_a0, ab_a_up=ab_a_up, ab_g_up=ab_g_up,
             ab_k_k=ab_k_k, ab_k_a=ab_k_a, ab_r_k=ab_r_k, ab_gn_g=ab_gn_g, ab_gn_b=ab_gn_b, ab_w_out=ab_w_out,
             gm_w_in=gm_w_in, gm_b_in=gm_b_in, gm_ln_g=gm_ln_g, gm_ln_b=gm_ln_b, gm_w_s=gm_w_s, gm_b_s=gm_b_s,
             gm_w_out=gm_w_out, moe_w_router=moe_w_router, moe_b_router=moe_b_router, moe_w_gate=moe_w_gate,
             moe_w_up=moe_w_up, moe_w_down=moe_w_down, moe_ws_gate=moe_ws_gate, moe_ws_up=moe_ws_up,
             moe_ws_down=moe_ws_down)
    zc = jnp.zeros((N_AB, x_prompt.shape[0], CONV_W - 1, D_A), x_prompt.dtype)
    zs = jnp.zeros((N_AB, x_prompt.shape[0], P_B), x_prompt.dtype)
    zw = jnp.zeros((N_AB, x_prompt.shape[0], H_B, HEAD_B, HEAD_B), jnp.float32)
    y_prompt, conv_p, shift_p, wkv_p, _ = trunk(x_prompt, c_prompt, zc, zs, zw, w)
    y_sample, conv_s, shift_s, wkv_s, chunk_v_s = trunk(x_sample, c_sample, state_conv, state_shift, state_wkv, w)
    return (y_prompt, y_sample, conv_p, conv_s, shift_p, shift_s, wkv_p, wkv_s, chunk_v_s)
```

```python
import functools

import jax
import jax.numpy as jnp
from jax import lax
from jax.experimental import pallas as pl
from jax.experimental.pallas import tpu as pltpu

F32 = jnp.float32
BF16 = jnp.bfloat16
HI = lax.Precision.HIGHEST

D = 1024
DA = 512
DB = 512
HEAD = 64
PB = 1792
PAB = 3 * DA + PB
NE = 64
NG = 8
EPG = NE // NG
TOPG = 4
TOPK = 8
DE = 256
ROUTE_SCALE = 2.5
EPS = 1e-6
GN_EPS = 64e-5
CHUNK = 128
WKV_SLAB = 128
WKV_CHUNK = 32

VMEM_LIMIT = 56 * 1024 * 1024


def _cparams(sem):
    return pltpu.CompilerParams(dimension_semantics=sem, vmem_limit_bytes=VMEM_LIMIT)


def _dot(a, b, precision=None):
    return jnp.dot(a, b, preferred_element_type=F32, precision=precision)


def _dot_nt(a, b, precision=None):
    return lax.dot_general(a, b, (((1,), (1,)), ((), ())), preferred_element_type=F32, precision=precision)


def _iota(shape, dim):
    return lax.broadcasted_iota(jnp.int32, shape, dim)


def _rms(x, g):
    return x * lax.rsqrt(jnp.mean(x * x, axis=-1, keepdims=True) + EPS) * g


def _silu(x):
    return x * jax.nn.sigmoid(x)


def _seg_ones(n, seg):
    return (_iota((n, n), 0) // seg == _iota((n, n), 1) // seg).astype(F32)


def _mod_kernel(c_ref, w_ref, b_ref, o_ref):
    s = _silu(c_ref[...]).astype(BF16)
    o_ref[...] = _dot(s, w_ref[...].astype(BF16)) + b_ref[...]


def _modulation(c_all, w_mod, b_mod):
    depth, _, width = w_mod.shape
    rows = c_all.shape[0]
    bn = 512
    return pl.pallas_call(
        _mod_kernel,
        grid=(depth, width // bn),
        in_specs=[pl.BlockSpec((rows, D), lambda l, j: (0, 0)),
                  pl.BlockSpec((None, D, bn), lambda l, j: (l, 0, j)),
                  pl.BlockSpec((None, 1, bn), lambda l, j: (l, 0, j))],
        out_specs=pl.BlockSpec((None, rows, bn), lambda l, j: (l, 0, j)),
        out_shape=jax.ShapeDtypeStruct((depth, rows, width), F32),
        compiler_params=_cparams(("arbitrary", "arbitrary")),
        name="modulation",
    )(c_all, w_mod, b_mod.reshape(depth, 1, width))


def _residual_in(x_ref, f_ref, pmod_ref, png_ref):
    x = x_ref[...]
    if f_ref is None:
        return x
    return x + pmod_ref[5] * _rms(f_ref[...], png_ref[3:4, :])


def _route(h, wrt_ref, br_ref):
    tm = h.shape[0]
    logits = _dot_nt(wrt_ref[...], h, precision=HI)
    scores = jax.nn.sigmoid(logits)
    sel = scores + br_ref[...]
    sub = _iota((EPG, tm), 0).astype(F32)
    neg = jnp.float32(-jnp.inf)
    blks, sblks, gscore = [], [], []
    for g in range(NG):
        blk = sel[g * EPG:(g + 1) * EPG, :]
        m1 = jnp.max(blk, axis=0, keepdims=True)
        first = jnp.min(jnp.where(blk == m1, sub, float(EPG)), axis=0, keepdims=True)
        m2 = jnp.max(jnp.where(sub == first, neg, blk), axis=0, keepdims=True)
        blks.append(blk)
        sblks.append(scores[g * EPG:(g + 1) * EPG, :])
        gscore.append(m1 + m2)
    masked = []
    for g in range(NG):
        cnt = jnp.zeros((1, tm), F32)
        for g2 in range(NG):
            if g2 == g:
                continue
            beats = (gscore[g2] >= gscore[g]) if g2 < g else (gscore[g2] > gscore[g])
            cnt = cnt + jnp.where(beats, 1.0, 0.0)
        masked.append(jnp.where(cnt < TOPG, blks[g], neg))
    cnts = [jnp.zeros((EPG, tm), F32) for _ in range(NG)]
    for g2 in range(NG):
        for j in range(EPG):
            row = masked[g2][j:j + 1, :]
            for g in range(NG):
                ge = jnp.where(row >= masked[g], 1.0, 0.0)
                gt = jnp.where(row > masked[g], 1.0, 0.0)
                if g > g2:
                    cnts[g] = cnts[g] + ge
                elif g < g2:
                    cnts[g] = cnts[g] + gt
                else:
                    cnts[g] = cnts[g] + jnp.where(sub > j, ge, gt)
    wts = [jnp.where(cnts[g] < TOPK, sblks[g], 0.0) for g in range(NG)]
    tot = wts[0]
    for g in range(1, NG):
        tot = tot + wts[g]
    denom = jnp.sum(tot, axis=0, keepdims=True)
    comb_t = jnp.concatenate([w / denom * ROUTE_SCALE for w in wts] + [jnp.zeros((NE, tm), F32)], axis=0)
    return comb_t.T


def _post(y, x, mod_ref, ng_ref, wrt_ref, br_ref, xo_ref, h2_ref, comb_ref):
    xn = x + mod_ref[2] * _rms(y, ng_ref[1:2, :])
    xo_ref[...] = xn
    h2 = _rms(xn, ng_ref[2:3, :]) * (1.0 + mod_ref[4]) + mod_ref[3]
    h2_ref[...] = h2.astype(BF16)
    comb_ref[...] = _route(h2, wrt_ref, br_ref)


def _ab_pre_kernel(*refs, seq_mode, tm):
    it = iter(refs)
    x_ref, mod_ref, ng_ref, win_ref = next(it), next(it), next(it), next(it)
    cw_ref, mu_ref, w0_ref, wd_ref, a0_ref, wa_ref, wg_ref, kk_ref, ka_ref = (next(it) for _ in range(9))
    if not seq_mode:
        p2_ref, p1_ref, ps_ref = next(it), next(it), next(it)
    ya_ref, r_ref, lw_ref, k_ref, v_ref, kkn_ref, a_ref, g_ref, ctail_ref, stail_ref = (next(it) for _ in range(10))
    if seq_mode:
        cc_ref, sc_ref = next(it), next(it)

        @pl.when(pl.program_id(1) == 0)
        def _():
            cc_ref[...] = jnp.zeros_like(cc_ref)
            sc_ref[...] = jnp.zeros_like(sc_ref)

    x = x_ref[...]
    h = _rms(x, ng_ref[0:1, :]) * (1.0 + mod_ref[1]) + mod_ref[0]
    proj = _dot(h.astype(BF16), win_ref[...])
    a_h, a_b, a_c = proj[:, 0:DA], proj[:, DA:2 * DA], proj[:, 2 * DA:3 * DA]
    pb = proj[:, 3 * DA:]
    gated = a_c * a_h
    if seq_mode:
        rows = _iota((tm, 1), 0)
        g1 = jnp.where(rows == 0, cc_ref[7:8, :], pltpu.roll(gated, 1, 0))
        g2 = jnp.where(rows == 0, cc_ref[6:7, :], jnp.where(rows == 1, cc_ref[7:8, :], pltpu.roll(gated, 2, 0)))
        pprev = jnp.where(rows == 0, sc_ref[7:8, :], pltpu.roll(pb, 1, 0))
        cc_ref[...] = gated[tm - 8:, :]
        sc_ref[...] = pb[tm - 8:, :]
        ctail_ref[...] = gated[tm - 8:, :]
        stail_ref[...] = pb[tm - 8:, :]
    else:
        g1, g2, pprev = p1_ref[...], p2_ref[...], ps_ref[...]
        ctail_ref[...] = gated
        stail_ref[...] = pb
    conv = g2 * cw_ref[0:1, :] + g1 * cw_ref[1:2, :] + gated * cw_ref[2:3, :]
    ya_ref[...] = (a_b * conv).astype(BF16)

    xm = pb + (pprev - pb) * mu_ref[...]
    r, k, v = xm[:, 0:DB], xm[:, DB:2 * DB], xm[:, 2 * DB:3 * DB]
    lowrank = xm[:, 3 * DB:3 * DB + 128]
    dg = xm[:, 3 * DB + 128:]
    w_pre = w0_ref[...] + _dot(jnp.tanh(lowrank).astype(BF16), wd_ref[...])
    z = -w_pre
    softplus = jnp.maximum(z, 0.0) + jnp.log(1.0 + jnp.exp(-jnp.abs(z)))
    lw_ref[...] = -jnp.exp(-softplus - 0.5)
    a = jax.nn.sigmoid(a0_ref[...] + _dot(lowrank.astype(BF16), wa_ref[...]))
    g_ref[...] = _dot(jax.nn.sigmoid(dg).astype(BF16), wg_ref[...])
    kk = k * kk_ref[...]
    ss = _dot(kk * kk, _seg_ones(DB, HEAD), precision=HI)
    kkn_ref[...] = kk / jnp.maximum(jnp.sqrt(ss), 1e-12)
    r_ref[...] = r
    k_ref[...] = k * (1.0 + (a - 1.0) * ka_ref[...])
    v_ref[...] = v
    a_ref[...] = a


def _ab_pre(x, mods, ng, win, small, prev, *, nb, seq_len, tm, seq_mode):
    nt = seq_len // tm
    rows = nb * seq_len
    tail = 8 if seq_mode else tm
    full = lambda shape: pl.BlockSpec(shape, lambda b, t: tuple(0 for _ in shape))
    tok = lambda w: pl.BlockSpec((tm, w), lambda b, t: (b * nt + t, 0))
    in_specs = [tok(D),
                pl.BlockSpec((None,) + mods.shape[1:], lambda b, t: (b, 0, 0, 0)),
                full(ng.shape), full(win.shape)] + [full(s.shape) for s in small]
    args = [x, mods, ng, win] + list(small)
    if not seq_mode:
        in_specs += [tok(DA), tok(DA), tok(PB)]
        args += list(prev)
    out_shape = ([jax.ShapeDtypeStruct((rows, DA), BF16)] + [jax.ShapeDtypeStruct((rows, DB), F32)] * 7
                 + [jax.ShapeDtypeStruct((nb * tail, DA), F32), jax.ShapeDtypeStruct((nb * tail, PB), F32)])
    out_specs = ([tok(DA)] + [tok(DB)] * 7
                 + [pl.BlockSpec((tail, DA), lambda b, t: (b, 0)), pl.BlockSpec((tail, PB), lambda b, t: (b, 0))])
    scratch = [pltpu.VMEM((8, DA), F32), pltpu.VMEM((8, PB), F32)] if seq_mode else []
    return pl.pallas_call(
        functools.partial(_ab_pre_kernel, seq_mode=seq_mode, tm=tm),
        grid=(nb, nt), in_specs=in_specs, out_specs=out_specs, out_shape=out_shape,
        scratch_shapes=scratch,
        compiler_params=_cparams(("arbitrary", "arbitrary")),
        name="ab_pre_seq" if seq_mode else "ab_pre_step",
    )(*args)


def _gn_gate(o, r, k, v, g, rk, gng, gnb, seg):
    mean = _dot(o, seg, precision=HI) * (1.0 / HEAD)
    d = o - mean
    var = _dot(d * d, seg, precision=HI) * (1.0 / HEAD)
    out = d * lax.rsqrt(var + GN_EPS) * gng + gnb
    out = out + _dot(r * k * rk, seg, precision=HI) * v
    return out * g


def _wkv_seq_kernel(r_ref, lw_ref, k_ref, v_ref, kk_ref, a_ref, g_ref, rk_ref, gng_ref, gnb_ref,
                    y_ref, sfin_ref, st_ref):
    ts, c = WKV_SLAB, WKV_CHUNK
    nchunk = ts // c

    @pl.when(pl.program_id(2) == 0)
    def _():
        st_ref[...] = jnp.zeros_like(st_ref)

    r, lw, k, v, kk, a = r_ref[...], lw_ref[...], k_ref[...], v_ref[...], kk_ref[...], a_ref[...]
    ri, ci = _iota((ts, ts), 0), _iota((ts, ts), 1)
    same = ri // c == ci // c
    strict = (same & (ci < ri)).astype(F32)
    incl = (same & (ci <= ri)).astype(F32)
    lane = _iota((1, 2 * HEAD), 1)
    m0 = (lane < HEAD).astype(F32)
    m1 = 1.0 - m0
    headdiag = _seg_ones(2 * HEAD, HEAD)

    cums = _dot(jnp.concatenate([incl, same.astype(F32)], axis=0), lw, precision=HI)
    cl, ctot = cums[:ts], cums[ts:]
    e_neg = jnp.exp(-cl)
    e_end = jnp.exp(ctot - cl)
    at = -kk * jnp.exp(cl - lw)
    rt = r * jnp.exp(cl)
    beta = kk * a
    bt, kt = beta * e_neg, k * e_neg
    bend, kend = beta * e_end, k * e_end

    abig = _dot_nt(jnp.concatenate([at * m0, rt * m0, at * m1, rt * m1], axis=0),
                   jnp.concatenate([bt, kt], axis=0), precision=HI)
    aab = (abig[0:ts, 0:ts], abig[2 * ts:3 * ts, 0:ts])
    aak = (abig[0:ts, ts:], abig[2 * ts:3 * ts, ts:])
    arb = (abig[ts:2 * ts, 0:ts], abig[3 * ts:, 0:ts])
    ark = (abig[ts:2 * ts, ts:], abig[3 * ts:, ts:])
    zero = jnp.zeros((ts, ts), F32)

    def two_heads(p0, p1):
        return jnp.concatenate([jnp.concatenate([p0, zero], axis=1), jnp.concatenate([zero, p1], axis=1)], axis=0)

    lbd = two_heads(aab[0] * strict, aab[1] * strict)
    r2, c2 = _iota((2 * ts, 2 * ts), 0), _iota((2 * ts, 2 * ts), 1)

    def blk(b):
        return r2 // b == c2 // b

    d1 = jnp.where(blk(8), lbd, 0.0)
    tinv = (r2 == c2).astype(F32) + d1
    d2 = _dot(d1, d1, precision=HI)
    tinv = tinv + _dot(tinv, d2, precision=HI)
    d4 = _dot(d2, d2, precision=HI)
    tinv = tinv + _dot(tinv, d4, precision=HI)
    b = 8
    while b < c:
        off = jnp.where(blk(2 * b) & jnp.logical_not(blk(b)), lbd, 0.0)
        tinv = tinv + _dot(_dot(tinv, off, precision=HI), tinv, precision=HI)
        b *= 2

    v_st = jnp.concatenate([v * m0, v * m1], axis=0)
    y_st = _dot(two_heads(aak[0] * strict, aak[1] * strict), v_st, precision=HI)
    at_st = jnp.concatenate([at * m0, at * m1], axis=0)
    wu = _dot(tinv, jnp.concatenate([at_st, y_st], axis=1), precision=HI)
    w = wu[:ts, :2 * HEAD] + wu[ts:, :2 * HEAD]
    u = wu[:ts, 2 * HEAD:] + wu[ts:, 2 * HEAD:]
    rhs = jnp.concatenate([jnp.concatenate([w, u], axis=1),
                           jnp.concatenate([jnp.zeros_like(v), v], axis=1)], axis=0)
    bend_t, kend_t = bend.T, kend.T
    tcol = _iota((1, ts), 1) // c
    lhs = []
    for ch in range(nchunk):
        cm = (tcol == ch).astype(F32)
        lhs.append(jnp.concatenate([bend_t * cm, kend_t * cm], axis=1))
    lhs.append(jnp.concatenate([arb[0] * incl, ark[0] * incl], axis=1))
    lhs.append(jnp.concatenate([arb[1] * incl, ark[1] * incl], axis=1))
    big = _dot(jnp.concatenate(lhs, axis=0), rhs, precision=HI)
    q0, q1 = big[nchunk * ts:(nchunk + 1) * ts], big[(nchunk + 1) * ts:]
    q = rt + q0[:, :2 * HEAD] * m0 + q1[:, :2 * HEAD] * m1
    olocal = q0[:, 2 * HEAD:] * m0 + q1[:, 2 * HEAD:] * m1

    st = st_ref[...]
    eye = (_iota((2 * HEAD, 2 * HEAD), 0) == _iota((2 * HEAD, 2 * HEAD), 1)).astype(F32)
    outs = []
    for ch in range(nchunk):
        mn = big[ch * ts:(ch + 1) * ts]
        outs.append(_dot(q[ch * c:(ch + 1) * c], st, precision=HI) + olocal[ch * c:(ch + 1) * c])
        trans = eye * jnp.exp(ctot[ch * c:ch * c + 1, :]) + mn[:, :2 * HEAD] * headdiag
        st = _dot(trans, st, precision=HI) + mn[:, 2 * HEAD:] * headdiag
    st_ref[...] = st
    o = jnp.concatenate(outs, axis=0)
    y_ref[...] = _gn_gate(o, r, k, v, g_ref[...], rk_ref[...], gng_ref[...], gnb_ref[...], headdiag).astype(BF16)

    @pl.when(pl.program_id(2) == pl.num_programs(2) - 1)
    def _():
        sfin_ref[...] = st


def _wkv_seq(r, lw, k, v, kk, a, g, rk, gng, gnb, *, nb, seq_len):
    ns = seq_len // WKV_SLAB
    npair = DB // (2 * HEAD)
    tok = pl.BlockSpec((WKV_SLAB, 2 * HEAD), lambda b, p, s: (b * ns + s, p))
    par = pl.BlockSpec((1, 2 * HEAD), lambda b, p, s: (0, p))
    return pl.pallas_call(
        _wkv_seq_kernel,
        grid=(nb, npair, ns),
        in_specs=[tok] * 7 + [par] * 3,
        out_specs=[tok, pl.BlockSpec((None, None, 2 * HEAD, 2 * HEAD), lambda b, p, s: (b, p, 0, 0))],
        out_shape=[jax.ShapeDtypeStruct((nb * seq_len, DB), BF16),
                   jax.ShapeDtypeStruct((nb, npair, 2 * HEAD, 2 * HEAD), F32)],
        scratch_shapes=[pltpu.VMEM((2 * HEAD, 2 * HEAD), F32)],
        compiler_params=_cparams(("arbitrary", "arbitrary", "arbitrary")),
        name="wkv_seq",
    )(r, lw, k, v, kk, a, g, rk, gng, gnb)


def _wkv_step_kernel(s_ref, r_ref, lw_ref, k_ref, v_ref, kk_ref, a_ref, g_ref, rk_ref, gng_ref, gnb_ref,
                     y_ref, so_ref):
    s = s_ref[...]
    r, k, v, kk, a = r_ref[...], k_ref[...], v_ref[...], kk_ref[...], a_ref[...]
    w = jnp.exp(lw_ref[...])
    eye = (_iota((1, HEAD, HEAD), 1) == _iota((1, HEAD, HEAD), 2)).astype(F32)
    v_col = jnp.sum(eye * v, axis=2, keepdims=True)
    s_kk = jnp.sum(s * kk, axis=2, keepdims=True)
    s = s * w - s_kk * (kk * a) + v_col * k
    so_ref[...] = s
    o_col = jnp.sum(s * r, axis=2, keepdims=True)
    o = jnp.sum(eye * o_col, axis=1, keepdims=True)
    mean = jnp.mean(o, axis=2, keepdims=True)
    d = o - mean
    var = jnp.mean(d * d, axis=2, keepdims=True)
    out = d * lax.rsqrt(var + GN_EPS) * gng_ref[...] + gnb_ref[...]
    out = out + jnp.sum(r * k * rk_ref[...], axis=2, keepdims=True) * v
    y_ref[...] = out * g_ref[...]


def _wkv_step(state, vecs, params, *, nheads):
    n = state.shape[0]
    tb = 8 * nheads
    sspec = pl.BlockSpec((tb, HEAD, HEAD), lambda i: (i, 0, 0))
    vspec = pl.BlockSpec((tb, 1, HEAD), lambda i: (i, 0, 0))
    pspec = pl.BlockSpec((tb, 1, HEAD), lambda i: (0, 0, 0))
    reps = tb // nheads
    params = [jnp.tile(p, (reps, 1, 1)) for p in params]
    return pl.pallas_call(
        _wkv_step_kernel,
        grid=(n // tb,),
        in_specs=[sspec] + [vspec] * 7 + [pspec] * 3,
        out_specs=[vspec, sspec],
        out_shape=[jax.ShapeDtypeStruct((n, 1, HEAD), F32), jax.ShapeDtypeStruct((n, HEAD, HEAD), F32)],
        compiler_params=_cparams(("arbitrary",)),
        name="wkv_step",
    )(state, *vecs, *params)


def _ab_post_kernel(ya_ref, yb_ref, x_ref, mod_ref, ng_ref, wo_ref, wrt_ref, br_ref,
                    h2in_ref, combin_ref, xo_ref, h2_ref, comb_ref):
    del h2in_ref, combin_ref
    y = _dot(ya_ref[...], wo_ref[0:DA, :]) + _dot(yb_ref[...], wo_ref[DA:, :])
    _post(y, x_ref[...], mod_ref, ng_ref, wrt_ref, br_ref, xo_ref, h2_ref, comb_ref)


def _ab_post(ya, yb, x, mods, ng, wo, wrt, br, h2_all, comb_all, *, nb, seq_len, tm, row0):
    nt = seq_len // tm
    t0 = row0 // tm
    full = lambda shape: pl.BlockSpec(shape, lambda b, t: tuple(0 for _ in shape))
    tok = lambda w: pl.BlockSpec((tm, w), lambda b, t: (b * nt + t, 0))
    tok_all = lambda w: pl.BlockSpec((tm, w), lambda b, t: (t0 + b * nt + t, 0))
    anyspec = pl.BlockSpec(memory_space=pl.ANY)
    return pl.pallas_call(
        _ab_post_kernel,
        grid=(nb, nt),
        in_specs=[tok(DA), tok(DB), tok(D),
                  pl.BlockSpec((None,) + mods.shape[1:], lambda b, t: (b, 0, 0, 0)),
                  full(ng.shape), full(wo.shape), full(wrt.shape), full(br.shape), anyspec, anyspec],
        out_specs=[tok(D), tok_all(D), tok_all(128)],
        out_shape=[jax.ShapeDtypeStruct(x.shape, F32), jax.ShapeDtypeStruct(h2_all.shape, BF16),
                   jax.ShapeDtypeStruct(comb_all.shape, F32)],
        input_output_aliases={8: 1, 9: 2},
        compiler_params=_cparams(("arbitrary", "arbitrary")),
        name="ab_post",
    )(ya, yb, x, mods, ng, wo, wrt, br, h2_all, comb_all)


def _gmlp_kernel(*refs, seq_mode, tm):
    it = iter(refs)
    x_ref, f_ref, pmod_ref, png_ref, mod_ref, ng_ref = (next(it) for _ in range(6))
    win_ref, bin_ref, lng_ref, lnb_ref, ws_ref, bs_ref, wo_ref, wrt_ref, br_ref = (next(it) for _ in range(9))
    next(it), next(it)
    xo_ref, h2_ref, comb_ref = next(it), next(it), next(it)
    if not seq_mode:
        cv_ref = next(it)
    x = _residual_in(x_ref, f_ref, pmod_ref, png_ref)
    h = _rms(x, ng_ref[0:1, :]) * (1.0 + mod_ref[1]) + mod_ref[0]
    z = jax.nn.gelu(_dot(h.astype(BF16), win_ref[...]) + bin_ref[...])
    u, v = z[:, :D], z[:, D:]
    mu = jnp.mean(v, axis=-1, keepdims=True)
    var = jnp.mean(jnp.square(v - mu), axis=-1, keepdims=True)
    v = (v - mu) * lax.rsqrt(var + EPS) * lng_ref[...] + lnb_ref[...]
    if seq_mode:
        vb = v.astype(BF16)
        causal = _iota((CHUNK, CHUNK), 1) <= _iota((CHUNK, CHUNK), 0)
        cols = []
        for g in range(D // CHUNK):
            wsg = jnp.where(causal, ws_ref[g], 0.0).astype(BF16)
            bsg = bs_ref[:, g:g + 1]
            rows = [_dot(wsg, vb[c * CHUNK:(c + 1) * CHUNK, g * CHUNK:(g + 1) * CHUNK]) + bsg
                    for c in range(tm // CHUNK)]
            cols.append(jnp.concatenate(rows, axis=0) if len(rows) > 1 else rows[0])
        s = jnp.concatenate(cols, axis=1)
    else:
        cv_ref[...] = v
        s = v * ws_ref[...] + bs_ref[...]
    y = _dot((u * s).astype(BF16), wo_ref[...])
    _post(y, x, mod_ref, ng_ref, wrt_ref, br_ref, xo_ref, h2_ref, comb_ref)


def _gmlp(x, f_all, pmods, png, mods, ng, weights, h2_all, comb_all, *, nb, seq_len, tm, row0, seq_mode):
    nt = seq_len // tm
    t0 = row0 // tm
    full = lambda shape: pl.BlockSpec(shape, lambda b, t: tuple(0 for _ in shape))
    tok = lambda w: pl.BlockSpec((tm, w), lambda b, t: (b * nt + t, 0))
    tok_all = lambda w: pl.BlockSpec((tm, w), lambda b, t: (t0 + b * nt + t, 0))
    modspec = lambda m: pl.BlockSpec((None,) + m.shape[1:], lambda b, t: (b, 0, 0, 0))
    anyspec = pl.BlockSpec(memory_space=pl.ANY)
    in_specs = ([tok(D), tok_all(D), modspec(pmods), full(png.shape), modspec(mods), full(ng.shape)]
                + [full(w.shape) for w in weights] + [anyspec, anyspec])
    out_specs = [tok(D), tok_all(D), tok_all(128)]
    out_shape = [jax.ShapeDtypeStruct(x.shape, F32), jax.ShapeDtypeStruct(h2_all.shape, BF16),
                 jax.ShapeDtypeStruct(comb_all.shape, F32)]
    if not seq_mode:
        out_specs.append(tok(D))
        out_shape.append(jax.ShapeDtypeStruct(x.shape, F32))
    n_in = len(in_specs)
    return pl.pallas_call(
        functools.partial(_gmlp_kernel, seq_mode=seq_mode, tm=tm),
        grid=(nb, nt), in_specs=in_specs, out_specs=out_specs, out_shape=out_shape,
        input_output_aliases={n_in - 2: 1, n_in - 1: 2},
        compiler_params=_cparams(("arbitrary", "arbitrary")),
        name="gmlp_seq" if seq_mode else "gmlp_step",
    )(x, f_all, pmods, png, mods, ng, *weights, h2_all, comb_all)


def _moe_kernel(h_ref, comb_ref, wg_ref, wu_ref, wd_ref, sg_ref, su_ref, sd_ref, o_ref):
    e = pl.program_id(1)
    h = h_ref[...]

    @pl.when(e == 0)
    def _():
        hs = _silu(_dot(h, sg_ref[...])) * _dot(h, su_ref[...])
        o_ref[...] = _dot(hs.astype(BF16), sd_ref[...])

    comb = comb_ref[...]
    c_e = jnp.sum(jnp.where(_iota(comb.shape, 1) == e, comb, 0.0), axis=1, keepdims=True)
    hw = _silu(_dot(h, wg_ref[...])) * _dot(h, wu_ref[...]) * c_e
    o_ref[...] += _dot(hw.astype(BF16), wd_ref[...])


def _moe(h_all, comb_all, wg, wu, wd, sg, su, sd, *, tm):
    rows = h_all.shape[0]
    full = lambda shape: pl.BlockSpec(shape, lambda i, e: tuple(0 for _ in shape))
    return pl.pallas_call(
        _moe_kernel,
        grid=(rows // tm, NE),
        in_specs=[pl.BlockSpec((tm, D), lambda i, e: (i, 0)), pl.BlockSpec((tm, 128), lambda i, e: (i, 0)),
                  pl.BlockSpec((None, D, DE), lambda i, e: (e, 0, 0)),
                  pl.BlockSpec((None, D, DE), lambda i, e: (e, 0, 0)),
                  pl.BlockSpec((None, DE, D), lambda i, e: (e, 0, 0)),
                  full(sg.shape), full(su.shape), full(sd.shape)],
        out_specs=pl.BlockSpec((tm, D), lambda i, e: (i, 0)),
        out_shape=jax.ShapeDtypeStruct((rows, D), F32),
        compiler_params=_cparams(("arbitrary", "arbitrary")),
        name="moe",
    )(h_all, comb_all, wg, wu, wd, sg, su, sd)


def _final_kernel(x_ref, f_ref, pmod_ref, png_ref, o_ref):
    o_ref[...] = _residual_in(x_ref, f_ref, pmod_ref, png_ref)


def _final(x, f_all, pmods, png, *, nb, seq_len, tm, row0):
    nt = seq_len // tm
    t0 = row0 // tm
    return pl.pallas_call(
        _final_kernel,
        grid=(nb, nt),
        in_specs=[pl.BlockSpec((tm, D), lambda b, t: (b * nt + t, 0)),
                  pl.BlockSpec((tm, D), lambda b, t: (t0 + b * nt + t, 0)),
                  pl.BlockSpec((None,) + pmods.shape[1:], lambda b, t: (b, 0, 0, 0)),
                  pl.BlockSpec(png.shape, lambda b, t: (0, 0))],
        out_specs=pl.BlockSpec((tm, D), lambda b, t: (b * nt + t, 0)),
        out_shape=jax.ShapeDtypeStruct(x.shape, F32),
        compiler_params=_cparams(("arbitrary", "arbitrary")),
        name="final_residual",
    )(x, f_all, pmods, png)


def kernel(x_prompt, x_sample, c_prompt, c_sample, state_conv, state_shift, state_wkv, w_mod, b_mod, norm_g, ab_w_in, ab_conv_w, ab_mu, ab_w0, ab_w_decay_up, ab_a0, ab_a_up, ab_g_up, ab_k_k, ab_k_a, ab_r_k, ab_gn_g, ab_gn_b, ab_w_out, gm_w_in, gm_b_in, gm_ln_g, gm_ln_b, gm_w_s, gm_b_s, gm_w_out, moe_w_router, moe_b_router, moe_w_gate, moe_w_up, moe_w_down, moe_ws_gate, moe_ws_up, moe_ws_down):
    bsz, seq_len, _ = x_prompt.shape
    nsamp = x_sample.shape[0]
    nheads = DB // HEAD
    n_prompt = bsz * seq_len
    tm_moe = 1664
    n_all = -(-(n_prompt + nsamp) // tm_moe) * tm_moe
    tm = 512

    xp = x_prompt.reshape(n_prompt, D)
    xs = x_sample.reshape(nsamp, D)

    m = _modulation(jnp.concatenate([c_prompt, c_sample], axis=0), w_mod, b_mod)
    mods_p = [m[l, :bsz].reshape(bsz, 6, 1, D) for l in range(2)]
    mods_s = [m[l, bsz:].reshape(nsamp, 6, D).transpose(1, 0, 2)[None] for l in range(2)]

    row = lambda t: t.reshape(1, -1)
    pad_rows = lambda t, lo, hi: jnp.pad(t, ((lo, hi), (0, 0)))
    small = [ab_conv_w[0], row(ab_mu[0]), row(ab_w0[0]),
             pad_rows(ab_w_decay_up[0], 0, 64).astype(BF16), row(ab_a0[0]),
             pad_rows(ab_a_up[0], 64, 0).astype(BF16), ab_g_up[0].astype(BF16),
             row(ab_k_k[0]), row(ab_k_a[0])]
    win0 = ab_w_in[0].astype(BF16)
    wo0 = ab_w_out[0].astype(BF16)
    rk, gng, gnb = row(ab_r_k[0]), row(ab_gn_g[0]), row(ab_gn_b[0])
    wrt = [moe_w_router[l].T for l in range(2)]
    br = [moe_b_router[l].reshape(NE, 1) for l in range(2)]

    h2_all = jnp.zeros((n_all, D), BF16)
    comb_all = jnp.zeros((n_all, 128), F32)

    ya, r, lw, k, v, kk, a, g, ctail, stail = _ab_pre(
        xp, mods_p[0], norm_g[0], win0, small, None, nb=bsz, seq_len=seq_len, tm=tm, seq_mode=True)
    conv_p = ctail.reshape(bsz, 8, DA)[:, 6:8][None]
    shift_p = stail.reshape(bsz, 8, PB)[:, 7][None]
    yb, sfin = _wkv_seq(r, lw, k, v, kk, a, g, rk, gng, gnb, nb=bsz, seq_len=seq_len)
    sfin = sfin.reshape(bsz, nheads // 2, 2, HEAD, 2, HEAD)
    wkv_p = jnp.stack([sfin[:, :, 0, :, 0, :], sfin[:, :, 1, :, 1, :]], axis=2)
    wkv_p = wkv_p.reshape(bsz, nheads, HEAD, HEAD).transpose(0, 1, 3, 2)[None]
    xp1, h2_all, comb_all = _ab_post(ya, yb, xp, mods_p[0], norm_g[0], wo0, wrt[0], br[0], h2_all, comb_all,
                                     nb=bsz, seq_len=seq_len, tm=tm, row0=0)

    prev = [state_conv[0, :, 0], state_conv[0, :, 1], state_shift[0]]
    ya_s, r, lw, k, v, kk, a, g, gated_s, pb_s = _ab_pre(
        xs, mods_s[0], norm_g[0], win0, small, prev, nb=1, seq_len=nsamp, tm=nsamp, seq_mode=False)
    conv_s = jnp.stack([state_conv[0, :, 1], gated_s], axis=1)[None]
    shift_s = pb_s[None]
    vecs = [t.reshape(nsamp * nheads, 1, HEAD) for t in (r, lw, k, v, kk, a, g)]
    params = [t.reshape(nheads, 1, HEAD) for t in (ab_r_k[0], ab_gn_g[0], ab_gn_b[0])]
    yb_s, snew = _wkv_step(state_wkv[0].reshape(nsamp * nheads, HEAD, HEAD), vecs, params, nheads=nheads)
    wkv_s = snew.reshape(1, nsamp, nheads, HEAD, HEAD)
    xs1, h2_all, comb_all = _ab_post(ya_s, yb_s.reshape(nsamp, DB).astype(BF16), xs, mods_s[0], norm_g[0], wo0,
                                     wrt[0], br[0], h2_all, comb_all, nb=1, seq_len=nsamp, tm=nsamp, row0=n_prompt)

    def moe_layer(l, h_all, c_all):
        return _moe(h_all, c_all, moe_w_gate[l].astype(BF16), moe_w_up[l].astype(BF16), moe_w_down[l].astype(BF16),
                    moe_ws_gate[l].astype(BF16), moe_ws_up[l].astype(BF16), moe_ws_down[l].astype(BF16), tm=tm_moe)

    f0 = moe_layer(0, h2_all, comb_all)

    gw = [gm_w_in[0].astype(BF16), row(gm_b_in[0]), row(gm_ln_g[0]), row(gm_ln_b[0])]
    gw_seq = gw + [gm_w_s[0], gm_b_s[0].T, gm_w_out[0].astype(BF16), wrt[1], br[1]]
    gw_step = gw + [row(jnp.repeat(gm_w_s[0, :, 0, 0], CHUNK)), row(jnp.repeat(gm_b_s[0, :, 0], CHUNK)),
                    gm_w_out[0].astype(BF16), wrt[1], br[1]]
    tm1 = 256
    xp2, h2_all, comb_all = _gmlp(xp1, f0, mods_p[0], norm_g[0], mods_p[1], norm_g[1], gw_seq, h2_all, comb_all,
                                  nb=bsz, seq_len=seq_len, tm=tm1, row0=0, seq_mode=True)
    xs2, h2_all, comb_all, chunk_v = _gmlp(xs1, f0, mods_s[0], norm_g[0], mods_s[1], norm_g[1], gw_step, h2_all,
                                           comb_all, nb=1, seq_len=nsamp, tm=nsamp, row0=n_prompt, seq_mode=False)
    f1 = moe_layer(1, h2_all, comb_all)
    y_p = _final(xp2, f1, mods_p[1], norm_g[1], nb=bsz, seq_len=seq_len, tm=tm, row0=0)
    y_s = _final(xs2, f1, mods_s[1], norm_g[1], nb=1, seq_len=nsamp, tm=nsamp, row0=n_prompt)

    return (y_p.reshape(bsz, seq_len, D), y_s.reshape(nsamp, 1, D), conv_p, conv_s, shift_p, shift_s,
            wkv_p, wkv_s, chunk_v.reshape(1, nsamp, 1, D))
```

```python
import functools

import jax
import jax.numpy as jnp
from jax import lax
from jax.experimental import pallas as pl
from jax.experimental.pallas import tpu as pltpu

F32 = jnp.float32
BF16 = jnp.bfloat16
HI = lax.Precision.HIGHEST

D = 1024
DA = 512
DB = 512
HEAD = 64
PB = 1792
PAB = 3 * DA + PB
NE = 64
NG = 8
EPG = NE // NG
TOPG = 4
TOPK = 8
DE = 256
ROUTE_SCALE = 2.5
EPS = 1e-6
GN_EPS = 64e-5
CHUNK = 128
WKV_SLAB = 128
WKV_CHUNK = 32
WKV_PAIRS_PER_STEP = 4
WKV_P_STATE = 2
WKV_P_INV = 1

VMEM_LIMIT = 56 * 1024 * 1024


def _cparams(sem):
    return pltpu.CompilerParams(dimension_semantics=sem, vmem_limit_bytes=VMEM_LIMIT)


def _dot(a, b, precision=None):
    return jnp.dot(a, b, preferred_element_type=F32, precision=precision)


def _dot_nt(a, b, precision=None):
    return lax.dot_general(a, b, (((1,), (1,)), ((), ())), preferred_element_type=F32, precision=precision)


def _iota(shape, dim):
    return lax.broadcasted_iota(jnp.int32, shape, dim)


def _rms(x, g):
    return x * lax.rsqrt(jnp.mean(x * x, axis=-1, keepdims=True) + EPS) * g


def _silu(x):
    return x * jax.nn.sigmoid(x)


def _seg_ones(n, seg):
    return (_iota((n, n), 0) // seg == _iota((n, n), 1) // seg).astype(F32)


def _mod_kernel(c_ref, w_ref, b_ref, o_ref):
    s = _silu(c_ref[...]).astype(BF16)
    o_ref[...] = _dot(s, w_ref[...].astype(BF16)) + b_ref[...]


def _modulation(c_all, w_mod, b_mod):
    depth, _, width = w_mod.shape
    rows = c_all.shape[0]
    bn = 512
    return pl.pallas_call(
        _mod_kernel,
        grid=(depth, width // bn),
        in_specs=[pl.BlockSpec((rows, D), lambda l, j: (0, 0)),
                  pl.BlockSpec((None, D, bn), lambda l, j: (l, 0, j)),
                  pl.BlockSpec((None, 1, bn), lambda l, j: (l, 0, j))],
        out_specs=pl.BlockSpec((None, rows, bn), lambda l, j: (l, 0, j)),
        out_shape=jax.ShapeDtypeStruct((depth, rows, width), F32),
        compiler_params=_cparams(("arbitrary", "arbitrary")),
        name="modulation",
    )(c_all, w_mod, b_mod.reshape(depth, 1, width))


def _residual_in(x_ref, f_ref, pmod_ref, png_ref):
    x = x_ref[...]
    if f_ref is None:
        return x
    return x + pmod_ref[5] * _rms(f_ref[...], png_ref[3:4, :])


def _route(h, wrt_ref, br_ref):
    tm = h.shape[0]
    logits = _dot_nt(wrt_ref[...], h, precision=HI)
    scores = jax.nn.sigmoid(logits)
    sel = scores + br_ref[...]
    sub = _iota((EPG, tm), 0).astype(F32)
    neg = jnp.float32(-jnp.inf)
    blks, sblks, gscore = [], [], []
    for g in range(NG):
        blk = sel[g * EPG:(g + 1) * EPG, :]
        m1 = jnp.max(blk, axis=0, keepdims=True)
        first = jnp.min(jnp.where(blk == m1, sub, float(EPG)), axis=0, keepdims=True)
        m2 = jnp.max(jnp.where(sub == first, neg, blk), axis=0, keepdims=True)
        blks.append(blk)
        sblks.append(scores[g * EPG:(g + 1) * EPG, :])
        gscore.append(m1 + m2)
    masked = []
    for g in range(NG):
        cnt = jnp.zeros((1, tm), F32)
        for g2 in range(NG):
            if g2 == g:
                continue
            beats = (gscore[g2] >= gscore[g]) if g2 < g else (gscore[g2] > gscore[g])
            cnt = cnt + jnp.where(beats, 1.0, 0.0)
        masked.append(jnp.where(cnt < TOPG, blks[g], neg))
    cnts = [jnp.zeros((EPG, tm), F32) for _ in range(NG)]
    for g2 in range(NG):
        for j in range(EPG):
            row = masked[g2][j:j + 1, :]
            for g in range(NG):
                ge = jnp.where(row >= masked[g], 1.0, 0.0)
                gt = jnp.where(row > masked[g], 1.0, 0.0)
                if g > g2:
                    cnts[g] = cnts[g] + ge
                elif g < g2:
                    cnts[g] = cnts[g] + gt
                else:
                    cnts[g] = cnts[g] + jnp.where(sub > j, ge, gt)
    wts = [jnp.where(cnts[g] < TOPK, sblks[g], 0.0) for g in range(NG)]
    tot = wts[0]
    for g in range(1, NG):
        tot = tot + wts[g]
    denom = jnp.sum(tot, axis=0, keepdims=True)
    comb_t = jnp.concatenate([w / denom * ROUTE_SCALE for w in wts] + [jnp.zeros((NE, tm), F32)], axis=0)
    return comb_t.T


def _post(y, x, mod_ref, ng_ref, wrt_ref, br_ref, xo_ref, h2_ref, comb_ref):
    xn = x + mod_ref[2] * _rms(y, ng_ref[1:2, :])
    xo_ref[...] = xn
    h2 = _rms(xn, ng_ref[2:3, :]) * (1.0 + mod_ref[4]) + mod_ref[3]
    h2_ref[...] = h2.astype(BF16)
    comb_ref[...] = _route(h2, wrt_ref, br_ref)


def _ab_pre_kernel(*refs, seq_mode, tm):
    it = iter(refs)
    x_ref, mod_ref, ng_ref, win_ref = next(it), next(it), next(it), next(it)
    cw_ref, mu_ref, w0_ref, wd_ref, a0_ref, wa_ref, wg_ref, kk_ref, ka_ref = (next(it) for _ in range(9))
    if not seq_mode:
        p2_ref, p1_ref, ps_ref = next(it), next(it), next(it)
    ya_ref, r_ref, lw_ref, k_ref, v_ref, kkn_ref, a_ref, g_ref, ctail_ref, stail_ref = (next(it) for _ in range(10))
    if seq_mode:
        cc_ref, sc_ref = next(it), next(it)

        @pl.when(pl.program_id(1) == 0)
        def _():
            cc_ref[...] = jnp.zeros_like(cc_ref)
            sc_ref[...] = jnp.zeros_like(sc_ref)

    x = x_ref[...]
    h = _rms(x, ng_ref[0:1, :]) * (1.0 + mod_ref[1]) + mod_ref[0]
    proj = _dot(h.astype(BF16), win_ref[...])
    a_h, a_b, a_c = proj[:, 0:DA], proj[:, DA:2 * DA], proj[:, 2 * DA:3 * DA]
    pb = proj[:, 3 * DA:]
    gated = a_c * a_h
    if seq_mode:
        rows = _iota((tm, 1), 0)
        g1 = jnp.where(rows == 0, cc_ref[7:8, :], pltpu.roll(gated, 1, 0))
        g2 = jnp.where(rows == 0, cc_ref[6:7, :], jnp.where(rows == 1, cc_ref[7:8, :], pltpu.roll(gated, 2, 0)))
        pprev = jnp.where(rows == 0, sc_ref[7:8, :], pltpu.roll(pb, 1, 0))
        cc_ref[...] = gated[tm - 8:, :]
        sc_ref[...] = pb[tm - 8:, :]
        ctail_ref[...] = gated[tm - 8:, :]
        stail_ref[...] = pb[tm - 8:, :]
    else:
        g1, g2, pprev = p1_ref[...], p2_ref[...], ps_ref[...]
        ctail_ref[...] = gated
        stail_ref[...] = pb
    conv = g2 * cw_ref[0:1, :] + g1 * cw_ref[1:2, :] + gated * cw_ref[2:3, :]
    ya_ref[...] = (a_b * conv).astype(BF16)

    xm = pb + (pprev - pb) * mu_ref[...]
    r, k, v = xm[:, 0:DB], xm[:, DB:2 * DB], xm[:, 2 * DB:3 * DB]
    lowrank = xm[:, 3 * DB:3 * DB + 128]
    dg = xm[:, 3 * DB + 128:]
    w_pre = w0_ref[...] + _dot(jnp.tanh(lowrank).astype(BF16), wd_ref[...])
    z = -w_pre
    softplus = jnp.maximum(z, 0.0) + jnp.log(1.0 + jnp.exp(-jnp.abs(z)))
    lw_ref[...] = -jnp.exp(-softplus - 0.5)
    a = jax.nn.sigmoid(a0_ref[...] + _dot(lowrank.astype(BF16), wa_ref[...]))
    g_ref[...] = _dot(jax.nn.sigmoid(dg).astype(BF16), wg_ref[...])
    kk = k * kk_ref[...]
    ss = _dot(kk * kk, _seg_ones(DB, HEAD), precision=HI)
    kkn_ref[...] = kk / jnp.maximum(jnp.sqrt(ss), 1e-12)
    r_ref[...] = r
    k_ref[...] = k * (1.0 + (a - 1.0) * ka_ref[...])
    v_ref[...] = v
    a_ref[...] = a


def _ab_pre(x, mods, ng, win, small, prev, *, nb, seq_len, tm, seq_mode):
    nt = seq_len // tm
    rows = nb * seq_len
    tail = 8 if seq_mode else tm
    full = lambda shape: pl.BlockSpec(shape, lambda b, t: tuple(0 for _ in shape))
    tok = lambda w: pl.BlockSpec((tm, w), lambda b, t: (b * nt + t, 0))
    in_specs = [tok(D),
                pl.BlockSpec((None,) + mods.shape[1:], lambda b, t: (b, 0, 0, 0)),
                full(ng.shape), full(win.shape)] + [full(s.shape) for s in small]
    args = [x, mods, ng, win] + list(small)
    if not seq_mode:
        in_specs += [tok(DA), tok(DA), tok(PB)]
        args += list(prev)
    out_shape = ([jax.ShapeDtypeStruct((rows, DA), BF16)] + [jax.ShapeDtypeStruct((rows, DB), F32)] * 7
                 + [jax.ShapeDtypeStruct((nb * tail, DA), F32), jax.ShapeDtypeStruct((nb * tail, PB), F32)])
    out_specs = ([tok(DA)] + [tok(DB)] * 7
                 + [pl.BlockSpec((tail, DA), lambda b, t: (b, 0)), pl.BlockSpec((tail, PB), lambda b, t: (b, 0))])
    scratch = [pltpu.VMEM((8, DA), F32), pltpu.VMEM((8, PB), F32)] if seq_mode else []
    return pl.pallas_call(
        functools.partial(_ab_pre_kernel, seq_mode=seq_mode, tm=tm),
        grid=(nb, nt), in_specs=in_specs, out_specs=out_specs, out_shape=out_shape,
        scratch_shapes=scratch,
        compiler_params=_cparams(("arbitrary", "arbitrary")),
        name="ab_pre_seq" if seq_mode else "ab_pre_step",
    )(*args)


def _split(x):
    hi = x.astype(BF16)
    return hi, (x - hi.astype(F32)).astype(BF16)


def _mm(a, b, pa=1, pb=1, nt=False):
    ah, al = _split(a) if pa == 2 else (a.astype(BF16), None)
    bh, bl = _split(b) if pb == 2 else (b.astype(BF16), None)
    bx = 1 if nt else 0
    if pa == 2 and pb == 2:
        ah, bh = jnp.concatenate([ah, al, ah], axis=1), jnp.concatenate([bh, bh, bl], axis=bx)
    elif pa == 2:
        ah, bh = jnp.concatenate([ah, al], axis=1), jnp.concatenate([bh, bh], axis=bx)
    elif pb == 2:
        ah, bh = jnp.concatenate([ah, ah], axis=1), jnp.concatenate([bh, bl], axis=bx)
    return _dot_nt(ah, bh) if nt else _dot(ah, bh)


def _gn_gate(o, r, k, v, g, rk, gng, gnb, seg):
    mean = _mm(o, seg, 2, 1) * (1.0 / HEAD)
    d = o - mean
    var = _mm(d * d, seg, 2, 1) * (1.0 / HEAD)
    out = d * lax.rsqrt(var + GN_EPS) * gng + gnb
    out = out + _mm(r * k * rk, seg, 2, 1) * v
    return out * g


def _wkv_pair(r, lw, k, v, kk, a, st, cst):
    ts, c = WKV_SLAB, WKV_CHUNK
    nchunk = ts // c
    ps, pi = WKV_P_STATE, WKV_P_INV
    strict, incl, cum_lhs, m0, m1, headdiag, eye, eye2, blks, chunk_cols = cst

    l1 = lw.astype(BF16)
    rem = lw - l1.astype(F32)
    l2 = rem.astype(BF16)
    l3 = (rem - l2.astype(F32)).astype(BF16)
    cums = _dot(cum_lhs, jnp.concatenate([l1, l2, l3], axis=0))
    yield
    cl, ctot = cums[:ts], cums[ts:]
    e_neg = jnp.exp(-cl)
    e_end = jnp.exp(ctot - cl)
    at = -kk * jnp.exp(cl - lw)
    rt = r * jnp.exp(cl)
    beta = kk * a
    bk = jnp.concatenate([beta * e_neg, k * e_neg], axis=0)
    bend, kend = beta * e_end, k * e_end

    at_st = jnp.concatenate([at * m0, at * m1], axis=0)
    a_st = _mm(at_st, bk, ps, ps, nt=True)
    a_out = _mm(jnp.concatenate([rt * m0, rt * m1], axis=0), bk, 1, 1, nt=True)
    yield
    zero = jnp.zeros((ts, ts), F32)

    def two_heads(p0, p1):
        return jnp.concatenate([jnp.concatenate([p0, zero], axis=1), jnp.concatenate([zero, p1], axis=1)], axis=0)

    lbd = two_heads(a_st[:ts, :ts] * strict, a_st[ts:, :ts] * strict)
    d1 = jnp.where(blks[0], lbd, 0.0)
    tinv = eye2 + d1
    d2 = _mm(d1, d1, pi, pi)
    yield
    tinv = tinv + _mm(tinv, d2, pi, pi)
    d4 = _mm(d2, d2, pi, pi)
    yield
    tinv = tinv + _mm(tinv, d4, pi, pi)
    yield
    for lvl in range(1, len(blks)):
        off = jnp.where(blks[lvl] & jnp.logical_not(blks[lvl - 1]), lbd, 0.0)
        half = _mm(tinv, off, pi, pi)
        yield
        tinv = tinv + _mm(half, tinv, pi, pi)
        yield

    v_st = jnp.concatenate([v * m0, v * m1], axis=0)
    y_st = _mm(two_heads(a_st[:ts, ts:] * strict, a_st[ts:, ts:] * strict), v_st, ps, ps)
    yield
    wu = _mm(tinv, jnp.concatenate([at_st, y_st], axis=1), ps, ps)
    yield
    w =wu[:ts, :2 * HEAD] + wu[ts:, :2 * HEAD]
    u = wu[:ts, 2 * HEAD:] + wu[ts:, 2 * HEAD:]
    rhs = jnp.concatenate([jnp.concatenate([w, u], axis=1),
                           jnp.concatenate([jnp.zeros_like(v), v], axis=1)], axis=0)
    bend_t, kend_t = bend.T, kend.T
    lhs = [jnp.concatenate([bend_t * cm, kend_t * cm], axis=1) for cm in chunk_cols]
    mn_all = _mm(jnp.concatenate(lhs, axis=0), rhs, ps, ps)
    lhs_q = jnp.concatenate([jnp.concatenate([a_out[:ts, :ts] * incl, a_out[:ts, ts:] * incl], axis=1),
                             jnp.concatenate([a_out[ts:, :ts] * incl, a_out[ts:, ts:] * incl], axis=1)], axis=0)
    qo = _mm(lhs_q, rhs, 1, 1)
    yield
    q =rt + qo[:ts, :2 * HEAD] * m0 + qo[ts:, :2 * HEAD] * m1
    olocal = qo[:ts, 2 * HEAD:] * m0 + qo[ts:, 2 * HEAD:] * m1

    outs = []
    for ch in range(nchunk):
        mn = mn_all[ch * ts:(ch + 1) * ts]
        outs.append(_mm(q[ch * c:(ch + 1) * c], st, 1, 1) + olocal[ch * c:(ch + 1) * c])
        trans = eye * jnp.exp(ctot[ch * c:ch * c + 1, :]) + mn[:, :2 * HEAD] * headdiag
        st = _mm(trans, st, ps, ps) + mn[:, 2 * HEAD:] * headdiag
        yield
    return jnp.concatenate(outs, axis=0), st


def _round_robin(gens):
    results = [None] * len(gens)
    active = list(enumerate(gens))
    while active:
        still = []
        for i, gen in active:
            try:
                next(gen)
                still.append((i, gen))
            except StopIteration as stop:
                results[i] = stop.value
        active = still
    return results


def _wkv_seq_kernel(r_ref, lw_ref, k_ref, v_ref, kk_ref, a_ref, g_ref, rk_ref, gng_ref, gnb_ref,
                    y_ref, sfin_ref, st_ref, *, npp):
    ts, c = WKV_SLAB, WKV_CHUNK

    @pl.when(pl.program_id(2) == 0)
    def _():
        st_ref[...] = jnp.zeros_like(st_ref)

    ri, ci = _iota((ts, ts), 0), _iota((ts, ts), 1)
    same = ri // c == ci // c
    strict = (same & (ci < ri)).astype(F32)
    incl = (same & (ci <= ri)).astype(F32)
    cum_lhs = jnp.concatenate([incl, same.astype(F32)], axis=0).astype(BF16)
    cum_lhs = jnp.concatenate([cum_lhs] * 3, axis=1)
    lane = _iota((1, 2 * HEAD), 1)
    m0 = (lane < HEAD).astype(F32)
    m1 = 1.0 - m0
    headdiag = _seg_ones(2 * HEAD, HEAD)
    eye = (_iota((2 * HEAD, 2 * HEAD), 0) == _iota((2 * HEAD, 2 * HEAD), 1)).astype(F32)
    r2, c2 = _iota((2 * ts, 2 * ts), 0), _iota((2 * ts, 2 * ts), 1)
    eye2 = (r2 == c2).astype(F32)
    blks, b = [], 8
    while b <= c:
        blks.append(r2 // b == c2 // b)
        b *= 2
    tcol = _iota((1, ts), 1) // c
    chunk_cols = [(tcol == ch).astype(F32) for ch in range(ts // c)]
    cst = (strict, incl, cum_lhs, m0, m1, headdiag, eye, eye2, blks, chunk_cols)

    sls = [slice(p * 2 * HEAD, (p + 1) * 2 * HEAD) for p in range(npp)]
    res = _round_robin([_wkv_pair(r_ref[:, sl], lw_ref[:, sl], k_ref[:, sl], v_ref[:, sl], kk_ref[:, sl],
                                  a_ref[:, sl], st_ref[p], cst) for p, sl in enumerate(sls)])
    for p, sl in enumerate(sls):
        o, st = res[p]
        st_ref[p] = st
        y_ref[:, sl] = _gn_gate(o, r_ref[:, sl], k_ref[:, sl], v_ref[:, sl], g_ref[:, sl], rk_ref[:, sl],
                                gng_ref[:, sl], gnb_ref[:, sl], headdiag).astype(BF16)

    @pl.when(pl.program_id(2) == pl.num_programs(2) - 1)
    def _():
        sfin_ref[...] = st_ref[...]


def _wkv_seq(r, lw, k, v, kk, a, g, rk, gng, gnb, *, nb, seq_len):
    ns = seq_len // WKV_SLAB
    npp = WKV_PAIRS_PER_STEP
    width = 2 * HEAD * npp
    tok = pl.BlockSpec((WKV_SLAB, width), lambda b, p, s: (b * ns + s, p))
    par = pl.BlockSpec((1, width), lambda b, p, s: (0, p))
    return pl.pallas_call(
        functools.partial(_wkv_seq_kernel, npp=npp),
        grid=(nb, DB // width, ns),
        in_specs=[tok] * 7 + [par] * 3,
        out_specs=[tok, pl.BlockSpec((None, npp, 2 * HEAD, 2 * HEAD), lambda b, p, s: (b, p, 0, 0))],
        out_shape=[jax.ShapeDtypeStruct((nb * seq_len, DB), BF16),
                   jax.ShapeDtypeStruct((nb, DB // (2 * HEAD), 2 * HEAD, 2 * HEAD), F32)],
        scratch_shapes=[pltpu.VMEM((npp, 2 * HEAD, 2 * HEAD), F32)],
        compiler_params=_cparams(("arbitrary", "arbitrary", "arbitrary")),
        name="wkv_seq",
    )(r, lw, k, v, kk, a, g, rk, gng, gnb)


def _wkv_step_kernel(s_ref, r_ref, lw_ref, k_ref, v_ref, kk_ref, a_ref, g_ref, rk_ref, gng_ref, gnb_ref,
                     y_ref, so_ref):
    s = s_ref[...]
    r, k, v, kk, a = r_ref[...], k_ref[...], v_ref[...], kk_ref[...], a_ref[...]
    w = jnp.exp(lw_ref[...])
    eye = (_iota((1, HEAD, HEAD), 1) == _iota((1, HEAD, HEAD), 2)).astype(F32)
    v_col = jnp.sum(eye * v, axis=2, keepdims=True)
    s_kk = jnp.sum(s * kk, axis=2, keepdims=True)
    s = s * w - s_kk * (kk * a) + v_col * k
    so_ref[...] = s
    o_col = jnp.sum(s * r, axis=2, keepdims=True)
    o = jnp.sum(eye * o_col, axis=1, keepdims=True)
    mean = jnp.mean(o, axis=2, keepdims=True)
    d = o - mean
    var = jnp.mean(d * d, axis=2, keepdims=True)
    out = d * lax.rsqrt(var + GN_EPS) * gng_ref[...] + gnb_ref[...]
    out = out + jnp.sum(r * k * rk_ref[...], axis=2, keepdims=True) * v
    y_ref[...] = out * g_ref[...]


def _wkv_step(state, vecs, params, *, nheads):
    n = state.shape[0]
    tb = 8 * nheads
    sspec = pl.BlockSpec((tb, HEAD, HEAD), lambda i: (i, 0, 0))
    vspec = pl.BlockSpec((tb, 1, HEAD), lambda i: (i, 0, 0))
    pspec = pl.BlockSpec((tb, 1, HEAD), lambda i: (0, 0, 0))
    reps = tb // nheads
    params = [jnp.tile(p, (reps, 1, 1)) for p in params]
    return pl.pallas_call(
        _wkv_step_kernel,
        grid=(n // tb,),
        in_specs=[sspec] + [vspec] * 7 + [pspec] * 3,
        out_specs=[vspec, sspec],
        out_shape=[jax.ShapeDtypeStruct((n, 1, HEAD), F32), jax.ShapeDtypeStruct((n, HEAD, HEAD), F32)],
        compiler_params=_cparams(("arbitrary",)),
        name="wkv_step",
    )(state, *vecs, *params)


def _ab_post_kernel(ya_ref, yb_ref, x_ref, mod_ref, ng_ref, wo_ref, wrt_ref, br_ref,
                    h2in_ref, combin_ref, xo_ref, h2_ref, comb_ref):
    del h2in_ref, combin_ref
    y = _dot(ya_ref[...], wo_ref[0:DA, :]) + _dot(yb_ref[...], wo_ref[DA:, :])
    _post(y, x_ref[...], mod_ref, ng_ref, wrt_ref, br_ref, xo_ref, h2_ref, comb_ref)


def _ab_post(ya, yb, x, mods, ng, wo, wrt, br, h2_all, comb_all, *, nb, seq_len, tm, row0):
    nt = seq_len // tm
    t0 = row0 // tm
    full = lambda shape: pl.BlockSpec(shape, lambda b, t: tuple(0 for _ in shape))
    tok = lambda w: pl.BlockSpec((tm, w), lambda b, t: (b * nt + t, 0))
    tok_all = lambda w: pl.BlockSpec((tm, w), lambda b, t: (t0 + b * nt + t, 0))
    anyspec = pl.BlockSpec(memory_space=pl.ANY)
    return pl.pallas_call(
        _ab_post_kernel,
        grid=(nb, nt),
        in_specs=[tok(DA), tok(DB), tok(D),
                  pl.BlockSpec((None,) + mods.shape[1:], lambda b, t: (b, 0, 0, 0)),
                  full(ng.shape), full(wo.shape), full(wrt.shape), full(br.shape), anyspec, anyspec],
        out_specs=[tok(D), tok_all(D), tok_all(128)],
        out_shape=[jax.ShapeDtypeStruct(x.shape, F32), jax.ShapeDtypeStruct(h2_all.shape, BF16),
                   jax.ShapeDtypeStruct(comb_all.shape, F32)],
        input_output_aliases={8: 1, 9: 2},
        compiler_params=_cparams(("arbitrary", "arbitrary")),
        name="ab_post",
    )(ya, yb, x, mods, ng, wo, wrt, br, h2_all, comb_all)


def _gmlp_kernel(*refs, seq_mode, tm):
    it = iter(refs)
    x_ref, f_ref, pmod_ref, png_ref, mod_ref, ng_ref = (next(it) for _ in range(6))
    win_ref, bin_ref, lng_ref, lnb_ref, ws_ref, bs_ref, wo_ref, wrt_ref, br_ref = (next(it) for _ in range(9))
    next(it), next(it)
    xo_ref, h2_ref, comb_ref = next(it), next(it), next(it)
    if not seq_mode:
        cv_ref = next(it)
    x = _residual_in(x_ref, f_ref, pmod_ref, png_ref)
    h = _rms(x, ng_ref[0:1, :]) * (1.0 + mod_ref[1]) + mod_ref[0]
    z = jax.nn.gelu(_dot(h.astype(BF16), win_ref[...]) + bin_ref[...])
    u, v = z[:, :D], z[:, D:]
    mu = jnp.mean(v, axis=-1, keepdims=True)
    var = jnp.mean(jnp.square(v - mu), axis=-1, keepdims=True)
    v = (v - mu) * lax.rsqrt(var + EPS) * lng_ref[...] + lnb_ref[...]
    if seq_mode:
        vb = v.astype(BF16)
        causal = _iota((CHUNK, CHUNK), 1) <= _iota((CHUNK, CHUNK), 0)
        cols = []
        for g in range(D // CHUNK):
            wsg = jnp.where(causal, ws_ref[g], 0.0).astype(BF16)
            bsg = bs_ref[:, g:g + 1]
            rows = [_dot(wsg, vb[c * CHUNK:(c + 1) * CHUNK, g * CHUNK:(g + 1) * CHUNK]) + bsg
                    for c in range(tm // CHUNK)]
            cols.append(jnp.concatenate(rows, axis=0) if len(rows) > 1 else rows[0])
        s = jnp.concatenate(cols, axis=1)
    else:
        cv_ref[...] = v
        s = v * ws_ref[...] + bs_ref[...]
    y = _dot((u * s).astype(BF16), wo_ref[...])
    _post(y, x, mod_ref, ng_ref, wrt_ref, br_ref, xo_ref, h2_ref, comb_ref)


def _gmlp(x, f_all, pmods, png, mods, ng, weights, h2_all, comb_all, *, nb, seq_len, tm, row0, seq_mode):
    nt = seq_len // tm
    t0 = row0 // tm
    full = lambda shape: pl.BlockSpec(shape, lambda b, t: tuple(0 for _ in shape))
    tok = lambda w: pl.BlockSpec((tm, w), lambda b, t: (b * nt + t, 0))
    tok_all = lambda w: pl.BlockSpec((tm, w), lambda b, t: (t0 + b * nt + t, 0))
    modspec = lambda m: pl.BlockSpec((None,) + m.shape[1:], lambda b, t: (b, 0, 0, 0))
    anyspec = pl.BlockSpec(memory_space=pl.ANY)
    in_specs = ([tok(D), tok_all(D), modspec(pmods), full(png.shape), modspec(mods), full(ng.shape)]
                + [full(w.shape) for w in weights] + [anyspec, anyspec])
    out_specs = [tok(D), tok_all(D), tok_all(128)]
    out_shape = [jax.ShapeDtypeStruct(x.shape, F32), jax.ShapeDtypeStruct(h2_all.shape, BF16),
                 jax.ShapeDtypeStruct(comb_all.shape, F32)]
    if not seq_mode:
        out_specs.append(tok(D))
        out_shape.append(jax.ShapeDtypeStruct(x.shape, F32))
    n_in = len(in_specs)
    return pl.pallas_call(
        functools.partial(_gmlp_kernel, seq_mode=seq_mode, tm=tm),
        grid=(nb, nt), in_specs=in_specs, out_specs=out_specs, out_shape=out_shape,
        input_output_aliases={n_in - 2: 1, n_in - 1: 2},
        compiler_params=_cparams(("arbitrary", "arbitrary")),
        name="gmlp_seq" if seq_mode else "gmlp_step",
    )(x, f_all, pmods, png, mods, ng, *weights, h2_all, comb_all)


def _moe_kernel(h_ref, comb_ref, wg_ref, wu_ref, wd_ref, sg_ref, su_ref, sd_ref, o_ref):
    e = pl.program_id(1)
    h = h_ref[...]

    @pl.when(e == 0)
    def _():
        hs = _silu(_dot(h, sg_ref[...].astype(BF16))) * _dot(h, su_ref[...].astype(BF16))
        o_ref[...] = _dot(hs.astype(BF16), sd_ref[...].astype(BF16))

    comb = comb_ref[...]
    c_e = jnp.sum(jnp.where(_iota(comb.shape, 1) == e, comb, 0.0), axis=1, keepdims=True)
    hw = _silu(_dot(h, wg_ref[...].astype(BF16))) * _dot(h, wu_ref[...].astype(BF16)) * c_e
    o_ref[...] += _dot(hw.astype(BF16), wd_ref[...].astype(BF16))


def _moe(h_all, comb_all, wg, wu, wd, sg, su, sd, *, layer, tm):
    rows = h_all.shape[0]
    shared = lambda w: pl.BlockSpec((None,) + w.shape[1:], lambda i, e: (layer, 0, 0))
    return pl.pallas_call(
        _moe_kernel,
        grid=(rows // tm, NE),
        in_specs=[pl.BlockSpec((tm, D), lambda i, e: (i, 0)), pl.BlockSpec((tm, 128), lambda i, e: (i, 0)),
                  pl.BlockSpec((None, None, D, DE), lambda i, e: (layer, e, 0, 0)),
                  pl.BlockSpec((None, None, D, DE), lambda i, e: (layer, e, 0, 0)),
                  pl.BlockSpec((None, None, DE, D), lambda i, e: (layer, e, 0, 0)),
                  shared(sg), shared(su), shared(sd)],
        out_specs=pl.BlockSpec((tm, D), lambda i, e: (i, 0)),
        out_shape=jax.ShapeDtypeStruct((rows, D), F32),
        compiler_params=_cparams(("arbitrary", "arbitrary")),
        name="moe",
    )(h_all, comb_all, wg, wu, wd, sg, su, sd)


def _final_kernel(x_ref, f_ref, pmod_ref, png_ref, o_ref):
    o_ref[...] = _residual_in(x_ref, f_ref, pmod_ref, png_ref)


def _final(x, f_all, pmods, png, *, nb, seq_len, tm, row0):
    nt = seq_len // tm
    t0 = row0 // tm
    return pl.pallas_call(
        _final_kernel,
        grid=(nb, nt),
        in_specs=[pl.BlockSpec((tm, D), lambda b, t: (b * nt + t, 0)),
                  pl.BlockSpec((tm, D), lambda b, t: (t0 + b * nt + t, 0)),
                  pl.BlockSpec((None,) + pmods.shape[1:], lambda b, t: (b, 0, 0, 0)),
                  pl.BlockSpec(png.shape, lambda b, t: (0, 0))],
        out_specs=pl.BlockSpec((tm, D), lambda b, t: (b * nt + t, 0)),
        out_shape=jax.ShapeDtypeStruct(x.shape, F32),
        compiler_params=_cparams(("arbitrary", "arbitrary")),
        name="final_residual",
    )(x, f_all, pmods, png)


def kernel(x_prompt, x_sample, c_prompt, c_sample, state_conv, state_shift, state_wkv, w_mod, b_mod, norm_g, ab_w_in, ab_conv_w, ab_mu, ab_w0, ab_w_decay_up, ab_a0, ab_a_up, ab_g_up, ab_k_k, ab_k_a, ab_r_k, ab_gn_g, ab_gn_b, ab_w_out, gm_w_in, gm_b_in, gm_ln_g, gm_ln_b, gm_w_s, gm_b_s, gm_w_out, moe_w_router, moe_b_router, moe_w_gate, moe_w_up, moe_w_down, moe_ws_gate, moe_ws_up, moe_ws_down):
    bsz, seq_len, _ = x_prompt.shape
    nsamp = x_sample.shape[0]
    nheads = DB // HEAD
    n_prompt = bsz * seq_len
    tm_moe = 1664
    n_all = -(-(n_prompt + nsamp) // tm_moe) * tm_moe
    tm = 512

    xp = x_prompt.reshape(n_prompt, D)
    xs = x_sample.reshape(nsamp, D)

    m = _modulation(jnp.concatenate([c_prompt, c_sample], axis=0), w_mod, b_mod)
    mods_p = [m[l, :bsz].reshape(bsz, 6, 1, D) for l in range(2)]
    mods_s = [m[l, bsz:].reshape(nsamp, 6, D).transpose(1, 0, 2)[None] for l in range(2)]

    row = lambda t: t.reshape(1, -1)
    pad_rows = lambda t, lo, hi: jnp.pad(t, ((lo, hi), (0, 0)))
    small = [ab_conv_w[0], row(ab_mu[0]), row(ab_w0[0]),
             pad_rows(ab_w_decay_up[0], 0, 64).astype(BF16), row(ab_a0[0]),
             pad_rows(ab_a_up[0], 64, 0).astype(BF16), ab_g_up[0].astype(BF16),
             row(ab_k_k[0]), row(ab_k_a[0])]
    win0 = ab_w_in[0].astype(BF16)
    wo0 = ab_w_out[0].astype(BF16)
    rk, gng, gnb = row(ab_r_k[0]), row(ab_gn_g[0]), row(ab_gn_b[0])
    wrt = [moe_w_router[l].T for l in range(2)]
    br = [moe_b_router[l].reshape(NE, 1) for l in range(2)]

    h2_all = jnp.zeros((n_all, D), BF16)
    comb_all = jnp.zeros((n_all, 128), F32)

    ya, r, lw, k, v, kk, a, g, ctail, stail = _ab_pre(
        xp, mods_p[0], norm_g[0], win0, small, None, nb=bsz, seq_len=seq_len, tm=tm, seq_mode=True)
    conv_p = ctail.reshape(bsz, 8, DA)[:, 6:8][None]
    shift_p = stail.reshape(bsz, 8, PB)[:, 7][None]
    yb, sfin = _wkv_seq(r, lw, k, v, kk, a, g, rk, gng, gnb, nb=bsz, seq_len=seq_len)
    sfin = sfin.reshape(bsz, nheads // 2, 2, HEAD, 2, HEAD)
    wkv_p = jnp.stack([sfin[:, :, 0, :, 0, :], sfin[:, :, 1, :, 1, :]], axis=2)
    wkv_p = wkv_p.reshape(bsz, nheads, HEAD, HEAD).transpose(0, 1, 3, 2)[None]
    xp1, h2_all, comb_all = _ab_post(ya, yb, xp, mods_p[0], norm_g[0], wo0, wrt[0], br[0], h2_all, comb_all,
                                     nb=bsz, seq_len=seq_len, tm=tm, row0=0)

    prev = [state_conv[0, :, 0], state_conv[0, :, 1], state_shift[0]]
    ya_s, r, lw, k, v, kk, a, g, gated_s, pb_s = _ab_pre(
        xs, mods_s[0], norm_g[0], win0, small, prev, nb=1, seq_len=nsamp, tm=nsamp, seq_mode=False)
    conv_s = jnp.stack([state_conv[0, :, 1], gated_s], axis=1)[None]
    shift_s = pb_s[None]
    vecs = [t.reshape(nsamp * nheads, 1, HEAD) for t in (r, lw, k, v, kk, a, g)]
    params = [t.reshape(nheads, 1, HEAD) for t in (ab_r_k[0], ab_gn_g[0], ab_gn_b[0])]
    yb_s, snew = _wkv_step(state_wkv[0].reshape(nsamp * nheads, HEAD, HEAD), vecs, params, nheads=nheads)
    wkv_s = snew.reshape(1, nsamp, nheads, HEAD, HEAD)
    xs1, h2_all, comb_all = _ab_post(ya_s, yb_s.reshape(nsamp, DB).astype(BF16), xs, mods_s[0], norm_g[0], wo0,
                                     wrt[0], br[0], h2_all, comb_all, nb=1, seq_len=nsamp, tm=nsamp, row0=n_prompt)

    def moe_layer(l, h_all, c_all):
        return _moe(h_all, c_all, moe_w_gate, moe_w_up, moe_w_down,
                    moe_ws_gate, moe_ws_up, moe_ws_down, layer=l, tm=tm_moe)

    f0 = moe_layer(0, h2_all, comb_all)

    gw = [gm_w_in[0].astype(BF16), row(gm_b_in[0]), row(gm_ln_g[0]), row(gm_ln_b[0])]
    gw_seq = gw + [gm_w_s[0], gm_b_s[0].T, gm_w_out[0].astype(BF16), wrt[1], br[1]]
    gw_step = gw + [row(jnp.repeat(gm_w_s[0, :, 0, 0], CHUNK)), row(jnp.repeat(gm_b_s[0, :, 0], CHUNK)),
                    gm_w_out[0].astype(BF16), wrt[1], br[1]]
    tm1 = 256
    xp2, h2_all, comb_all = _gmlp(xp1, f0, mods_p[0], norm_g[0], mods_p[1], norm_g[1], gw_seq, h2_all, comb_all,
                                  nb=bsz, seq_len=seq_len, tm=tm1, row0=0, seq_mode=True)
    xs2, h2_all, comb_all, chunk_v = _gmlp(xs1, f0, mods_s[0], norm_g[0], mods_s[1], norm_g[1], gw_step, h2_all,
                                           comb_all, nb=1, seq_len=nsamp, tm=nsamp, row0=n_prompt, seq_mode=False)
    f1 = moe_layer(1, h2_all, comb_all)
    y_p = _final(xp2, f1, mods_p[1], norm_g[1], nb=bsz, seq_len=seq_len, tm=tm, row0=0)
    y_s = _final(xs2, f1, mods_s[1], norm_g[1], nb=1, seq_len=nsamp, tm=nsamp, row0=n_prompt)

    return (y_p.reshape(bsz, seq_len, D), y_s.reshape(nsamp, 1, D), conv_p, conv_s, shift_p, shift_s,
            wkv_p, wkv_s, chunk_v.reshape(1, nsamp, 1, D))
```

```python
import functools

import jax
import jax.numpy as jnp
from jax import lax
from jax.experimental import pallas as pl
from jax.experimental.pallas import tpu as pltpu

F32 = jnp.float32
BF16 = jnp.bfloat16
HI = lax.Precision.HIGHEST

D = 1024
DA = 512
DB = 512
HEAD = 64
PB = 1792
PAB = 3 * DA + PB
NE = 64
NG = 8
EPG = NE // NG
TOPG = 4
TOPK = 8
DE = 256
ROUTE_SCALE = 2.5
EPS = 1e-6
GN_EPS = 64e-5
CHUNK = 128
WKV_SLAB = 128
WKV_CHUNK = 32
WKV_PAIRS_PER_STEP = 4
WKV_P_STATE = 2
WKV_P_INV = 1
MOE_TD = 256
MOE_SEG = 16
MOE_LS = 3072
MOE_TR = 256
MOE_XW = D + 128
PLAN_ROWS = 128

VMEM_LIMIT = 56 * 1024 * 1024


def _cparams(sem):
    return pltpu.CompilerParams(dimension_semantics=sem, vmem_limit_bytes=VMEM_LIMIT)


def _dot(a, b, precision=None):
    return jnp.dot(a, b, preferred_element_type=F32, precision=precision)


def _dot_nt(a, b, precision=None):
    return lax.dot_general(a, b, (((1,), (1,)), ((), ())), preferred_element_type=F32, precision=precision)


def _iota(shape, dim):
    return lax.broadcasted_iota(jnp.int32, shape, dim)


def _rms(x, g):
    return x * lax.rsqrt(jnp.mean(x * x, axis=-1, keepdims=True) + EPS) * g


def _silu(x):
    return x * jax.nn.sigmoid(x)


def _seg_ones(n, seg):
    return (_iota((n, n), 0) // seg == _iota((n, n), 1) // seg).astype(F32)


def _mod_kernel(c_ref, w_ref, b_ref, o_ref):
    s = _silu(c_ref[...]).astype(BF16)
    o_ref[...] = _dot(s, w_ref[...].astype(BF16)) + b_ref[...]


def _modulation(c_all, w_mod, b_mod):
    depth, _, width = w_mod.shape
    rows = c_all.shape[0]
    bn = 512
    return pl.pallas_call(
        _mod_kernel,
        grid=(depth, width // bn),
        in_specs=[pl.BlockSpec((rows, D), lambda l, j: (0, 0)),
                  pl.BlockSpec((None, D, bn), lambda l, j: (l, 0, j)),
                  pl.BlockSpec((None, 1, bn), lambda l, j: (l, 0, j))],
        out_specs=pl.BlockSpec((None, rows, bn), lambda l, j: (l, 0, j)),
        out_shape=jax.ShapeDtypeStruct((depth, rows, width), F32),
        compiler_params=_cparams(("arbitrary", "arbitrary")),
        name="modulation",
    )(c_all, w_mod, b_mod.reshape(depth, 1, width))


def _residual_in(x_ref, f_ref, pmod_ref, png_ref):
    x = x_ref[...]
    if f_ref is None:
        return x
    return x + pmod_ref[5] * _rms(f_ref[...], png_ref[3:4, :])


def _route(h, wrt_ref, br_ref):
    tm = h.shape[0]
    logits = _dot_nt(wrt_ref[...], h, precision=HI)
    scores = jax.nn.sigmoid(logits)
    sel = scores + br_ref[...]
    sub = _iota((EPG, tm), 0).astype(F32)
    neg = jnp.float32(-jnp.inf)
    blks, sblks, gscore = [], [], []
    for g in range(NG):
        blk = sel[g * EPG:(g + 1) * EPG, :]
        m1 = jnp.max(blk, axis=0, keepdims=True)
        first = jnp.min(jnp.where(blk == m1, sub, float(EPG)), axis=0, keepdims=True)
        m2 = jnp.max(jnp.where(sub == first, neg, blk), axis=0, keepdims=True)
        blks.append(blk)
        sblks.append(scores[g * EPG:(g + 1) * EPG, :])
        gscore.append(m1 + m2)
    masked = []
    for g in range(NG):
        cnt = jnp.zeros((1, tm), F32)
        for g2 in range(NG):
            if g2 == g:
                continue
            beats = (gscore[g2] >= gscore[g]) if g2 < g else (gscore[g2] > gscore[g])
            cnt = cnt + jnp.where(beats, 1.0, 0.0)
        masked.append(jnp.where(cnt < TOPG, blks[g], neg))
    cnts = [jnp.zeros((EPG, tm), F32) for _ in range(NG)]
    for g2 in range(NG):
        for j in range(EPG):
            row = masked[g2][j:j + 1, :]
            for g in range(NG):
                ge = jnp.where(row >= masked[g], 1.0, 0.0)
                gt = jnp.where(row > masked[g], 1.0, 0.0)
                if g > g2:
                    cnts[g] = cnts[g] + ge
                elif g < g2:
                    cnts[g] = cnts[g] + gt
                else:
                    cnts[g] = cnts[g] + jnp.where(sub > j, ge, gt)
    wts = [jnp.where(cnts[g] < TOPK, sblks[g], 0.0) for g in range(NG)]
    tot = wts[0]
    for g in range(1, NG):
        tot = tot + wts[g]
    denom = jnp.sum(tot, axis=0, keepdims=True)
    comb_t = jnp.concatenate([w / denom * ROUTE_SCALE for w in wts] + [jnp.zeros((NE, tm), F32)], axis=0)
    return comb_t.T


def _post(y, x, mod_ref, ng_ref, wrt_ref, br_ref, xo_ref, h2_ref, comb_ref):
    xn = x + mod_ref[2] * _rms(y, ng_ref[1:2, :])
    xo_ref[...] = xn
    h2 = _rms(xn, ng_ref[2:3, :]) * (1.0 + mod_ref[4]) + mod_ref[3]
    h2_ref[...] = h2.astype(BF16)
    comb_ref[...] = _route(h2, wrt_ref, br_ref)


def _ab_pre_kernel(*refs, seq_mode, tm):
    it = iter(refs)
    x_ref, mod_ref, ng_ref, win_ref = next(it), next(it), next(it), next(it)
    cw_ref, mu_ref, w0_ref, wd_ref, a0_ref, wa_ref, wg_ref, kk_ref, ka_ref = (next(it) for _ in range(9))
    if not seq_mode:
        p2_ref, p1_ref, ps_ref = next(it), next(it), next(it)
    ya_ref, r_ref, lw_ref, k_ref, v_ref, kkn_ref, a_ref, g_ref, ctail_ref, stail_ref = (next(it) for _ in range(10))
    if seq_mode:
        cc_ref, sc_ref = next(it), next(it)

        @pl.when(pl.program_id(1) == 0)
        def _():
            cc_ref[...] = jnp.zeros_like(cc_ref)
            sc_ref[...] = jnp.zeros_like(sc_ref)

    x = x_ref[...]
    h = _rms(x, ng_ref[0:1, :]) * (1.0 + mod_ref[1]) + mod_ref[0]
    proj = _dot(h.astype(BF16), win_ref[...])
    a_h, a_b, a_c = proj[:, 0:DA], proj[:, DA:2 * DA], proj[:, 2 * DA:3 * DA]
    pb = proj[:, 3 * DA:]
    gated = a_c * a_h
    if seq_mode:
        rows = _iota((tm, 1), 0)
        g1 = jnp.where(rows == 0, cc_ref[7:8, :], pltpu.roll(gated, 1, 0))
        g2 = jnp.where(rows == 0, cc_ref[6:7, :], jnp.where(rows == 1, cc_ref[7:8, :], pltpu.roll(gated, 2, 0)))
        pprev = jnp.where(rows == 0, sc_ref[7:8, :], pltpu.roll(pb, 1, 0))
        cc_ref[...] = gated[tm - 8:, :]
        sc_ref[...] = pb[tm - 8:, :]
        ctail_ref[...] = gated[tm - 8:, :]
        stail_ref[...] = pb[tm - 8:, :]
    else:
        g1, g2, pprev = p1_ref[...], p2_ref[...], ps_ref[...]
        ctail_ref[...] = gated
        stail_ref[...] = pb
    conv = g2 * cw_ref[0:1, :] + g1 * cw_ref[1:2, :] + gated * cw_ref[2:3, :]
    ya_ref[...] = (a_b * conv).astype(BF16)

    xm = pb + (pprev - pb) * mu_ref[...]
    r, k, v = xm[:, 0:DB], xm[:, DB:2 * DB], xm[:, 2 * DB:3 * DB]
    lowrank = xm[:, 3 * DB:3 * DB + 128]
    dg = xm[:, 3 * DB + 128:]
    w_pre = w0_ref[...] + _dot(jnp.tanh(lowrank).astype(BF16), wd_ref[...])
    z = -w_pre
    softplus = jnp.maximum(z, 0.0) + jnp.log(1.0 + jnp.exp(-jnp.abs(z)))
    lw_ref[...] = -jnp.exp(-softplus - 0.5)
    a = jax.nn.sigmoid(a0_ref[...] + _dot(lowrank.astype(BF16), wa_ref[...]))
    g_ref[...] = _dot(jax.nn.sigmoid(dg).astype(BF16), wg_ref[...])
    kk = k * kk_ref[...]
    ss = _dot(kk * kk, _seg_ones(DB, HEAD), precision=HI)
    kkn_ref[...] = kk / jnp.maximum(jnp.sqrt(ss), 1e-12)
    r_ref[...] = r
    k_ref[...] = k * (1.0 + (a - 1.0) * ka_ref[...])
    v_ref[...] = v
    a_ref[...] = a


def _ab_pre(x, mods, ng, win, small, prev, *, nb, seq_len, tm, seq_mode):
    nt = seq_len // tm
    rows = nb * seq_len
    tail = 8 if seq_mode else tm
    full = lambda shape: pl.BlockSpec(shape, lambda b, t: tuple(0 for _ in shape))
    tok = lambda w: pl.BlockSpec((tm, w), lambda b, t: (b * nt + t, 0))
    in_specs = [tok(D),
                pl.BlockSpec((None,) + mods.shape[1:], lambda b, t: (b, 0, 0, 0)),
                full(ng.shape), full(win.shape)] + [full(s.shape) for s in small]
    args = [x, mods, ng, win] + list(small)
    if not seq_mode:
        in_specs += [tok(DA), tok(DA), tok(PB)]
        args += list(prev)
    out_shape = ([jax.ShapeDtypeStruct((rows, DA), BF16)] + [jax.ShapeDtypeStruct((rows, DB), F32)] * 7
                 + [jax.ShapeDtypeStruct((nb * tail, DA), F32), jax.ShapeDtypeStruct((nb * tail, PB), F32)])
    out_specs = ([tok(DA)] + [tok(DB)] * 7
                 + [pl.BlockSpec((tail, DA), lambda b, t: (b, 0)), pl.BlockSpec((tail, PB), lambda b, t: (b, 0))])
    scratch = [pltpu.VMEM((8, DA), F32), pltpu.VMEM((8, PB), F32)] if seq_mode else []
    return pl.pallas_call(
        functools.partial(_ab_pre_kernel, seq_mode=seq_mode, tm=tm),
        grid=(nb, nt), in_specs=in_specs, out_specs=out_specs, out_shape=out_shape,
        scratch_shapes=scratch,
        compiler_params=_cparams(("arbitrary", "arbitrary")),
        name="ab_pre_seq" if seq_mode else "ab_pre_step",
    )(*args)


def _split(x):
    hi = x.astype(BF16)
    return hi, (x - hi.astype(F32)).astype(BF16)


def _mm(a, b, pa=1, pb=1, nt=False):
    ah, al = _split(a) if pa == 2 else (a.astype(BF16), None)
    bh, bl = _split(b) if pb == 2 else (b.astype(BF16), None)
    bx = 1 if nt else 0
    if pa == 2 and pb == 2:
        ah, bh = jnp.concatenate([ah, al, ah], axis=1), jnp.concatenate([bh, bh, bl], axis=bx)
    elif pa == 2:
        ah, bh = jnp.concatenate([ah, al], axis=1), jnp.concatenate([bh, bh], axis=bx)
    elif pb == 2:
        ah, bh = jnp.concatenate([ah, ah], axis=1), jnp.concatenate([bh, bl], axis=bx)
    return _dot_nt(ah, bh) if nt else _dot(ah, bh)


def _gn_gate(o, r, k, v, g, rk, gng, gnb, seg):
    mean = _mm(o, seg, 2, 1) * (1.0 / HEAD)
    d = o - mean
    var = _mm(d * d, seg, 2, 1) * (1.0 / HEAD)
    out = d * lax.rsqrt(var + GN_EPS) * gng + gnb
    out = out + _mm(r * k * rk, seg, 2, 1) * v
    return out * g


def _wkv_pair(r, lw, k, v, kk, a, st, cst):
    ts, c = WKV_SLAB, WKV_CHUNK
    nchunk = ts // c
    ps, pi = WKV_P_STATE, WKV_P_INV
    strict, incl, cum_lhs, m0, m1, headdiag, eye, eye2, blks, chunk_cols = cst

    l1 = lw.astype(BF16)
    rem = lw - l1.astype(F32)
    l2 = rem.astype(BF16)
    l3 = (rem - l2.astype(F32)).astype(BF16)
    cums = _dot(cum_lhs, jnp.concatenate([l1, l2, l3], axis=0))
    yield
    cl, ctot = cums[:ts], cums[ts:]
    e_neg = jnp.exp(-cl)
    e_end = jnp.exp(ctot - cl)
    at = -kk * jnp.exp(cl - lw)
    rt = r * jnp.exp(cl)
    beta = kk * a
    bk = jnp.concatenate([beta * e_neg, k * e_neg], axis=0)
    bend, kend = beta * e_end, k * e_end

    at_st = jnp.concatenate([at * m0, at * m1], axis=0)
    a_st = _mm(at_st, bk, ps, ps, nt=True)
    a_out = _mm(jnp.concatenate([rt * m0, rt * m1], axis=0), bk, 1, 1, nt=True)
    yield
    zero = jnp.zeros((ts, ts), F32)

    def two_heads(p0, p1):
        return jnp.concatenate([jnp.concatenate([p0, zero], axis=1), jnp.concatenate([zero, p1], axis=1)], axis=0)

    lbd = two_heads(a_st[:ts, :ts] * strict, a_st[ts:, :ts] * strict)
    d1 = jnp.where(blks[0], lbd, 0.0)
    tinv = eye2 + d1
    d2 = _mm(d1, d1, pi, pi)
    yield
    tinv = tinv + _mm(tinv, d2, pi, pi)
    d4 = _mm(d2, d2, pi, pi)
    yield
    tinv = tinv + _mm(tinv, d4, pi, pi)
    yield
    for lvl in range(1, len(blks)):
        off = jnp.where(blks[lvl] & jnp.logical_not(blks[lvl - 1]), lbd, 0.0)
        half = _mm(tinv, off, pi, pi)
        yield
        tinv = tinv + _mm(half, tinv, pi, pi)
        yield

    v_st = jnp.concatenate([v * m0, v * m1], axis=0)
    y_st = _mm(two_heads(a_st[:ts, ts:] * strict, a_st[ts:, ts:] * strict), v_st, ps, ps)
    yield
    wu = _mm(tinv, jnp.concatenate([at_st, y_st], axis=1), ps, ps)
    yield
    w =wu[:ts, :2 * HEAD] + wu[ts:, :2 * HEAD]
    u = wu[:ts, 2 * HEAD:] + wu[ts:, 2 * HEAD:]
    rhs = jnp.concatenate([jnp.concatenate([w, u], axis=1),
                           jnp.concatenate([jnp.zeros_like(v), v], axis=1)], axis=0)
    bend_t, kend_t = bend.T, kend.T
    lhs = [jnp.concatenate([bend_t * cm, kend_t * cm], axis=1) for cm in chunk_cols]
    mn_all = _mm(jnp.concatenate(lhs, axis=0), rhs, ps, ps)
    lhs_q = jnp.concatenate([jnp.concatenate([a_out[:ts, :ts] * incl, a_out[:ts, ts:] * incl], axis=1),
                             jnp.concatenate([a_out[ts:, :ts] * incl, a_out[ts:, ts:] * incl], axis=1)], axis=0)
    qo = _mm(lhs_q, rhs, 1, 1)
    yield
    q =rt + qo[:ts, :2 * HEAD] * m0 + qo[ts:, :2 * HEAD] * m1
    olocal = qo[:ts, 2 * HEAD:] * m0 + qo[ts:, 2 * HEAD:] * m1

    outs = []
    for ch in range(nchunk):
        mn = mn_all[ch * ts:(ch + 1) * ts]
        outs.append(_mm(q[ch * c:(ch + 1) * c], st, 1, 1) + olocal[ch * c:(ch + 1) * c])
        trans = eye * jnp.exp(ctot[ch * c:ch * c + 1, :]) + mn[:, :2 * HEAD] * headdiag
        st = _mm(trans, st, ps, ps) + mn[:, 2 * HEAD:] * headdiag
        yield
    return jnp.concatenate(outs, axis=0), st


def _round_robin(gens):
    results = [None] * len(gens)
    active = list(enumerate(gens))
    while active:
        still = []
        for i, gen in active:
            try:
                next(gen)
                still.append((i, gen))
            except StopIteration as stop:
                results[i] = stop.value
        active = still
    return results


def _wkv_seq_kernel(r_ref, lw_ref, k_ref, v_ref, kk_ref, a_ref, g_ref, rk_ref, gng_ref, gnb_ref,
                    y_ref, sfin_ref, st_ref, *, npp):
    ts, c = WKV_SLAB, WKV_CHUNK

    @pl.when(pl.program_id(2) == 0)
    def _():
        st_ref[...] = jnp.zeros_like(st_ref)

    ri, ci = _iota((ts, ts), 0), _iota((ts, ts), 1)
    same = ri // c == ci // c
    strict = (same & (ci < ri)).astype(F32)
    incl = (same & (ci <= ri)).astype(F32)
    cum_lhs = jnp.concatenate([incl, same.astype(F32)], axis=0).astype(BF16)
    cum_lhs = jnp.concatenate([cum_lhs] * 3, axis=1)
    lane = _iota((1, 2 * HEAD), 1)
    m0 = (lane < HEAD).astype(F32)
    m1 = 1.0 - m0
    headdiag = _seg_ones(2 * HEAD, HEAD)
    eye = (_iota((2 * HEAD, 2 * HEAD), 0) == _iota((2 * HEAD, 2 * HEAD), 1)).astype(F32)
    r2, c2 = _iota((2 * ts, 2 * ts), 0), _iota((2 * ts, 2 * ts), 1)
    eye2 = (r2 == c2).astype(F32)
    blks, b = [], 8
    while b <= c:
        blks.append(r2 // b == c2 // b)
        b *= 2
    tcol = _iota((1, ts), 1) // c
    chunk_cols = [(tcol == ch).astype(F32) for ch in range(ts // c)]
    cst = (strict, incl, cum_lhs, m0, m1, headdiag, eye, eye2, blks, chunk_cols)

    sls = [slice(p * 2 * HEAD, (p + 1) * 2 * HEAD) for p in range(npp)]
    res = _round_robin([_wkv_pair(r_ref[:, sl], lw_ref[:, sl], k_ref[:, sl], v_ref[:, sl], kk_ref[:, sl],
                                  a_ref[:, sl], st_ref[p], cst) for p, sl in enumerate(sls)])
    for p, sl in enumerate(sls):
        o, st = res[p]
        st_ref[p] = st
        y_ref[:, sl] = _gn_gate(o, r_ref[:, sl], k_ref[:, sl], v_ref[:, sl], g_ref[:, sl], rk_ref[:, sl],
                                gng_ref[:, sl], gnb_ref[:, sl], headdiag).astype(BF16)

    @pl.when(pl.program_id(2) == pl.num_programs(2) - 1)
    def _():
        sfin_ref[...] = st_ref[...]


def _wkv_seq(r, lw, k, v, kk, a, g, rk, gng, gnb, *, nb, seq_len):
    ns = seq_len // WKV_SLAB
    npp = WKV_PAIRS_PER_STEP
    width = 2 * HEAD * npp
    tok = pl.BlockSpec((WKV_SLAB, width), lambda b, p, s: (b * ns + s, p))
    par = pl.BlockSpec((1, width), lambda b, p, s: (0, p))
    return pl.pallas_call(
        functools.partial(_wkv_seq_kernel, npp=npp),
        grid=(nb, DB // width, ns),
        in_specs=[tok] * 7 + [par] * 3,
        out_specs=[tok, pl.BlockSpec((None, npp, 2 * HEAD, 2 * HEAD), lambda b, p, s: (b, p, 0, 0))],
        out_shape=[jax.ShapeDtypeStruct((nb * seq_len, DB), BF16),
                   jax.ShapeDtypeStruct((nb, DB // (2 * HEAD), 2 * HEAD, 2 * HEAD), F32)],
        scratch_shapes=[pltpu.VMEM((npp, 2 * HEAD, 2 * HEAD), F32)],
        compiler_params=_cparams(("arbitrary", "arbitrary", "arbitrary")),
        name="wkv_seq",
    )(r, lw, k, v, kk, a, g, rk, gng, gnb)


def _wkv_step_kernel(s_ref, r_ref, lw_ref, k_ref, v_ref, kk_ref, a_ref, g_ref, rk_ref, gng_ref, gnb_ref,
                     y_ref, so_ref):
    s = s_ref[...]
    r, k, v, kk, a = r_ref[...], k_ref[...], v_ref[...], kk_ref[...], a_ref[...]
    w = jnp.exp(lw_ref[...])
    eye = (_iota((1, HEAD, HEAD), 1) == _iota((1, HEAD, HEAD), 2)).astype(F32)
    v_col = jnp.sum(eye * v, axis=2, keepdims=True)
    s_kk = jnp.sum(s * kk, axis=2, keepdims=True)
    s = s * w - s_kk * (kk * a) + v_col * k
    so_ref[...] = s
    o_col = jnp.sum(s * r, axis=2, keepdims=True)
    o = jnp.sum(eye * o_col, axis=1, keepdims=True)
    mean = jnp.mean(o, axis=2, keepdims=True)
    d = o - mean
    var = jnp.mean(d * d, axis=2, keepdims=True)
    out = d * lax.rsqrt(var + GN_EPS) * gng_ref[...] + gnb_ref[...]
    out = out + jnp.sum(r * k * rk_ref[...], axis=2, keepdims=True) * v
    y_ref[...] = out * g_ref[...]


def _wkv_step(state, vecs, params, *, nheads):
    n = state.shape[0]
    tb = 8 * nheads
    sspec = pl.BlockSpec((tb, HEAD, HEAD), lambda i: (i, 0, 0))
    vspec = pl.BlockSpec((tb, 1, HEAD), lambda i: (i, 0, 0))
    pspec = pl.BlockSpec((tb, 1, HEAD), lambda i: (0, 0, 0))
    reps = tb // nheads
    params = [jnp.tile(p, (reps, 1, 1)) for p in params]
    return pl.pallas_call(
        _wkv_step_kernel,
        grid=(n // tb,),
        in_specs=[sspec] + [vspec] * 7 + [pspec] * 3,
        out_specs=[vspec, sspec],
        out_shape=[jax.ShapeDtypeStruct((n, 1, HEAD), F32), jax.ShapeDtypeStruct((n, HEAD, HEAD), F32)],
        compiler_params=_cparams(("arbitrary",)),
        name="wkv_step",
    )(state, *vecs, *params)


def _ab_post_kernel(ya_ref, yb_ref, x_ref, mod_ref, ng_ref, wo_ref, wrt_ref, br_ref,
                    h2in_ref, combin_ref, xo_ref, h2_ref, comb_ref):
    del h2in_ref, combin_ref
    y = _dot(ya_ref[...], wo_ref[0:DA, :]) + _dot(yb_ref[...], wo_ref[DA:, :])
    _post(y, x_ref[...], mod_ref, ng_ref, wrt_ref, br_ref, xo_ref, h2_ref, comb_ref)


def _ab_post(ya, yb, x, mods, ng, wo, wrt, br, h2_all, comb_all, *, nb, seq_len, tm, row0):
    nt = seq_len // tm
    t0 = row0 // tm
    full = lambda shape: pl.BlockSpec(shape, lambda b, t: tuple(0 for _ in shape))
    tok = lambda w: pl.BlockSpec((tm, w), lambda b, t: (b * nt + t, 0))
    tok_all = lambda w: pl.BlockSpec((tm, w), lambda b, t: (t0 + b * nt + t, 0))
    anyspec = pl.BlockSpec(memory_space=pl.ANY)
    return pl.pallas_call(
        _ab_post_kernel,
        grid=(nb, nt),
        in_specs=[tok(DA), tok(DB), tok(D),
                  pl.BlockSpec((None,) + mods.shape[1:], lambda b, t: (b, 0, 0, 0)),
                  full(ng.shape), full(wo.shape), full(wrt.shape), full(br.shape), anyspec, anyspec],
        out_specs=[tok(D), tok_all(D), tok_all(128)],
        out_shape=[jax.ShapeDtypeStruct(x.shape, F32), jax.ShapeDtypeStruct(h2_all.shape, BF16),
                   jax.ShapeDtypeStruct(comb_all.shape, F32)],
        input_output_aliases={8: 1, 9: 2},
        compiler_params=_cparams(("arbitrary", "arbitrary")),
        name="ab_post",
    )(ya, yb, x, mods, ng, wo, wrt, br, h2_all, comb_all)


def _gmlp_kernel(*refs, seq_mode, tm):
    it = iter(refs)
    x_ref, f_ref, pmod_ref, png_ref, mod_ref, ng_ref = (next(it) for _ in range(6))
    win_ref, bin_ref, lng_ref, lnb_ref, ws_ref, bs_ref, wo_ref, wrt_ref, br_ref = (next(it) for _ in range(9))
    next(it), next(it)
    xo_ref, h2_ref, comb_ref = next(it), next(it), next(it)
    if not seq_mode:
        cv_ref = next(it)
    x = _residual_in(x_ref, f_ref, pmod_ref, png_ref)
    h = _rms(x, ng_ref[0:1, :]) * (1.0 + mod_ref[1]) + mod_ref[0]
    z = jax.nn.gelu(_dot(h.astype(BF16), win_ref[...]) + bin_ref[...])
    u, v = z[:, :D], z[:, D:]
    mu = jnp.mean(v, axis=-1, keepdims=True)
    var = jnp.mean(jnp.square(v - mu), axis=-1, keepdims=True)
    v = (v - mu) * lax.rsqrt(var + EPS) * lng_ref[...] + lnb_ref[...]
    if seq_mode:
        vb = v.astype(BF16)
        causal = _iota((CHUNK, CHUNK), 1) <= _iota((CHUNK, CHUNK), 0)
        cols = []
        for g in range(D // CHUNK):
            wsg = jnp.where(causal, ws_ref[g], 0.0).astype(BF16)
            bsg = bs_ref[:, g:g + 1]
            rows = [_dot(wsg, vb[c * CHUNK:(c + 1) * CHUNK, g * CHUNK:(g + 1) * CHUNK]) + bsg
                    for c in range(tm // CHUNK)]
            cols.append(jnp.concatenate(rows, axis=0) if len(rows) > 1 else rows[0])
        s = jnp.concatenate(cols, axis=1)
    else:
        cv_ref[...] = v
        s = v * ws_ref[...] + bs_ref[...]
    y = _dot((u * s).astype(BF16), wo_ref[...])
    _post(y, x, mod_ref, ng_ref, wrt_ref, br_ref, xo_ref, h2_ref, comb_ref)


def _gmlp(x, f_all, pmods, png, mods, ng, weights, h2_all, comb_all, *, nb, seq_len, tm, row0, seq_mode):
    nt = seq_len // tm
    t0 = row0 // tm
    full = lambda shape: pl.BlockSpec(shape, lambda b, t: tuple(0 for _ in shape))
    tok = lambda w: pl.BlockSpec((tm, w), lambda b, t: (b * nt + t, 0))
    tok_all = lambda w: pl.BlockSpec((tm, w), lambda b, t: (t0 + b * nt + t, 0))
    modspec = lambda m: pl.BlockSpec((None,) + m.shape[1:], lambda b, t: (b, 0, 0, 0))
    anyspec = pl.BlockSpec(memory_space=pl.ANY)
    in_specs = ([tok(D), tok_all(D), modspec(pmods), full(png.shape), modspec(mods), full(ng.shape)]
                + [full(w.shape) for w in weights] + [anyspec, anyspec])
    out_specs = [tok(D), tok_all(D), tok_all(128)]
    out_shape = [jax.ShapeDtypeStruct(x.shape, F32), jax.ShapeDtypeStruct(h2_all.shape, BF16),
                 jax.ShapeDtypeStruct(comb_all.shape, F32)]
    if not seq_mode:
        out_specs.append(tok(D))
        out_shape.append(jax.ShapeDtypeStruct(x.shape, F32))
    n_in = len(in_specs)
    return pl.pallas_call(
        functools.partial(_gmlp_kernel, seq_mode=seq_mode, tm=tm),
        grid=(nb, nt), in_specs=in_specs, out_specs=out_specs, out_shape=out_shape,
        input_output_aliases={n_in - 2: 1, n_in - 1: 2},
        compiler_params=_cparams(("arbitrary", "arbitrary")),
        name="gmlp_seq" if seq_mode else "gmlp_step",
    )(x, f_all, pmods, png, mods, ng, *weights, h2_all, comb_all)


def _strict_lower(n):
    return _iota((n, n), 1) < _iota((n, n), 0)


def _col_of_row(row):
    return jnp.broadcast_to(row, (128, 128)).T[:, 0:1]


def _plan_kernel(comb_ref, dst_ref, lo_ref, n_ref, te_ref, misc_ref, lof_ref, cnt_ref, *, ntile, ntmax):
    i = pl.program_id(0)

    @pl.when(i == 0)
    def _():
        cnt_ref[...] = jnp.zeros_like(cnt_ref)

    sel = jnp.where(comb_ref[...] > 0.0, 1.0, 0.0)
    cnt_ref[pl.ds(i, 1), :] = jnp.sum(sel, axis=0, keepdims=True)

    @pl.when(i == ntile - 1)
    def _():
        c = cnt_ref[...]
        cpad = jnp.floor((c + (MOE_SEG - 1.0)) * (1.0 / MOE_SEG)) * MOE_SEG
        cb = cpad.astype(BF16)
        lower = jnp.where(_strict_lower(PLAN_ROWS), 1.0, 0.0).astype(BF16)
        upper = jnp.where(_iota((128, 128), 0) < _iota((128, 128), 1), 1.0, 0.0).astype(BF16)
        before = _dot(lower, cb)
        lo = _dot(cb, upper)
        tot = jnp.sum(cpad, axis=0, keepdims=True)
        rt = jnp.floor((tot + (MOE_TR - 1.0)) * (1.0 / MOE_TR))
        gt = _dot(jnp.broadcast_to(rt, (8, 128)).astype(BF16), upper)[0:1, :]
        dst_ref[...] = (gt * MOE_TR + before).astype(jnp.int32)
        lo_ref[...] = lo.astype(jnp.int32)
        lof_ref[...] = lo
        n_ref[...] = cpad.astype(jnp.int32)
        ends_col = _col_of_row(gt + rt)
        jrow = _iota((1, ntmax), 1).astype(F32)
        te = jnp.sum(jnp.where(ends_col <= jrow, 1.0, 0.0), axis=0, keepdims=True)
        te_ref[...] = jnp.broadcast_to(jnp.minimum(te, NE - 1.0), (8, ntmax)).astype(jnp.int32)
        nused = jnp.sum(rt, axis=1, keepdims=True)
        npieces = jnp.sum(cpad, axis=1, keepdims=True) * (1.0 / MOE_SEG)
        rowi = _iota((8, 128), 0)
        misc = jnp.where(rowi == 0, gt * MOE_TR + tot, jnp.where(rowi == 1, rt * MOE_TR - tot, nused))
        misc_ref[0:8, :] = misc.astype(jnp.int32)
        misc_ref[8:, :] = jnp.broadcast_to(npieces, (PLAN_ROWS, 128)).astype(jnp.int32)


def _moe_plan(comb_all, *, ntmax):
    ntile = comb_all.shape[0] // MOE_TD
    full = lambda shape: pl.BlockSpec(shape, lambda i: tuple(0 for _ in shape))
    shapes = [((PLAN_ROWS, 128), jnp.int32)] * 3 + [((8, ntmax), jnp.int32), ((8 + PLAN_ROWS, 128), jnp.int32),
                                                   ((PLAN_ROWS, 128), F32)]
    return pl.pallas_call(
        functools.partial(_plan_kernel, ntile=ntile, ntmax=ntmax),
        grid=(ntile,),
        in_specs=[pl.BlockSpec((MOE_TD, 128), lambda i: (i, 0))],
        out_specs=[full(s) for s, _ in shapes],
        out_shape=[jax.ShapeDtypeStruct(s, d) for s, d in shapes],
        scratch_shapes=[pltpu.VMEM((PLAN_ROWS, 128), F32)],
        compiler_params=_cparams(("arbitrary",)),
        name="moe_plan",
    )(comb_all)


def _rank_plus(comb):
    sel = jnp.where(comb > 0.0, 1.0, 0.0)
    lower = jnp.where(_strict_lower(MOE_TD), 1.0, 0.0).astype(BF16)
    return sel * (_dot(lower, sel.astype(BF16)) + 1.0)


def _dispatch_kernel(dst_s, lo_s, n_s, misc_s, h_ref, comb_ref, lof_ref, nf_ref, xs_hbm, buf, zbuf, sem, tsem,
                     *, ntile):
    i = pl.program_id(0)
    slot = lax.rem(i, 2)
    comb = comb_ref[...]
    lo_row = lof_ref[...]
    hi_row = lo_row + nf_ref[...]
    srow = _iota((MOE_LS, 1), 0).astype(F32)
    e_t = jnp.where((srow >= lo_row) & (srow < hi_row), 1.0, 0.0)
    e_tb = e_t.astype(BF16)
    g_t = _dot(e_tb, _rank_plus(comb).T.astype(BF16))
    lo16 = _col_of_row(lo_row * (1.0 / MOE_SEG))
    lo_of_s = _dot(e_tb, jnp.broadcast_to(lo16, (128, 128)).astype(BF16)) * MOE_SEG
    target = srow + 1.0 - lo_of_s
    target = jnp.concatenate([target] * (MOE_TD // 128), axis=1)
    perm = jnp.where(g_t == target, 1.0, 0.0).astype(BF16)
    h = h_ref[...]
    for nb in range(D // 256):
        buf[slot, :, nb * 256:(nb + 1) * 256] = _dot(perm, h[:, nb * 256:(nb + 1) * 256]).astype(BF16)
    c_hi, c_lo = _split(comb)
    z = (_dot(perm, c_hi) + _dot(perm, c_lo)) * e_t
    z_hi, z_lo = _split(z)
    ones = jnp.ones((128, 128), BF16)
    w_hi, w_lo = _split(_dot(z_hi, ones) + _dot(z_lo, ones))
    buf[slot, :, D:] = jnp.where(_iota((1, 128), 1) < 64, w_hi, w_lo)

    def piece_copy(sl, src_row, dst_row):
        return pltpu.make_async_copy(buf.at[sl, pl.ds(pl.multiple_of(src_row, MOE_SEG), MOE_SEG)],
                                     xs_hbm.at[pl.ds(pl.multiple_of(dst_row, MOE_SEG), MOE_SEG)], sem.at[sl])

    def start_seg(e, carry):
        idx = i * 128 + e
        lo, dst = lo_s[idx], dst_s[idx]

        def piece(p, c2):
            piece_copy(slot, lo + p * MOE_SEG, dst + p * MOE_SEG).start()
            return c2

        lax.fori_loop(0, n_s[idx] // MOE_SEG, piece, 0)
        return carry

    lax.fori_loop(0, NE, start_seg, 0)

    def wait_tile(tile, sl):
        def wait_one(p, c2):
            piece_copy(sl, 0, 0).wait()
            return c2

        lax.fori_loop(0, misc_s[(8 + tile) * 128], wait_one, 0)

    @pl.when(i > 0)
    def _():
        wait_tile(i - 1, 1 - slot)

    @pl.when(i == ntile - 1)
    def _():
        wait_tile(i, slot)
        zbuf[...] = jnp.zeros_like(zbuf)

        def tail(e, carry):
            dst = misc_s[e]

            def piece(p, c2):
                cp = pltpu.make_async_copy(
                    zbuf, xs_hbm.at[pl.ds(pl.multiple_of(dst + p * MOE_SEG, MOE_SEG), MOE_SEG)], tsem)
                cp.start()
                cp.wait()
                return c2

            lax.fori_loop(0, misc_s[128 + e] // MOE_SEG, piece, 0)
            return carry

        lax.fori_loop(0, NE, tail, 0)


def _moe_dispatch(plan, h_all, comb_all, *, ntmax):
    dst, lo, n, _, misc, lof = plan
    ntile = h_all.shape[0] // MOE_TD
    grid_spec = pltpu.PrefetchScalarGridSpec(
        num_scalar_prefetch=4, grid=(ntile,),
        in_specs=[pl.BlockSpec((MOE_TD, D), lambda i, *_: (i, 0)),
                  pl.BlockSpec((MOE_TD, 128), lambda i, *_: (i, 0)),
                  pl.BlockSpec((None, 1, 128), lambda i, *_: (i, 0, 0)),
                  pl.BlockSpec((None, 1, 128), lambda i, *_: (i, 0, 0))],
        out_specs=pl.BlockSpec(memory_space=pl.ANY),
        scratch_shapes=[pltpu.VMEM((2, MOE_LS, MOE_XW), BF16), pltpu.VMEM((MOE_SEG, MOE_XW), BF16),
                        pltpu.SemaphoreType.DMA((2,)), pltpu.SemaphoreType.DMA(())])
    return pl.pallas_call(
        functools.partial(_dispatch_kernel, ntile=ntile),
        grid_spec=grid_spec,
        out_shape=jax.ShapeDtypeStruct((ntmax * MOE_TR, MOE_XW), BF16),
        compiler_params=_cparams(("arbitrary",)),
        name="moe_dispatch",
    )(dst.reshape(-1), lo.reshape(-1), n.reshape(-1), misc.reshape(-1), h_all, comb_all,
      lof.reshape(PLAN_ROWS, 1, 128), n.astype(F32).reshape(PLAN_ROWS, 1, 128))


def _ffn_kernel(te_s, misc_s, xs_ref, wg_ref, wu_ref, wd_ref, ys_ref, wgb, wub, wdb):
    j = pl.program_id(0)

    @pl.when(j < misc_s[2 * 128])
    def _():
        @pl.when((j == 0) | (te_s[j] != te_s[jnp.maximum(j - 1, 0)]))
        def _():
            wgb[...] = wg_ref[...].astype(BF16)
            wub[...] = wu_ref[...].astype(BF16)
            wdb[...] = wd_ref[...].astype(BF16)

        x = xs_ref[:, :D]
        w = xs_ref[:, D:D + 1].astype(F32) + xs_ref[:, D + 64:D + 65].astype(F32)
        hw = _silu(_dot(x, wgb[...])) * _dot(x, wub[...]) * w
        ys_ref[...] = _dot(hw.astype(BF16), wdb[...]).astype(BF16)


def _moe_ffn(plan, xs, wg, wu, wd, *, layer, ntmax):
    te, misc = plan[3], plan[4]

    def tile(j, misc_s):
        return jnp.minimum(j, misc_s[2 * 128] - 1)

    wspec = lambda shape: pl.BlockSpec((None, None) + shape,
                                       lambda j, te_s, misc_s: (layer, te_s[tile(j, misc_s)], 0, 0))
    grid_spec = pltpu.PrefetchScalarGridSpec(
        num_scalar_prefetch=2, grid=(ntmax,),
        in_specs=[pl.BlockSpec((MOE_TR, MOE_XW), lambda j, te_s, misc_s: (tile(j, misc_s), 0)),
                  wspec((D, DE)), wspec((D, DE)), wspec((DE, D))],
        out_specs=pl.BlockSpec((MOE_TR, D), lambda j, te_s, misc_s: (tile(j, misc_s), 0)),
        scratch_shapes=[pltpu.VMEM((D, DE), BF16), pltpu.VMEM((D, DE), BF16), pltpu.VMEM((DE, D), BF16)])
    return pl.pallas_call(
        _ffn_kernel, grid_spec=grid_spec,
        out_shape=jax.ShapeDtypeStruct((ntmax * MOE_TR, D), BF16),
        compiler_params=_cparams(("arbitrary",)),
        name="moe_ffn",
    )(te[0], misc.reshape(-1), xs, wg, wu, wd)


def _combine_kernel(dst_s, lo_s, n_s, misc_s, h_ref, comb_ref, lof_ref, nf_ref, sg_ref, su_ref, sd_ref, ys_hbm,
                    o_ref, ybuf, sem, *, ntile):
    i = pl.program_id(0)
    slot = lax.rem(i, 2)

    def piece_copy(sl, src_row, dst_row):
        return pltpu.make_async_copy(ys_hbm.at[pl.ds(pl.multiple_of(src_row, MOE_SEG), MOE_SEG)],
                                     ybuf.at[sl, pl.ds(pl.multiple_of(dst_row, MOE_SEG), MOE_SEG)], sem.at[sl])

    def start_tile(tile, sl):
        def start_seg(e, carry):
            idx = tile * 128 + e
            lo, src = lo_s[idx], dst_s[idx]

            def piece(p, c2):
                piece_copy(sl, src + p * MOE_SEG, lo + p * MOE_SEG).start()
                return c2

            lax.fori_loop(0, n_s[idx] // MOE_SEG, piece, 0)
            return carry

        lax.fori_loop(0, NE, start_seg, 0)

    @pl.when(i == 0)
    def _():
        ybuf[...] = jnp.zeros_like(ybuf)
        start_tile(0, 0)

    @pl.when(i + 1 < ntile)
    def _():
        start_tile(i + 1, 1 - slot)

    lo_row = lof_ref[...]
    hi_row = lo_row + nf_ref[...]
    scol = _iota((1, MOE_LS), 1).astype(F32)
    e_m = jnp.where((scol >= _col_of_row(lo_row)) & (scol < _col_of_row(hi_row)), 1.0, 0.0).astype(BF16)
    g = _dot(_rank_plus(comb_ref[...]).astype(BF16), e_m)
    lo_of_s = _dot(jnp.broadcast_to(lo_row * (1.0 / MOE_SEG), (8, 128)).astype(BF16), e_m)[0:1, :] * MOE_SEG
    perm_t = jnp.where(g == scol + 1.0 - lo_of_s, 1.0, 0.0).astype(BF16)

    h = h_ref[...]
    hs = _silu(_dot(h, sg_ref[...].astype(BF16))) * _dot(h, su_ref[...].astype(BF16))
    shared = _dot(hs.astype(BF16), sd_ref[...].astype(BF16))

    def wait_one(p, c2):
        piece_copy(slot, 0, 0).wait()
        return c2

    lax.fori_loop(0, misc_s[(8 + i) * 128], wait_one, 0)
    o_ref[...] = shared + _dot(perm_t, ybuf[slot])


def _moe_combine(plan, ys, h_all, comb_all, sg, su, sd, *, layer):
    dst, lo, n, _, misc, lof = plan
    ntile = h_all.shape[0] // MOE_TD
    shared = lambda w: pl.BlockSpec((None,) + w.shape[1:], lambda i, *_: (layer, 0, 0))
    grid_spec = pltpu.PrefetchScalarGridSpec(
        num_scalar_prefetch=4, grid=(ntile,),
        in_specs=[pl.BlockSpec((MOE_TD, D), lambda i, *_: (i, 0)),
                  pl.BlockSpec((MOE_TD, 128), lambda i, *_: (i, 0)),
                  pl.BlockSpec((None, 1, 128), lambda i, *_: (i, 0, 0)),
                  pl.BlockSpec((None, 1, 128), lambda i, *_: (i, 0, 0)),
                  shared(sg), shared(su), shared(sd),
                  pl.BlockSpec(memory_space=pl.ANY)],
        out_specs=pl.BlockSpec((MOE_TD, D), lambda i, *_: (i, 0)),
        scratch_shapes=[pltpu.VMEM((2, MOE_LS, D), BF16), pltpu.SemaphoreType.DMA((2,))])
    return pl.pallas_call(
        functools.partial(_combine_kernel, ntile=ntile),
        grid_spec=grid_spec,
        out_shape=jax.ShapeDtypeStruct((h_all.shape[0], D), F32),
        compiler_params=_cparams(("arbitrary",)),
        name="moe_combine",
    )(dst.reshape(-1), lo.reshape(-1), n.reshape(-1), misc.reshape(-1), h_all, comb_all,
      lof.reshape(PLAN_ROWS, 1, 128), n.astype(F32).reshape(PLAN_ROWS, 1, 128), sg, su, sd, ys)


def _moe(h_all, comb_all, wg, wu, wd, sg, su, sd, *, layer):
    ntile = h_all.shape[0] // MOE_TD
    max_rows = h_all.shape[0] * TOPK + ntile * NE * (MOE_SEG - 1) + NE * (MOE_TR - 1)
    ntmax = -(-max_rows // (MOE_TR * 128)) * 128
    plan = _moe_plan(comb_all, ntmax=ntmax)
    xs = _moe_dispatch(plan, h_all, comb_all, ntmax=ntmax)
    ys = _moe_ffn(plan, xs, wg, wu, wd, layer=layer, ntmax=ntmax)
    return _moe_combine(plan, ys, h_all, comb_all, sg, su, sd, layer=layer)


def _final_kernel(x_ref, f_ref, pmod_ref, png_ref, o_ref):
    o_ref[...] = _residual_in(x_ref, f_ref, pmod_ref, png_ref)


def _final(x, f_all, pmods, png, *, nb, seq_len, tm, row0):
    nt = seq_len // tm
    t0 = row0 // tm
    return pl.pallas_call(
        _final_kernel,
        grid=(nb, nt),
        in_specs=[pl.BlockSpec((tm, D), lambda b, t: (b * nt + t, 0)),
                  pl.BlockSpec((tm, D), lambda b, t: (t0 + b * nt + t, 0)),
                  pl.BlockSpec((None,) + pmods.shape[1:], lambda b, t: (b, 0, 0, 0)),
                  pl.BlockSpec(png.shape, lambda b, t: (0, 0))],
        out_specs=pl.BlockSpec((tm, D), lambda b, t: (b * nt + t, 0)),
        out_shape=jax.ShapeDtypeStruct(x.shape, F32),
        compiler_params=_cparams(("arbitrary", "arbitrary")),
        name="final_residual",
    )(x, f_all, pmods, png)


def kernel(x_prompt, x_sample, c_prompt, c_sample, state_conv, state_shift, state_wkv, w_mod, b_mod, norm_g, ab_w_in, ab_conv_w, ab_mu, ab_w0, ab_w_decay_up, ab_a0, ab_a_up, ab_g_up, ab_k_k, ab_k_a, ab_r_k, ab_gn_g, ab_gn_b, ab_w_out, gm_w_in, gm_b_in, gm_ln_g, gm_ln_b, gm_w_s, gm_b_s, gm_w_out, moe_w_router, moe_b_router, moe_w_gate, moe_w_up, moe_w_down, moe_ws_gate, moe_ws_up, moe_ws_down):
    bsz, seq_len, _ = x_prompt.shape
    nsamp = x_sample.shape[0]
    nheads = DB // HEAD
    n_prompt = bsz * seq_len
    n_all = -(-(n_prompt + nsamp) // MOE_TD) * MOE_TD
    tm = 512

    xp = x_prompt.reshape(n_prompt, D)
    xs = x_sample.reshape(nsamp, D)

    m = _modulation(jnp.concatenate([c_prompt, c_sample], axis=0), w_mod, b_mod)
    mods_p = [m[l, :bsz].reshape(bsz, 6, 1, D) for l in range(2)]
    mods_s = [m[l, bsz:].reshape(nsamp, 6, D).transpose(1, 0, 2)[None] for l in range(2)]

    row = lambda t: t.reshape(1, -1)
    pad_rows = lambda t, lo, hi: jnp.pad(t, ((lo, hi), (0, 0)))
    small = [ab_conv_w[0], row(ab_mu[0]), row(ab_w0[0]),
             pad_rows(ab_w_decay_up[0], 0, 64).astype(BF16), row(ab_a0[0]),
             pad_rows(ab_a_up[0], 64, 0).astype(BF16), ab_g_up[0].astype(BF16),
             row(ab_k_k[0]), row(ab_k_a[0])]
    win0 = ab_w_in[0].astype(BF16)
    wo0 = ab_w_out[0].astype(BF16)
    rk, gng, gnb = row(ab_r_k[0]), row(ab_gn_g[0]), row(ab_gn_b[0])
    wrt = [moe_w_router[l].T for l in range(2)]
    br = [moe_b_router[l].reshape(NE, 1) for l in range(2)]

    h2_all = jnp.zeros((n_all, D), BF16)
    comb_all = jnp.zeros((n_all, 128), F32)

    ya, r, lw, k, v, kk, a, g, ctail, stail = _ab_pre(
        xp, mods_p[0], norm_g[0], win0, small, None, nb=bsz, seq_len=seq_len, tm=tm, seq_mode=True)
    conv_p = ctail.reshape(bsz, 8, DA)[:, 6:8][None]
    shift_p = stail.reshape(bsz, 8, PB)[:, 7][None]
    yb, sfin = _wkv_seq(r, lw, k, v, kk, a, g, rk, gng, gnb, nb=bsz, seq_len=seq_len)
    sfin = sfin.reshape(bsz, nheads // 2, 2, HEAD, 2, HEAD)
    wkv_p = jnp.stack([sfin[:, :, 0, :, 0, :], sfin[:, :, 1, :, 1, :]], axis=2)
    wkv_p = wkv_p.reshape(bsz, nheads, HEAD, HEAD).transpose(0, 1, 3, 2)[None]
    xp1, h2_all, comb_all = _ab_post(ya, yb, xp, mods_p[0], norm_g[0], wo0, wrt[0], br[0], h2_all, comb_all,
                                     nb=bsz, seq_len=seq_len, tm=tm, row0=0)

    prev = [state_conv[0, :, 0], state_conv[0, :, 1], state_shift[0]]
    ya_s, r, lw, k, v, kk, a, g, gated_s, pb_s = _ab_pre(
        xs, mods_s[0], norm_g[0], win0, small, prev, nb=1, seq_len=nsamp, tm=nsamp, seq_mode=False)
    conv_s = jnp.stack([state_conv[0, :, 1], gated_s], axis=1)[None]
    shift_s = pb_s[None]
    vecs = [t.reshape(nsamp * nheads, 1, HEAD) for t in (r, lw, k, v, kk, a, g)]
    params = [t.reshape(nheads, 1, HEAD) for t in (ab_r_k[0], ab_gn_g[0], ab_gn_b[0])]
    yb_s, snew = _wkv_step(state_wkv[0].reshape(nsamp * nheads, HEAD, HEAD), vecs, params, nheads=nheads)
    wkv_s = snew.reshape(1, nsamp, nheads, HEAD, HEAD)
    xs1, h2_all, comb_all = _ab_post(ya_s, yb_s.reshape(nsamp, DB).astype(BF16), xs, mods_s[0], norm_g[0], wo0,
                                     wrt[0], br[0], h2_all, comb_all, nb=1, seq_len=nsamp, tm=nsamp, row0=n_prompt)

    def moe_layer(l, h_all, c_all):
        return _moe(h_all, c_all, moe_w_gate, moe_w_up, moe_w_down,
                    moe_ws_gate, moe_ws_up, moe_ws_down, layer=l)

    f0 = moe_layer(0, h2_all, comb_all)

    gw = [gm_w_in[0].astype(BF16), row(gm_b_in[0]), row(gm_ln_g[0]), row(gm_ln_b[0])]
    gw_seq = gw + [gm_w_s[0], gm_b_s[0].T, gm_w_out[0].astype(BF16), wrt[1], br[1]]
    gw_step = gw + [row(jnp.repeat(gm_w_s[0, :, 0, 0], CHUNK)), row(jnp.repeat(gm_b_s[0, :, 0], CHUNK)),
                    gm_w_out[0].astype(BF16), wrt[1], br[1]]
    tm1 = 256
    xp2, h2_all, comb_all = _gmlp(xp1, f0, mods_p[0], norm_g[0], mods_p[1], norm_g[1], gw_seq, h2_all, comb_all,
                                  nb=bsz, seq_len=seq_len, tm=tm1, row0=0, seq_mode=True)
    xs2, h2_all, comb_all, chunk_v = _gmlp(xs1, f0, mods_s[0], norm_g[0], mods_s[1], norm_g[1], gw_step, h2_all,
                                           comb_all, nb=1, seq_len=nsamp, tm=nsamp, row0=n_prompt, seq_mode=False)
    f1 = moe_layer(1, h2_all, comb_all)
    y_p = _final(xp2, f1, mods_p[1], norm_g[1], nb=bsz, seq_len=seq_len, tm=tm, row0=0)
    y_s = _final(xs2, f1, mods_s[1], norm_g[1], nb=1, seq_len=nsamp, tm=nsamp, row0=n_prompt)

    return (y_p.reshape(bsz, seq_len, D), y_s.reshape(nsamp, 1, D), conv_p, conv_s, shift_p, shift_s,
            wkv_p, wkv_s, chunk_v.reshape(1, nsamp, 1, D))
```

```python
import functools

import jax
import jax.numpy as jnp
from jax import lax
from jax.experimental import pallas as pl
from jax.experimental.pallas import tpu as pltpu

F32 = jnp.float32
BF16 = jnp.bfloat16
HI = lax.Precision.HIGHEST

D = 1024
DA = 512
DB = 512
HEAD = 64
PB = 1792
PAB = 3 * DA + PB
NE = 64
NG = 8
EPG = NE // NG
TOPG = 4
TOPK = 8
DE = 256
ROUTE_SCALE = 2.5
EPS = 1e-6
GN_EPS = 64e-5
CHUNK = 128
WKV_SLAB = 128
WKV_CHUNK = 32
WKV_PAIRS_PER_STEP = 4
WKV_P_STATE = 2
WKV_P_INV = 1
MOE_TD = 512
MOE_SEG = 16
MOE_LS = 5120
MOE_CH = 1024
MOE_PIECES = (128, 64, 32, 16)
MOE_TR = 512
MOE_XW = D + 128
PLAN_ROWS = 128

VMEM_LIMIT = 56 * 1024 * 1024


def _cparams(sem):
    return pltpu.CompilerParams(dimension_semantics=sem, vmem_limit_bytes=VMEM_LIMIT)


def _dot(a, b, precision=None):
    return jnp.dot(a, b, preferred_element_type=F32, precision=precision)


def _dot_nt(a, b, precision=None):
    return lax.dot_general(a, b, (((1,), (1,)), ((), ())), preferred_element_type=F32, precision=precision)


def _iota(shape, dim):
    return lax.broadcasted_iota(jnp.int32, shape, dim)


def _rms(x, g):
    return x * lax.rsqrt(jnp.mean(x * x, axis=-1, keepdims=True) + EPS) * g


def _silu(x):
    return x * jax.nn.sigmoid(x)


def _seg_ones(n, seg):
    return (_iota((n, n), 0) // seg == _iota((n, n), 1) // seg).astype(F32)


def _mod_kernel(c_ref, w_ref, b_ref, o_ref):
    s = _silu(c_ref[...]).astype(BF16)
    o_ref[...] = _dot(s, w_ref[...].astype(BF16)) + b_ref[...]


def _modulation(c_all, w_mod, b_mod):
    depth, _, width = w_mod.shape
    rows = c_all.shape[0]
    bn = 512
    return pl.pallas_call(
        _mod_kernel,
        grid=(depth, width // bn),
        in_specs=[pl.BlockSpec((rows, D), lambda l, j: (0, 0)),
                  pl.BlockSpec((None, D, bn), lambda l, j: (l, 0, j)),
                  pl.BlockSpec((None, 1, bn), lambda l, j: (l, 0, j))],
        out_specs=pl.BlockSpec((None, rows, bn), lambda l, j: (l, 0, j)),
        out_shape=jax.ShapeDtypeStruct((depth, rows, width), F32),
        compiler_params=_cparams(("arbitrary", "arbitrary")),
        name="modulation",
    )(c_all, w_mod, b_mod.reshape(depth, 1, width))


def _residual_in(x_ref, f_ref, pmod_ref, png_ref):
    x = x_ref[...]
    if f_ref is None:
        return x
    return x + pmod_ref[5] * _rms(f_ref[...], png_ref[3:4, :])


def _route(h, wrt_ref, br_ref):
    tm = h.shape[0]
    logits = _dot_nt(wrt_ref[...], h, precision=HI)
    scores = jax.nn.sigmoid(logits)
    sel = scores + br_ref[...]
    sub = _iota((EPG, tm), 0).astype(F32)
    neg = jnp.float32(-jnp.inf)
    blks, sblks, gscore = [], [], []
    for g in range(NG):
        blk = sel[g * EPG:(g + 1) * EPG, :]
        m1 = jnp.max(blk, axis=0, keepdims=True)
        first = jnp.min(jnp.where(blk == m1, sub, float(EPG)), axis=0, keepdims=True)
        m2 = jnp.max(jnp.where(sub == first, neg, blk), axis=0, keepdims=True)
        blks.append(blk)
        sblks.append(scores[g * EPG:(g + 1) * EPG, :])
        gscore.append(m1 + m2)
    masked = []
    for g in range(NG):
        cnt = jnp.zeros((1, tm), F32)
        for g2 in range(NG):
            if g2 == g:
                continue
            beats = (gscore[g2] >= gscore[g]) if g2 < g else (gscore[g2] > gscore[g])
            cnt = cnt + jnp.where(beats, 1.0, 0.0)
        masked.append(jnp.where(cnt < TOPG, blks[g], neg))
    cnts = [jnp.zeros((EPG, tm), F32) for _ in range(NG)]
    for g2 in range(NG):
        for j in range(EPG):
            row = masked[g2][j:j + 1, :]
            for g in range(NG):
                ge = jnp.where(row >= masked[g], 1.0, 0.0)
                gt = jnp.where(row > masked[g], 1.0, 0.0)
                if g > g2:
                    cnts[g] = cnts[g] + ge
                elif g < g2:
                    cnts[g] = cnts[g] + gt
                else:
                    cnts[g] = cnts[g] + jnp.where(sub > j, ge, gt)
    wts = [jnp.where(cnts[g] < TOPK, sblks[g], 0.0) for g in range(NG)]
    tot = wts[0]
    for g in range(1, NG):
        tot = tot + wts[g]
    denom = jnp.sum(tot, axis=0, keepdims=True)
    comb_t = jnp.concatenate([w / denom * ROUTE_SCALE for w in wts] + [jnp.zeros((NE, tm), F32)], axis=0)
    return comb_t.T


def _post(y, x, mod_ref, ng_ref, wrt_ref, br_ref, xo_ref, h2_ref, comb_ref):
    xn = x + mod_ref[2] * _rms(y, ng_ref[1:2, :])
    xo_ref[...] = xn
    h2 = _rms(xn, ng_ref[2:3, :]) * (1.0 + mod_ref[4]) + mod_ref[3]
    h2_ref[...] = h2.astype(BF16)
    comb_ref[...] = _route(h2, wrt_ref, br_ref)


def _ab_pre_kernel(*refs, seq_mode, tm):
    it = iter(refs)
    x_ref, mod_ref, ng_ref, win_ref = next(it), next(it), next(it), next(it)
    cw_ref, mu_ref, w0_ref, wd_ref, a0_ref, wa_ref, wg_ref, kk_ref, ka_ref = (next(it) for _ in range(9))
    if not seq_mode:
        p2_ref, p1_ref, ps_ref = next(it), next(it), next(it)
    ya_ref, r_ref, lw_ref, k_ref, v_ref, kkn_ref, a_ref, g_ref, ctail_ref, stail_ref = (next(it) for _ in range(10))
    if seq_mode:
        cc_ref, sc_ref = next(it), next(it)

        @pl.when(pl.program_id(1) == 0)
        def _():
            cc_ref[...] = jnp.zeros_like(cc_ref)
            sc_ref[...] = jnp.zeros_like(sc_ref)

    x = x_ref[...]
    h = _rms(x, ng_ref[0:1, :]) * (1.0 + mod_ref[1]) + mod_ref[0]
    proj = _dot(h.astype(BF16), win_ref[...])
    a_h, a_b, a_c = proj[:, 0:DA], proj[:, DA:2 * DA], proj[:, 2 * DA:3 * DA]
    pb = proj[:, 3 * DA:]
    gated = a_c * a_h
    if seq_mode:
        rows = _iota((tm, 1), 0)
        g1 = jnp.where(rows == 0, cc_ref[7:8, :], pltpu.roll(gated, 1, 0))
        g2 = jnp.where(rows == 0, cc_ref[6:7, :], jnp.where(rows == 1, cc_ref[7:8, :], pltpu.roll(gated, 2, 0)))
        pprev = jnp.where(rows == 0, sc_ref[7:8, :], pltpu.roll(pb, 1, 0))
        cc_ref[...] = gated[tm - 8:, :]
        sc_ref[...] = pb[tm - 8:, :]
        ctail_ref[...] = gated[tm - 8:, :]
        stail_ref[...] = pb[tm - 8:, :]
    else:
        g1, g2, pprev = p1_ref[...], p2_ref[...], ps_ref[...]
        ctail_ref[...] = gated
        stail_ref[...] = pb
    conv = g2 * cw_ref[0:1, :] + g1 * cw_ref[1:2, :] + gated * cw_ref[2:3, :]
    ya_ref[...] = (a_b * conv).astype(BF16)

    xm = pb + (pprev - pb) * mu_ref[...]
    r, k, v = xm[:, 0:DB], xm[:, DB:2 * DB], xm[:, 2 * DB:3 * DB]
    lowrank = xm[:, 3 * DB:3 * DB + 128]
    dg = xm[:, 3 * DB + 128:]
    w_pre = w0_ref[...] + _dot(jnp.tanh(lowrank).astype(BF16), wd_ref[...])
    z = -w_pre
    softplus = jnp.maximum(z, 0.0) + jnp.log(1.0 + jnp.exp(-jnp.abs(z)))
    lw_ref[...] = -jnp.exp(-softplus - 0.5)
    a = jax.nn.sigmoid(a0_ref[...] + _dot(lowrank.astype(BF16), wa_ref[...]))
    g_ref[...] = _dot(jax.nn.sigmoid(dg).astype(BF16), wg_ref[...])
    kk = k * kk_ref[...]
    ss = _dot(kk * kk, _seg_ones(DB, HEAD), precision=HI)
    kkn_ref[...] = kk / jnp.maximum(jnp.sqrt(ss), 1e-12)
    r_ref[...] = r
    k_ref[...] = k * (1.0 + (a - 1.0) * ka_ref[...])
    v_ref[...] = v
    a_ref[...] = a


def _ab_pre(x, mods, ng, win, small, prev, *, nb, seq_len, tm, seq_mode):
    nt = seq_len // tm
    rows = nb * seq_len
    tail = 8 if seq_mode else tm
    full = lambda shape: pl.BlockSpec(shape, lambda b, t: tuple(0 for _ in shape))
    tok = lambda w: pl.BlockSpec((tm, w), lambda b, t: (b * nt + t, 0))
    in_specs = [tok(D),
                pl.BlockSpec((None,) + mods.shape[1:], lambda b, t: (b, 0, 0, 0)),
                full(ng.shape), full(win.shape)] + [full(s.shape) for s in small]
    args = [x, mods, ng, win] + list(small)
    if not seq_mode:
        in_specs += [tok(DA), tok(DA), tok(PB)]
        args += list(prev)
    out_shape = ([jax.ShapeDtypeStruct((rows, DA), BF16)] + [jax.ShapeDtypeStruct((rows, DB), F32)] * 7
                 + [jax.ShapeDtypeStruct((nb * tail, DA), F32), jax.ShapeDtypeStruct((nb * tail, PB), F32)])
    out_specs = ([tok(DA)] + [tok(DB)] * 7
                 + [pl.BlockSpec((tail, DA), lambda b, t: (b, 0)), pl.BlockSpec((tail, PB), lambda b, t: (b, 0))])
    scratch = [pltpu.VMEM((8, DA), F32), pltpu.VMEM((8, PB), F32)] if seq_mode else []
    return pl.pallas_call(
        functools.partial(_ab_pre_kernel, seq_mode=seq_mode, tm=tm),
        grid=(nb, nt), in_specs=in_specs, out_specs=out_specs, out_shape=out_shape,
        scratch_shapes=scratch,
        compiler_params=_cparams(("arbitrary", "arbitrary")),
        name="ab_pre_seq" if seq_mode else "ab_pre_step",
    )(*args)


def _split(x):
    hi = x.astype(BF16)
    return hi, (x - hi.astype(F32)).astype(BF16)


def _mm(a, b, pa=1, pb=1, nt=False):
    ah, al = _split(a) if pa == 2 else (a.astype(BF16), None)
    bh, bl = _split(b) if pb == 2 else (b.astype(BF16), None)
    bx = 1 if nt else 0
    if pa == 2 and pb == 2:
        ah, bh = jnp.concatenate([ah, al, ah], axis=1), jnp.concatenate([bh, bh, bl], axis=bx)
    elif pa == 2:
        ah, bh = jnp.concatenate([ah, al], axis=1), jnp.concatenate([bh, bh], axis=bx)
    elif pb == 2:
        ah, bh = jnp.concatenate([ah, ah], axis=1), jnp.concatenate([bh, bl], axis=bx)
    return _dot_nt(ah, bh) if nt else _dot(ah, bh)


def _gn_gate(o, r, k, v, g, rk, gng, gnb, seg):
    mean = _mm(o, seg, 2, 1) * (1.0 / HEAD)
    d = o - mean
    var = _mm(d * d, seg, 2, 1) * (1.0 / HEAD)
    out = d * lax.rsqrt(var + GN_EPS) * gng + gnb
    out = out + _mm(r * k * rk, seg, 2, 1) * v
    return out * g


def _wkv_pair(r, lw, k, v, kk, a, st, cst):
    ts, c = WKV_SLAB, WKV_CHUNK
    nchunk = ts // c
    ps, pi = WKV_P_STATE, WKV_P_INV
    strict, incl, cum_lhs, m0, m1, headdiag, eye, eye2, blks, chunk_cols = cst

    l1 = lw.astype(BF16)
    rem = lw - l1.astype(F32)
    l2 = rem.astype(BF16)
    l3 = (rem - l2.astype(F32)).astype(BF16)
    cums = _dot(cum_lhs, jnp.concatenate([l1, l2, l3], axis=0))
    yield
    cl, ctot = cums[:ts], cums[ts:]
    e_neg = jnp.exp(-cl)
    e_end = jnp.exp(ctot - cl)
    at = -kk * jnp.exp(cl - lw)
    rt = r * jnp.exp(cl)
    beta = kk * a
    bk = jnp.concatenate([beta * e_neg, k * e_neg], axis=0)
    bend, kend = beta * e_end, k * e_end

    at_st = jnp.concatenate([at * m0, at * m1], axis=0)
    a_st = _mm(at_st, bk, ps, ps, nt=True)
    a_out = _mm(jnp.concatenate([rt * m0, rt * m1], axis=0), bk, 1, 1, nt=True)
    yield
    zero = jnp.zeros((ts, ts), F32)

    def two_heads(p0, p1):
        return jnp.concatenate([jnp.concatenate([p0, zero], axis=1), jnp.concatenate([zero, p1], axis=1)], axis=0)

    lbd = two_heads(a_st[:ts, :ts] * strict, a_st[ts:, :ts] * strict)
    d1 = jnp.where(blks[0], lbd, 0.0)
    tinv = eye2 + d1
    d2 = _mm(d1, d1, pi, pi)
    yield
    tinv = tinv + _mm(tinv, d2, pi, pi)
    d4 = _mm(d2, d2, pi, pi)
    yield
    tinv = tinv + _mm(tinv, d4, pi, pi)
    yield
    for lvl in range(1, len(blks)):
        off = jnp.where(blks[lvl] & jnp.logical_not(blks[lvl - 1]), lbd, 0.0)
        half = _mm(tinv, off, pi, pi)
        yield
        tinv = tinv + _mm(half, tinv, pi, pi)
        yield

    v_st = jnp.concatenate([v * m0, v * m1], axis=0)
    y_st = _mm(two_heads(a_st[:ts, ts:] * strict, a_st[ts:, ts:] * strict), v_st, ps, ps)
    yield
    wu = _mm(tinv, jnp.concatenate([at_st, y_st], axis=1), ps, ps)
    yield
    w =wu[:ts, :2 * HEAD] + wu[ts:, :2 * HEAD]
    u = wu[:ts, 2 * HEAD:] + wu[ts:, 2 * HEAD:]
    rhs = jnp.concatenate([jnp.concatenate([w, u], axis=1),
                           jnp.concatenate([jnp.zeros_like(v), v], axis=1)], axis=0)
    bend_t, kend_t = bend.T, kend.T
    lhs = [jnp.concatenate([bend_t * cm, kend_t * cm], axis=1) for cm in chunk_cols]
    mn_all = _mm(jnp.concatenate(lhs, axis=0), rhs, ps, ps)
    lhs_q = jnp.concatenate([jnp.concatenate([a_out[:ts, :ts] * incl, a_out[:ts, ts:] * incl], axis=1),
                             jnp.concatenate([a_out[ts:, :ts] * incl, a_out[ts:, ts:] * incl], axis=1)], axis=0)
    qo = _mm(lhs_q, rhs, 1, 1)
    yield
    q =rt + qo[:ts, :2 * HEAD] * m0 + qo[ts:, :2 * HEAD] * m1
    olocal = qo[:ts, 2 * HEAD:] * m0 + qo[ts:, 2 * HEAD:] * m1

    outs = []
    for ch in range(nchunk):
        mn = mn_all[ch * ts:(ch + 1) * ts]
        outs.append(_mm(q[ch * c:(ch + 1) * c], st, 1, 1) + olocal[ch * c:(ch + 1) * c])
        trans = eye * jnp.exp(ctot[ch * c:ch * c + 1, :]) + mn[:, :2 * HEAD] * headdiag
        st = _mm(trans, st, ps, ps) + mn[:, 2 * HEAD:] * headdiag
        yield
    return jnp.concatenate(outs, axis=0), st


def _round_robin(gens):
    results = [None] * len(gens)
    active = list(enumerate(gens))
    while active:
        still = []
        for i, gen in active:
            try:
                next(gen)
                still.append((i, gen))
            except StopIteration as stop:
                results[i] = stop.value
        active = still
    return results


def _wkv_seq_kernel(r_ref, lw_ref, k_ref, v_ref, kk_ref, a_ref, g_ref, rk_ref, gng_ref, gnb_ref,
                    y_ref, sfin_ref, st_ref, *, npp):
    ts, c = WKV_SLAB, WKV_CHUNK

    @pl.when(pl.program_id(2) == 0)
    def _():
        st_ref[...] = jnp.zeros_like(st_ref)

    ri, ci = _iota((ts, ts), 0), _iota((ts, ts), 1)
    same = ri // c == ci // c
    strict = (same & (ci < ri)).astype(F32)
    incl = (same & (ci <= ri)).astype(F32)
    cum_lhs = jnp.concatenate([incl, same.astype(F32)], axis=0).astype(BF16)
    cum_lhs = jnp.concatenate([cum_lhs] * 3, axis=1)
    lane = _iota((1, 2 * HEAD), 1)
    m0 = (lane < HEAD).astype(F32)
    m1 = 1.0 - m0
    headdiag = _seg_ones(2 * HEAD, HEAD)
    eye = (_iota((2 * HEAD, 2 * HEAD), 0) == _iota((2 * HEAD, 2 * HEAD), 1)).astype(F32)
    r2, c2 = _iota((2 * ts, 2 * ts), 0), _iota((2 * ts, 2 * ts), 1)
    eye2 = (r2 == c2).astype(F32)
    blks, b = [], 8
    while b <= c:
        blks.append(r2 // b == c2 // b)
        b *= 2
    tcol = _iota((1, ts), 1) // c
    chunk_cols = [(tcol == ch).astype(F32) for ch in range(ts // c)]
    cst = (strict, incl, cum_lhs, m0, m1, headdiag, eye, eye2, blks, chunk_cols)

    sls = [slice(p * 2 * HEAD, (p + 1) * 2 * HEAD) for p in range(npp)]
    res = _round_robin([_wkv_pair(r_ref[:, sl], lw_ref[:, sl], k_ref[:, sl], v_ref[:, sl], kk_ref[:, sl],
                                  a_ref[:, sl], st_ref[p], cst) for p, sl in enumerate(sls)])
    for p, sl in enumerate(sls):
        o, st = res[p]
        st_ref[p] = st
        y_ref[:, sl] = _gn_gate(o, r_ref[:, sl], k_ref[:, sl], v_ref[:, sl], g_ref[:, sl], rk_ref[:, sl],
                                gng_ref[:, sl], gnb_ref[:, sl], headdiag).astype(BF16)

    @pl.when(pl.program_id(2) == pl.num_programs(2) - 1)
    def _():
        sfin_ref[...] = st_ref[...]


def _wkv_seq(r, lw, k, v, kk, a, g, rk, gng, gnb, *, nb, seq_len):
    ns = seq_len // WKV_SLAB
    npp = WKV_PAIRS_PER_STEP
    width = 2 * HEAD * npp
    tok = pl.BlockSpec((WKV_SLAB, width), lambda b, p, s: (b * ns + s, p))
    par = pl.BlockSpec((1, width), lambda b, p, s: (0, p))
    return pl.pallas_call(
        functools.partial(_wkv_seq_kernel, npp=npp),
        grid=(nb, DB // width, ns),
        in_specs=[tok] * 7 + [par] * 3,
        out_specs=[tok, pl.BlockSpec((None, npp, 2 * HEAD, 2 * HEAD), lambda b, p, s: (b, p, 0, 0))],
        out_shape=[jax.ShapeDtypeStruct((nb * seq_len, DB), BF16),
                   jax.ShapeDtypeStruct((nb, DB // (2 * HEAD), 2 * HEAD, 2 * HEAD), F32)],
        scratch_shapes=[pltpu.VMEM((npp, 2 * HEAD, 2 * HEAD), F32)],
        compiler_params=_cparams(("arbitrary", "arbitrary", "arbitrary")),
        name="wkv_seq",
    )(r, lw, k, v, kk, a, g, rk, gng, gnb)


def _wkv_step_kernel(s_ref, r_ref, lw_ref, k_ref, v_ref, kk_ref, a_ref, g_ref, rk_ref, gng_ref, gnb_ref,
                     y_ref, so_ref):
    s = s_ref[...]
    r, k, v, kk, a = r_ref[...], k_ref[...], v_ref[...], kk_ref[...], a_ref[...]
    w = jnp.exp(lw_ref[...])
    eye = (_iota((1, HEAD, HEAD), 1) == _iota((1, HEAD, HEAD), 2)).astype(F32)
    v_col = jnp.sum(eye * v, axis=2, keepdims=True)
    s_kk = jnp.sum(s * kk, axis=2, keepdims=True)
    s = s * w - s_kk * (kk * a) + v_col * k
    so_ref[...] = s
    o_col = jnp.sum(s * r, axis=2, keepdims=True)
    o = jnp.sum(eye * o_col, axis=1, keepdims=True)
    mean = jnp.mean(o, axis=2, keepdims=True)
    d = o - mean
    var = jnp.mean(d * d, axis=2, keepdims=True)
    out = d * lax.rsqrt(var + GN_EPS) * gng_ref[...] + gnb_ref[...]
    out = out + jnp.sum(r * k * rk_ref[...], axis=2, keepdims=True) * v
    y_ref[...] = out * g_ref[...]


def _wkv_step(state, vecs, params, *, nheads):
    n = state.shape[0]
    tb = 8 * nheads
    sspec = pl.BlockSpec((tb, HEAD, HEAD), lambda i: (i, 0, 0))
    vspec = pl.BlockSpec((tb, 1, HEAD), lambda i: (i, 0, 0))
    pspec = pl.BlockSpec((tb, 1, HEAD), lambda i: (0, 0, 0))
    reps = tb // nheads
    params = [jnp.tile(p, (reps, 1, 1)) for p in params]
    return pl.pallas_call(
        _wkv_step_kernel,
        grid=(n // tb,),
        in_specs=[sspec] + [vspec] * 7 + [pspec] * 3,
        out_specs=[vspec, sspec],
        out_shape=[jax.ShapeDtypeStruct((n, 1, HEAD), F32), jax.ShapeDtypeStruct((n, HEAD, HEAD), F32)],
        compiler_params=_cparams(("arbitrary",)),
        name="wkv_step",
    )(state, *vecs, *params)


def _ab_post_kernel(ya_ref, yb_ref, x_ref, mod_ref, ng_ref, wo_ref, wrt_ref, br_ref,
                    h2in_ref, combin_ref, xo_ref, h2_ref, comb_ref):
    del h2in_ref, combin_ref
    y = _dot(ya_ref[...], wo_ref[0:DA, :]) + _dot(yb_ref[...], wo_ref[DA:, :])
    _post(y, x_ref[...], mod_ref, ng_ref, wrt_ref, br_ref, xo_ref, h2_ref, comb_ref)


def _ab_post(ya, yb, x, mods, ng, wo, wrt, br, h2_all, comb_all, *, nb, seq_len, tm, row0):
    nt = seq_len // tm
    t0 = row0 // tm
    full = lambda shape: pl.BlockSpec(shape, lambda b, t: tuple(0 for _ in shape))
    tok = lambda w: pl.BlockSpec((tm, w), lambda b, t: (b * nt + t, 0))
    tok_all = lambda w: pl.BlockSpec((tm, w), lambda b, t: (t0 + b * nt + t, 0))
    anyspec = pl.BlockSpec(memory_space=pl.ANY)
    return pl.pallas_call(
        _ab_post_kernel,
        grid=(nb, nt),
        in_specs=[tok(DA), tok(DB), tok(D),
                  pl.BlockSpec((None,) + mods.shape[1:], lambda b, t: (b, 0, 0, 0)),
                  full(ng.shape), full(wo.shape), full(wrt.shape), full(br.shape), anyspec, anyspec],
        out_specs=[tok(D), tok_all(D), tok_all(128)],
        out_shape=[jax.ShapeDtypeStruct(x.shape, F32), jax.ShapeDtypeStruct(h2_all.shape, BF16),
                   jax.ShapeDtypeStruct(comb_all.shape, F32)],
        input_output_aliases={8: 1, 9: 2},
        compiler_params=_cparams(("arbitrary", "arbitrary")),
        name="ab_post",
    )(ya, yb, x, mods, ng, wo, wrt, br, h2_all, comb_all)


def _gmlp_kernel(*refs, seq_mode, tm):
    it = iter(refs)
    x_ref, f_ref, pmod_ref, png_ref, mod_ref, ng_ref = (next(it) for _ in range(6))
    win_ref, bin_ref, lng_ref, lnb_ref, ws_ref, bs_ref, wo_ref, wrt_ref, br_ref = (next(it) for _ in range(9))
    next(it), next(it)
    xo_ref, h2_ref, comb_ref = next(it), next(it), next(it)
    if not seq_mode:
        cv_ref = next(it)
    x = _residual_in(x_ref, f_ref, pmod_ref, png_ref)
    h = _rms(x, ng_ref[0:1, :]) * (1.0 + mod_ref[1]) + mod_ref[0]
    z = jax.nn.gelu(_dot(h.astype(BF16), win_ref[...]) + bin_ref[...])
    u, v = z[:, :D], z[:, D:]
    mu = jnp.mean(v, axis=-1, keepdims=True)
    var = jnp.mean(jnp.square(v - mu), axis=-1, keepdims=True)
    v = (v - mu) * lax.rsqrt(var + EPS) * lng_ref[...] + lnb_ref[...]
    if seq_mode:
        vb = v.astype(BF16)
        causal = _iota((CHUNK, CHUNK), 1) <= _iota((CHUNK, CHUNK), 0)
        cols = []
        for g in range(D // CHUNK):
            wsg = jnp.where(causal, ws_ref[g], 0.0).astype(BF16)
            bsg = bs_ref[:, g:g + 1]
            rows = [_dot(wsg, vb[c * CHUNK:(c + 1) * CHUNK, g * CHUNK:(g + 1) * CHUNK]) + bsg
                    for c in range(tm // CHUNK)]
            cols.append(jnp.concatenate(rows, axis=0) if len(rows) > 1 else rows[0])
        s = jnp.concatenate(cols, axis=1)
    else:
        cv_ref[...] = v
        s = v * ws_ref[...] + bs_ref[...]
    y = _dot((u * s).astype(BF16), wo_ref[...])
    _post(y, x, mod_ref, ng_ref, wrt_ref, br_ref, xo_ref, h2_ref, comb_ref)


def _gmlp(x, f_all, pmods, png, mods, ng, weights, h2_all, comb_all, *, nb, seq_len, tm, row0, seq_mode):
    nt = seq_len // tm
    t0 = row0 // tm
    full = lambda shape: pl.BlockSpec(shape, lambda b, t: tuple(0 for _ in shape))
    tok = lambda w: pl.BlockSpec((tm, w), lambda b, t: (b * nt + t, 0))
    tok_all = lambda w: pl.BlockSpec((tm, w), lambda b, t: (t0 + b * nt + t, 0))
    modspec = lambda m: pl.BlockSpec((None,) + m.shape[1:], lambda b, t: (b, 0, 0, 0))
    anyspec = pl.BlockSpec(memory_space=pl.ANY)
    in_specs = ([tok(D), tok_all(D), modspec(pmods), full(png.shape), modspec(mods), full(ng.shape)]
                + [full(w.shape) for w in weights] + [anyspec, anyspec])
    out_specs = [tok(D), tok_all(D), tok_all(128)]
    out_shape = [jax.ShapeDtypeStruct(x.shape, F32), jax.ShapeDtypeStruct(h2_all.shape, BF16),
                 jax.ShapeDtypeStruct(comb_all.shape, F32)]
    if not seq_mode:
        out_specs.append(tok(D))
        out_shape.append(jax.ShapeDtypeStruct(x.shape, F32))
    n_in = len(in_specs)
    return pl.pallas_call(
        functools.partial(_gmlp_kernel, seq_mode=seq_mode, tm=tm),
        grid=(nb, nt), in_specs=in_specs, out_specs=out_specs, out_shape=out_shape,
        input_output_aliases={n_in - 2: 1, n_in - 1: 2},
        compiler_params=_cparams(("arbitrary", "arbitrary")),
        name="gmlp_seq" if seq_mode else "gmlp_step",
    )(x, f_all, pmods, png, mods, ng, *weights, h2_all, comb_all)


def _strict_lower(n):
    return _iota((n, n), 1) < _iota((n, n), 0)


def _col_of_row(row):
    return jnp.broadcast_to(row, (128, 128)).T[:, 0:1]


def _plan_kernel(comb_ref, dst_ref, lo_ref, n_ref, te_ref, misc_ref, lof_ref, pieces_ref, cnt_ref, *, ntile, te_cols):
    i = pl.program_id(0)

    @pl.when(i == 0)
    def _():
        cnt_ref[...] = jnp.zeros_like(cnt_ref)

    sel = jnp.where(comb_ref[...] > 0.0, 1.0, 0.0)
    cnt_ref[pl.ds(i, 1), :] = jnp.sum(sel, axis=0, keepdims=True)

    @pl.when(i == ntile - 1)
    def _():
        c = cnt_ref[...]
        cpad = jnp.floor((c + (MOE_SEG - 1.0)) * (1.0 / MOE_SEG)) * MOE_SEG
        cb = cpad.astype(BF16)
        lower = jnp.where(_strict_lower(PLAN_ROWS), 1.0, 0.0).astype(BF16)
        upper = jnp.where(_iota((128, 128), 0) < _iota((128, 128), 1), 1.0, 0.0).astype(BF16)
        before = _dot(lower, cb)
        lo = _dot(cb, upper)
        tot = jnp.sum(cpad, axis=0, keepdims=True)
        rt = jnp.floor((tot + (MOE_TR - 1.0)) * (1.0 / MOE_TR))
        gt = _dot(jnp.broadcast_to(rt, (8, 128)).astype(BF16), upper)[0:1, :]
        dst_ref[...] = (gt * MOE_TR + before).astype(jnp.int32)
        lo_ref[...] = lo.astype(jnp.int32)
        lof_ref[...] = lo
        n_ref[...] = cpad.astype(jnp.int32)
        ends_col = _col_of_row(gt + rt)
        jrow = _iota((1, te_cols), 1).astype(F32)
        te = jnp.sum(jnp.where(ends_col <= jrow, 1.0, 0.0), axis=0, keepdims=True)
        te_ref[...] = jnp.broadcast_to(jnp.minimum(te, NE - 1.0), (8, te_cols)).astype(jnp.int32)
        nused = jnp.sum(rt, axis=1, keepdims=True)
        rowi = _iota((8, 128), 0)
        misc = jnp.where(rowi == 0, gt * MOE_TR + tot, jnp.where(rowi == 1, rt * MOE_TR - tot, nused))
        misc_ref[...] = misc.astype(jnp.int32)
        lane = _iota((1, 128), 1)
        pieces = jnp.zeros((PLAN_ROWS, 128), F32)
        taken = jnp.zeros_like(cpad)
        for k, size in enumerate(MOE_PIECES):
            cnt = jnp.floor((cpad - taken) * (1.0 / size))
            taken = taken + cnt * size
            pieces = pieces + jnp.where(lane == k, jnp.sum(cnt, axis=1, keepdims=True), 0.0)
        pieces_ref[...] = pieces.astype(jnp.int32)


def _moe_plan(comb_all, *, ntmax):
    ntile = comb_all.shape[0] // MOE_TD
    te_cols = -(-ntmax // 128) * 128
    full = lambda shape: pl.BlockSpec(shape, lambda i: tuple(0 for _ in shape))
    shapes = [((PLAN_ROWS, 128), jnp.int32)] * 3 + [((8, te_cols), jnp.int32), ((8, 128), jnp.int32),
                                                   ((PLAN_ROWS, 128), F32), ((PLAN_ROWS, 128), jnp.int32)]
    return pl.pallas_call(
        functools.partial(_plan_kernel, ntile=ntile, te_cols=te_cols),
        grid=(ntile,),
        in_specs=[pl.BlockSpec((MOE_TD, 128), lambda i: (i, 0))],
        out_specs=[full(s) for s, _ in shapes],
        out_shape=[jax.ShapeDtypeStruct(s, d) for s, d in shapes],
        scratch_shapes=[pltpu.VMEM((PLAN_ROWS, 128), F32)],
        compiler_params=_cparams(("arbitrary",)),
        name="moe_plan",
    )(comb_all)


def _rank_plus(comb):
    sel = jnp.where(comb > 0.0, 1.0, 0.0)
    lower = jnp.where(_strict_lower(MOE_TD), 1.0, 0.0).astype(BF16)
    return sel * (_dot(lower, sel.astype(BF16)) + 1.0)


def _for_each_piece(n, fn):
    big = MOE_PIECES[0]
    nbig = n // big

    def body(p, carry):
        fn(p * big, big)
        return carry

    lax.fori_loop(0, nbig, body, 0)
    off = nbig * big
    for size in MOE_PIECES[1:]:
        bit = ((n - off) // size) > 0

        @pl.when(bit)
        def _(off=off, size=size):
            fn(off, size)

        off = off + jnp.where(bit, size, 0)


def _wait_pieces(pieces_s, tile, make_copy):
    for k, size in enumerate(MOE_PIECES):
        def wait_one(p, carry, size=size):
            make_copy(size).wait()
            return carry

        lax.fori_loop(0, pieces_s[tile * 128 + k], wait_one, 0)


def _dispatch_kernel(dst_s, lo_s, n_s, misc_s, pieces_s, h_ref, comb_ref, lof_ref, nf_ref, xs_hbm, buf, zbuf, sem,
                     tsem, *, ntile):
    i = pl.program_id(0)
    slot = lax.rem(i, 2)
    comb = comb_ref[...]
    lo_row = lof_ref[...]
    hi_row = lo_row + nf_ref[...]
    rp_hi, rp_lo = _split(_rank_plus(comb).T)
    l_hi, l_lo = _split(jnp.broadcast_to(_col_of_row(lo_row * (1.0 / MOE_SEG)), (128, 128)))
    rhs_meta = jnp.concatenate([jnp.concatenate([rp_hi, l_hi], axis=1), jnp.concatenate([rp_lo, l_lo], axis=1)], axis=0)
    c_hi, c_lo = _split(comb)
    h_ext = jnp.concatenate([h_ref[...], c_hi, c_lo], axis=1)
    ones2 = jnp.ones((256, 128), BF16)
    lane_lo = _iota((1, 128), 1) < 64
    for ch in range(MOE_LS // MOE_CH):
        srow = (_iota((MOE_CH, 1), 0) + ch * MOE_CH).astype(F32)
        e_t = jnp.where((srow >= lo_row) & (srow < hi_row), 1.0, 0.0)
        e_tb = e_t.astype(BF16)
        meta = _dot(jnp.concatenate([e_tb, e_tb], axis=1), rhs_meta)
        target = srow + 1.0 - meta[:, MOE_TD:] * MOE_SEG
        target = jnp.concatenate([target] * (MOE_TD // 128), axis=1)
        perm = jnp.where(meta[:, :MOE_TD] == target, 1.0, 0.0).astype(BF16)
        rows = pl.ds(ch * MOE_CH, MOE_CH)
        sorted_rows = _dot(perm, h_ext)
        buf[slot, rows, :D] = sorted_rows[:, :D].astype(BF16)
        z_hi, z_lo = _split((sorted_rows[:, D:D + 128] + sorted_rows[:, D + 128:]) * e_t)
        w_hi, w_lo = _split(_dot(jnp.concatenate([z_hi, z_lo], axis=1), ones2))
        buf[slot, rows, D:] = jnp.where(lane_lo, w_hi, w_lo)

    def copy_out(sl, src_row, dst_row, size):
        return pltpu.make_async_copy(buf.at[sl, pl.ds(pl.multiple_of(src_row, MOE_SEG), size)],
                                     xs_hbm.at[pl.ds(pl.multiple_of(dst_row, MOE_SEG), size)], sem.at[sl])

    def start_seg(e, carry):
        idx = i * 128 + e
        lo, dst = lo_s[idx], dst_s[idx]
        _for_each_piece(n_s[idx], lambda off, size: copy_out(slot, lo + off, dst + off, size).start())
        return carry

    lax.fori_loop(0, NE, start_seg, 0)

    @pl.when(i > 0)
    def _():
        _wait_pieces(pieces_s, i - 1, lambda size: copy_out(1 - slot, 0, 0, size))

    @pl.when(i == ntile - 1)
    def _():
        _wait_pieces(pieces_s, i, lambda size: copy_out(slot, 0, 0, size))
        zbuf[...] = jnp.zeros_like(zbuf)

        def tail(e, carry):
            dst = misc_s[e]

            def piece(p, c2):
                cp = pltpu.make_async_copy(
                    zbuf, xs_hbm.at[pl.ds(pl.multiple_of(dst + p * MOE_SEG, MOE_SEG), MOE_SEG)], tsem)
                cp.start()
                cp.wait()
                return c2

            lax.fori_loop(0, misc_s[128 + e] // MOE_SEG, piece, 0)
            return carry

        lax.fori_loop(0, NE, tail, 0)


def _moe_dispatch(plan, h_all, comb_all, *, ntmax):
    dst, lo, n, _, misc, lof, pieces = plan
    ntile = h_all.shape[0] // MOE_TD
    grid_spec = pltpu.PrefetchScalarGridSpec(
        num_scalar_prefetch=5, grid=(ntile,),
        in_specs=[pl.BlockSpec((MOE_TD, D), lambda i, *_: (i, 0)),
                  pl.BlockSpec((MOE_TD, 128), lambda i, *_: (i, 0)),
                  pl.BlockSpec((None, 1, 128), lambda i, *_: (i, 0, 0)),
                  pl.BlockSpec((None, 1, 128), lambda i, *_: (i, 0, 0))],
        out_specs=pl.BlockSpec(memory_space=pl.ANY),
        scratch_shapes=[pltpu.VMEM((2, MOE_LS, MOE_XW), BF16), pltpu.VMEM((MOE_SEG, MOE_XW), BF16),
                        pltpu.SemaphoreType.DMA((2,)), pltpu.SemaphoreType.DMA(())])
    return pl.pallas_call(
        functools.partial(_dispatch_kernel, ntile=ntile),
        grid_spec=grid_spec,
        out_shape=jax.ShapeDtypeStruct((ntmax * MOE_TR, MOE_XW), BF16),
        compiler_params=_cparams(("arbitrary",)),
        name="moe_dispatch",
    )(dst.reshape(-1), lo.reshape(-1), n.reshape(-1), misc.reshape(-1), pieces.reshape(-1), h_all, comb_all,
      lof.reshape(PLAN_ROWS, 1, 128), n.astype(F32).reshape(PLAN_ROWS, 1, 128))


def _ffn_kernel(te_s, misc_s, xs_ref, wg_ref, wu_ref, wd_ref, ys_ref, wgb, wub, wdb):
    j = pl.program_id(0)

    @pl.when(j < misc_s[2 * 128])
    def _():
        @pl.when((j == 0) | (te_s[j] != te_s[jnp.maximum(j - 1, 0)]))
        def _():
            wgb[...] = wg_ref[...].astype(BF16)
            wub[...] = wu_ref[...].astype(BF16)
            wdb[...] = wd_ref[...].astype(BF16)

        x = xs_ref[:, :D]
        w = xs_ref[:, D:D + 1].astype(F32) + xs_ref[:, D + 64:D + 65].astype(F32)
        hw = _silu(_dot(x, wgb[...])) * _dot(x, wub[...]) * w
        ys_ref[...] = _dot(hw.astype(BF16), wdb[...]).astype(BF16)


def _moe_ffn(plan, xs, wg, wu, wd, *, layer, ntmax):
    te, misc = plan[3], plan[4]

    def tile(j, misc_s):
        return jnp.minimum(j, misc_s[2 * 128] - 1)

    wspec = lambda shape: pl.BlockSpec((None, None) + shape,
                                       lambda j, te_s, misc_s: (layer, te_s[tile(j, misc_s)], 0, 0))
    grid_spec = pltpu.PrefetchScalarGridSpec(
        num_scalar_prefetch=2, grid=(ntmax,),
        in_specs=[pl.BlockSpec((MOE_TR, MOE_XW), lambda j, te_s, misc_s: (tile(j, misc_s), 0)),
                  wspec((D, DE)), wspec((D, DE)), wspec((DE, D))],
        out_specs=pl.BlockSpec((MOE_TR, D), lambda j, te_s, misc_s: (tile(j, misc_s), 0)),
        scratch_shapes=[pltpu.VMEM((D, DE), BF16), pltpu.VMEM((D, DE), BF16), pltpu.VMEM((DE, D), BF16)])
    return pl.pallas_call(
        _ffn_kernel, grid_spec=grid_spec,
        out_shape=jax.ShapeDtypeStruct((ntmax * MOE_TR, D), BF16),
        compiler_params=_cparams(("arbitrary",)),
        name="moe_ffn",
    )(te[0], misc.reshape(-1), xs, wg, wu, wd)


def _combine_kernel(dst_s, lo_s, n_s, pieces_s, h_ref, comb_ref, lof_ref, nf_ref, sg_ref, su_ref, sd_ref, ys_hbm,
                    o_ref, ybuf, sem, *, ntile):
    i = pl.program_id(0)
    slot = lax.rem(i, 2)

    def copy_in(sl, src_row, dst_row, size):
        return pltpu.make_async_copy(ys_hbm.at[pl.ds(pl.multiple_of(src_row, MOE_SEG), size)],
                                     ybuf.at[sl, pl.ds(pl.multiple_of(dst_row, MOE_SEG), size)], sem.at[sl])

    def start_tile(tile, sl):
        def start_seg(e, carry):
            idx = tile * 128 + e
            lo, src = lo_s[idx], dst_s[idx]
            _for_each_piece(n_s[idx], lambda off, size: copy_in(sl, src + off, lo + off, size).start())
            return carry

        lax.fori_loop(0, NE, start_seg, 0)

    @pl.when(i == 0)
    def _():
        ybuf[...] = jnp.zeros_like(ybuf)
        start_tile(0, 0)

    @pl.when(i + 1 < ntile)
    def _():
        start_tile(i + 1, 1 - slot)

    lo_row = lof_ref[...]
    hi_row = lo_row + nf_ref[...]
    lo_col, hi_col = _col_of_row(lo_row), _col_of_row(hi_row)
    rp_hi, rp_lo = _split(_rank_plus(comb_ref[...]))
    l_hi, l_lo = _split(jnp.broadcast_to(lo_row * (1.0 / MOE_SEG), (8, 128)))
    lhs_meta = jnp.concatenate([jnp.concatenate([rp_hi, rp_lo], axis=1), jnp.concatenate([l_hi, l_lo], axis=1)], axis=0)

    h = h_ref[...]
    hs = _silu(_dot(h, sg_ref[...].astype(BF16))) * _dot(h, su_ref[...].astype(BF16))
    acc = _dot(hs.astype(BF16), sd_ref[...].astype(BF16))

    _wait_pieces(pieces_s, i, lambda size: copy_in(slot, 0, 0, size))
    for ch in range(MOE_LS // MOE_CH):
        scol = (_iota((1, MOE_CH), 1) + ch * MOE_CH).astype(F32)
        e_m = jnp.where((scol >= lo_col) & (scol < hi_col), 1.0, 0.0).astype(BF16)
        meta = _dot(lhs_meta, jnp.concatenate([e_m, e_m], axis=0))
        target = scol + 1.0 - meta[MOE_TD:MOE_TD + 1, :] * MOE_SEG
        perm_t = jnp.where(meta[:MOE_TD] == target, 1.0, 0.0).astype(BF16)
        acc = acc + _dot(perm_t, ybuf[slot, pl.ds(ch * MOE_CH, MOE_CH), :])
    o_ref[...] = acc


def _moe_combine(plan, ys, h_all, comb_all, sg, su, sd, *, layer):
    dst, lo, n, _, _, lof, pieces = plan
    ntile = h_all.shape[0] // MOE_TD
    shared = lambda w: pl.BlockSpec((None,) + w.shape[1:], lambda i, *_: (layer, 0, 0))
    grid_spec = pltpu.PrefetchScalarGridSpec(
        num_scalar_prefetch=4, grid=(ntile,),
        in_specs=[pl.BlockSpec((MOE_TD, D), lambda i, *_: (i, 0)),
                  pl.BlockSpec((MOE_TD, 128), lambda i, *_: (i, 0)),
                  pl.BlockSpec((None, 1, 128), lambda i, *_: (i, 0, 0)),
                  pl.BlockSpec((None, 1, 128), lambda i, *_: (i, 0, 0)),
                  shared(sg), shared(su), shared(sd),
                  pl.BlockSpec(memory_space=pl.ANY)],
        out_specs=pl.BlockSpec((MOE_TD, D), lambda i, *_: (i, 0)),
        scratch_shapes=[pltpu.VMEM((2, MOE_LS, D), BF16), pltpu.SemaphoreType.DMA((2,))])
    return pl.pallas_call(
        functools.partial(_combine_kernel, ntile=ntile),
        grid_spec=grid_spec,
        out_shape=jax.ShapeDtypeStruct((h_all.shape[0], D), F32),
        compiler_params=_cparams(("arbitrary",)),
        name="moe_combine",
    )(dst.reshape(-1), lo.reshape(-1), n.reshape(-1), pieces.reshape(-1), h_all, comb_all,
      lof.reshape(PLAN_ROWS, 1, 128), n.astype(F32).reshape(PLAN_ROWS, 1, 128), sg, su, sd, ys)


def _moe(h_all, comb_all, wg, wu, wd, sg, su, sd, *, layer, n_tokens):
    ntile = h_all.shape[0] // MOE_TD
    max_rows = n_tokens * TOPK + ntile * NE * (MOE_SEG - 1) + NE * (MOE_TR - 1)
    ntmax = -(-max_rows // MOE_TR)
    plan = _moe_plan(comb_all, ntmax=ntmax)
    xs = _moe_dispatch(plan, h_all, comb_all, ntmax=ntmax)
    ys = _moe_ffn(plan, xs, wg, wu, wd, layer=layer, ntmax=ntmax)
    return _moe_combine(plan, ys, h_all, comb_all, sg, su, sd, layer=layer)


def _final_kernel(x_ref, f_ref, pmod_ref, png_ref, o_ref):
    o_ref[...] = _residual_in(x_ref, f_ref, pmod_ref, png_ref)


def _final(x, f_all, pmods, png, *, nb, seq_len, tm, row0):
    nt = seq_len // tm
    t0 = row0 // tm
    return pl.pallas_call(
        _final_kernel,
        grid=(nb, nt),
        in_specs=[pl.BlockSpec((tm, D), lambda b, t: (b * nt + t, 0)),
                  pl.BlockSpec((tm, D), lambda b, t: (t0 + b * nt + t, 0)),
                  pl.BlockSpec((None,) + pmods.shape[1:], lambda b, t: (b, 0, 0, 0)),
                  pl.BlockSpec(png.shape, lambda b, t: (0, 0))],
        out_specs=pl.BlockSpec((tm, D), lambda b, t: (b * nt + t, 0)),
        out_shape=jax.ShapeDtypeStruct(x.shape, F32),
        compiler_params=_cparams(("arbitrary", "arbitrary")),
        name="final_residual",
    )(x, f_all, pmods, png)


def kernel(x_prompt, x_sample, c_prompt, c_sample, state_conv, state_shift, state_wkv, w_mod, b_mod, norm_g, ab_w_in, ab_conv_w, ab_mu, ab_w0, ab_w_decay_up, ab_a0, ab_a_up, ab_g_up, ab_k_k, ab_k_a, ab_r_k, ab_gn_g, ab_gn_b, ab_w_out, gm_w_in, gm_b_in, gm_ln_g, gm_ln_b, gm_w_s, gm_b_s, gm_w_out, moe_w_router, moe_b_router, moe_w_gate, moe_w_up, moe_w_down, moe_ws_gate, moe_ws_up, moe_ws_down):
    bsz, seq_len, _ = x_prompt.shape
    nsamp = x_sample.shape[0]
    nheads = DB // HEAD
    n_prompt = bsz * seq_len
    n_all = -(-(n_prompt + nsamp) // MOE_TD) * MOE_TD
    tm = 512

    xp = x_prompt.reshape(n_prompt, D)
    xs = x_sample.reshape(nsamp, D)

    m = _modulation(jnp.concatenate([c_prompt, c_sample], axis=0), w_mod, b_mod)
    mods_p = [m[l, :bsz].reshape(bsz, 6, 1, D) for l in range(2)]
    mods_s = [m[l, bsz:].reshape(nsamp, 6, D).transpose(1, 0, 2)[None] for l in range(2)]

    row = lambda t: t.reshape(1, -1)
    pad_rows = lambda t, lo, hi: jnp.pad(t, ((lo, hi), (0, 0)))
    small = [ab_conv_w[0], row(ab_mu[0]), row(ab_w0[0]),
             pad_rows(ab_w_decay_up[0], 0, 64).astype(BF16), row(ab_a0[0]),
             pad_rows(ab_a_up[0], 64, 0).astype(BF16), ab_g_up[0].astype(BF16),
             row(ab_k_k[0]), row(ab_k_a[0])]
    win0 = ab_w_in[0].astype(BF16)
    wo0 = ab_w_out[0].astype(BF16)
    rk, gng, gnb = row(ab_r_k[0]), row(ab_gn_g[0]), row(ab_gn_b[0])
    wrt = [moe_w_router[l].T for l in range(2)]
    br = [moe_b_router[l].reshape(NE, 1) for l in range(2)]

    h2_all = jnp.zeros((n_all, D), BF16)
    comb_all = jnp.zeros((n_all, 128), F32)

    ya, r, lw, k, v, kk, a, g, ctail, stail = _ab_pre(
        xp, mods_p[0], norm_g[0], win0, small, None, nb=bsz, seq_len=seq_len, tm=tm, seq_mode=True)
    conv_p = ctail.reshape(bsz, 8, DA)[:, 6:8][None]
    shift_p = stail.reshape(bsz, 8, PB)[:, 7][None]
    yb, sfin = _wkv_seq(r, lw, k, v, kk, a, g, rk, gng, gnb, nb=bsz, seq_len=seq_len)
    sfin = sfin.reshape(bsz, nheads // 2, 2, HEAD, 2, HEAD)
    wkv_p = jnp.stack([sfin[:, :, 0, :, 0, :], sfin[:, :, 1, :, 1, :]], axis=2)
    wkv_p = wkv_p.reshape(bsz, nheads, HEAD, HEAD).transpose(0, 1, 3, 2)[None]
    xp1, h2_all, comb_all = _ab_post(ya, yb, xp, mods_p[0], norm_g[0], wo0, wrt[0], br[0], h2_all, comb_all,
                                     nb=bsz, seq_len=seq_len, tm=tm, row0=0)

    prev = [state_conv[0, :, 0], state_conv[0, :, 1], state_shift[0]]
    ya_s, r, lw, k, v, kk, a, g, gated_s, pb_s = _ab_pre(
        xs, mods_s[0], norm_g[0], win0, small, prev, nb=1, seq_len=nsamp, tm=nsamp, seq_mode=False)
    conv_s = jnp.stack([state_conv[0, :, 1], gated_s], axis=1)[None]
    shift_s = pb_s[None]
    vecs = [t.reshape(nsamp * nheads, 1, HEAD) for t in (r, lw, k, v, kk, a, g)]
    params = [t.reshape(nheads, 1, HEAD) for t in (ab_r_k[0], ab_gn_g[0], ab_gn_b[0])]
    yb_s, snew = _wkv_step(state_wkv[0].reshape(nsamp * nheads, HEAD, HEAD), vecs, params, nheads=nheads)
    wkv_s = snew.reshape(1, nsamp, nheads, HEAD, HEAD)
    xs1, h2_all, comb_all = _ab_post(ya_s, yb_s.reshape(nsamp, DB).astype(BF16), xs, mods_s[0], norm_g[0], wo0,
                                     wrt[0], br[0], h2_all, comb_all, nb=1, seq_len=nsamp, tm=nsamp, row0=n_prompt)

    def moe_layer(l, h_all, c_all):
        return _moe(h_all, c_all, moe_w_gate, moe_w_up, moe_w_down,
                    moe_ws_gate, moe_ws_up, moe_ws_down, layer=l, n_tokens=n_prompt + nsamp)

    f0 = moe_layer(0, h2_all, comb_all)

    gw = [gm_w_in[0].astype(BF16), row(gm_b_in[0]), row(gm_ln_g[0]), row(gm_ln_b[0])]
    gw_seq = gw + [gm_w_s[0], gm_b_s[0].T, gm_w_out[0].astype(BF16), wrt[1], br[1]]
    gw_step = gw + [row(jnp.repeat(gm_w_s[0, :, 0, 0], CHUNK)), row(jnp.repeat(gm_b_s[0, :, 0], CHUNK)),
                    gm_w_out[0].astype(BF16), wrt[1], br[1]]
    tm1 = 256
    xp2, h2_all, comb_all = _gmlp(xp1, f0, mods_p[0], norm_g[0], mods_p[1], norm_g[1], gw_seq, h2_all, comb_all,
                                  nb=bsz, seq_len=seq_len, tm=tm1, row0=0, seq_mode=True)
    xs2, h2_all, comb_all, chunk_v = _gmlp(xs1, f0, mods_s[0], norm_g[0], mods_s[1], norm_g[1], gw_step, h2_all,
                                           comb_all, nb=1, seq_len=nsamp, tm=nsamp, row0=n_prompt, seq_mode=False)
    f1 = moe_layer(1, h2_all, comb_all)
    y_p = _final(xp2, f1, mods_p[1], norm_g[1], nb=bsz, seq_len=seq_len, tm=tm, row0=0)
    y_s = _final(xs2, f1, mods_s[1], norm_g[1], nb=1, seq_len=nsamp, tm=nsamp, row0=n_prompt)

    return (y_p.reshape(bsz, seq_len, D), y_s.reshape(nsamp, 1, D), conv_p, conv_s, shift_p, shift_s,
            wkv_p, wkv_s, chunk_v.reshape(1, nsamp, 1, D))
```

```python
import functools

import jax
import jax.numpy as jnp
from jax import lax
from jax.experimental import pallas as pl
from jax.experimental.pallas import tpu as pltpu

F32 = jnp.float32
BF16 = jnp.bfloat16
HI = lax.Precision.HIGHEST

D = 1024
DA = 512
DB = 512
HEAD = 64
PB = 1792
PAB = 3 * DA + PB
NE = 64
NG = 8
EPG = NE // NG
TOPG = 4
TOPK = 8
DE = 256
ROUTE_SCALE = 2.5
EPS = 1e-6
GN_EPS = 64e-5
CHUNK = 128
WKV_SLAB = 128
WKV_CHUNK = 32
WKV_PAIRS_PER_STEP = 4
WKV_P_STATE = 1
WKV_P_INV = 1
MOE_TD = 512
MOE_SEG = 16
MOE_LS = 5120
MOE_CH = 1024
MOE_PIECES = (128, 64, 32, 16)
MOE_TR = 512
MOE_XW = D + 128
PLAN_ROWS = 128

VMEM_LIMIT = 56 * 1024 * 1024


def _cparams(sem):
    return pltpu.CompilerParams(dimension_semantics=sem, vmem_limit_bytes=VMEM_LIMIT)


def _dot(a, b, precision=None):
    return jnp.dot(a, b, preferred_element_type=F32, precision=precision)


def _dot_nt(a, b, precision=None):
    return lax.dot_general(a, b, (((1,), (1,)), ((), ())), preferred_element_type=F32, precision=precision)


def _iota(shape, dim):
    return lax.broadcasted_iota(jnp.int32, shape, dim)


def _rms(x, g):
    return x * lax.rsqrt(jnp.mean(x * x, axis=-1, keepdims=True) + EPS) * g


def _silu(x):
    return x * jax.nn.sigmoid(x)


def _seg_ones(n, seg):
    return (_iota((n, n), 0) // seg == _iota((n, n), 1) // seg).astype(F32)


def _mod_kernel(c_ref, w_ref, b_ref, o_ref):
    s = _silu(c_ref[...]).astype(BF16)
    o_ref[...] = _dot(s, w_ref[...].astype(BF16)) + b_ref[...]


def _modulation(c_all, w_mod, b_mod):
    depth, _, width = w_mod.shape
    rows = c_all.shape[0]
    bn = 512
    return pl.pallas_call(
        _mod_kernel,
        grid=(depth, width // bn),
        in_specs=[pl.BlockSpec((rows, D), lambda l, j: (0, 0)),
                  pl.BlockSpec((None, D, bn), lambda l, j: (l, 0, j)),
                  pl.BlockSpec((None, 1, bn), lambda l, j: (l, 0, j))],
        out_specs=pl.BlockSpec((None, rows, bn), lambda l, j: (l, 0, j)),
        out_shape=jax.ShapeDtypeStruct((depth, rows, width), F32),
        compiler_params=_cparams(("arbitrary", "arbitrary")),
        name="modulation",
    )(c_all, w_mod, b_mod.reshape(depth, 1, width))


def _residual_in(x_ref, f_ref, pmod_ref, png_ref):
    x = x_ref[...]
    if f_ref is None:
        return x
    return x + pmod_ref[5] * _rms(f_ref[...], png_ref[3:4, :])


def _route(h, wrt_ref, br_ref):
    tm = h.shape[0]
    logits = _mm(wrt_ref[...], h, 2, 2, nt=True)
    scores = jax.nn.sigmoid(logits)
    sel = scores + br_ref[...]
    sub = _iota((EPG, tm), 0).astype(F32)
    neg = jnp.float32(-jnp.inf)
    blks, sblks, gscore = [], [], []
    for g in range(NG):
        blk = sel[g * EPG:(g + 1) * EPG, :]
        m1 = jnp.max(blk, axis=0, keepdims=True)
        first = jnp.min(jnp.where(blk == m1, sub, float(EPG)), axis=0, keepdims=True)
        m2 = jnp.max(jnp.where(sub == first, neg, blk), axis=0, keepdims=True)
        blks.append(blk)
        sblks.append(scores[g * EPG:(g + 1) * EPG, :])
        gscore.append(m1 + m2)
    masked = []
    for g in range(NG):
        cnt = jnp.zeros((1, tm), F32)
        for g2 in range(NG):
            if g2 == g:
                continue
            beats = (gscore[g2] >= gscore[g]) if g2 < g else (gscore[g2] > gscore[g])
            cnt = cnt + jnp.where(beats, 1.0, 0.0)
        masked.append(jnp.where(cnt < TOPG, blks[g], neg))
    cnts = [jnp.zeros((EPG, tm), F32) for _ in range(NG)]
    for g2 in range(NG):
        for j in range(EPG):
            row = masked[g2][j:j + 1, :]
            for g in range(NG):
                ge = jnp.where(row >= masked[g], 1.0, 0.0)
                gt = jnp.where(row > masked[g], 1.0, 0.0)
                if g > g2:
                    cnts[g] = cnts[g] + ge
                elif g < g2:
                    cnts[g] = cnts[g] + gt
                else:
                    cnts[g] = cnts[g] + jnp.where(sub > j, ge, gt)
    wts = [jnp.where(cnts[g] < TOPK, sblks[g], 0.0) for g in range(NG)]
    tot = wts[0]
    for g in range(1, NG):
        tot = tot + wts[g]
    denom = jnp.sum(tot, axis=0, keepdims=True)
    comb_t = jnp.concatenate([w / denom * ROUTE_SCALE for w in wts] + [jnp.zeros((NE, tm), F32)], axis=0)
    return comb_t.T


def _post(y, x, mod_ref, ng_ref, wrt_ref, br_ref, xo_ref, h2_ref, comb_ref):
    xn = x + mod_ref[2] * _rms(y, ng_ref[1:2, :])
    xo_ref[...] = xn
    h2 = _rms(xn, ng_ref[2:3, :]) * (1.0 + mod_ref[4]) + mod_ref[3]
    h2_ref[...] = h2.astype(BF16)
    comb_ref[...] = _route(h2, wrt_ref, br_ref)


def _ab_pre_kernel(*refs, seq_mode, tm):
    it = iter(refs)
    x_ref, mod_ref, ng_ref, win_ref = next(it), next(it), next(it), next(it)
    cw_ref, mu_ref, w0_ref, wd_ref, a0_ref, wa_ref, wg_ref, kk_ref, ka_ref = (next(it) for _ in range(9))
    if not seq_mode:
        p2_ref, p1_ref, ps_ref = next(it), next(it), next(it)
    ya_ref, r_ref, lw_ref, k_ref, v_ref, kkn_ref, a_ref, g_ref, ctail_ref, stail_ref = (next(it) for _ in range(10))
    if seq_mode:
        cc_ref, sc_ref = next(it), next(it)

        @pl.when(pl.program_id(1) == 0)
        def _():
            cc_ref[...] = jnp.zeros_like(cc_ref)
            sc_ref[...] = jnp.zeros_like(sc_ref)

    x = x_ref[...]
    h = _rms(x, ng_ref[0:1, :]) * (1.0 + mod_ref[1]) + mod_ref[0]
    proj = _dot(h.astype(BF16), win_ref[...])
    a_h, a_b, a_c = proj[:, 0:DA], proj[:, DA:2 * DA], proj[:, 2 * DA:3 * DA]
    pb = proj[:, 3 * DA:]
    gated = a_c * a_h
    if seq_mode:
        rows = _iota((tm, 1), 0)
        g1 = jnp.where(rows == 0, cc_ref[7:8, :], pltpu.roll(gated, 1, 0))
        g2 = jnp.where(rows == 0, cc_ref[6:7, :], jnp.where(rows == 1, cc_ref[7:8, :], pltpu.roll(gated, 2, 0)))
        pprev = jnp.where(rows == 0, sc_ref[7:8, :], pltpu.roll(pb, 1, 0))
        cc_ref[...] = gated[tm - 8:, :]
        sc_ref[...] = pb[tm - 8:, :]
        ctail_ref[...] = gated[tm - 8:, :]
        stail_ref[...] = pb[tm - 8:, :]
    else:
        g1, g2, pprev = p1_ref[...], p2_ref[...], ps_ref[...]
        ctail_ref[...] = gated
        stail_ref[...] = pb
    conv = g2 * cw_ref[0:1, :] + g1 * cw_ref[1:2, :] + gated * cw_ref[2:3, :]
    ya_ref[...] = (a_b * conv).astype(BF16)

    xm = pb + (pprev - pb) * mu_ref[...]
    r, k, v = xm[:, 0:DB], xm[:, DB:2 * DB], xm[:, 2 * DB:3 * DB]
    lowrank = xm[:, 3 * DB:3 * DB + 128]
    dg = xm[:, 3 * DB + 128:]
    w_pre = w0_ref[...] + _dot(jnp.tanh(lowrank).astype(BF16), wd_ref[...])
    z = -w_pre
    softplus = jnp.maximum(z, 0.0) + jnp.log(1.0 + jnp.exp(-jnp.abs(z)))
    lw_ref[...] = -jnp.exp(-softplus - 0.5)
    a = jax.nn.sigmoid(a0_ref[...] + _dot(lowrank.astype(BF16), wa_ref[...]))
    g_ref[...] = _dot(jax.nn.sigmoid(dg).astype(BF16), wg_ref[...])
    kk = k * kk_ref[...]
    ss = _mm(kk * kk, _seg_ones(DB, HEAD), 2, 1)
    kkn_ref[...] = kk / jnp.maximum(jnp.sqrt(ss), 1e-12)
    r_ref[...] = r
    k_ref[...] = k * (1.0 + (a - 1.0) * ka_ref[...])
    v_ref[...] = v
    a_ref[...] = a


def _ab_pre(x, mods, ng, win, small, prev, *, nb, seq_len, tm, seq_mode):
    nt = seq_len // tm
    rows = nb * seq_len
    tail = 8 if seq_mode else tm
    full = lambda shape: pl.BlockSpec(shape, lambda b, t: tuple(0 for _ in shape))
    tok = lambda w: pl.BlockSpec((tm, w), lambda b, t: (b * nt + t, 0))
    in_specs = [tok(D),
                pl.BlockSpec((None,) + mods.shape[1:], lambda b, t: (b, 0, 0, 0)),
                full(ng.shape), full(win.shape)] + [full(s.shape) for s in small]
    args = [x, mods, ng, win] + list(small)
    if not seq_mode:
        in_specs += [tok(DA), tok(DA), tok(PB)]
        args += list(prev)
    out_shape = ([jax.ShapeDtypeStruct((rows, DA), BF16)] + [jax.ShapeDtypeStruct((rows, DB), F32)] * 7
                 + [jax.ShapeDtypeStruct((nb * tail, DA), F32), jax.ShapeDtypeStruct((nb * tail, PB), F32)])
    out_specs = ([tok(DA)] + [tok(DB)] * 7
                 + [pl.BlockSpec((tail, DA), lambda b, t: (b, 0)), pl.BlockSpec((tail, PB), lambda b, t: (b, 0))])
    scratch = [pltpu.VMEM((8, DA), F32), pltpu.VMEM((8, PB), F32)] if seq_mode else []
    return pl.pallas_call(
        functools.partial(_ab_pre_kernel, seq_mode=seq_mode, tm=tm),
        grid=(nb, nt), in_specs=in_specs, out_specs=out_specs, out_shape=out_shape,
        scratch_shapes=scratch,
        compiler_params=_cparams(("arbitrary", "arbitrary")),
        name="ab_pre_seq" if seq_mode else "ab_pre_step",
    )(*args)


def _split(x):
    hi = x.astype(BF16)
    return hi, (x - hi.astype(F32)).astype(BF16)


def _mm(a, b, pa=1, pb=1, nt=False):
    ah, al = _split(a) if pa == 2 else (a.astype(BF16), None)
    bh, bl = _split(b) if pb == 2 else (b.astype(BF16), None)
    bx = 1 if nt else 0
    if pa == 2 and pb == 2:
        ah, bh = jnp.concatenate([ah, al, ah], axis=1), jnp.concatenate([bh, bh, bl], axis=bx)
    elif pa == 2:
        ah, bh = jnp.concatenate([ah, al], axis=1), jnp.concatenate([bh, bh], axis=bx)
    elif pb == 2:
        ah, bh = jnp.concatenate([ah, ah], axis=1), jnp.concatenate([bh, bl], axis=bx)
    return _dot_nt(ah, bh) if nt else _dot(ah, bh)


def _gn_gate(o, r, k, v, g, rk, gng, gnb, seg):
    mean = _mm(o, seg, 2, 1) * (1.0 / HEAD)
    d = o - mean
    var = _mm(d * d, seg, 2, 1) * (1.0 / HEAD)
    out = d * lax.rsqrt(var + GN_EPS) * gng + gnb
    out = out + _mm(r * k * rk, seg, 2, 1) * v
    return out * g


def _wkv_pair(r, lw, k, v, kk, a, st, cst):
    ts, c = WKV_SLAB, WKV_CHUNK
    nchunk = ts // c
    ps, pi = WKV_P_STATE, WKV_P_INV
    strict, incl, cum_lhs, m0, m1, headdiag, eye, eye2, blks, chunk_cols = cst

    l1 = lw.astype(BF16)
    rem = lw - l1.astype(F32)
    l2 = rem.astype(BF16)
    l3 = (rem - l2.astype(F32)).astype(BF16)
    cums = _dot(cum_lhs, jnp.concatenate([l1, l2, l3], axis=0))
    yield
    cl, ctot = cums[:ts], cums[ts:]
    e_neg = jnp.exp(-cl)
    e_end = jnp.exp(ctot - cl)
    at = -kk * jnp.exp(cl - lw)
    rt = r * jnp.exp(cl)
    beta = kk * a
    bk = jnp.concatenate([beta * e_neg, k * e_neg], axis=0)
    bend, kend = beta * e_end, k * e_end

    at_st = jnp.concatenate([at * m0, at * m1], axis=0)
    a_st = _mm(at_st, bk, ps, ps, nt=True)
    a_out = _mm(jnp.concatenate([rt * m0, rt * m1], axis=0), bk, 1, 1, nt=True)
    yield
    zero = jnp.zeros((ts, ts), F32)

    def two_heads(p0, p1):
        return jnp.concatenate([jnp.concatenate([p0, zero], axis=1), jnp.concatenate([zero, p1], axis=1)], axis=0)

    lbd = two_heads(a_st[:ts, :ts] * strict, a_st[ts:, :ts] * strict)
    d1 = jnp.where(blks[0], lbd, 0.0)
    tinv = eye2 + d1
    d2 = _mm(d1, d1, pi, pi)
    yield
    tinv = tinv + _mm(tinv, d2, pi, pi)
    d4 = _mm(d2, d2, pi, pi)
    yield
    tinv = tinv + _mm(tinv, d4, pi, pi)
    yield
    for lvl in range(1, len(blks)):
        off = jnp.where(blks[lvl] & jnp.logical_not(blks[lvl - 1]), lbd, 0.0)
        half = _mm(tinv, off, pi, pi)
        yield
        tinv = tinv + _mm(half, tinv, pi, pi)
        yield

    v_st = jnp.concatenate([v * m0, v * m1], axis=0)
    y_st = _mm(two_heads(a_st[:ts, ts:] * strict, a_st[ts:, ts:] * strict), v_st, ps, ps)
    yield
    wu = _mm(tinv, jnp.concatenate([at_st, y_st], axis=1), ps, ps)
    yield
    w =wu[:ts, :2 * HEAD] + wu[ts:, :2 * HEAD]
    u = wu[:ts, 2 * HEAD:] + wu[ts:, 2 * HEAD:]
    rhs = jnp.concatenate([jnp.concatenate([w, u], axis=1),
                           jnp.concatenate([jnp.zeros_like(v), v], axis=1)], axis=0)
    bend_t, kend_t = bend.T, kend.T
    lhs = [jnp.concatenate([bend_t * cm, kend_t * cm], axis=1) for cm in chunk_cols]
    mn_all = _mm(jnp.concatenate(lhs, axis=0), rhs, ps, ps)
    lhs_q = jnp.concatenate([jnp.concatenate([a_out[:ts, :ts] * incl, a_out[:ts, ts:] * incl], axis=1),
                             jnp.concatenate([a_out[ts:, :ts] * incl, a_out[ts:, ts:] * incl], axis=1)], axis=0)
    qo = _mm(lhs_q, rhs, 1, 1)
    yield
    q =rt + qo[:ts, :2 * HEAD] * m0 + qo[ts:, :2 * HEAD] * m1
    olocal = qo[:ts, 2 * HEAD:] * m0 + qo[ts:, 2 * HEAD:] * m1

    outs = []
    for ch in range(nchunk):
        mn = mn_all[ch * ts:(ch + 1) * ts]
        outs.append(_mm(q[ch * c:(ch + 1) * c], st, 1, 1) + olocal[ch * c:(ch + 1) * c])
        trans = eye * jnp.exp(ctot[ch * c:ch * c + 1, :]) + mn[:, :2 * HEAD] * headdiag
        st = _mm(trans, st, ps, ps) + mn[:, 2 * HEAD:] * headdiag
        yield
    return jnp.concatenate(outs, axis=0), st


def _round_robin(gens):
    results = [None] * len(gens)
    active = list(enumerate(gens))
    while active:
        still = []
        for i, gen in active:
            try:
                next(gen)
                still.append((i, gen))
            except StopIteration as stop:
                results[i] = stop.value
        active = still
    return results


def _wkv_seq_kernel(r_ref, lw_ref, k_ref, v_ref, kk_ref, a_ref, g_ref, rk_ref, gng_ref, gnb_ref,
                    y_ref, sfin_ref, st_ref, *, npp):
    ts, c = WKV_SLAB, WKV_CHUNK

    @pl.when(pl.program_id(2) == 0)
    def _():
        st_ref[...] = jnp.zeros_like(st_ref)

    ri, ci = _iota((ts, ts), 0), _iota((ts, ts), 1)
    same = ri // c == ci // c
    strict = (same & (ci < ri)).astype(F32)
    incl = (same & (ci <= ri)).astype(F32)
    cum_lhs = jnp.concatenate([incl, same.astype(F32)], axis=0).astype(BF16)
    cum_lhs = jnp.concatenate([cum_lhs] * 3, axis=1)
    lane = _iota((1, 2 * HEAD), 1)
    m0 = (lane < HEAD).astype(F32)
    m1 = 1.0 - m0
    headdiag = _seg_ones(2 * HEAD, HEAD)
    eye = (_iota((2 * HEAD, 2 * HEAD), 0) == _iota((2 * HEAD, 2 * HEAD), 1)).astype(F32)
    r2, c2 = _iota((2 * ts, 2 * ts), 0), _iota((2 * ts, 2 * ts), 1)
    eye2 = (r2 == c2).astype(F32)
    blks, b = [], 8
    while b <= c:
        blks.append(r2 // b == c2 // b)
        b *= 2
    tcol = _iota((1, ts), 1) // c
    chunk_cols = [(tcol == ch).astype(F32) for ch in range(ts // c)]
    cst = (strict, incl, cum_lhs, m0, m1, headdiag, eye, eye2, blks, chunk_cols)

    sls = [slice(p * 2 * HEAD, (p + 1) * 2 * HEAD) for p in range(npp)]
    res = _round_robin([_wkv_pair(r_ref[:, sl], lw_ref[:, sl], k_ref[:, sl], v_ref[:, sl], kk_ref[:, sl],
                                  a_ref[:, sl], st_ref[p], cst) for p, sl in enumerate(sls)])
    for p, sl in enumerate(sls):
        o, st = res[p]
        st_ref[p] = st
        y_ref[:, sl] = _gn_gate(o, r_ref[:, sl], k_ref[:, sl], v_ref[:, sl], g_ref[:, sl], rk_ref[:, sl],
                                gng_ref[:, sl], gnb_ref[:, sl], headdiag).astype(BF16)

    @pl.when(pl.program_id(2) == pl.num_programs(2) - 1)
    def _():
        sfin_ref[...] = st_ref[...]


def _wkv_seq(r, lw, k, v, kk, a, g, rk, gng, gnb, *, nb, seq_len):
    ns = seq_len // WKV_SLAB
    npp = WKV_PAIRS_PER_STEP
    width = 2 * HEAD * npp
    tok = pl.BlockSpec((WKV_SLAB, width), lambda b, p, s: (b * ns + s, p))
    par = pl.BlockSpec((1, width), lambda b, p, s: (0, p))
    return pl.pallas_call(
        functools.partial(_wkv_seq_kernel, npp=npp),
        grid=(nb, DB // width, ns),
        in_specs=[tok] * 7 + [par] * 3,
        out_specs=[tok, pl.BlockSpec((None, npp, 2 * HEAD, 2 * HEAD), lambda b, p, s: (b, p, 0, 0))],
        out_shape=[jax.ShapeDtypeStruct((nb * seq_len, DB), BF16),
                   jax.ShapeDtypeStruct((nb, DB // (2 * HEAD), 2 * HEAD, 2 * HEAD), F32)],
        scratch_shapes=[pltpu.VMEM((npp, 2 * HEAD, 2 * HEAD), F32)],
        compiler_params=_cparams(("arbitrary", "arbitrary", "arbitrary")),
        name="wkv_seq",
    )(r, lw, k, v, kk, a, g, rk, gng, gnb)


def _wkv_step_kernel(s_ref, r_ref, lw_ref, k_ref, v_ref, kk_ref, a_ref, g_ref, rk_ref, gng_ref, gnb_ref,
                     y_ref, so_ref):
    s = s_ref[...]
    r, k, v, kk, a = r_ref[...], k_ref[...], v_ref[...], kk_ref[...], a_ref[...]
    w = jnp.exp(lw_ref[...])
    eye = (_iota((1, HEAD, HEAD), 1) == _iota((1, HEAD, HEAD), 2)).astype(F32)
    v_col = jnp.sum(eye * v, axis=2, keepdims=True)
    s_kk = jnp.sum(s * kk, axis=2, keepdims=True)
    s = s * w - s_kk * (kk * a) + v_col * k
    so_ref[...] = s
    o_col = jnp.sum(s * r, axis=2, keepdims=True)
    o = jnp.sum(eye * o_col, axis=1, keepdims=True)
    mean = jnp.mean(o, axis=2, keepdims=True)
    d = o - mean
    var = jnp.mean(d * d, axis=2, keepdims=True)
    out = d * lax.rsqrt(var + GN_EPS) * gng_ref[...] + gnb_ref[...]
    out = out + jnp.sum(r * k * rk_ref[...], axis=2, keepdims=True) * v
    y_ref[...] = out * g_ref[...]


def _wkv_step(state, vecs, params, *, nheads):
    n = state.shape[0]
    tb = 8 * nheads
    sspec = pl.BlockSpec((tb, HEAD, HEAD), lambda i: (i, 0, 0))
    vspec = pl.BlockSpec((tb, 1, HEAD), lambda i: (i, 0, 0))
    pspec = pl.BlockSpec((tb, 1, HEAD), lambda i: (0, 0, 0))
    reps = tb // nheads
    params = [jnp.tile(p, (reps, 1, 1)) for p in params]
    return pl.pallas_call(
        _wkv_step_kernel,
        grid=(n // tb,),
        in_specs=[sspec] + [vspec] * 7 + [pspec] * 3,
        out_specs=[vspec, sspec],
        out_shape=[jax.ShapeDtypeStruct((n, 1, HEAD), F32), jax.ShapeDtypeStruct((n, HEAD, HEAD), F32)],
        compiler_params=_cparams(("arbitrary",)),
        name="wkv_step",
    )(state, *vecs, *params)


def _ab_post_kernel(ya_ref, yb_ref, x_ref, mod_ref, ng_ref, wo_ref, wrt_ref, br_ref,
                    h2in_ref, combin_ref, xo_ref, h2_ref, comb_ref):
    del h2in_ref, combin_ref
    y = _dot(ya_ref[...], wo_ref[0:DA, :]) + _dot(yb_ref[...], wo_ref[DA:, :])
    _post(y, x_ref[...], mod_ref, ng_ref, wrt_ref, br_ref, xo_ref, h2_ref, comb_ref)


def _ab_post(ya, yb, x, mods, ng, wo, wrt, br, h2_all, comb_all, *, nb, seq_len, tm, row0):
    nt = seq_len // tm
    t0 = row0 // tm
    full = lambda shape: pl.BlockSpec(shape, lambda b, t: tuple(0 for _ in shape))
    tok = lambda w: pl.BlockSpec((tm, w), lambda b, t: (b * nt + t, 0))
    tok_all = lambda w: pl.BlockSpec((tm, w), lambda b, t: (t0 + b * nt + t, 0))
    anyspec = pl.BlockSpec(memory_space=pl.ANY)
    return pl.pallas_call(
        _ab_post_kernel,
        grid=(nb, nt),
        in_specs=[tok(DA), tok(DB), tok(D),
                  pl.BlockSpec((None,) + mods.shape[1:], lambda b, t: (b, 0, 0, 0)),
                  full(ng.shape), full(wo.shape), full(wrt.shape), full(br.shape), anyspec, anyspec],
        out_specs=[tok(D), tok_all(D), tok_all(128)],
        out_shape=[jax.ShapeDtypeStruct(x.shape, F32), jax.ShapeDtypeStruct(h2_all.shape, BF16),
                   jax.ShapeDtypeStruct(comb_all.shape, F32)],
        input_output_aliases={8: 1, 9: 2},
        compiler_params=_cparams(("arbitrary", "arbitrary")),
        name="ab_post",
    )(ya, yb, x, mods, ng, wo, wrt, br, h2_all, comb_all)


def _gmlp_kernel(*refs, seq_mode, tm):
    it = iter(refs)
    x_ref, f_ref, pmod_ref, png_ref, mod_ref, ng_ref = (next(it) for _ in range(6))
    win_ref, bin_ref, lng_ref, lnb_ref, ws_ref, bs_ref, wo_ref, wrt_ref, br_ref = (next(it) for _ in range(9))
    next(it), next(it)
    xo_ref, h2_ref, comb_ref = next(it), next(it), next(it)
    if not seq_mode:
        cv_ref = next(it)
    x = _residual_in(x_ref, f_ref, pmod_ref, png_ref)
    h = _rms(x, ng_ref[0:1, :]) * (1.0 + mod_ref[1]) + mod_ref[0]
    z = jax.nn.gelu(_dot(h.astype(BF16), win_ref[...]) + bin_ref[...])
    u, v = z[:, :D], z[:, D:]
    mu = jnp.mean(v, axis=-1, keepdims=True)
    var = jnp.mean(jnp.square(v - mu), axis=-1, keepdims=True)
    v = (v - mu) * lax.rsqrt(var + EPS) * lng_ref[...] + lnb_ref[...]
    if seq_mode:
        vb = v.astype(BF16)
        causal = _iota((CHUNK, CHUNK), 1) <= _iota((CHUNK, CHUNK), 0)
        cols = []
        for g in range(D // CHUNK):
            wsg = jnp.where(causal, ws_ref[g], 0.0).astype(BF16)
            bsg = bs_ref[:, g:g + 1]
            rows = [_dot(wsg, vb[c * CHUNK:(c + 1) * CHUNK, g * CHUNK:(g + 1) * CHUNK]) + bsg
                    for c in range(tm // CHUNK)]
            cols.append(jnp.concatenate(rows, axis=0) if len(rows) > 1 else rows[0])
        s = jnp.concatenate(cols, axis=1)
    else:
        cv_ref[...] = v
        s = v * ws_ref[...] + bs_ref[...]
    y = _dot((u * s).astype(BF16), wo_ref[...])
    _post(y, x, mod_ref, ng_ref, wrt_ref, br_ref, xo_ref, h2_ref, comb_ref)


def _gmlp(x, f_all, pmods, png, mods, ng, weights, h2_all, comb_all, *, nb, seq_len, tm, row0, seq_mode):
    nt = seq_len // tm
    t0 = row0 // tm
    full = lambda shape: pl.BlockSpec(shape, lambda b, t: tuple(0 for _ in shape))
    tok = lambda w: pl.BlockSpec((tm, w), lambda b, t: (b * nt + t, 0))
    tok_all = lambda w: pl.BlockSpec((tm, w), lambda b, t: (t0 + b * nt + t, 0))
    modspec = lambda m: pl.BlockSpec((None,) + m.shape[1:], lambda b, t: (b, 0, 0, 0))
    anyspec = pl.BlockSpec(memory_space=pl.ANY)
    in_specs = ([tok(D), tok_all(D), modspec(pmods), full(png.shape), modspec(mods), full(ng.shape)]
                + [full(w.shape) for w in weights] + [anyspec, anyspec])
    out_specs = [tok(D), tok_all(D), tok_all(128)]
    out_shape = [jax.ShapeDtypeStruct(x.shape, F32), jax.ShapeDtypeStruct(h2_all.shape, BF16),
                 jax.ShapeDtypeStruct(comb_all.shape, F32)]
    if not seq_mode:
        out_specs.append(tok(D))
        out_shape.append(jax.ShapeDtypeStruct(x.shape, F32))
    n_in = len(in_specs)
    return pl.pallas_call(
        functools.partial(_gmlp_kernel, seq_mode=seq_mode, tm=tm),
        grid=(nb, nt), in_specs=in_specs, out_specs=out_specs, out_shape=out_shape,
        input_output_aliases={n_in - 2: 1, n_in - 1: 2},
        compiler_params=_cparams(("arbitrary", "arbitrary")),
        name="gmlp_seq" if seq_mode else "gmlp_step",
    )(x, f_all, pmods, png, mods, ng, *weights, h2_all, comb_all)


def _strict_lower(n):
    return _iota((n, n), 1) < _iota((n, n), 0)


def _col_of_row(row):
    return jnp.broadcast_to(row, (128, 128)).T[:, 0:1]


def _plan_kernel(comb_ref, dst_ref, lo_ref, n_ref, te_ref, misc_ref, lof_ref, pieces_ref, cnt_ref, *, ntile, te_cols):
    i = pl.program_id(0)

    @pl.when(i == 0)
    def _():
        cnt_ref[...] = jnp.zeros_like(cnt_ref)

    sel = jnp.where(comb_ref[...] > 0.0, 1.0, 0.0)
    cnt_ref[pl.ds(i, 1), :] = jnp.sum(sel, axis=0, keepdims=True)

    @pl.when(i == ntile - 1)
    def _():
        c = cnt_ref[...]
        cpad = jnp.floor((c + (MOE_SEG - 1.0)) * (1.0 / MOE_SEG)) * MOE_SEG
        cb = cpad.astype(BF16)
        lower = jnp.where(_strict_lower(PLAN_ROWS), 1.0, 0.0).astype(BF16)
        upper = jnp.where(_iota((128, 128), 0) < _iota((128, 128), 1), 1.0, 0.0).astype(BF16)
        before = _dot(lower, cb)
        lo = _dot(cb, upper)
        tot = jnp.sum(cpad, axis=0, keepdims=True)
        rt = jnp.floor((tot + (MOE_TR - 1.0)) * (1.0 / MOE_TR))
        gt = _dot(jnp.broadcast_to(rt, (8, 128)).astype(BF16), upper)[0:1, :]
        dst_ref[...] = (gt * MOE_TR + before).astype(jnp.int32)
        lo_ref[...] = lo.astype(jnp.int32)
        lof_ref[...] = lo
        n_ref[...] = cpad.astype(jnp.int32)
        ends_col = _col_of_row(gt + rt)
        jrow = _iota((1, te_cols), 1).astype(F32)
        te = jnp.sum(jnp.where(ends_col <= jrow, 1.0, 0.0), axis=0, keepdims=True)
        te_ref[...] = jnp.broadcast_to(jnp.minimum(te, NE - 1.0), (8, te_cols)).astype(jnp.int32)
        nused = jnp.sum(rt, axis=1, keepdims=True)
        rowi = _iota((8, 128), 0)
        misc = jnp.where(rowi == 0, gt * MOE_TR + tot, jnp.where(rowi == 1, rt * MOE_TR - tot, nused))
        misc_ref[...] = misc.astype(jnp.int32)
        lane = _iota((1, 128), 1)
        pieces = jnp.zeros((PLAN_ROWS, 128), F32)
        taken = jnp.zeros_like(cpad)
        for k, size in enumerate(MOE_PIECES):
            cnt = jnp.floor((cpad - taken) * (1.0 / size))
            taken = taken + cnt * size
            pieces = pieces + jnp.where(lane == k, jnp.sum(cnt, axis=1, keepdims=True), 0.0)
        pieces_ref[...] = pieces.astype(jnp.int32)


def _moe_plan(comb_all, *, ntmax):
    ntile = comb_all.shape[0] // MOE_TD
    te_cols = -(-ntmax // 128) * 128
    full = lambda shape: pl.BlockSpec(shape, lambda i: tuple(0 for _ in shape))
    shapes = [((PLAN_ROWS, 128), jnp.int32)] * 3 + [((8, te_cols), jnp.int32), ((8, 128), jnp.int32),
                                                   ((PLAN_ROWS, 128), F32), ((PLAN_ROWS, 128), jnp.int32)]
    return pl.pallas_call(
        functools.partial(_plan_kernel, ntile=ntile, te_cols=te_cols),
        grid=(ntile,),
        in_specs=[pl.BlockSpec((MOE_TD, 128), lambda i: (i, 0))],
        out_specs=[full(s) for s, _ in shapes],
        out_shape=[jax.ShapeDtypeStruct(s, d) for s, d in shapes],
        scratch_shapes=[pltpu.VMEM((PLAN_ROWS, 128), F32)],
        compiler_params=_cparams(("arbitrary",)),
        name="moe_plan",
    )(comb_all)


def _rank_plus(comb):
    sel = jnp.where(comb > 0.0, 1.0, 0.0)
    lower = jnp.where(_strict_lower(MOE_TD), 1.0, 0.0).astype(BF16)
    return sel * (_dot(lower, sel.astype(BF16)) + 1.0)


def _for_each_piece(n, fn):
    big = MOE_PIECES[0]
    nbig = n // big

    def body(p, carry):
        fn(p * big, big)
        return carry

    lax.fori_loop(0, nbig, body, 0)
    off = nbig * big
    for size in MOE_PIECES[1:]:
        bit = ((n - off) // size) > 0

        @pl.when(bit)
        def _(off=off, size=size):
            fn(off, size)

        off = off + jnp.where(bit, size, 0)


def _wait_pieces(pieces_s, tile, make_copy):
    for k, size in enumerate(MOE_PIECES):
        def wait_one(p, carry, size=size):
            make_copy(size).wait()
            return carry

        lax.fori_loop(0, pieces_s[tile * 128 + k], wait_one, 0)


def _dispatch_kernel(dst_s, lo_s, n_s, misc_s, pieces_s, h_ref, comb_ref, lof_ref, nf_ref, xs_hbm, buf, zbuf, sem,
                     tsem, *, ntile):
    i = pl.program_id(0)
    slot = lax.rem(i, 2)
    comb = comb_ref[...]
    lo_row = lof_ref[...]
    hi_row = lo_row + nf_ref[...]
    rp_hi, rp_lo = _split(_rank_plus(comb).T)
    l_hi, l_lo = _split(jnp.broadcast_to(_col_of_row(lo_row * (1.0 / MOE_SEG)), (128, 128)))
    rhs_meta = jnp.concatenate([jnp.concatenate([rp_hi, l_hi], axis=1), jnp.concatenate([rp_lo, l_lo], axis=1)], axis=0)
    c_hi, c_lo = _split(comb)
    h_ext = jnp.concatenate([h_ref[...], c_hi, c_lo], axis=1)
    ones2 = jnp.ones((256, 128), BF16)
    lane_lo = _iota((1, 128), 1) < 64
    for ch in range(MOE_LS // MOE_CH):
        srow = (_iota((MOE_CH, 1), 0) + ch * MOE_CH).astype(F32)
        e_t = jnp.where((srow >= lo_row) & (srow < hi_row), 1.0, 0.0)
        e_tb = e_t.astype(BF16)
        meta = _dot(jnp.concatenate([e_tb, e_tb], axis=1), rhs_meta)
        target = srow + 1.0 - meta[:, MOE_TD:] * MOE_SEG
        target = jnp.concatenate([target] * (MOE_TD // 128), axis=1)
        perm = jnp.where(meta[:, :MOE_TD] == target, 1.0, 0.0).astype(BF16)
        rows = pl.ds(ch * MOE_CH, MOE_CH)
        sorted_rows = _dot(perm, h_ext)
        buf[slot, rows, :D] = sorted_rows[:, :D].astype(BF16)
        z_hi, z_lo = _split((sorted_rows[:, D:D + 128] + sorted_rows[:, D + 128:]) * e_t)
        w_hi, w_lo = _split(_dot(jnp.concatenate([z_hi, z_lo], axis=1), ones2))
        buf[slot, rows, D:] = jnp.where(lane_lo, w_hi, w_lo)

    def copy_out(sl, src_row, dst_row, size):
        return pltpu.make_async_copy(buf.at[sl, pl.ds(pl.multiple_of(src_row, MOE_SEG), size)],
                                     xs_hbm.at[pl.ds(pl.multiple_of(dst_row, MOE_SEG), size)], sem.at[sl])

    def start_seg(e, carry):
        idx = i * 128 + e
        lo, dst = lo_s[idx], dst_s[idx]
        _for_each_piece(n_s[idx], lambda off, size: copy_out(slot, lo + off, dst + off, size).start())
        return carry

    lax.fori_loop(0, NE, start_seg, 0)

    @pl.when(i > 0)
    def _():
        _wait_pieces(pieces_s, i - 1, lambda size: copy_out(1 - slot, 0, 0, size))

    @pl.when(i == ntile - 1)
    def _():
        _wait_pieces(pieces_s, i, lambda size: copy_out(slot, 0, 0, size))
        zbuf[...] = jnp.zeros_like(zbuf)

        def tail_copy(e, off, size):
            return pltpu.make_async_copy(
                zbuf.at[pl.ds(0, size)], xs_hbm.at[pl.ds(pl.multiple_of(misc_s[e] + off, MOE_SEG), size)], tsem)

        def tail_start(e, carry):
            _for_each_piece(misc_s[128 + e], lambda off, size: tail_copy(e, off, size).start())
            return carry

        def tail_wait(e, carry):
            _for_each_piece(misc_s[128 + e], lambda off, size: tail_copy(e, off, size).wait())
            return carry

        lax.fori_loop(0, NE, tail_start, 0)
        lax.fori_loop(0, NE, tail_wait, 0)


def _moe_dispatch(plan, h_all, comb_all, *, ntmax):
    dst, lo, n, _, misc, lof, pieces = plan
    ntile = h_all.shape[0] // MOE_TD
    grid_spec = pltpu.PrefetchScalarGridSpec(
        num_scalar_prefetch=5, grid=(ntile,),
        in_specs=[pl.BlockSpec((MOE_TD, D), lambda i, *_: (i, 0)),
                  pl.BlockSpec((MOE_TD, 128), lambda i, *_: (i, 0)),
                  pl.BlockSpec((None, 1, 128), lambda i, *_: (i, 0, 0)),
                  pl.BlockSpec((None, 1, 128), lambda i, *_: (i, 0, 0))],
        out_specs=pl.BlockSpec(memory_space=pl.ANY),
        scratch_shapes=[pltpu.VMEM((2, MOE_LS, MOE_XW), BF16), pltpu.VMEM((MOE_PIECES[0], MOE_XW), BF16),
                        pltpu.SemaphoreType.DMA((2,)), pltpu.SemaphoreType.DMA(())])
    return pl.pallas_call(
        functools.partial(_dispatch_kernel, ntile=ntile),
        grid_spec=grid_spec,
        out_shape=jax.ShapeDtypeStruct((ntmax * MOE_TR, MOE_XW), BF16),
        compiler_params=_cparams(("arbitrary",)),
        name="moe_dispatch",
    )(dst.reshape(-1), lo.reshape(-1), n.reshape(-1), misc.reshape(-1), pieces.reshape(-1), h_all, comb_all,
      lof.reshape(PLAN_ROWS, 1, 128), n.astype(F32).reshape(PLAN_ROWS, 1, 128))


def _ffn_kernel(te_s, misc_s, xs_ref, wg_ref, wu_ref, wd_ref, ys_ref, wgb, wub, wdb):
    j = pl.program_id(0)

    @pl.when(j < misc_s[2 * 128])
    def _():
        @pl.when((j == 0) | (te_s[j] != te_s[jnp.maximum(j - 1, 0)]))
        def _():
            wgb[...] = wg_ref[...].astype(BF16)
            wub[...] = wu_ref[...].astype(BF16)
            wdb[...] = wd_ref[...].astype(BF16)

        x = xs_ref[:, :D]
        w = xs_ref[:, D:D + 1].astype(F32) + xs_ref[:, D + 64:D + 65].astype(F32)
        hw = _silu(_dot(x, wgb[...])) * _dot(x, wub[...]) * w
        ys_ref[...] = _dot(hw.astype(BF16), wdb[...]).astype(BF16)


def _moe_ffn(plan, xs, wg, wu, wd, *, layer, ntmax):
    te, misc = plan[3], plan[4]

    def tile(j, misc_s):
        return jnp.minimum(j, misc_s[2 * 128] - 1)

    wspec = lambda shape: pl.BlockSpec((None, None) + shape,
                                       lambda j, te_s, misc_s: (layer, te_s[tile(j, misc_s)], 0, 0))
    grid_spec = pltpu.PrefetchScalarGridSpec(
        num_scalar_prefetch=2, grid=(ntmax,),
        in_specs=[pl.BlockSpec((MOE_TR, MOE_XW), lambda j, te_s, misc_s: (tile(j, misc_s), 0)),
                  wspec((D, DE)), wspec((D, DE)), wspec((DE, D))],
        out_specs=pl.BlockSpec((MOE_TR, D), lambda j, te_s, misc_s: (tile(j, misc_s), 0)),
        scratch_shapes=[pltpu.VMEM((D, DE), BF16), pltpu.VMEM((D, DE), BF16), pltpu.VMEM((DE, D), BF16)])
    return pl.pallas_call(
        _ffn_kernel, grid_spec=grid_spec,
        out_shape=jax.ShapeDtypeStruct((ntmax * MOE_TR, D), BF16),
        compiler_params=_cparams(("arbitrary",)),
        name="moe_ffn",
    )(te[0], misc.reshape(-1), xs, wg, wu, wd)


def _combine_kernel(dst_s, lo_s, n_s, pieces_s, h_ref, comb_ref, lof_ref, nf_ref, sg_ref, su_ref, sd_ref, ys_hbm,
                    o_ref, ybuf, sem, *, ntile):
    i = pl.program_id(0)
    slot = lax.rem(i, 2)

    def copy_in(sl, src_row, dst_row, size):
        return pltpu.make_async_copy(ys_hbm.at[pl.ds(pl.multiple_of(src_row, MOE_SEG), size)],
                                     ybuf.at[sl, pl.ds(pl.multiple_of(dst_row, MOE_SEG), size)], sem.at[sl])

    def start_tile(tile, sl):
        def start_seg(e, carry):
            idx = tile * 128 + e
            lo, src = lo_s[idx], dst_s[idx]
            _for_each_piece(n_s[idx], lambda off, size: copy_in(sl, src + off, lo + off, size).start())
            return carry

        lax.fori_loop(0, NE, start_seg, 0)

    @pl.when(i == 0)
    def _():
        ybuf[...] = jnp.zeros_like(ybuf)
        start_tile(0, 0)

    @pl.when(i + 1 < ntile)
    def _():
        start_tile(i + 1, 1 - slot)

    lo_row = lof_ref[...]
    hi_row = lo_row + nf_ref[...]
    lo_col, hi_col = _col_of_row(lo_row), _col_of_row(hi_row)
    rp_hi, rp_lo = _split(_rank_plus(comb_ref[...]))
    l_hi, l_lo = _split(jnp.broadcast_to(lo_row * (1.0 / MOE_SEG), (8, 128)))
    lhs_meta = jnp.concatenate([jnp.concatenate([rp_hi, rp_lo], axis=1), jnp.concatenate([l_hi, l_lo], axis=1)], axis=0)

    h = h_ref[...]
    hs = _silu(_dot(h, sg_ref[...].astype(BF16))) * _dot(h, su_ref[...].astype(BF16))
    acc = _dot(hs.astype(BF16), sd_ref[...].astype(BF16))

    _wait_pieces(pieces_s, i, lambda size: copy_in(slot, 0, 0, size))
    for ch in range(MOE_LS // MOE_CH):
        scol = (_iota((1, MOE_CH), 1) + ch * MOE_CH).astype(F32)
        e_m = jnp.where((scol >= lo_col) & (scol < hi_col), 1.0, 0.0).astype(BF16)
        meta = _dot(lhs_meta, jnp.concatenate([e_m, e_m], axis=0))
        target = scol + 1.0 - meta[MOE_TD:MOE_TD + 1, :] * MOE_SEG
        perm_t = jnp.where(meta[:MOE_TD] == target, 1.0, 0.0).astype(BF16)
        acc = acc + _dot(perm_t, ybuf[slot, pl.ds(ch * MOE_CH, MOE_CH), :])
    o_ref[...] = acc


def _moe_combine(plan, ys, h_all, comb_all, sg, su, sd, *, layer):
    dst, lo, n, _, _, lof, pieces = plan
    ntile = h_all.shape[0] // MOE_TD
    shared = lambda w: pl.BlockSpec((None,) + w.shape[1:], lambda i, *_: (layer, 0, 0))
    grid_spec = pltpu.PrefetchScalarGridSpec(
        num_scalar_prefetch=4, grid=(ntile,),
        in_specs=[pl.BlockSpec((MOE_TD, D), lambda i, *_: (i, 0)),
                  pl.BlockSpec((MOE_TD, 128), lambda i, *_: (i, 0)),
                  pl.BlockSpec((None, 1, 128), lambda i, *_: (i, 0, 0)),
                  pl.BlockSpec((None, 1, 128), lambda i, *_: (i, 0, 0)),
                  shared(sg), shared(su), shared(sd),
                  pl.BlockSpec(memory_space=pl.ANY)],
        out_specs=pl.BlockSpec((MOE_TD, D), lambda i, *_: (i, 0)),
        scratch_shapes=[pltpu.VMEM((2, MOE_LS, D), BF16), pltpu.SemaphoreType.DMA((2,))])
    return pl.pallas_call(
        functools.partial(_combine_kernel, ntile=ntile),
        grid_spec=grid_spec,
        out_shape=jax.ShapeDtypeStruct((h_all.shape[0], D), F32),
        compiler_params=_cparams(("arbitrary",)),
        name="moe_combine",
    )(dst.reshape(-1), lo.reshape(-1), n.reshape(-1), pieces.reshape(-1), h_all, comb_all,
      lof.reshape(PLAN_ROWS, 1, 128), n.astype(F32).reshape(PLAN_ROWS, 1, 128), sg, su, sd, ys)


def _moe(h_all, comb_all, wg, wu, wd, sg, su, sd, *, layer, n_tokens):
    ntile = h_all.shape[0] // MOE_TD
    max_rows = n_tokens * TOPK + ntile * NE * (MOE_SEG - 1) + NE * (MOE_TR - 1)
    ntmax = -(-max_rows // MOE_TR)
    plan = _moe_plan(comb_all, ntmax=ntmax)
    xs = _moe_dispatch(plan, h_all, comb_all, ntmax=ntmax)
    ys = _moe_ffn(plan, xs, wg, wu, wd, layer=layer, ntmax=ntmax)
    return _moe_combine(plan, ys, h_all, comb_all, sg, su, sd, layer=layer)


def _final_kernel(x_ref, f_ref, pmod_ref, png_ref, o_ref):
    o_ref[...] = _residual_in(x_ref, f_ref, pmod_ref, png_ref)


def _final(x, f_all, pmods, png, *, nb, seq_len, tm, row0):
    nt = seq_len // tm
    t0 = row0 // tm
    return pl.pallas_call(
        _final_kernel,
        grid=(nb, nt),
        in_specs=[pl.BlockSpec((tm, D), lambda b, t: (b * nt + t, 0)),
                  pl.BlockSpec((tm, D), lambda b, t: (t0 + b * nt + t, 0)),
                  pl.BlockSpec((None,) + pmods.shape[1:], lambda b, t: (b, 0, 0, 0)),
                  pl.BlockSpec(png.shape, lambda b, t: (0, 0))],
        out_specs=pl.BlockSpec((tm, D), lambda b, t: (b * nt + t, 0)),
        out_shape=jax.ShapeDtypeStruct(x.shape, F32),
        compiler_params=_cparams(("arbitrary", "arbitrary")),
        name="final_residual",
    )(x, f_all, pmods, png)


def kernel(x_prompt, x_sample, c_prompt, c_sample, state_conv, state_shift, state_wkv, w_mod, b_mod, norm_g, ab_w_in, ab_conv_w, ab_mu, ab_w0, ab_w_decay_up, ab_a0, ab_a_up, ab_g_up, ab_k_k, ab_k_a, ab_r_k, ab_gn_g, ab_gn_b, ab_w_out, gm_w_in, gm_b_in, gm_ln_g, gm_ln_b, gm_w_s, gm_b_s, gm_w_out, moe_w_router, moe_b_router, moe_w_gate, moe_w_up, moe_w_down, moe_ws_gate, moe_ws_up, moe_ws_down):
    bsz, seq_len, _ = x_prompt.shape
    nsamp = x_sample.shape[0]
    nheads = DB // HEAD
    n_prompt = bsz * seq_len
    n_all = -(-(n_prompt + nsamp) // MOE_TD) * MOE_TD
    tm = 512

    xp = x_prompt.reshape(n_prompt, D)
    xs = x_sample.reshape(nsamp, D)

    m = _modulation(jnp.concatenate([c_prompt, c_sample], axis=0), w_mod, b_mod)
    mods_p = [m[l, :bsz].reshape(bsz, 6, 1, D) for l in range(2)]
    mods_s = [m[l, bsz:].reshape(nsamp, 6, D).transpose(1, 0, 2)[None] for l in range(2)]

    row = lambda t: t.reshape(1, -1)
    pad_rows = lambda t, lo, hi: jnp.pad(t, ((lo, hi), (0, 0)))
    small = [ab_conv_w[0], row(ab_mu[0]), row(ab_w0[0]),
             pad_rows(ab_w_decay_up[0], 0, 64).astype(BF16), row(ab_a0[0]),
             pad_rows(ab_a_up[0], 64, 0).astype(BF16), ab_g_up[0].astype(BF16),
             row(ab_k_k[0]), row(ab_k_a[0])]
    win0 = ab_w_in[0].astype(BF16)
    wo0 = ab_w_out[0].astype(BF16)
    rk, gng, gnb = row(ab_r_k[0]), row(ab_gn_g[0]), row(ab_gn_b[0])
    wrt = [moe_w_router[l].T for l in range(2)]
    br = [moe_b_router[l].reshape(NE, 1) for l in range(2)]

    h2_all = jnp.zeros((n_all, D), BF16)
    comb_all = jnp.zeros((n_all, 128), F32)

    ya, r, lw, k, v, kk, a, g, ctail, stail = _ab_pre(
        xp, mods_p[0], norm_g[0], win0, small, None, nb=bsz, seq_len=seq_len, tm=tm, seq_mode=True)
    conv_p = ctail.reshape(bsz, 8, DA)[:, 6:8][None]
    shift_p = stail.reshape(bsz, 8, PB)[:, 7][None]
    yb, sfin = _wkv_seq(r, lw, k, v, kk, a, g, rk, gng, gnb, nb=bsz, seq_len=seq_len)
    sfin = sfin.reshape(bsz, nheads // 2, 2, HEAD, 2, HEAD)
    wkv_p = jnp.stack([sfin[:, :, 0, :, 0, :], sfin[:, :, 1, :, 1, :]], axis=2)
    wkv_p = wkv_p.reshape(bsz, nheads, HEAD, HEAD).transpose(0, 1, 3, 2)[None]
    xp1, h2_all, comb_all = _ab_post(ya, yb, xp, mods_p[0], norm_g[0], wo0, wrt[0], br[0], h2_all, comb_all,
                                     nb=bsz, seq_len=seq_len, tm=tm, row0=0)

    prev = [state_conv[0, :, 0], state_conv[0, :, 1], state_shift[0]]
    ya_s, r, lw, k, v, kk, a, g, gated_s, pb_s = _ab_pre(
        xs, mods_s[0], norm_g[0], win0, small, prev, nb=1, seq_len=nsamp, tm=nsamp, seq_mode=False)
    conv_s = jnp.stack([state_conv[0, :, 1], gated_s], axis=1)[None]
    shift_s = pb_s[None]
    vecs = [t.reshape(nsamp * nheads, 1, HEAD) for t in (r, lw, k, v, kk, a, g)]
    params = [t.reshape(nheads, 1, HEAD) for t in (ab_r_k[0], ab_gn_g[0], ab_gn_b[0])]
    yb_s, snew = _wkv_step(state_wkv[0].reshape(nsamp * nheads, HEAD, HEAD), vecs, params, nheads=nheads)
    wkv_s = snew.reshape(1, nsamp, nheads, HEAD, HEAD)
    xs1, h2_all, comb_all = _ab_post(ya_s, yb_s.reshape(nsamp, DB).astype(BF16), xs, mods_s[0], norm_g[0], wo0,
                                     wrt[0], br[0], h2_all, comb_all, nb=1, seq_len=nsamp, tm=nsamp, row0=n_prompt)

    def moe_layer(l, h_all, c_all):
        return _moe(h_all, c_all, moe_w_gate, moe_w_up, moe_w_down,
                    moe_ws_gate, moe_ws_up, moe_ws_down, layer=l, n_tokens=n_prompt + nsamp)

    f0 = moe_layer(0, h2_all, comb_all)

    gw = [gm_w_in[0].astype(BF16), row(gm_b_in[0]), row(gm_ln_g[0]), row(gm_ln_b[0])]
    gw_seq = gw + [gm_w_s[0], gm_b_s[0].T, gm_w_out[0].astype(BF16), wrt[1], br[1]]
    gw_step = gw + [row(jnp.repeat(gm_w_s[0, :, 0, 0], CHUNK)), row(jnp.repeat(gm_b_s[0, :, 0], CHUNK)),
                    gm_w_out[0].astype(BF16), wrt[1], br[1]]
    tm1 = 256
    xp2, h2_all, comb_all = _gmlp(xp1, f0, mods_p[0], norm_g[0], mods_p[1], norm_g[1], gw_seq, h2_all, comb_all,
                                  nb=bsz, seq_len=seq_len, tm=tm1, row0=0, seq_mode=True)
    xs2, h2_all, comb_all, chunk_v = _gmlp(xs1, f0, mods_s[0], norm_g[0], mods_s[1], norm_g[1], gw_step, h2_all,
                                           comb_all, nb=1, seq_len=nsamp, tm=nsamp, row0=n_prompt, seq_mode=False)
    f1 = moe_layer(1, h2_all, comb_all)
    y_p = _final(xp2, f1, mods_p[1], norm_g[1], nb=bsz, seq_len=seq_len, tm=tm, row0=0)
    y_s = _final(xs2, f1, mods_s[1], norm_g[1], nb=1, seq_len=nsamp, tm=nsamp, row0=n_prompt)

    return (y_p.reshape(bsz, seq_len, D), y_s.reshape(nsamp, 1, D), conv_p, conv_s, shift_p, shift_s,
            wkv_p, wkv_s, chunk_v.reshape(1, nsamp, 1, D))
```

```python
import functools

import jax
import jax.numpy as jnp
from jax import lax
from jax.experimental import pallas as pl
from jax.experimental.pallas import tpu as pltpu

F32 = jnp.float32
BF16 = jnp.bfloat16
HI = lax.Precision.HIGHEST

D = 1024
DA = 512
DB = 512
HEAD = 64
PB = 1792
PAB = 3 * DA + PB
NE = 64
NG = 8
EPG = NE // NG
TOPG = 4
TOPK = 8
DE = 256
ROUTE_SCALE = 2.5
EPS = 1e-6
GN_EPS = 64e-5
CHUNK = 128
WKV_SLAB = 128
WKV_CHUNK = 32
WKV_PAIRS_PER_STEP = 4
WKV_P_STATE = 1
WKV_P_INV = 1
MOE_TD = 512
MOE_SEG = 16
MOE_LS = 5120
MOE_CH = 1024
MOE_PIECES = (128, 64, 32, 16)
MOE_TR = 1024
MOE_XW = D + 128
PLAN_ROWS = 128

VMEM_LIMIT = 56 * 1024 * 1024


def _cparams(sem):
    return pltpu.CompilerParams(dimension_semantics=sem, vmem_limit_bytes=VMEM_LIMIT)


def _dot(a, b, precision=None):
    return jnp.dot(a, b, preferred_element_type=F32, precision=precision)


def _dot_nt(a, b, precision=None):
    return lax.dot_general(a, b, (((1,), (1,)), ((), ())), preferred_element_type=F32, precision=precision)


def _iota(shape, dim):
    return lax.broadcasted_iota(jnp.int32, shape, dim)


def _rms(x, g):
    return x * lax.rsqrt(jnp.mean(x * x, axis=-1, keepdims=True) + EPS) * g


def _silu(x):
    return x * jax.nn.sigmoid(x)


def _seg_ones(n, seg):
    return (_iota((n, n), 0) // seg == _iota((n, n), 1) // seg).astype(F32)


def _mod_kernel(c_ref, w_ref, b_ref, o_ref):
    s = _silu(c_ref[...]).astype(BF16)
    o_ref[...] = _dot(s, w_ref[...].astype(BF16)) + b_ref[...]


def _modulation(c_all, w_mod, b_mod):
    depth, _, width = w_mod.shape
    rows = c_all.shape[0]
    bn = 512
    return pl.pallas_call(
        _mod_kernel,
        grid=(depth, width // bn),
        in_specs=[pl.BlockSpec((rows, D), lambda l, j: (0, 0)),
                  pl.BlockSpec((None, D, bn), lambda l, j: (l, 0, j)),
                  pl.BlockSpec((None, 1, bn), lambda l, j: (l, 0, j))],
        out_specs=pl.BlockSpec((None, rows, bn), lambda l, j: (l, 0, j)),
        out_shape=jax.ShapeDtypeStruct((depth, rows, width), F32),
        compiler_params=_cparams(("arbitrary", "arbitrary")),
        name="modulation",
    )(c_all, w_mod, b_mod.reshape(depth, 1, width))


def _residual_in(x_ref, f_ref, pmod_ref, png_ref):
    x = x_ref[...]
    if f_ref is None:
        return x
    return x + pmod_ref[5] * _rms(f_ref[...], png_ref[3:4, :])


def _route(h, wrt_ref, br_ref):
    tm = h.shape[0]
    logits = _mm(wrt_ref[...], h, 2, 2, nt=True)
    scores = jax.nn.sigmoid(logits)
    sel = scores + br_ref[...]
    sub = _iota((EPG, tm), 0).astype(F32)
    neg = jnp.float32(-jnp.inf)
    blks, sblks, gscore = [], [], []
    for g in range(NG):
        blk = sel[g * EPG:(g + 1) * EPG, :]
        m1 = jnp.max(blk, axis=0, keepdims=True)
        first = jnp.min(jnp.where(blk == m1, sub, float(EPG)), axis=0, keepdims=True)
        m2 = jnp.max(jnp.where(sub == first, neg, blk), axis=0, keepdims=True)
        blks.append(blk)
        sblks.append(scores[g * EPG:(g + 1) * EPG, :])
        gscore.append(m1 + m2)
    masked = []
    for g in range(NG):
        cnt = jnp.zeros((1, tm), F32)
        for g2 in range(NG):
            if g2 == g:
                continue
            beats = (gscore[g2] >= gscore[g]) if g2 < g else (gscore[g2] > gscore[g])
            cnt = cnt + jnp.where(beats, 1.0, 0.0)
        masked.append(jnp.where(cnt < TOPG, blks[g], neg))
    cnts = [jnp.zeros((EPG, tm), F32) for _ in range(NG)]
    for g2 in range(NG):
        for j in range(EPG):
            row = masked[g2][j:j + 1, :]
            for g in range(NG):
                ge = jnp.where(row >= masked[g], 1.0, 0.0)
                gt = jnp.where(row > masked[g], 1.0, 0.0)
                if g > g2:
                    cnts[g] = cnts[g] + ge
                elif g < g2:
                    cnts[g] = cnts[g] + gt
                else:
                    cnts[g] = cnts[g] + jnp.where(sub > j, ge, gt)
    wts = [jnp.where(cnts[g] < TOPK, sblks[g], 0.0) for g in range(NG)]
    tot = wts[0]
    for g in range(1, NG):
        tot = tot + wts[g]
    denom = jnp.sum(tot, axis=0, keepdims=True)
    comb_t = jnp.concatenate([w / denom * ROUTE_SCALE for w in wts] + [jnp.zeros((NE, tm), F32)], axis=0)
    return comb_t.T


def _post(y, x, mod_ref, ng_ref, wrt_ref, br_ref, xo_ref, h2_ref, comb_ref):
    xn = x + mod_ref[2] * _rms(y, ng_ref[1:2, :])
    xo_ref[...] = xn
    h2 = _rms(xn, ng_ref[2:3, :]) * (1.0 + mod_ref[4]) + mod_ref[3]
    h2_ref[...] = h2.astype(BF16)
    comb_ref[...] = _route(h2, wrt_ref, br_ref)


def _ab_pre_kernel(*refs, seq_mode, tm):
    it = iter(refs)
    x_ref, mod_ref, ng_ref, win_ref = next(it), next(it), next(it), next(it)
    cw_ref, mu_ref, w0_ref, wd_ref, a0_ref, wa_ref, wg_ref, kk_ref, ka_ref = (next(it) for _ in range(9))
    if not seq_mode:
        p2_ref, p1_ref, ps_ref = next(it), next(it), next(it)
    ya_ref, r_ref, lw_ref, k_ref, v_ref, kkn_ref, a_ref, g_ref, ctail_ref, stail_ref = (next(it) for _ in range(10))
    if seq_mode:
        cc_ref, sc_ref = next(it), next(it)

        @pl.when(pl.program_id(1) == 0)
        def _():
            cc_ref[...] = jnp.zeros_like(cc_ref)
            sc_ref[...] = jnp.zeros_like(sc_ref)

    x = x_ref[...]
    h = _rms(x, ng_ref[0:1, :]) * (1.0 + mod_ref[1]) + mod_ref[0]
    proj = _dot(h.astype(BF16), win_ref[...])
    a_h, a_b, a_c = proj[:, 0:DA], proj[:, DA:2 * DA], proj[:, 2 * DA:3 * DA]
    pb = proj[:, 3 * DA:]
    gated = a_c * a_h
    if seq_mode:
        rows = _iota((tm, 1), 0)
        g1 = jnp.where(rows == 0, cc_ref[7:8, :], pltpu.roll(gated, 1, 0))
        g2 = jnp.where(rows == 0, cc_ref[6:7, :], jnp.where(rows == 1, cc_ref[7:8, :], pltpu.roll(gated, 2, 0)))
        pprev = jnp.where(rows == 0, sc_ref[7:8, :], pltpu.roll(pb, 1, 0))
        cc_ref[...] = gated[tm - 8:, :]
        sc_ref[...] = pb[tm - 8:, :]
        ctail_ref[...] = gated[tm - 8:, :]
        stail_ref[...] = pb[tm - 8:, :]
    else:
        g1, g2, pprev = p1_ref[...], p2_ref[...], ps_ref[...]
        ctail_ref[...] = gated
        stail_ref[...] = pb
    conv = g2 * cw_ref[0:1, :] + g1 * cw_ref[1:2, :] + gated * cw_ref[2:3, :]
    ya_ref[...] = (a_b * conv).astype(BF16)

    xm = pb + (pprev - pb) * mu_ref[...]
    r, k, v = xm[:, 0:DB], xm[:, DB:2 * DB], xm[:, 2 * DB:3 * DB]
    lowrank = xm[:, 3 * DB:3 * DB + 128]
    dg = xm[:, 3 * DB + 128:]
    w_pre = w0_ref[...] + _dot(jnp.tanh(lowrank).astype(BF16), wd_ref[...])
    z = -w_pre
    softplus = jnp.maximum(z, 0.0) + jnp.log(1.0 + jnp.exp(-jnp.abs(z)))
    lw_ref[...] = -jnp.exp(-softplus - 0.5)
    a = jax.nn.sigmoid(a0_ref[...] + _dot(lowrank.astype(BF16), wa_ref[...]))
    g_ref[...] = _dot(jax.nn.sigmoid(dg).astype(BF16), wg_ref[...])
    kk = k * kk_ref[...]
    ss = _mm(kk * kk, _seg_ones(DB, HEAD), 2, 1)
    kkn_ref[...] = kk / jnp.maximum(jnp.sqrt(ss), 1e-12)
    r_ref[...] = r
    k_ref[...] = k * (1.0 + (a - 1.0) * ka_ref[...])
    v_ref[...] = v
    a_ref[...] = a


def _ab_pre(x, mods, ng, win, small, prev, *, nb, seq_len, tm, seq_mode):
    nt = seq_len // tm
    rows = nb * seq_len
    tail = 8 if seq_mode else tm
    full = lambda shape: pl.BlockSpec(shape, lambda b, t: tuple(0 for _ in shape))
    tok = lambda w: pl.BlockSpec((tm, w), lambda b, t: (b * nt + t, 0))
    in_specs = [tok(D),
                pl.BlockSpec((None,) + mods.shape[1:], lambda b, t: (b, 0, 0, 0)),
                full(ng.shape), full(win.shape)] + [full(s.shape) for s in small]
    args = [x, mods, ng, win] + list(small)
    if not seq_mode:
        in_specs += [tok(DA), tok(DA), tok(PB)]
        args += list(prev)
    out_shape = ([jax.ShapeDtypeStruct((rows, DA), BF16)] + [jax.ShapeDtypeStruct((rows, DB), F32)] * 7
                 + [jax.ShapeDtypeStruct((nb * tail, DA), F32), jax.ShapeDtypeStruct((nb * tail, PB), F32)])
    out_specs = ([tok(DA)] + [tok(DB)] * 7
                 + [pl.BlockSpec((tail, DA), lambda b, t: (b, 0)), pl.BlockSpec((tail, PB), lambda b, t: (b, 0))])
    scratch = [pltpu.VMEM((8, DA), F32), pltpu.VMEM((8, PB), F32)] if seq_mode else []
    return pl.pallas_call(
        functools.partial(_ab_pre_kernel, seq_mode=seq_mode, tm=tm),
        grid=(nb, nt), in_specs=in_specs, out_specs=out_specs, out_shape=out_shape,
        scratch_shapes=scratch,
        compiler_params=_cparams(("arbitrary", "arbitrary")),
        name="ab_pre_seq" if seq_mode else "ab_pre_step",
    )(*args)


def _split(x):
    hi = x.astype(BF16)
    return hi, (x - hi.astype(F32)).astype(BF16)


def _mm(a, b, pa=1, pb=1, nt=False):
    ah, al = _split(a) if pa == 2 else (a.astype(BF16), None)
    bh, bl = _split(b) if pb == 2 else (b.astype(BF16), None)
    bx = 1 if nt else 0
    if pa == 2 and pb == 2:
        ah, bh = jnp.concatenate([ah, al, ah], axis=1), jnp.concatenate([bh, bh, bl], axis=bx)
    elif pa == 2:
        ah, bh = jnp.concatenate([ah, al], axis=1), jnp.concatenate([bh, bh], axis=bx)
    elif pb == 2:
        ah, bh = jnp.concatenate([ah, ah], axis=1), jnp.concatenate([bh, bl], axis=bx)
    return _dot_nt(ah, bh) if nt else _dot(ah, bh)


def _gn_gate(o, r, k, v, g, rk, gng, gnb, seg):
    mean = _mm(o, seg, 2, 1) * (1.0 / HEAD)
    d = o - mean
    var = _mm(d * d, seg, 2, 1) * (1.0 / HEAD)
    out = d * lax.rsqrt(var + GN_EPS) * gng + gnb
    out = out + _mm(r * k * rk, seg, 2, 1) * v
    return out * g


def _wkv_pair(r, lw, k, v, kk, a, st, cst):
    ts, c = WKV_SLAB, WKV_CHUNK
    nchunk = ts // c
    ps, pi = WKV_P_STATE, WKV_P_INV
    strict, incl, cum_lhs, m0, m1, headdiag, eye, eye2, blks, chunk_cols = cst

    l1 = lw.astype(BF16)
    rem = lw - l1.astype(F32)
    l2 = rem.astype(BF16)
    l3 = (rem - l2.astype(F32)).astype(BF16)
    cums = _dot(cum_lhs, jnp.concatenate([l1, l2, l3], axis=0))
    yield
    cl, ctot = cums[:ts], cums[ts:]
    e_neg = jnp.exp(-cl)
    e_end = jnp.exp(ctot - cl)
    at = -kk * jnp.exp(cl - lw)
    rt = r * jnp.exp(cl)
    beta = kk * a
    bk = jnp.concatenate([beta * e_neg, k * e_neg], axis=0)
    bend, kend = beta * e_end, k * e_end

    at_st = jnp.concatenate([at * m0, at * m1], axis=0)
    a_st = _mm(at_st, bk, ps, ps, nt=True)
    a_out = _mm(jnp.concatenate([rt * m0, rt * m1], axis=0), bk, 1, 1, nt=True)
    yield
    zero = jnp.zeros((ts, ts), F32)

    def two_heads(p0, p1):
        return jnp.concatenate([jnp.concatenate([p0, zero], axis=1), jnp.concatenate([zero, p1], axis=1)], axis=0)

    lbd = two_heads(a_st[:ts, :ts] * strict, a_st[ts:, :ts] * strict)
    d1 = jnp.where(blks[0], lbd, 0.0)
    tinv = eye2 + d1
    d2 = _mm(d1, d1, pi, pi)
    yield
    tinv = tinv + _mm(tinv, d2, pi, pi)
    d4 = _mm(d2, d2, pi, pi)
    yield
    tinv = tinv + _mm(tinv, d4, pi, pi)
    yield
    for lvl in range(1, len(blks)):
        off = jnp.where(blks[lvl] & jnp.logical_not(blks[lvl - 1]), lbd, 0.0)
        half = _mm(tinv, off, pi, pi)
        yield
        tinv = tinv + _mm(half, tinv, pi, pi)
        yield

    v_st = jnp.concatenate([v * m0, v * m1], axis=0)
    y_st = _mm(two_heads(a_st[:ts, ts:] * strict, a_st[ts:, ts:] * strict), v_st, ps, ps)
    yield
    wu = _mm(tinv, jnp.concatenate([at_st, y_st], axis=1), ps, ps)
    yield
    w =wu[:ts, :2 * HEAD] + wu[ts:, :2 * HEAD]
    u = wu[:ts, 2 * HEAD:] + wu[ts:, 2 * HEAD:]
    rhs = jnp.concatenate([jnp.concatenate([w, u], axis=1),
                           jnp.concatenate([jnp.zeros_like(v), v], axis=1)], axis=0)
    bend_t, kend_t = bend.T, kend.T
    lhs = [jnp.concatenate([bend_t * cm, kend_t * cm], axis=1) for cm in chunk_cols]
    mn_all = _mm(jnp.concatenate(lhs, axis=0), rhs, ps, ps)
    lhs_q = jnp.concatenate([jnp.concatenate([a_out[:ts, :ts] * incl, a_out[:ts, ts:] * incl], axis=1),
                             jnp.concatenate([a_out[ts:, :ts] * incl, a_out[ts:, ts:] * incl], axis=1)], axis=0)
    qo = _mm(lhs_q, rhs, 1, 1)
    yield
    q =rt + qo[:ts, :2 * HEAD] * m0 + qo[ts:, :2 * HEAD] * m1
    olocal = qo[:ts, 2 * HEAD:] * m0 + qo[ts:, 2 * HEAD:] * m1

    outs = []
    for ch in range(nchunk):
        mn = mn_all[ch * ts:(ch + 1) * ts]
        outs.append(_mm(q[ch * c:(ch + 1) * c], st, 1, 1) + olocal[ch * c:(ch + 1) * c])
        trans = eye * jnp.exp(ctot[ch * c:ch * c + 1, :]) + mn[:, :2 * HEAD] * headdiag
        st = _mm(trans, st, ps, ps) + mn[:, 2 * HEAD:] * headdiag
        yield
    return jnp.concatenate(outs, axis=0), st


def _round_robin(gens):
    results = [None] * len(gens)
    active = list(enumerate(gens))
    while active:
        still = []
        for i, gen in active:
            try:
                next(gen)
                still.append((i, gen))
            except StopIteration as stop:
                results[i] = stop.value
        active = still
    return results


def _wkv_seq_kernel(r_ref, lw_ref, k_ref, v_ref, kk_ref, a_ref, g_ref, rk_ref, gng_ref, gnb_ref,
                    y_ref, sfin_ref, st_ref, *, npp):
    ts, c = WKV_SLAB, WKV_CHUNK

    @pl.when(pl.program_id(2) == 0)
    def _():
        st_ref[...] = jnp.zeros_like(st_ref)

    ri, ci = _iota((ts, ts), 0), _iota((ts, ts), 1)
    same = ri // c == ci // c
    strict = (same & (ci < ri)).astype(F32)
    incl = (same & (ci <= ri)).astype(F32)
    cum_lhs = jnp.concatenate([incl, same.astype(F32)], axis=0).astype(BF16)
    cum_lhs = jnp.concatenate([cum_lhs] * 3, axis=1)
    lane = _iota((1, 2 * HEAD), 1)
    m0 = (lane < HEAD).astype(F32)
    m1 = 1.0 - m0
    headdiag = _seg_ones(2 * HEAD, HEAD)
    eye = (_iota((2 * HEAD, 2 * HEAD), 0) == _iota((2 * HEAD, 2 * HEAD), 1)).astype(F32)
    r2, c2 = _iota((2 * ts, 2 * ts), 0), _iota((2 * ts, 2 * ts), 1)
    eye2 = (r2 == c2).astype(F32)
    blks, b = [], 8
    while b <= c:
        blks.append(r2 // b == c2 // b)
        b *= 2
    tcol = _iota((1, ts), 1) // c
    chunk_cols = [(tcol == ch).astype(F32) for ch in range(ts // c)]
    cst = (strict, incl, cum_lhs, m0, m1, headdiag, eye, eye2, blks, chunk_cols)

    sls = [slice(p * 2 * HEAD, (p + 1) * 2 * HEAD) for p in range(npp)]
    res = _round_robin([_wkv_pair(r_ref[:, sl], lw_ref[:, sl], k_ref[:, sl], v_ref[:, sl], kk_ref[:, sl],
                                  a_ref[:, sl], st_ref[p], cst) for p, sl in enumerate(sls)])
    for p, sl in enumerate(sls):
        o, st = res[p]
        st_ref[p] = st
        y_ref[:, sl] = _gn_gate(o, r_ref[:, sl], k_ref[:, sl], v_ref[:, sl], g_ref[:, sl], rk_ref[:, sl],
                                gng_ref[:, sl], gnb_ref[:, sl], headdiag).astype(BF16)

    @pl.when(pl.program_id(2) == pl.num_programs(2) - 1)
    def _():
        sfin_ref[...] = st_ref[...]


def _wkv_seq(r, lw, k, v, kk, a, g, rk, gng, gnb, *, nb, seq_len):
    ns = seq_len // WKV_SLAB
    npp = WKV_PAIRS_PER_STEP
    width = 2 * HEAD * npp
    tok = pl.BlockSpec((WKV_SLAB, width), lambda b, p, s: (b * ns + s, p))
    par = pl.BlockSpec((1, width), lambda b, p, s: (0, p))
    return pl.pallas_call(
        functools.partial(_wkv_seq_kernel, npp=npp),
        grid=(nb, DB // width, ns),
        in_specs=[tok] * 7 + [par] * 3,
        out_specs=[tok, pl.BlockSpec((None, npp, 2 * HEAD, 2 * HEAD), lambda b, p, s: (b, p, 0, 0))],
        out_shape=[jax.ShapeDtypeStruct((nb * seq_len, DB), BF16),
                   jax.ShapeDtypeStruct((nb, DB // (2 * HEAD), 2 * HEAD, 2 * HEAD), F32)],
        scratch_shapes=[pltpu.VMEM((npp, 2 * HEAD, 2 * HEAD), F32)],
        compiler_params=_cparams(("arbitrary", "arbitrary", "arbitrary")),
        name="wkv_seq",
    )(r, lw, k, v, kk, a, g, rk, gng, gnb)


def _wkv_step_kernel(s_ref, r_ref, lw_ref, k_ref, v_ref, kk_ref, a_ref, g_ref, rk_ref, gng_ref, gnb_ref,
                     y_ref, so_ref):
    s = s_ref[...]
    r, k, v, kk, a = r_ref[...], k_ref[...], v_ref[...], kk_ref[...], a_ref[...]
    w = jnp.exp(lw_ref[...])
    eye = (_iota((1, HEAD, HEAD), 1) == _iota((1, HEAD, HEAD), 2)).astype(F32)
    v_col = jnp.sum(eye * v, axis=2, keepdims=True)
    s_kk = jnp.sum(s * kk, axis=2, keepdims=True)
    s = s * w - s_kk * (kk * a) + v_col * k
    so_ref[...] = s
    o_col = jnp.sum(s * r, axis=2, keepdims=True)
    o = jnp.sum(eye * o_col, axis=1, keepdims=True)
    mean = jnp.mean(o, axis=2, keepdims=True)
    d = o - mean
    var = jnp.mean(d * d, axis=2, keepdims=True)
    out = d * lax.rsqrt(var + GN_EPS) * gng_ref[...] + gnb_ref[...]
    out = out + jnp.sum(r * k * rk_ref[...], axis=2, keepdims=True) * v
    y_ref[...] = out * g_ref[...]


def _wkv_step(state, vecs, params, *, nheads):
    n = state.shape[0]
    tb = 8 * nheads
    sspec = pl.BlockSpec((tb, HEAD, HEAD), lambda i: (i, 0, 0))
    vspec = pl.BlockSpec((tb, 1, HEAD), lambda i: (i, 0, 0))
    pspec = pl.BlockSpec((tb, 1, HEAD), lambda i: (0, 0, 0))
    reps = tb // nheads
    params = [jnp.tile(p, (reps, 1, 1)) for p in params]
    return pl.pallas_call(
        _wkv_step_kernel,
        grid=(n // tb,),
        in_specs=[sspec] + [vspec] * 7 + [pspec] * 3,
        out_specs=[vspec, sspec],
        out_shape=[jax.ShapeDtypeStruct((n, 1, HEAD), F32), jax.ShapeDtypeStruct((n, HEAD, HEAD), F32)],
        compiler_params=_cparams(("arbitrary",)),
        name="wkv_step",
    )(state, *vecs, *params)


def _ab_post_kernel(ya_ref, yb_ref, x_ref, mod_ref, ng_ref, wo_ref, wrt_ref, br_ref,
                    h2in_ref, combin_ref, xo_ref, h2_ref, comb_ref):
    del h2in_ref, combin_ref
    y = _dot(ya_ref[...], wo_ref[0:DA, :]) + _dot(yb_ref[...], wo_ref[DA:, :])
    _post(y, x_ref[...], mod_ref, ng_ref, wrt_ref, br_ref, xo_ref, h2_ref, comb_ref)


def _ab_post(ya, yb, x, mods, ng, wo, wrt, br, h2_all, comb_all, *, nb, seq_len, tm, row0):
    nt = seq_len // tm
    t0 = row0 // tm
    full = lambda shape: pl.BlockSpec(shape, lambda b, t: tuple(0 for _ in shape))
    tok = lambda w: pl.BlockSpec((tm, w), lambda b, t: (b * nt + t, 0))
    tok_all = lambda w: pl.BlockSpec((tm, w), lambda b, t: (t0 + b * nt + t, 0))
    anyspec = pl.BlockSpec(memory_space=pl.ANY)
    return pl.pallas_call(
        _ab_post_kernel,
        grid=(nb, nt),
        in_specs=[tok(DA), tok(DB), tok(D),
                  pl.BlockSpec((None,) + mods.shape[1:], lambda b, t: (b, 0, 0, 0)),
                  full(ng.shape), full(wo.shape), full(wrt.shape), full(br.shape), anyspec, anyspec],
        out_specs=[tok(D), tok_all(D), tok_all(128)],
        out_shape=[jax.ShapeDtypeStruct(x.shape, F32), jax.ShapeDtypeStruct(h2_all.shape, BF16),
                   jax.ShapeDtypeStruct(comb_all.shape, F32)],
        input_output_aliases={8: 1, 9: 2},
        compiler_params=_cparams(("arbitrary", "arbitrary")),
        name="ab_post",
    )(ya, yb, x, mods, ng, wo, wrt, br, h2_all, comb_all)


def _gmlp_kernel(*refs, seq_mode, tm):
    it = iter(refs)
    x_ref, f_ref, pmod_ref, png_ref, mod_ref, ng_ref = (next(it) for _ in range(6))
    win_ref, bin_ref, lng_ref, lnb_ref, ws_ref, bs_ref, wo_ref, wrt_ref, br_ref = (next(it) for _ in range(9))
    next(it), next(it)
    xo_ref, h2_ref, comb_ref = next(it), next(it), next(it)
    if not seq_mode:
        cv_ref = next(it)
    x = _residual_in(x_ref, f_ref, pmod_ref, png_ref)
    h = _rms(x, ng_ref[0:1, :]) * (1.0 + mod_ref[1]) + mod_ref[0]
    z = jax.nn.gelu(_dot(h.astype(BF16), win_ref[...]) + bin_ref[...])
    u, v = z[:, :D], z[:, D:]
    mu = jnp.mean(v, axis=-1, keepdims=True)
    var = jnp.mean(jnp.square(v - mu), axis=-1, keepdims=True)
    v = (v - mu) * lax.rsqrt(var + EPS) * lng_ref[...] + lnb_ref[...]
    if seq_mode:
        vb = v.astype(BF16)
        causal = _iota((CHUNK, CHUNK), 1) <= _iota((CHUNK, CHUNK), 0)
        cols = []
        for g in range(D // CHUNK):
            wsg = jnp.where(causal, ws_ref[g], 0.0).astype(BF16)
            bsg = bs_ref[:, g:g + 1]
            rows = [_dot(wsg, vb[c * CHUNK:(c + 1) * CHUNK, g * CHUNK:(g + 1) * CHUNK]) + bsg
                    for c in range(tm // CHUNK)]
            cols.append(jnp.concatenate(rows, axis=0) if len(rows) > 1 else rows[0])
        s = jnp.concatenate(cols, axis=1)
    else:
        cv_ref[...] = v
        s = v * ws_ref[...] + bs_ref[...]
    y = _dot((u * s).astype(BF16), wo_ref[...])
    _post(y, x, mod_ref, ng_ref, wrt_ref, br_ref, xo_ref, h2_ref, comb_ref)


def _gmlp(x, f_all, pmods, png, mods, ng, weights, h2_all, comb_all, *, nb, seq_len, tm, row0, seq_mode):
    nt = seq_len // tm
    t0 = row0 // tm
    full = lambda shape: pl.BlockSpec(shape, lambda b, t: tuple(0 for _ in shape))
    tok = lambda w: pl.BlockSpec((tm, w), lambda b, t: (b * nt + t, 0))
    tok_all = lambda w: pl.BlockSpec((tm, w), lambda b, t: (t0 + b * nt + t, 0))
    modspec = lambda m: pl.BlockSpec((None,) + m.shape[1:], lambda b, t: (b, 0, 0, 0))
    anyspec = pl.BlockSpec(memory_space=pl.ANY)
    in_specs = ([tok(D), tok_all(D), modspec(pmods), full(png.shape), modspec(mods), full(ng.shape)]
                + [full(w.shape) for w in weights] + [anyspec, anyspec])
    out_specs = [tok(D), tok_all(D), tok_all(128)]
    out_shape = [jax.ShapeDtypeStruct(x.shape, F32), jax.ShapeDtypeStruct(h2_all.shape, BF16),
                 jax.ShapeDtypeStruct(comb_all.shape, F32)]
    if not seq_mode:
        out_specs.append(tok(D))
        out_shape.append(jax.ShapeDtypeStruct(x.shape, F32))
    n_in = len(in_specs)
    return pl.pallas_call(
        functools.partial(_gmlp_kernel, seq_mode=seq_mode, tm=tm),
        grid=(nb, nt), in_specs=in_specs, out_specs=out_specs, out_shape=out_shape,
        input_output_aliases={n_in - 2: 1, n_in - 1: 2},
        compiler_params=_cparams(("arbitrary", "arbitrary")),
        name="gmlp_seq" if seq_mode else "gmlp_step",
    )(x, f_all, pmods, png, mods, ng, *weights, h2_all, comb_all)


def _strict_lower(n):
    return _iota((n, n), 1) < _iota((n, n), 0)


def _col_of_row(row):
    return jnp.broadcast_to(row, (128, 128)).T[:, 0:1]


def _plan_kernel(comb_ref, dst_ref, lo_ref, n_ref, te_ref, misc_ref, lof_ref, pieces_ref, cnt_ref, *, ntile, te_cols):
    i = pl.program_id(0)

    @pl.when(i == 0)
    def _():
        cnt_ref[...] = jnp.zeros_like(cnt_ref)

    sel = jnp.where(comb_ref[...] > 0.0, 1.0, 0.0)
    cnt_ref[pl.ds(i, 1), :] = jnp.sum(sel, axis=0, keepdims=True)

    @pl.when(i == ntile - 1)
    def _():
        c = cnt_ref[...]
        cpad = jnp.floor((c + (MOE_SEG - 1.0)) * (1.0 / MOE_SEG)) * MOE_SEG
        cb = cpad.astype(BF16)
        lower = jnp.where(_strict_lower(PLAN_ROWS), 1.0, 0.0).astype(BF16)
        upper = jnp.where(_iota((128, 128), 0) < _iota((128, 128), 1), 1.0, 0.0).astype(BF16)
        before = _dot(lower, cb)
        lo = _dot(cb, upper)
        tot = jnp.sum(cpad, axis=0, keepdims=True)
        rt = jnp.floor((tot + (MOE_TR - 1.0)) * (1.0 / MOE_TR))
        gt = _dot(jnp.broadcast_to(rt, (8, 128)).astype(BF16), upper)[0:1, :]
        dst_ref[...] = (gt * MOE_TR + before).astype(jnp.int32)
        lo_ref[...] = lo.astype(jnp.int32)
        lof_ref[...] = lo
        n_ref[...] = cpad.astype(jnp.int32)
        ends_col = _col_of_row(gt + rt)
        jrow = _iota((1, te_cols), 1).astype(F32)
        te = jnp.sum(jnp.where(ends_col <= jrow, 1.0, 0.0), axis=0, keepdims=True)
        te_ref[...] = jnp.broadcast_to(jnp.minimum(te, NE - 1.0), (8, te_cols)).astype(jnp.int32)
        nused = jnp.sum(rt, axis=1, keepdims=True)
        rowi = _iota((8, 128), 0)
        misc = jnp.where(rowi == 0, gt * MOE_TR + tot, jnp.where(rowi == 1, rt * MOE_TR - tot, nused))
        misc_ref[...] = misc.astype(jnp.int32)
        lane = _iota((1, 128), 1)
        pieces = jnp.zeros((PLAN_ROWS, 128), F32)
        taken = jnp.zeros_like(cpad)
        for k, size in enumerate(MOE_PIECES):
            cnt = jnp.floor((cpad - taken) * (1.0 / size))
            taken = taken + cnt * size
            pieces = pieces + jnp.where(lane == k, jnp.sum(cnt, axis=1, keepdims=True), 0.0)
        pieces_ref[...] = pieces.astype(jnp.int32)


def _moe_plan(comb_all, *, ntmax):
    ntile = comb_all.shape[0] // MOE_TD
    te_cols = -(-ntmax // 128) * 128
    full = lambda shape: pl.BlockSpec(shape, lambda i: tuple(0 for _ in shape))
    shapes = [((PLAN_ROWS, 128), jnp.int32)] * 3 + [((8, te_cols), jnp.int32), ((8, 128), jnp.int32),
                                                   ((PLAN_ROWS, 128), F32), ((PLAN_ROWS, 128), jnp.int32)]
    return pl.pallas_call(
        functools.partial(_plan_kernel, ntile=ntile, te_cols=te_cols),
        grid=(ntile,),
        in_specs=[pl.BlockSpec((MOE_TD, 128), lambda i: (i, 0))],
        out_specs=[full(s) for s, _ in shapes],
        out_shape=[jax.ShapeDtypeStruct(s, d) for s, d in shapes],
        scratch_shapes=[pltpu.VMEM((PLAN_ROWS, 128), F32)],
        compiler_params=_cparams(("arbitrary",)),
        name="moe_plan",
    )(comb_all)


def _rank_plus(comb):
    sel = jnp.where(comb > 0.0, 1.0, 0.0)
    lower = jnp.where(_strict_lower(MOE_TD), 1.0, 0.0).astype(BF16)
    return sel * (_dot(lower, sel.astype(BF16)) + 1.0)


def _for_each_piece(n, fn):
    big = MOE_PIECES[0]
    nbig = n // big

    def body(p, carry):
        fn(p * big, big)
        return carry

    lax.fori_loop(0, nbig, body, 0)
    off = nbig * big
    for size in MOE_PIECES[1:]:
        bit = ((n - off) // size) > 0

        @pl.when(bit)
        def _(off=off, size=size):
            fn(off, size)

        off = off + jnp.where(bit, size, 0)


def _wait_pieces(pieces_s, tile, make_copy):
    for k, size in enumerate(MOE_PIECES):
        def wait_one(p, carry, size=size):
            make_copy(size).wait()
            return carry

        lax.fori_loop(0, pieces_s[tile * 128 + k], wait_one, 0)


def _dispatch_kernel(dst_s, lo_s, n_s, misc_s, pieces_s, h_ref, comb_ref, lof_ref, nf_ref, xs_hbm, buf, zbuf, sem,
                     tsem, *, ntile):
    i = pl.program_id(0)
    slot = lax.rem(i, 2)
    comb = comb_ref[...]
    lo_row = lof_ref[...]
    hi_row = lo_row + nf_ref[...]
    rp_hi, rp_lo = _split(_rank_plus(comb).T)
    l_hi, l_lo = _split(jnp.broadcast_to(_col_of_row(lo_row * (1.0 / MOE_SEG)), (128, 128)))
    rhs_meta = jnp.concatenate([jnp.concatenate([rp_hi, l_hi], axis=1), jnp.concatenate([rp_lo, l_lo], axis=1)], axis=0)
    c_hi, c_lo = _split(comb)
    h_ext = jnp.concatenate([h_ref[...], c_hi, c_lo], axis=1)
    ones2 = jnp.ones((256, 128), BF16)
    lane_lo = _iota((1, 128), 1) < 64
    for ch in range(MOE_LS // MOE_CH):
        srow = (_iota((MOE_CH, 1), 0) + ch * MOE_CH).astype(F32)
        e_t = jnp.where((srow >= lo_row) & (srow < hi_row), 1.0, 0.0)
        e_tb = e_t.astype(BF16)
        meta = _dot(jnp.concatenate([e_tb, e_tb], axis=1), rhs_meta)
        target = srow + 1.0 - meta[:, MOE_TD:] * MOE_SEG
        target = jnp.concatenate([target] * (MOE_TD // 128), axis=1)
        perm = jnp.where(meta[:, :MOE_TD] == target, 1.0, 0.0).astype(BF16)
        rows = pl.ds(ch * MOE_CH, MOE_CH)
        sorted_rows = _dot(perm, h_ext)
        buf[slot, rows, :D] = sorted_rows[:, :D].astype(BF16)
        z_hi, z_lo = _split((sorted_rows[:, D:D + 128] + sorted_rows[:, D + 128:]) * e_t)
        w_hi, w_lo = _split(_dot(jnp.concatenate([z_hi, z_lo], axis=1), ones2))
        buf[slot, rows, D:] = jnp.where(lane_lo, w_hi, w_lo)

    def copy_out(sl, src_row, dst_row, size):
        return pltpu.make_async_copy(buf.at[sl, pl.ds(pl.multiple_of(src_row, MOE_SEG), size)],
                                     xs_hbm.at[pl.ds(pl.multiple_of(dst_row, MOE_SEG), size)], sem.at[sl])

    def start_seg(e, carry):
        idx = i * 128 + e
        lo, dst = lo_s[idx], dst_s[idx]
        _for_each_piece(n_s[idx], lambda off, size: copy_out(slot, lo + off, dst + off, size).start())
        return carry

    lax.fori_loop(0, NE, start_seg, 0)

    @pl.when(i > 0)
    def _():
        _wait_pieces(pieces_s, i - 1, lambda size: copy_out(1 - slot, 0, 0, size))

    @pl.when(i == ntile - 1)
    def _():
        _wait_pieces(pieces_s, i, lambda size: copy_out(slot, 0, 0, size))
        zbuf[...] = jnp.zeros_like(zbuf)

        def tail_copy(e, off, size):
            return pltpu.make_async_copy(
                zbuf.at[pl.ds(0, size)], xs_hbm.at[pl.ds(pl.multiple_of(misc_s[e] + off, MOE_SEG), size)], tsem)

        def tail_start(e, carry):
            _for_each_piece(misc_s[128 + e], lambda off, size: tail_copy(e, off, size).start())
            return carry

        def tail_wait(e, carry):
            _for_each_piece(misc_s[128 + e], lambda off, size: tail_copy(e, off, size).wait())
            return carry

        lax.fori_loop(0, NE, tail_start, 0)
        lax.fori_loop(0, NE, tail_wait, 0)


def _moe_dispatch(plan, h_all, comb_all, *, ntmax):
    dst, lo, n, _, misc, lof, pieces = plan
    ntile = h_all.shape[0] // MOE_TD
    grid_spec = pltpu.PrefetchScalarGridSpec(
        num_scalar_prefetch=5, grid=(ntile,),
        in_specs=[pl.BlockSpec((MOE_TD, D), lambda i, *_: (i, 0)),
                  pl.BlockSpec((MOE_TD, 128), lambda i, *_: (i, 0)),
                  pl.BlockSpec((None, 1, 128), lambda i, *_: (i, 0, 0)),
                  pl.BlockSpec((None, 1, 128), lambda i, *_: (i, 0, 0))],
        out_specs=pl.BlockSpec(memory_space=pl.ANY),
        scratch_shapes=[pltpu.VMEM((2, MOE_LS, MOE_XW), BF16), pltpu.VMEM((MOE_PIECES[0], MOE_XW), BF16),
                        pltpu.SemaphoreType.DMA((2,)), pltpu.SemaphoreType.DMA(())])
    return pl.pallas_call(
        functools.partial(_dispatch_kernel, ntile=ntile),
        grid_spec=grid_spec,
        out_shape=jax.ShapeDtypeStruct((ntmax * MOE_TR, MOE_XW), BF16),
        compiler_params=_cparams(("arbitrary",)),
        name="moe_dispatch",
    )(dst.reshape(-1), lo.reshape(-1), n.reshape(-1), misc.reshape(-1), pieces.reshape(-1), h_all, comb_all,
      lof.reshape(PLAN_ROWS, 1, 128), n.astype(F32).reshape(PLAN_ROWS, 1, 128))


def _ffn_kernel(te_s, misc_s, xs_ref, wg_ref, wu_ref, wd_ref, ys_ref, wgb, wub, wdb):
    j = pl.program_id(0)

    @pl.when(j < misc_s[2 * 128])
    def _():
        @pl.when((j == 0) | (te_s[j] != te_s[jnp.maximum(j - 1, 0)]))
        def _():
            wgb[...] = wg_ref[...].astype(BF16)
            wub[...] = wu_ref[...].astype(BF16)
            wdb[...] = wd_ref[...].astype(BF16)

        x = xs_ref[:, :D]
        w = xs_ref[:, D:D + 1].astype(F32) + xs_ref[:, D + 64:D + 65].astype(F32)
        hw = _silu(_dot(x, wgb[...])) * _dot(x, wub[...]) * w
        ys_ref[...] = _dot(hw.astype(BF16), wdb[...]).astype(BF16)


def _moe_ffn(plan, xs, wg, wu, wd, *, layer, ntmax):
    te, misc = plan[3], plan[4]

    def tile(j, misc_s):
        return jnp.minimum(j, misc_s[2 * 128] - 1)

    wspec = lambda shape: pl.BlockSpec((None, None) + shape,
                                       lambda j, te_s, misc_s: (layer, te_s[tile(j, misc_s)], 0, 0))
    grid_spec = pltpu.PrefetchScalarGridSpec(
        num_scalar_prefetch=2, grid=(ntmax,),
        in_specs=[pl.BlockSpec((MOE_TR, MOE_XW), lambda j, te_s, misc_s: (tile(j, misc_s), 0)),
                  wspec((D, DE)), wspec((D, DE)), wspec((DE, D))],
        out_specs=pl.BlockSpec((MOE_TR, D), lambda j, te_s, misc_s: (tile(j, misc_s), 0)),
        scratch_shapes=[pltpu.VMEM((D, DE), BF16), pltpu.VMEM((D, DE), BF16), pltpu.VMEM((DE, D), BF16)])
    return pl.pallas_call(
        _ffn_kernel, grid_spec=grid_spec,
        out_shape=jax.ShapeDtypeStruct((ntmax * MOE_TR, D), BF16),
        compiler_params=_cparams(("arbitrary",)),
        name="moe_ffn",
    )(te[0], misc.reshape(-1), xs, wg, wu, wd)


def _combine_kernel(dst_s, lo_s, n_s, pieces_s, h_ref, comb_ref, lof_ref, nf_ref, sg_ref, su_ref, sd_ref, ys_hbm,
                    o_ref, ybuf, sem, *, ntile):
    i = pl.program_id(0)
    slot = lax.rem(i, 2)

    def copy_in(sl, src_row, dst_row, size):
        return pltpu.make_async_copy(ys_hbm.at[pl.ds(pl.multiple_of(src_row, MOE_SEG), size)],
                                     ybuf.at[sl, pl.ds(pl.multiple_of(dst_row, MOE_SEG), size)], sem.at[sl])

    def start_tile(tile, sl):
        def start_seg(e, carry):
            idx = tile * 128 + e
            lo, src = lo_s[idx], dst_s[idx]
            _for_each_piece(n_s[idx], lambda off, size: copy_in(sl, src + off, lo + off, size).start())
            return carry

        lax.fori_loop(0, NE, start_seg, 0)

    @pl.when(i == 0)
    def _():
        ybuf[...] = jnp.zeros_like(ybuf)
        start_tile(0, 0)

    @pl.when(i + 1 < ntile)
    def _():
        start_tile(i + 1, 1 - slot)

    lo_row = lof_ref[...]
    hi_row = lo_row + nf_ref[...]
    lo_col, hi_col = _col_of_row(lo_row), _col_of_row(hi_row)
    rp_hi, rp_lo = _split(_rank_plus(comb_ref[...]))
    l_hi, l_lo = _split(jnp.broadcast_to(lo_row * (1.0 / MOE_SEG), (8, 128)))
    lhs_meta = jnp.concatenate([jnp.concatenate([rp_hi, rp_lo], axis=1), jnp.concatenate([l_hi, l_lo], axis=1)], axis=0)

    h = h_ref[...]
    hs = _silu(_dot(h, sg_ref[...].astype(BF16))) * _dot(h, su_ref[...].astype(BF16))
    acc = _dot(hs.astype(BF16), sd_ref[...].astype(BF16))

    _wait_pieces(pieces_s, i, lambda size: copy_in(slot, 0, 0, size))
    for ch in range(MOE_LS // MOE_CH):
        scol = (_iota((1, MOE_CH), 1) + ch * MOE_CH).astype(F32)
        e_m = jnp.where((scol >= lo_col) & (scol < hi_col), 1.0, 0.0).astype(BF16)
        meta = _dot(lhs_meta, jnp.concatenate([e_m, e_m], axis=0))
        target = scol + 1.0 - meta[MOE_TD:MOE_TD + 1, :] * MOE_SEG
        perm_t = jnp.where(meta[:MOE_TD] == target, 1.0, 0.0).astype(BF16)
        acc = acc + _dot(perm_t, ybuf[slot, pl.ds(ch * MOE_CH, MOE_CH), :])
    o_ref[...] = acc


def _moe_combine(plan, ys, h_all, comb_all, sg, su, sd, *, layer):
    dst, lo, n, _, _, lof, pieces = plan
    ntile = h_all.shape[0] // MOE_TD
    shared = lambda w: pl.BlockSpec((None,) + w.shape[1:], lambda i, *_: (layer, 0, 0))
    grid_spec = pltpu.PrefetchScalarGridSpec(
        num_scalar_prefetch=4, grid=(ntile,),
        in_specs=[pl.BlockSpec((MOE_TD, D), lambda i, *_: (i, 0)),
                  pl.BlockSpec((MOE_TD, 128), lambda i, *_: (i, 0)),
                  pl.BlockSpec((None, 1, 128), lambda i, *_: (i, 0, 0)),
                  pl.BlockSpec((None, 1, 128), lambda i, *_: (i, 0, 0)),
                  shared(sg), shared(su), shared(sd),
                  pl.BlockSpec(memory_space=pl.ANY)],
        out_specs=pl.BlockSpec((MOE_TD, D), lambda i, *_: (i, 0)),
        scratch_shapes=[pltpu.VMEM((2, MOE_LS, D), BF16), pltpu.SemaphoreType.DMA((2,))])
    return pl.pallas_call(
        functools.partial(_combine_kernel, ntile=ntile),
        grid_spec=grid_spec,
        out_shape=jax.ShapeDtypeStruct((h_all.shape[0], D), F32),
        compiler_params=_cparams(("arbitrary",)),
        name="moe_combine",
    )(dst.reshape(-1), lo.reshape(-1), n.reshape(-1), pieces.reshape(-1), h_all, comb_all,
      lof.reshape(PLAN_ROWS, 1, 128), n.astype(F32).reshape(PLAN_ROWS, 1, 128), sg, su, sd, ys)


def _moe(h_all, comb_all, wg, wu, wd, sg, su, sd, *, layer, n_tokens):
    ntile = h_all.shape[0] // MOE_TD
    max_rows = n_tokens * TOPK + ntile * NE * (MOE_SEG - 1) + NE * (MOE_TR - 1)
    ntmax = -(-max_rows // MOE_TR)
    plan = _moe_plan(comb_all, ntmax=ntmax)
    xs = _moe_dispatch(plan, h_all, comb_all, ntmax=ntmax)
    ys = _moe_ffn(plan, xs, wg, wu, wd, layer=layer, ntmax=ntmax)
    return _moe_combine(plan, ys, h_all, comb_all, sg, su, sd, layer=layer)


def _final_kernel(x_ref, f_ref, pmod_ref, png_ref, o_ref):
    o_ref[...] = _residual_in(x_ref, f_ref, pmod_ref, png_ref)


def _final(x, f_all, pmods, png, *, nb, seq_len, tm, row0):
    nt = seq_len // tm
    t0 = row0 // tm
    return pl.pallas_call(
        _final_kernel,
        grid=(nb, nt),
        in_specs=[pl.BlockSpec((tm, D), lambda b, t: (b * nt + t, 0)),
                  pl.BlockSpec((tm, D), lambda b, t: (t0 + b * nt + t, 0)),
                  pl.BlockSpec((None,) + pmods.shape[1:], lambda b, t: (b, 0, 0, 0)),
                  pl.BlockSpec(png.shape, lambda b, t: (0, 0))],
        out_specs=pl.BlockSpec((tm, D), lambda b, t: (b * nt + t, 0)),
        out_shape=jax.ShapeDtypeStruct(x.shape, F32),
        compiler_params=_cparams(("arbitrary", "arbitrary")),
        name="final_residual",
    )(x, f_all, pmods, png)


def kernel(x_prompt, x_sample, c_prompt, c_sample, state_conv, state_shift, state_wkv, w_mod, b_mod, norm_g, ab_w_in, ab_conv_w, ab_mu, ab_w0, ab_w_decay_up, ab_a0, ab_a_up, ab_g_up, ab_k_k, ab_k_a, ab_r_k, ab_gn_g, ab_gn_b, ab_w_out, gm_w_in, gm_b_in, gm_ln_g, gm_ln_b, gm_w_s, gm_b_s, gm_w_out, moe_w_router, moe_b_router, moe_w_gate, moe_w_up, moe_w_down, moe_ws_gate, moe_ws_up, moe_ws_down):
    bsz, seq_len, _ = x_prompt.shape
    nsamp = x_sample.shape[0]
    nheads = DB // HEAD
    n_prompt = bsz * seq_len
    n_all = -(-(n_prompt + nsamp) // MOE_TD) * MOE_TD
    tm = 512

    xp = x_prompt.reshape(n_prompt, D)
    xs = x_sample.reshape(nsamp, D)

    m = _modulation(jnp.concatenate([c_prompt, c_sample], axis=0), w_mod, b_mod)
    mods_p = [m[l, :bsz].reshape(bsz, 6, 1, D) for l in range(2)]
    mods_s = [m[l, bsz:].reshape(nsamp, 6, D).transpose(1, 0, 2)[None] for l in range(2)]

    row = lambda t: t.reshape(1, -1)
    pad_rows = lambda t, lo, hi: jnp.pad(t, ((lo, hi), (0, 0)))
    small = [ab_conv_w[0], row(ab_mu[0]), row(ab_w0[0]),
             pad_rows(ab_w_decay_up[0], 0, 64).astype(BF16), row(ab_a0[0]),
             pad_rows(ab_a_up[0], 64, 0).astype(BF16), ab_g_up[0].astype(BF16),
             row(ab_k_k[0]), row(ab_k_a[0])]
    win0 = ab_w_in[0].astype(BF16)
    wo0 = ab_w_out[0].astype(BF16)
    rk, gng, gnb = row(ab_r_k[0]), row(ab_gn_g[0]), row(ab_gn_b[0])
    wrt = [moe_w_router[l].T for l in range(2)]
    br = [moe_b_router[l].reshape(NE, 1) for l in range(2)]

    h2_all = jnp.zeros((n_all, D), BF16)
    comb_all = jnp.zeros((n_all, 128), F32)

    ya, r, lw, k, v, kk, a, g, ctail, stail = _ab_pre(
        xp, mods_p[0], norm_g[0], win0, small, None, nb=bsz, seq_len=seq_len, tm=tm, seq_mode=True)
    conv_p = ctail.reshape(bsz, 8, DA)[:, 6:8][None]
    shift_p = stail.reshape(bsz, 8, PB)[:, 7][None]
    yb, sfin = _wkv_seq(r, lw, k, v, kk, a, g, rk, gng, gnb, nb=bsz, seq_len=seq_len)
    sfin = sfin.reshape(bsz, nheads // 2, 2, HEAD, 2, HEAD)
    wkv_p = jnp.stack([sfin[:, :, 0, :, 0, :], sfin[:, :, 1, :, 1, :]], axis=2)
    wkv_p = wkv_p.reshape(bsz, nheads, HEAD, HEAD).transpose(0, 1, 3, 2)[None]
    xp1, h2_all, comb_all = _ab_post(ya, yb, xp, mods_p[0], norm_g[0], wo0, wrt[0], br[0], h2_all, comb_all,
                                     nb=bsz, seq_len=seq_len, tm=tm, row0=0)

    prev = [state_conv[0, :, 0], state_conv[0, :, 1], state_shift[0]]
    ya_s, r, lw, k, v, kk, a, g, gated_s, pb_s = _ab_pre(
        xs, mods_s[0], norm_g[0], win0, small, prev, nb=1, seq_len=nsamp, tm=nsamp, seq_mode=False)
    conv_s = jnp.stack([state_conv[0, :, 1], gated_s], axis=1)[None]
    shift_s = pb_s[None]
    vecs = [t.reshape(nsamp * nheads, 1, HEAD) for t in (r, lw, k, v, kk, a, g)]
    params = [t.reshape(nheads, 1, HEAD) for t in (ab_r_k[0], ab_gn_g[0], ab_gn_b[0])]
    yb_s, snew = _wkv_step(state_wkv[0].reshape(nsamp * nheads, HEAD, HEAD), vecs, params, nheads=nheads)
    wkv_s = snew.reshape(1, nsamp, nheads, HEAD, HEAD)
    xs1, h2_all, comb_all = _ab_post(ya_s, yb_s.reshape(nsamp, DB).astype(BF16), xs, mods_s[0], norm_g[0], wo0,
                                     wrt[0], br[0], h2_all, comb_all, nb=1, seq_len=nsamp, tm=nsamp, row0=n_prompt)

    def moe_layer(l, h_all, c_all):
        return _moe(h_all, c_all, moe_w_gate, moe_w_up, moe_w_down,
                    moe_ws_gate, moe_ws_up, moe_ws_down, layer=l, n_tokens=n_prompt + nsamp)

    f0 = moe_layer(0, h2_all, comb_all)

    gw = [gm_w_in[0].astype(BF16), row(gm_b_in[0]), row(gm_ln_g[0]), row(gm_ln_b[0])]
    gw_seq = gw + [gm_w_s[0], gm_b_s[0].T, gm_w_out[0].astype(BF16), wrt[1], br[1]]
    gw_step = gw + [row(jnp.repeat(gm_w_s[0, :, 0, 0], CHUNK)), row(jnp.repeat(gm_b_s[0, :, 0], CHUNK)),
                    gm_w_out[0].astype(BF16), wrt[1], br[1]]
    tm1 = 512
    xp2, h2_all, comb_all = _gmlp(xp1, f0, mods_p[0], norm_g[0], mods_p[1], norm_g[1], gw_seq, h2_all, comb_all,
                                  nb=bsz, seq_len=seq_len, tm=tm1, row0=0, seq_mode=True)
    xs2, h2_all, comb_all, chunk_v = _gmlp(xs1, f0, mods_s[0], norm_g[0], mods_s[1], norm_g[1], gw_step, h2_all,
                                           comb_all, nb=1, seq_len=nsamp, tm=nsamp, row0=n_prompt, seq_mode=False)
    f1 = moe_layer(1, h2_all, comb_all)
    y_p = _final(xp2, f1, mods_p[1], norm_g[1], nb=bsz, seq_len=seq_len, tm=tm, row0=0)
    y_s = _final(xs2, f1, mods_s[1], norm_g[1], nb=1, seq_len=nsamp, tm=nsamp, row0=n_prompt)

    return (y_p.reshape(bsz, seq_len, D), y_s.reshape(nsamp, 1, D), conv_p, conv_s, shift_p, shift_s,
            wkv_p, wkv_s, chunk_v.reshape(1, nsamp, 1, D))
```

```python
import functools

import jax
import jax.numpy as jnp
from jax import lax
from jax.experimental import pallas as pl
from jax.experimental.pallas import tpu as pltpu

F32 = jnp.float32
BF16 = jnp.bfloat16
HI = lax.Precision.HIGHEST

D = 1024
DA = 512
DB = 512
HEAD = 64
PB = 1792
PAB = 3 * DA + PB
NE = 64
NG = 8
EPG = NE // NG
TOPG = 4
TOPK = 8
DE = 256
ROUTE_SCALE = 2.5
EPS = 1e-6
GN_EPS = 64e-5
CHUNK = 128
WKV_SLAB = 128
WKV_CHUNK = 32
WKV_PAIRS_PER_STEP = 4
WKV_P_STATE = 1
WKV_P_INV = 1
MOE_TD = 512
MOE_SEG = 16
MOE_LS = 5120
MOE_CH = 1024
MOE_PIECES = (64, 16)
MOE_TR = 1024
MOE_XW = D + 128
PLAN_ROWS = 128

VMEM_LIMIT = 56 * 1024 * 1024


def _cparams(sem):
    return pltpu.CompilerParams(dimension_semantics=sem, vmem_limit_bytes=VMEM_LIMIT)


def _dot(a, b, precision=None):
    return jnp.dot(a, b, preferred_element_type=F32, precision=precision)


def _dot_nt(a, b, precision=None):
    return lax.dot_general(a, b, (((1,), (1,)), ((), ())), preferred_element_type=F32, precision=precision)


def _iota(shape, dim):
    return lax.broadcasted_iota(jnp.int32, shape, dim)


def _rms(x, g):
    return x * lax.rsqrt(jnp.mean(x * x, axis=-1, keepdims=True) + EPS) * g


def _silu(x):
    return x * jax.nn.sigmoid(x)


def _seg_ones(n, seg):
    return (_iota((n, n), 0) // seg == _iota((n, n), 1) // seg).astype(F32)


def _mod_kernel(c_ref, w_ref, b_ref, o_ref):
    s = _silu(c_ref[...]).astype(BF16)
    o_ref[...] = _dot(s, w_ref[...].astype(BF16)) + b_ref[...]


def _modulation(c_all, w_mod, b_mod):
    depth, _, width = w_mod.shape
    rows = c_all.shape[0]
    bn = 512
    return pl.pallas_call(
        _mod_kernel,
        grid=(depth, width // bn),
        in_specs=[pl.BlockSpec((rows, D), lambda l, j: (0, 0)),
                  pl.BlockSpec((None, D, bn), lambda l, j: (l, 0, j)),
                  pl.BlockSpec((None, 1, bn), lambda l, j: (l, 0, j))],
        out_specs=pl.BlockSpec((None, rows, bn), lambda l, j: (l, 0, j)),
        out_shape=jax.ShapeDtypeStruct((depth, rows, width), F32),
        compiler_params=_cparams(("arbitrary", "arbitrary")),
        name="modulation",
    )(c_all, w_mod, b_mod.reshape(depth, 1, width))


def _residual_in(x_ref, f_ref, pmod_ref, png_ref):
    x = x_ref[...]
    if f_ref is None:
        return x
    return x + pmod_ref[5] * _rms(f_ref[...], png_ref[3:4, :])


def _route(h, wrt_ref, br_ref):
    tm = h.shape[0]
    logits = _mm(wrt_ref[...], h, 2, 2, nt=True)
    scores = jax.nn.sigmoid(logits)
    sel = scores + br_ref[...]
    sub = _iota((EPG, tm), 0).astype(F32)
    neg = jnp.float32(-jnp.inf)
    blks, sblks, gscore = [], [], []
    for g in range(NG):
        blk = sel[g * EPG:(g + 1) * EPG, :]
        m1 = jnp.max(blk, axis=0, keepdims=True)
        first = jnp.min(jnp.where(blk == m1, sub, float(EPG)), axis=0, keepdims=True)
        m2 = jnp.max(jnp.where(sub == first, neg, blk), axis=0, keepdims=True)
        blks.append(blk)
        sblks.append(scores[g * EPG:(g + 1) * EPG, :])
        gscore.append(m1 + m2)
    masked = []
    for g in range(NG):
        cnt = jnp.zeros((1, tm), F32)
        for g2 in range(NG):
            if g2 == g:
                continue
            beats = (gscore[g2] >= gscore[g]) if g2 < g else (gscore[g2] > gscore[g])
            cnt = cnt + jnp.where(beats, 1.0, 0.0)
        masked.append(jnp.where(cnt < TOPG, blks[g], neg))
    cnts = [jnp.zeros((EPG, tm), F32) for _ in range(NG)]
    for g2 in range(NG):
        for j in range(EPG):
            row = masked[g2][j:j + 1, :]
            for g in range(NG):
                ge = jnp.where(row >= masked[g], 1.0, 0.0)
                gt = jnp.where(row > masked[g], 1.0, 0.0)
                if g > g2:
                    cnts[g] = cnts[g] + ge
                elif g < g2:
                    cnts[g] = cnts[g] + gt
                else:
                    cnts[g] = cnts[g] + jnp.where(sub > j, ge, gt)
    wts = [jnp.where(cnts[g] < TOPK, sblks[g], 0.0) for g in range(NG)]
    tot = wts[0]
    for g in range(1, NG):
        tot = tot + wts[g]
    denom = jnp.sum(tot, axis=0, keepdims=True)
    comb_t = jnp.concatenate([w / denom * ROUTE_SCALE for w in wts] + [jnp.zeros((NE, tm), F32)], axis=0)
    return comb_t.T


def _post(y, x, mod_ref, ng_ref, wrt_ref, br_ref, xo_ref, h2_ref, comb_ref):
    xn = x + mod_ref[2] * _rms(y, ng_ref[1:2, :])
    xo_ref[...] = xn
    h2 = _rms(xn, ng_ref[2:3, :]) * (1.0 + mod_ref[4]) + mod_ref[3]
    h2_ref[...] = h2.astype(BF16)
    comb_ref[...] = _route(h2, wrt_ref, br_ref)


def _ab_pre_kernel(*refs, seq_mode, tm):
    it = iter(refs)
    x_ref, mod_ref, ng_ref, win_ref = next(it), next(it), next(it), next(it)
    cw_ref, mu_ref, w0_ref, wd_ref, a0_ref, wa_ref, wg_ref, kk_ref, ka_ref = (next(it) for _ in range(9))
    if not seq_mode:
        p2_ref, p1_ref, ps_ref = next(it), next(it), next(it)
    ya_ref, r_ref, lw_ref, k_ref, v_ref, kkn_ref, a_ref, g_ref, ctail_ref, stail_ref = (next(it) for _ in range(10))
    if seq_mode:
        cc_ref, sc_ref = next(it), next(it)

        @pl.when(pl.program_id(1) == 0)
        def _():
            cc_ref[...] = jnp.zeros_like(cc_ref)
            sc_ref[...] = jnp.zeros_like(sc_ref)

    x = x_ref[...]
    h = _rms(x, ng_ref[0:1, :]) * (1.0 + mod_ref[1]) + mod_ref[0]
    proj = _dot(h.astype(BF16), win_ref[...])
    a_h, a_b, a_c = proj[:, 0:DA], proj[:, DA:2 * DA], proj[:, 2 * DA:3 * DA]
    pb = proj[:, 3 * DA:]
    gated = a_c * a_h
    if seq_mode:
        rows = _iota((tm, 1), 0)
        g1 = jnp.where(rows == 0, cc_ref[7:8, :], pltpu.roll(gated, 1, 0))
        g2 = jnp.where(rows == 0, cc_ref[6:7, :], jnp.where(rows == 1, cc_ref[7:8, :], pltpu.roll(gated, 2, 0)))
        pprev = jnp.where(rows == 0, sc_ref[7:8, :], pltpu.roll(pb, 1, 0))
        cc_ref[...] = gated[tm - 8:, :]
        sc_ref[...] = pb[tm - 8:, :]
        ctail_ref[...] = gated[tm - 8:, :]
        stail_ref[...] = pb[tm - 8:, :]
    else:
        g1, g2, pprev = p1_ref[...], p2_ref[...], ps_ref[...]
        ctail_ref[...] = gated
        stail_ref[...] = pb
    conv = g2 * cw_ref[0:1, :] + g1 * cw_ref[1:2, :] + gated * cw_ref[2:3, :]
    ya_ref[...] = (a_b * conv).astype(BF16)

    xm = pb + (pprev - pb) * mu_ref[...]
    r, k, v = xm[:, 0:DB], xm[:, DB:2 * DB], xm[:, 2 * DB:3 * DB]
    lowrank = xm[:, 3 * DB:3 * DB + 128]
    dg = xm[:, 3 * DB + 128:]
    w_pre = w0_ref[...] + _dot(jnp.tanh(lowrank).astype(BF16), wd_ref[...])
    z = -w_pre
    softplus = jnp.maximum(z, 0.0) + jnp.log(1.0 + jnp.exp(-jnp.abs(z)))
    lw_ref[...] = -jnp.exp(-softplus - 0.5)
    a = jax.nn.sigmoid(a0_ref[...] + _dot(lowrank.astype(BF16), wa_ref[...]))
    g_ref[...] = _dot(jax.nn.sigmoid(dg).astype(BF16), wg_ref[...])
    kk = k * kk_ref[...]
    ss = _mm(kk * kk, _seg_ones(DB, HEAD), 2, 1)
    kkn_ref[...] = kk / jnp.maximum(jnp.sqrt(ss), 1e-12)
    r_ref[...] = r
    k_ref[...] = k * (1.0 + (a - 1.0) * ka_ref[...])
    v_ref[...] = v
    a_ref[...] = a


def _ab_pre(x, mods, ng, win, small, prev, *, nb, seq_len, tm, seq_mode):
    nt = seq_len // tm
    rows = nb * seq_len
    tail = 8 if seq_mode else tm
    full = lambda shape: pl.BlockSpec(shape, lambda b, t: tuple(0 for _ in shape))
    tok = lambda w: pl.BlockSpec((tm, w), lambda b, t: (b * nt + t, 0))
    in_specs = [tok(D),
                pl.BlockSpec((None,) + mods.shape[1:], lambda b, t: (b, 0, 0, 0)),
                full(ng.shape), full(win.shape)] + [full(s.shape) for s in small]
    args = [x, mods, ng, win] + list(small)
    if not seq_mode:
        in_specs += [tok(DA), tok(DA), tok(PB)]
        args += list(prev)
    out_shape = ([jax.ShapeDtypeStruct((rows, DA), BF16)] + [jax.ShapeDtypeStruct((rows, DB), F32)] * 7
                 + [jax.ShapeDtypeStruct((nb * tail, DA), F32), jax.ShapeDtypeStruct((nb * tail, PB), F32)])
    out_specs = ([tok(DA)] + [tok(DB)] * 7
                 + [pl.BlockSpec((tail, DA), lambda b, t: (b, 0)), pl.BlockSpec((tail, PB), lambda b, t: (b, 0))])
    scratch = [pltpu.VMEM((8, DA), F32), pltpu.VMEM((8, PB), F32)] if seq_mode else []
    return pl.pallas_call(
        functools.partial(_ab_pre_kernel, seq_mode=seq_mode, tm=tm),
        grid=(nb, nt), in_specs=in_specs, out_specs=out_specs, out_shape=out_shape,
        scratch_shapes=scratch,
        compiler_params=_cparams(("arbitrary", "arbitrary")),
        name="ab_pre_seq" if seq_mode else "ab_pre_step",
    )(*args)


def _split(x):
    hi = x.astype(BF16)
    return hi, (x - hi.astype(F32)).astype(BF16)


def _mm(a, b, pa=1, pb=1, nt=False):
    ah, al = _split(a) if pa == 2 else (a.astype(BF16), None)
    bh, bl = _split(b) if pb == 2 else (b.astype(BF16), None)
    bx = 1 if nt else 0
    if pa == 2 and pb == 2:
        ah, bh = jnp.concatenate([ah, al, ah], axis=1), jnp.concatenate([bh, bh, bl], axis=bx)
    elif pa == 2:
        ah, bh = jnp.concatenate([ah, al], axis=1), jnp.concatenate([bh, bh], axis=bx)
    elif pb == 2:
        ah, bh = jnp.concatenate([ah, ah], axis=1), jnp.concatenate([bh, bl], axis=bx)
    return _dot_nt(ah, bh) if nt else _dot(ah, bh)


def _gn_gate(o, r, k, v, g, rk, gng, gnb, seg):
    mean = _mm(o, seg, 2, 1) * (1.0 / HEAD)
    d = o - mean
    var = _mm(d * d, seg, 2, 1) * (1.0 / HEAD)
    out = d * lax.rsqrt(var + GN_EPS) * gng + gnb
    out = out + _mm(r * k * rk, seg, 2, 1) * v
    return out * g


def _wkv_pair(r, lw, k, v, kk, a, st, cst):
    ts, c = WKV_SLAB, WKV_CHUNK
    nchunk = ts // c
    ps, pi = WKV_P_STATE, WKV_P_INV
    strict, incl, cum_lhs, m0, m1, headdiag, eye, eye2, blks, chunk_cols = cst

    l1 = lw.astype(BF16)
    rem = lw - l1.astype(F32)
    l2 = rem.astype(BF16)
    l3 = (rem - l2.astype(F32)).astype(BF16)
    cums = _dot(cum_lhs, jnp.concatenate([l1, l2, l3], axis=0))
    yield
    cl, ctot = cums[:ts], cums[ts:]
    e_neg = jnp.exp(-cl)
    e_end = jnp.exp(ctot - cl)
    at = -kk * jnp.exp(cl - lw)
    rt = r * jnp.exp(cl)
    beta = kk * a
    bk = jnp.concatenate([beta * e_neg, k * e_neg], axis=0)
    bend, kend = beta * e_end, k * e_end

    at_st = jnp.concatenate([at * m0, at * m1], axis=0)
    a_st = _mm(at_st, bk, ps, ps, nt=True)
    a_out = _mm(jnp.concatenate([rt * m0, rt * m1], axis=0), bk, 1, 1, nt=True)
    yield
    zero = jnp.zeros((ts, ts), F32)

    def two_heads(p0, p1):
        return jnp.concatenate([jnp.concatenate([p0, zero], axis=1), jnp.concatenate([zero, p1], axis=1)], axis=0)

    lbd = two_heads(a_st[:ts, :ts] * strict, a_st[ts:, :ts] * strict)
    d1 = jnp.where(blks[0], lbd, 0.0)
    tinv = eye2 + d1
    d2 = _mm(d1, d1, pi, pi)
    yield
    tinv = tinv + _mm(tinv, d2, pi, pi)
    d4 = _mm(d2, d2, pi, pi)
    yield
    tinv = tinv + _mm(tinv, d4, pi, pi)
    yield
    for lvl in range(1, len(blks)):
        off = jnp.where(blks[lvl] & jnp.logical_not(blks[lvl - 1]), lbd, 0.0)
        half = _mm(tinv, off, pi, pi)
        yield
        tinv = tinv + _mm(half, tinv, pi, pi)
        yield

    v_st = jnp.concatenate([v * m0, v * m1], axis=0)
    y_st = _mm(two_heads(a_st[:ts, ts:] * strict, a_st[ts:, ts:] * strict), v_st, ps, ps)
    yield
    wu = _mm(tinv, jnp.concatenate([at_st, y_st], axis=1), ps, ps)
    yield
    w =wu[:ts, :2 * HEAD] + wu[ts:, :2 * HEAD]
    u = wu[:ts, 2 * HEAD:] + wu[ts:, 2 * HEAD:]
    rhs = jnp.concatenate([jnp.concatenate([w, u], axis=1),
                           jnp.concatenate([jnp.zeros_like(v), v], axis=1)], axis=0)
    bend_t, kend_t = bend.T, kend.T
    lhs = [jnp.concatenate([bend_t * cm, kend_t * cm], axis=1) for cm in chunk_cols]
    mn_all = _mm(jnp.concatenate(lhs, axis=0), rhs, ps, ps)
    lhs_q = jnp.concatenate([jnp.concatenate([a_out[:ts, :ts] * incl, a_out[:ts, ts:] * incl], axis=1),
                             jnp.concatenate([a_out[ts:, :ts] * incl, a_out[ts:, ts:] * incl], axis=1)], axis=0)
    qo = _mm(lhs_q, rhs, 1, 1)
    yield
    q =rt + qo[:ts, :2 * HEAD] * m0 + qo[ts:, :2 * HEAD] * m1
    olocal = qo[:ts, 2 * HEAD:] * m0 + qo[ts:, 2 * HEAD:] * m1

    outs = []
    for ch in range(nchunk):
        mn = mn_all[ch * ts:(ch + 1) * ts]
        outs.append(_mm(q[ch * c:(ch + 1) * c], st, 1, 1) + olocal[ch * c:(ch + 1) * c])
        trans = eye * jnp.exp(ctot[ch * c:ch * c + 1, :]) + mn[:, :2 * HEAD] * headdiag
        st = _mm(trans, st, ps, ps) + mn[:, 2 * HEAD:] * headdiag
        yield
    return jnp.concatenate(outs, axis=0), st


def _round_robin(gens):
    results = [None] * len(gens)
    active = list(enumerate(gens))
    while active:
        still = []
        for i, gen in active:
            try:
                next(gen)
                still.append((i, gen))
            except StopIteration as stop:
                results[i] = stop.value
        active = still
    return results


def _wkv_seq_kernel(r_ref, lw_ref, k_ref, v_ref, kk_ref, a_ref, g_ref, rk_ref, gng_ref, gnb_ref,
                    y_ref, sfin_ref, st_ref, *, npp):
    ts, c = WKV_SLAB, WKV_CHUNK

    @pl.when(pl.program_id(2) == 0)
    def _():
        st_ref[...] = jnp.zeros_like(st_ref)

    ri, ci = _iota((ts, ts), 0), _iota((ts, ts), 1)
    same = ri // c == ci // c
    strict = (same & (ci < ri)).astype(F32)
    incl = (same & (ci <= ri)).astype(F32)
    cum_lhs = jnp.concatenate([incl, same.astype(F32)], axis=0).astype(BF16)
    cum_lhs = jnp.concatenate([cum_lhs] * 3, axis=1)
    lane = _iota((1, 2 * HEAD), 1)
    m0 = (lane < HEAD).astype(F32)
    m1 = 1.0 - m0
    headdiag = _seg_ones(2 * HEAD, HEAD)
    eye = (_iota((2 * HEAD, 2 * HEAD), 0) == _iota((2 * HEAD, 2 * HEAD), 1)).astype(F32)
    r2, c2 = _iota((2 * ts, 2 * ts), 0), _iota((2 * ts, 2 * ts), 1)
    eye2 = (r2 == c2).astype(F32)
    blks, b = [], 8
    while b <= c:
        blks.append(r2 // b == c2 // b)
        b *= 2
    tcol = _iota((1, ts), 1) // c
    chunk_cols = [(tcol == ch).astype(F32) for ch in range(ts // c)]
    cst = (strict, incl, cum_lhs, m0, m1, headdiag, eye, eye2, blks, chunk_cols)

    sls = [slice(p * 2 * HEAD, (p + 1) * 2 * HEAD) for p in range(npp)]
    res = _round_robin([_wkv_pair(r_ref[:, sl], lw_ref[:, sl], k_ref[:, sl], v_ref[:, sl], kk_ref[:, sl],
                                  a_ref[:, sl], st_ref[p], cst) for p, sl in enumerate(sls)])
    for p, sl in enumerate(sls):
        o, st = res[p]
        st_ref[p] = st
        y_ref[:, sl] = _gn_gate(o, r_ref[:, sl], k_ref[:, sl], v_ref[:, sl], g_ref[:, sl], rk_ref[:, sl],
                                gng_ref[:, sl], gnb_ref[:, sl], headdiag).astype(BF16)

    @pl.when(pl.program_id(2) == pl.num_programs(2) - 1)
    def _():
        sfin_ref[...] = st_ref[...]


def _wkv_seq(r, lw, k, v, kk, a, g, rk, gng, gnb, *, nb, seq_len):
    ns = seq_len // WKV_SLAB
    npp = WKV_PAIRS_PER_STEP
    width = 2 * HEAD * npp
    tok = pl.BlockSpec((WKV_SLAB, width), lambda b, p, s: (b * ns + s, p))
    par = pl.BlockSpec((1, width), lambda b, p, s: (0, p))
    return pl.pallas_call(
        functools.partial(_wkv_seq_kernel, npp=npp),
        grid=(nb, DB // width, ns),
        in_specs=[tok] * 7 + [par] * 3,
        out_specs=[tok, pl.BlockSpec((None, npp, 2 * HEAD, 2 * HEAD), lambda b, p, s: (b, p, 0, 0))],
        out_shape=[jax.ShapeDtypeStruct((nb * seq_len, DB), BF16),
                   jax.ShapeDtypeStruct((nb, DB // (2 * HEAD), 2 * HEAD, 2 * HEAD), F32)],
        scratch_shapes=[pltpu.VMEM((npp, 2 * HEAD, 2 * HEAD), F32)],
        compiler_params=_cparams(("arbitrary", "arbitrary", "arbitrary")),
        name="wkv_seq",
    )(r, lw, k, v, kk, a, g, rk, gng, gnb)


def _wkv_step_kernel(s_ref, r_ref, lw_ref, k_ref, v_ref, kk_ref, a_ref, g_ref, rk_ref, gng_ref, gnb_ref,
                     y_ref, so_ref):
    s = s_ref[...]
    r, k, v, kk, a = r_ref[...], k_ref[...], v_ref[...], kk_ref[...], a_ref[...]
    w = jnp.exp(lw_ref[...])
    eye = (_iota((1, HEAD, HEAD), 1) == _iota((1, HEAD, HEAD), 2)).astype(F32)
    v_col = jnp.sum(eye * v, axis=2, keepdims=True)
    s_kk = jnp.sum(s * kk, axis=2, keepdims=True)
    s = s * w - s_kk * (kk * a) + v_col * k
    so_ref[...] = s
    o_col = jnp.sum(s * r, axis=2, keepdims=True)
    o = jnp.sum(eye * o_col, axis=1, keepdims=True)
    mean = jnp.mean(o, axis=2, keepdims=True)
    d = o - mean
    var = jnp.mean(d * d, axis=2, keepdims=True)
    out = d * lax.rsqrt(var + GN_EPS) * gng_ref[...] + gnb_ref[...]
    out = out + jnp.sum(r * k * rk_ref[...], axis=2, keepdims=True) * v
    y_ref[...] = out * g_ref[...]


def _wkv_step(state, vecs, params, *, nheads):
    n = state.shape[0]
    tb = 8 * nheads
    sspec = pl.BlockSpec((tb, HEAD, HEAD), lambda i: (i, 0, 0))
    vspec = pl.BlockSpec((tb, 1, HEAD), lambda i: (i, 0, 0))
    pspec = pl.BlockSpec((tb, 1, HEAD), lambda i: (0, 0, 0))
    reps = tb // nheads
    params = [jnp.tile(p, (reps, 1, 1)) for p in params]
    return pl.pallas_call(
        _wkv_step_kernel,
        grid=(n // tb,),
        in_specs=[sspec] + [vspec] * 7 + [pspec] * 3,
        out_specs=[vspec, sspec],
        out_shape=[jax.ShapeDtypeStruct((n, 1, HEAD), F32), jax.ShapeDtypeStruct((n, HEAD, HEAD), F32)],
        compiler_params=_cparams(("arbitrary",)),
        name="wkv_step",
    )(state, *vecs, *params)


def _ab_post_kernel(ya_ref, yb_ref, x_ref, mod_ref, ng_ref, wo_ref, wrt_ref, br_ref,
                    h2in_ref, combin_ref, xo_ref, h2_ref, comb_ref):
    del h2in_ref, combin_ref
    y = _dot(ya_ref[...], wo_ref[0:DA, :]) + _dot(yb_ref[...], wo_ref[DA:, :])
    _post(y, x_ref[...], mod_ref, ng_ref, wrt_ref, br_ref, xo_ref, h2_ref, comb_ref)


def _ab_post(ya, yb, x, mods, ng, wo, wrt, br, h2_all, comb_all, *, nb, seq_len, tm, row0):
    nt = seq_len // tm
    t0 = row0 // tm
    full = lambda shape: pl.BlockSpec(shape, lambda b, t: tuple(0 for _ in shape))
    tok = lambda w: pl.BlockSpec((tm, w), lambda b, t: (b * nt + t, 0))
    tok_all = lambda w: pl.BlockSpec((tm, w), lambda b, t: (t0 + b * nt + t, 0))
    anyspec = pl.BlockSpec(memory_space=pl.ANY)
    return pl.pallas_call(
        _ab_post_kernel,
        grid=(nb, nt),
        in_specs=[tok(DA), tok(DB), tok(D),
                  pl.BlockSpec((None,) + mods.shape[1:], lambda b, t: (b, 0, 0, 0)),
                  full(ng.shape), full(wo.shape), full(wrt.shape), full(br.shape), anyspec, anyspec],
        out_specs=[tok(D), tok_all(D), tok_all(128)],
        out_shape=[jax.ShapeDtypeStruct(x.shape, F32), jax.ShapeDtypeStruct(h2_all.shape, BF16),
                   jax.ShapeDtypeStruct(comb_all.shape, F32)],
        input_output_aliases={8: 1, 9: 2},
        compiler_params=_cparams(("arbitrary", "arbitrary")),
        name="ab_post",
    )(ya, yb, x, mods, ng, wo, wrt, br, h2_all, comb_all)


def _gmlp_kernel(*refs, seq_mode, tm):
    it = iter(refs)
    x_ref, f_ref, pmod_ref, png_ref, mod_ref, ng_ref = (next(it) for _ in range(6))
    win_ref, bin_ref, lng_ref, lnb_ref, ws_ref, bs_ref, wo_ref, wrt_ref, br_ref = (next(it) for _ in range(9))
    next(it), next(it)
    xo_ref, h2_ref, comb_ref = next(it), next(it), next(it)
    if not seq_mode:
        cv_ref = next(it)
    x = _residual_in(x_ref, f_ref, pmod_ref, png_ref)
    h = _rms(x, ng_ref[0:1, :]) * (1.0 + mod_ref[1]) + mod_ref[0]
    z = jax.nn.gelu(_dot(h.astype(BF16), win_ref[...]) + bin_ref[...])
    u, v = z[:, :D], z[:, D:]
    mu = jnp.mean(v, axis=-1, keepdims=True)
    var = jnp.mean(jnp.square(v - mu), axis=-1, keepdims=True)
    v = (v - mu) * lax.rsqrt(var + EPS) * lng_ref[...] + lnb_ref[...]
    if seq_mode:
        vb = v.astype(BF16)
        causal = _iota((CHUNK, CHUNK), 1) <= _iota((CHUNK, CHUNK), 0)
        cols = []
        for g in range(D // CHUNK):
            wsg = jnp.where(causal, ws_ref[g], 0.0).astype(BF16)
            bsg = bs_ref[:, g:g + 1]
            rows = [_dot(wsg, vb[c * CHUNK:(c + 1) * CHUNK, g * CHUNK:(g + 1) * CHUNK]) + bsg
                    for c in range(tm // CHUNK)]
            cols.append(jnp.concatenate(rows, axis=0) if len(rows) > 1 else rows[0])
        s = jnp.concatenate(cols, axis=1)
    else:
        cv_ref[...] = v
        s = v * ws_ref[...] + bs_ref[...]
    y = _dot((u * s).astype(BF16), wo_ref[...])
    _post(y, x, mod_ref, ng_ref, wrt_ref, br_ref, xo_ref, h2_ref, comb_ref)


def _gmlp(x, f_all, pmods, png, mods, ng, weights, h2_all, comb_all, *, nb, seq_len, tm, row0, seq_mode):
    nt = seq_len // tm
    t0 = row0 // tm
    full = lambda shape: pl.BlockSpec(shape, lambda b, t: tuple(0 for _ in shape))
    tok = lambda w: pl.BlockSpec((tm, w), lambda b, t: (b * nt + t, 0))
    tok_all = lambda w: pl.BlockSpec((tm, w), lambda b, t: (t0 + b * nt + t, 0))
    modspec = lambda m: pl.BlockSpec((None,) + m.shape[1:], lambda b, t: (b, 0, 0, 0))
    anyspec = pl.BlockSpec(memory_space=pl.ANY)
    in_specs = ([tok(D), tok_all(D), modspec(pmods), full(png.shape), modspec(mods), full(ng.shape)]
                + [full(w.shape) for w in weights] + [anyspec, anyspec])
    out_specs = [tok(D), tok_all(D), tok_all(128)]
    out_shape = [jax.ShapeDtypeStruct(x.shape, F32), jax.ShapeDtypeStruct(h2_all.shape, BF16),
                 jax.ShapeDtypeStruct(comb_all.shape, F32)]
    if not seq_mode:
        out_specs.append(tok(D))
        out_shape.append(jax.ShapeDtypeStruct(x.shape, F32))
    n_in = len(in_specs)
    return pl.pallas_call(
        functools.partial(_gmlp_kernel, seq_mode=seq_mode, tm=tm),
        grid=(nb, nt), in_specs=in_specs, out_specs=out_specs, out_shape=out_shape,
        input_output_aliases={n_in - 2: 1, n_in - 1: 2},
        compiler_params=_cparams(("arbitrary", "arbitrary")),
        name="gmlp_seq" if seq_mode else "gmlp_step",
    )(x, f_all, pmods, png, mods, ng, *weights, h2_all, comb_all)


def _strict_lower(n):
    return _iota((n, n), 1) < _iota((n, n), 0)


def _col_of_row(row):
    return jnp.broadcast_to(row, (128, 128)).T[:, 0:1]


def _plan_kernel(comb_ref, dst_ref, lo_ref, n_ref, te_ref, misc_ref, lof_ref, pieces_ref, cnt_ref, *, ntile, te_cols):
    i = pl.program_id(0)

    @pl.when(i == 0)
    def _():
        cnt_ref[...] = jnp.zeros_like(cnt_ref)

    sel = jnp.where(comb_ref[...] > 0.0, 1.0, 0.0)
    cnt_ref[pl.ds(i, 1), :] = jnp.sum(sel, axis=0, keepdims=True)

    @pl.when(i == ntile - 1)
    def _():
        c = cnt_ref[...]
        cpad = jnp.floor((c + (MOE_SEG - 1.0)) * (1.0 / MOE_SEG)) * MOE_SEG
        cb = cpad.astype(BF16)
        lower = jnp.where(_strict_lower(PLAN_ROWS), 1.0, 0.0).astype(BF16)
        upper = jnp.where(_iota((128, 128), 0) < _iota((128, 128), 1), 1.0, 0.0).astype(BF16)
        before = _dot(lower, cb)
        lo = _dot(cb, upper)
        tot = jnp.sum(cpad, axis=0, keepdims=True)
        rt = jnp.floor((tot + (MOE_TR - 1.0)) * (1.0 / MOE_TR))
        gt = _dot(jnp.broadcast_to(rt, (8, 128)).astype(BF16), upper)[0:1, :]
        dst_ref[...] = (gt * MOE_TR + before).astype(jnp.int32)
        lo_ref[...] = lo.astype(jnp.int32)
        lof_ref[...] = lo
        n_ref[...] = cpad.astype(jnp.int32)
        ends_col = _col_of_row(gt + rt)
        jrow = _iota((1, te_cols), 1).astype(F32)
        te = jnp.sum(jnp.where(ends_col <= jrow, 1.0, 0.0), axis=0, keepdims=True)
        te_ref[...] = jnp.broadcast_to(jnp.minimum(te, NE - 1.0), (8, te_cols)).astype(jnp.int32)
        nused = jnp.sum(rt, axis=1, keepdims=True)
        rowi = _iota((8, 128), 0)
        misc = jnp.where(rowi == 0, gt * MOE_TR + tot, jnp.where(rowi == 1, rt * MOE_TR - tot, nused))
        misc_ref[...] = misc.astype(jnp.int32)
        lane = _iota((1, 128), 1)
        pieces = jnp.zeros((PLAN_ROWS, 128), F32)
        taken = jnp.zeros_like(cpad)
        for k, size in enumerate(MOE_PIECES):
            cnt = jnp.floor((cpad - taken) * (1.0 / size))
            taken = taken + cnt * size
            pieces = pieces + jnp.where(lane == k, jnp.sum(cnt, axis=1, keepdims=True), 0.0)
        pieces_ref[...] = pieces.astype(jnp.int32)


def _moe_plan(comb_all, *, ntmax):
    ntile = comb_all.shape[0] // MOE_TD
    te_cols = -(-ntmax // 128) * 128
    full = lambda shape: pl.BlockSpec(shape, lambda i: tuple(0 for _ in shape))
    shapes = [((PLAN_ROWS, 128), jnp.int32)] * 3 + [((8, te_cols), jnp.int32), ((8, 128), jnp.int32),
                                                   ((PLAN_ROWS, 128), F32), ((PLAN_ROWS, 128), jnp.int32)]
    return pl.pallas_call(
        functools.partial(_plan_kernel, ntile=ntile, te_cols=te_cols),
        grid=(ntile,),
        in_specs=[pl.BlockSpec((MOE_TD, 128), lambda i: (i, 0))],
        out_specs=[full(s) for s, _ in shapes],
        out_shape=[jax.ShapeDtypeStruct(s, d) for s, d in shapes],
        scratch_shapes=[pltpu.VMEM((PLAN_ROWS, 128), F32)],
        compiler_params=_cparams(("arbitrary",)),
        name="moe_plan",
    )(comb_all)


def _rank_plus(comb):
    sel = jnp.where(comb > 0.0, 1.0, 0.0)
    lower = jnp.where(_strict_lower(MOE_TD), 1.0, 0.0).astype(BF16)
    return sel * (_dot(lower, sel.astype(BF16)) + 1.0)


def _for_each_piece(n, fn):
    off = 0
    for size in MOE_PIECES:
        cnt = (n - off) // size

        def body(p, carry, off=off, size=size):
            fn(off + p * size, size)
            return carry

        lax.fori_loop(0, cnt, body, 0)
        off = off + cnt * size


def _wait_pieces(pieces_s, tile, make_copy):
    for k, size in enumerate(MOE_PIECES):
        def wait_one(p, carry, size=size):
            make_copy(size).wait()
            return carry

        lax.fori_loop(0, pieces_s[tile * 128 + k], wait_one, 0)


def _dispatch_kernel(dst_s, lo_s, n_s, misc_s, pieces_s, h_ref, comb_ref, lof_ref, nf_ref, xs_hbm, buf, zbuf, sem,
                     tsem, *, ntile):
    i = pl.program_id(0)
    slot = lax.rem(i, 2)
    comb = comb_ref[...]
    lo_row = lof_ref[...]
    hi_row = lo_row + nf_ref[...]
    rp_hi, rp_lo = _split(_rank_plus(comb).T)
    rhs_rank = jnp.concatenate([rp_hi, rp_lo], axis=0)
    c_hi, c_lo = _split(comb)
    h_ext = jnp.concatenate([h_ref[...], c_hi, c_lo], axis=1)
    lane_lo = _iota((1, 128), 1) < 64
    for ch in range(MOE_LS // MOE_CH):
        srow = (_iota((MOE_CH, 1), 0) + ch * MOE_CH).astype(F32)
        e_t = jnp.where((srow >= lo_row) & (srow < hi_row), 1.0, 0.0)
        e_tb = e_t.astype(BF16)
        rank_of = _dot(jnp.concatenate([e_tb, e_tb], axis=1), rhs_rank)
        target = srow + 1.0 - jnp.sum(e_t * lo_row, axis=1, keepdims=True)
        perm = jnp.where(rank_of == target, 1.0, 0.0).astype(BF16)
        rows = pl.ds(ch * MOE_CH, MOE_CH)
        sorted_rows = _dot(perm, h_ext)
        buf[slot, rows, :D] = sorted_rows[:, :D].astype(BF16)
        w = jnp.sum((sorted_rows[:, D:D + 128] + sorted_rows[:, D + 128:]) * e_t, axis=1, keepdims=True)
        w_hi = w.astype(BF16).astype(F32)
        buf[slot, rows, D:] = jnp.where(lane_lo, w_hi, w - w_hi).astype(BF16)

    def copy_out(sl, src_row, dst_row, size):
        return pltpu.make_async_copy(buf.at[sl, pl.ds(pl.multiple_of(src_row, MOE_SEG), size)],
                                     xs_hbm.at[pl.ds(pl.multiple_of(dst_row, MOE_SEG), size)], sem.at[sl])

    def start_seg(e, carry):
        idx = i * 128 + e
        lo, dst = lo_s[idx], dst_s[idx]
        _for_each_piece(n_s[idx], lambda off, size: copy_out(slot, lo + off, dst + off, size).start())
        return carry

    lax.fori_loop(0, NE, start_seg, 0)

    @pl.when(i > 0)
    def _():
        _wait_pieces(pieces_s, i - 1, lambda size: copy_out(1 - slot, 0, 0, size))

    @pl.when(i == ntile - 1)
    def _():
        _wait_pieces(pieces_s, i, lambda size: copy_out(slot, 0, 0, size))
        zbuf[...] = jnp.zeros_like(zbuf)

        def tail_copy(e, off, size):
            return pltpu.make_async_copy(
                zbuf.at[pl.ds(0, size)], xs_hbm.at[pl.ds(pl.multiple_of(misc_s[e] + off, MOE_SEG), size)], tsem)

        def tail_start(e, carry):
            _for_each_piece(misc_s[128 + e], lambda off, size: tail_copy(e, off, size).start())
            return carry

        def tail_wait(e, carry):
            _for_each_piece(misc_s[128 + e], lambda off, size: tail_copy(e, off, size).wait())
            return carry

        lax.fori_loop(0, NE, tail_start, 0)
        lax.fori_loop(0, NE, tail_wait, 0)


def _moe_dispatch(plan, h_all, comb_all, *, ntmax):
    dst, lo, n, _, misc, lof, pieces = plan
    ntile = h_all.shape[0] // MOE_TD
    grid_spec = pltpu.PrefetchScalarGridSpec(
        num_scalar_prefetch=5, grid=(ntile,),
        in_specs=[pl.BlockSpec((MOE_TD, D), lambda i, *_: (i, 0)),
                  pl.BlockSpec((MOE_TD, 128), lambda i, *_: (i, 0)),
                  pl.BlockSpec((None, 1, 128), lambda i, *_: (i, 0, 0)),
                  pl.BlockSpec((None, 1, 128), lambda i, *_: (i, 0, 0))],
        out_specs=pl.BlockSpec(memory_space=pl.ANY),
        scratch_shapes=[pltpu.VMEM((2, MOE_LS, MOE_XW), BF16), pltpu.VMEM((MOE_PIECES[0], MOE_XW), BF16),
                        pltpu.SemaphoreType.DMA((2,)), pltpu.SemaphoreType.DMA(())])
    return pl.pallas_call(
        functools.partial(_dispatch_kernel, ntile=ntile),
        grid_spec=grid_spec,
        out_shape=jax.ShapeDtypeStruct((ntmax * MOE_TR, MOE_XW), BF16),
        compiler_params=_cparams(("arbitrary",)),
        name="moe_dispatch",
    )(dst.reshape(-1), lo.reshape(-1), n.reshape(-1), misc.reshape(-1), pieces.reshape(-1), h_all, comb_all,
      lof.reshape(PLAN_ROWS, 1, 128), n.astype(F32).reshape(PLAN_ROWS, 1, 128))


def _ffn_kernel(te_s, misc_s, xs_ref, wg_ref, wu_ref, wd_ref, ys_ref, wgb, wub, wdb):
    j = pl.program_id(0)

    @pl.when(j < misc_s[2 * 128])
    def _():
        @pl.when((j == 0) | (te_s[j] != te_s[jnp.maximum(j - 1, 0)]))
        def _():
            wgb[...] = wg_ref[...].astype(BF16)
            wub[...] = wu_ref[...].astype(BF16)
            wdb[...] = wd_ref[...].astype(BF16)

        x = xs_ref[:, :D]
        w = xs_ref[:, D:D + 1].astype(F32) + xs_ref[:, D + 64:D + 65].astype(F32)
        hw = _silu(_dot(x, wgb[...])) * _dot(x, wub[...]) * w
        ys_ref[...] = _dot(hw.astype(BF16), wdb[...]).astype(BF16)


def _moe_ffn(plan, xs, wg, wu, wd, *, layer, ntmax):
    te, misc = plan[3], plan[4]

    def tile(j, misc_s):
        return jnp.minimum(j, misc_s[2 * 128] - 1)

    wspec = lambda shape: pl.BlockSpec((None, None) + shape,
                                       lambda j, te_s, misc_s: (layer, te_s[tile(j, misc_s)], 0, 0))
    grid_spec = pltpu.PrefetchScalarGridSpec(
        num_scalar_prefetch=2, grid=(ntmax,),
        in_specs=[pl.BlockSpec((MOE_TR, MOE_XW), lambda j, te_s, misc_s: (tile(j, misc_s), 0)),
                  wspec((D, DE)), wspec((D, DE)), wspec((DE, D))],
        out_specs=pl.BlockSpec((MOE_TR, D), lambda j, te_s, misc_s: (tile(j, misc_s), 0)),
        scratch_shapes=[pltpu.VMEM((D, DE), BF16), pltpu.VMEM((D, DE), BF16), pltpu.VMEM((DE, D), BF16)])
    return pl.pallas_call(
        _ffn_kernel, grid_spec=grid_spec,
        out_shape=jax.ShapeDtypeStruct((ntmax * MOE_TR, D), BF16),
        compiler_params=_cparams(("arbitrary",)),
        name="moe_ffn",
    )(te[0], misc.reshape(-1), xs, wg, wu, wd)


def _combine_kernel(dst_s, lo_s, n_s, pieces_s, h_ref, comb_ref, lof_ref, nf_ref, sg_ref, su_ref, sd_ref, ys_hbm,
                    o_ref, ybuf, sem, *, ntile):
    i = pl.program_id(0)
    slot = lax.rem(i, 2)

    def copy_in(sl, src_row, dst_row, size):
        return pltpu.make_async_copy(ys_hbm.at[pl.ds(pl.multiple_of(src_row, MOE_SEG), size)],
                                     ybuf.at[sl, pl.ds(pl.multiple_of(dst_row, MOE_SEG), size)], sem.at[sl])

    def start_tile(tile, sl):
        def start_seg(e, carry):
            idx = tile * 128 + e
            lo, src = lo_s[idx], dst_s[idx]
            _for_each_piece(n_s[idx], lambda off, size: copy_in(sl, src + off, lo + off, size).start())
            return carry

        lax.fori_loop(0, NE, start_seg, 0)

    @pl.when(i == 0)
    def _():
        ybuf[...] = jnp.zeros_like(ybuf)
        start_tile(0, 0)

    @pl.when(i + 1 < ntile)
    def _():
        start_tile(i + 1, 1 - slot)

    lo_row = lof_ref[...]
    hi_row = lo_row + nf_ref[...]
    lo_col, hi_col = _col_of_row(lo_row), _col_of_row(hi_row)
    rp_hi, rp_lo = _split(_rank_plus(comb_ref[...]))
    l_hi, l_lo = _split(jnp.broadcast_to(lo_row * (1.0 / MOE_SEG), (8, 128)))
    lhs_meta = jnp.concatenate([jnp.concatenate([rp_hi, rp_lo], axis=1), jnp.concatenate([l_hi, l_lo], axis=1)], axis=0)

    h = h_ref[...]
    hs = _silu(_dot(h, sg_ref[...].astype(BF16))) * _dot(h, su_ref[...].astype(BF16))
    acc = _dot(hs.astype(BF16), sd_ref[...].astype(BF16))

    _wait_pieces(pieces_s, i, lambda size: copy_in(slot, 0, 0, size))
    for ch in range(MOE_LS // MOE_CH):
        scol = (_iota((1, MOE_CH), 1) + ch * MOE_CH).astype(F32)
        e_m = jnp.where((scol >= lo_col) & (scol < hi_col), 1.0, 0.0).astype(BF16)
        meta = _dot(lhs_meta, jnp.concatenate([e_m, e_m], axis=0))
        target = scol + 1.0 - meta[MOE_TD:MOE_TD + 1, :] * MOE_SEG
        perm_t = jnp.where(meta[:MOE_TD] == target, 1.0, 0.0).astype(BF16)
        acc = acc + _dot(perm_t, ybuf[slot, pl.ds(ch * MOE_CH, MOE_CH), :])
    o_ref[...] = acc


def _moe_combine(plan, ys, h_all, comb_all, sg, su, sd, *, layer):
    dst, lo, n, _, _, lof, pieces = plan
    ntile = h_all.shape[0] // MOE_TD
    shared = lambda w: pl.BlockSpec((None,) + w.shape[1:], lambda i, *_: (layer, 0, 0))
    grid_spec = pltpu.PrefetchScalarGridSpec(
        num_scalar_prefetch=4, grid=(ntile,),
        in_specs=[pl.BlockSpec((MOE_TD, D), lambda i, *_: (i, 0)),
                  pl.BlockSpec((MOE_TD, 128), lambda i, *_: (i, 0)),
                  pl.BlockSpec((None, 1, 128), lambda i, *_: (i, 0, 0)),
                  pl.BlockSpec((None, 1, 128), lambda i, *_: (i, 0, 0)),
                  shared(sg), shared(su), shared(sd),
                  pl.BlockSpec(memory_space=pl.ANY)],
        out_specs=pl.BlockSpec((MOE_TD, D), lambda i, *_: (i, 0)),
        scratch_shapes=[pltpu.VMEM((2, MOE_LS, D), BF16), pltpu.SemaphoreType.DMA((2,))])
    return pl.pallas_call(
        functools.partial(_combine_kernel, ntile=ntile),
        grid_spec=grid_spec,
        out_shape=jax.ShapeDtypeStruct((h_all.shape[0], D), F32),
        compiler_params=_cparams(("arbitrary",)),
        name="moe_combine",
    )(dst.reshape(-1), lo.reshape(-1), n.reshape(-1), pieces.reshape(-1), h_all, comb_all,
      lof.reshape(PLAN_ROWS, 1, 128), n.astype(F32).reshape(PLAN_ROWS, 1, 128), sg, su, sd, ys)


def _moe(h_all, comb_all, wg, wu, wd, sg, su, sd, *, layer, n_tokens):
    ntile = h_all.shape[0] // MOE_TD
    max_rows = n_tokens * TOPK + ntile * NE * (MOE_SEG - 1) + NE * (MOE_TR - 1)
    ntmax = -(-max_rows // MOE_TR)
    plan = _moe_plan(comb_all, ntmax=ntmax)
    xs = _moe_dispatch(plan, h_all, comb_all, ntmax=ntmax)
    ys = _moe_ffn(plan, xs, wg, wu, wd, layer=layer, ntmax=ntmax)
    return _moe_combine(plan, ys, h_all, comb_all, sg, su, sd, layer=layer)


def _final_kernel(x_ref, f_ref, pmod_ref, png_ref, o_ref):
    o_ref[...] = _residual_in(x_ref, f_ref, pmod_ref, png_ref)


def _final(x, f_all, pmods, png, *, nb, seq_len, tm, row0):
    nt = seq_len // tm
    t0 = row0 // tm
    return pl.pallas_call(
        _final_kernel,
        grid=(nb, nt),
        in_specs=[pl.BlockSpec((tm, D), lambda b, t: (b * nt + t, 0)),
                  pl.BlockSpec((tm, D), lambda b, t: (t0 + b * nt + t, 0)),
                  pl.BlockSpec((None,) + pmods.shape[1:], lambda b, t: (b, 0, 0, 0)),
                  pl.BlockSpec(png.shape, lambda b, t: (0, 0))],
        out_specs=pl.BlockSpec((tm, D), lambda b, t: (b * nt + t, 0)),
        out_shape=jax.ShapeDtypeStruct(x.shape, F32),
        compiler_params=_cparams(("arbitrary", "arbitrary")),
        name="final_residual",
    )(x, f_all, pmods, png)


def kernel(x_prompt, x_sample, c_prompt, c_sample, state_conv, state_shift, state_wkv, w_mod, b_mod, norm_g, ab_w_in, ab_conv_w, ab_mu, ab_w0, ab_w_decay_up, ab_a0, ab_a_up, ab_g_up, ab_k_k, ab_k_a, ab_r_k, ab_gn_g, ab_gn_b, ab_w_out, gm_w_in, gm_b_in, gm_ln_g, gm_ln_b, gm_w_s, gm_b_s, gm_w_out, moe_w_router, moe_b_router, moe_w_gate, moe_w_up, moe_w_down, moe_ws_gate, moe_ws_up, moe_ws_down):
    bsz, seq_len, _ = x_prompt.shape
    nsamp = x_sample.shape[0]
    nheads = DB // HEAD
    n_prompt = bsz * seq_len
    n_all = -(-(n_prompt + nsamp) // MOE_TD) * MOE_TD
    tm = 512

    xp = x_prompt.reshape(n_prompt, D)
    xs = x_sample.reshape(nsamp, D)

    m = _modulation(jnp.concatenate([c_prompt, c_sample], axis=0), w_mod, b_mod)
    mods_p = [m[l, :bsz].reshape(bsz, 6, 1, D) for l in range(2)]
    mods_s = [m[l, bsz:].reshape(nsamp, 6, D).transpose(1, 0, 2)[None] for l in range(2)]

    row = lambda t: t.reshape(1, -1)
    pad_rows = lambda t, lo, hi: jnp.pad(t, ((lo, hi), (0, 0)))
    small = [ab_conv_w[0], row(ab_mu[0]), row(ab_w0[0]),
             pad_rows(ab_w_decay_up[0], 0, 64).astype(BF16), row(ab_a0[0]),
             pad_rows(ab_a_up[0], 64, 0).astype(BF16), ab_g_up[0].astype(BF16),
             row(ab_k_k[0]), row(ab_k_a[0])]
    win0 = ab_w_in[0].astype(BF16)
    wo0 = ab_w_out[0].astype(BF16)
    rk, gng, gnb = row(ab_r_k[0]), row(ab_gn_g[0]), row(ab_gn_b[0])
    wrt = [moe_w_router[l].T for l in range(2)]
    br = [moe_b_router[l].reshape(NE, 1) for l in range(2)]

    h2_all = jnp.zeros((n_all, D), BF16)
    comb_all = jnp.zeros((n_all, 128), F32)

    ya, r, lw, k, v, kk, a, g, ctail, stail = _ab_pre(
        xp, mods_p[0], norm_g[0], win0, small, None, nb=bsz, seq_len=seq_len, tm=tm, seq_mode=True)
    conv_p = ctail.reshape(bsz, 8, DA)[:, 6:8][None]
    shift_p = stail.reshape(bsz, 8, PB)[:, 7][None]
    yb, sfin = _wkv_seq(r, lw, k, v, kk, a, g, rk, gng, gnb, nb=bsz, seq_len=seq_len)
    sfin = sfin.reshape(bsz, nheads // 2, 2, HEAD, 2, HEAD)
    wkv_p = jnp.stack([sfin[:, :, 0, :, 0, :], sfin[:, :, 1, :, 1, :]], axis=2)
    wkv_p = wkv_p.reshape(bsz, nheads, HEAD, HEAD).transpose(0, 1, 3, 2)[None]
    xp1, h2_all, comb_all = _ab_post(ya, yb, xp, mods_p[0], norm_g[0], wo0, wrt[0], br[0], h2_all, comb_all,
                                     nb=bsz, seq_len=seq_len, tm=tm, row0=0)

    prev = [state_conv[0, :, 0], state_conv[0, :, 1], state_shift[0]]
    ya_s, r, lw, k, v, kk, a, g, gated_s, pb_s = _ab_pre(
        xs, mods_s[0], norm_g[0], win0, small, prev, nb=1, seq_len=nsamp, tm=nsamp, seq_mode=False)
    conv_s = jnp.stack([state_conv[0, :, 1], gated_s], axis=1)[None]
    shift_s = pb_s[None]
    vecs = [t.reshape(nsamp * nheads, 1, HEAD) for t in (r, lw, k, v, kk, a, g)]
    params = [t.reshape(nheads, 1, HEAD) for t in (ab_r_k[0], ab_gn_g[0], ab_gn_b[0])]
    yb_s, snew = _wkv_step(state_wkv[0].reshape(nsamp * nheads, HEAD, HEAD), vecs, params, nheads=nheads)
    wkv_s = snew.reshape(1, nsamp, nheads, HEAD, HEAD)
    xs1, h2_all, comb_all = _ab_post(ya_s, yb_s.reshape(nsamp, DB).astype(BF16), xs, mods_s[0], norm_g[0], wo0,
                                     wrt[0], br[0], h2_all, comb_all, nb=1, seq_len=nsamp, tm=nsamp, row0=n_prompt)

    def moe_layer(l, h_all, c_all):
        return _moe(h_all, c_all, moe_w_gate, moe_w_up, moe_w_down,
                    moe_ws_gate, moe_ws_up, moe_ws_down, layer=l, n_tokens=n_prompt + nsamp)

    f0 = moe_layer(0, h2_all, comb_all)

    gw = [gm_w_in[0].astype(BF16), row(gm_b_in[0]), row(gm_ln_g[0]), row(gm_ln_b[0])]
    gw_seq = gw + [gm_w_s[0], gm_b_s[0].T, gm_w_out[0].astype(BF16), wrt[1], br[1]]
    gw_step = gw + [row(jnp.repeat(gm_w_s[0, :, 0, 0], CHUNK)), row(jnp.repeat(gm_b_s[0, :, 0], CHUNK)),
                    gm_w_out[0].astype(BF16), wrt[1], br[1]]
    tm1 = 512
    xp2, h2_all, comb_all = _gmlp(xp1, f0, mods_p[0], norm_g[0], mods_p[1], norm_g[1], gw_seq, h2_all, comb_all,
                                  nb=bsz, seq_len=seq_len, tm=tm1, row0=0, seq_mode=True)
    xs2, h2_all, comb_all, chunk_v = _gmlp(xs1, f0, mods_s[0], norm_g[0], mods_s[1], norm_g[1], gw_step, h2_all,
                                           comb_all, nb=1, seq_len=nsamp, tm=nsamp, row0=n_prompt, seq_mode=False)
    f1 = moe_layer(1, h2_all, comb_all)
    y_p = _final(xp2, f1, mods_p[1], norm_g[1], nb=bsz, seq_len=seq_len, tm=tm, row0=0)
    y_s = _final(xs2, f1, mods_s[1], norm_g[1], nb=1, seq_len=nsamp, tm=nsamp, row0=n_prompt)

    return (y_p.reshape(bsz, seq_len, D), y_s.reshape(nsamp, 1, D), conv_p, conv_s, shift_p, shift_s,
            wkv_p, wkv_s, chunk_v.reshape(1, nsamp, 1, D))
```

```python
import functools

import jax
import jax.numpy as jnp
from jax import lax
from jax.experimental import pallas as pl
from jax.experimental.pallas import tpu as pltpu

F32 = jnp.float32
BF16 = jnp.bfloat16

D = 1024
DA = 512
DB = 512
HEAD = 64
PB = 1792
PAB = 3 * DA + PB
NE = 64
NG = 8
EPG = NE // NG
TOPG = 4
TOPK = 8
DE = 256
ROUTE_SCALE = 2.5
EPS = 1e-6
GN_EPS = 64e-5
CHUNK = 128
WKV_SLAB = 128
WKV_CHUNK = 32
WKV_PAIRS_PER_STEP = 4
MOE_TD = 512
MOE_SEG = 16
MOE_LS = 5120
MOE_CH = 1024
MOE_PIECES = (64, 16)
MOE_TR = 1024
MOE_XW = D + 128
PLAN_ROWS = 128

VMEM_LIMIT = 56 * 1024 * 1024


def _cparams(sem):
    return pltpu.CompilerParams(dimension_semantics=sem, vmem_limit_bytes=VMEM_LIMIT)


def _dot(a, b, precision=None):
    return jnp.dot(a, b, preferred_element_type=F32, precision=precision)


def _dot_nt(a, b, precision=None):
    return lax.dot_general(a, b, (((1,), (1,)), ((), ())), preferred_element_type=F32, precision=precision)


def _iota(shape, dim):
    return lax.broadcasted_iota(jnp.int32, shape, dim)


def _rms(x, g):
    return x * lax.rsqrt(jnp.mean(x * x, axis=-1, keepdims=True) + EPS) * g


def _silu(x):
    return x * jax.nn.sigmoid(x)


def _seg_ones(n, seg):
    return (_iota((n, n), 0) // seg == _iota((n, n), 1) // seg).astype(F32)


def _mod_kernel(c_ref, w_ref, b_ref, o_ref):
    s = _silu(c_ref[...]).astype(BF16)
    o_ref[...] = _dot(s, w_ref[...].astype(BF16)) + b_ref[...]


def _modulation(c_all, w_mod, b_mod):
    depth, _, width = w_mod.shape
    rows = c_all.shape[0]
    bn = 512
    return pl.pallas_call(
        _mod_kernel,
        grid=(depth, width // bn),
        in_specs=[pl.BlockSpec((rows, D), lambda l, j: (0, 0)),
                  pl.BlockSpec((None, D, bn), lambda l, j: (l, 0, j)),
                  pl.BlockSpec((None, 1, bn), lambda l, j: (l, 0, j))],
        out_specs=pl.BlockSpec((None, rows, bn), lambda l, j: (l, 0, j)),
        out_shape=jax.ShapeDtypeStruct((depth, rows, width), F32),
        compiler_params=_cparams(("arbitrary", "arbitrary")),
        name="modulation",
    )(c_all, w_mod, b_mod.reshape(depth, 1, width))


def _residual_in(x_ref, f_ref, pmod_ref, png_ref):
    x = x_ref[...]
    if f_ref is None:
        return x
    return x + pmod_ref[5] * _rms(f_ref[...], png_ref[3:4, :])


def _route(h, wrt_ref, br_ref):
    tm = h.shape[0]
    logits = _mm(wrt_ref[...], h, 2, 2, nt=True)
    scores = jax.nn.sigmoid(logits)
    sel = scores + br_ref[...]
    sub = _iota((EPG, tm), 0).astype(F32)
    neg = jnp.float32(-jnp.inf)
    blks, sblks, gscore = [], [], []
    for g in range(NG):
        blk = sel[g * EPG:(g + 1) * EPG, :]
        m1 = jnp.max(blk, axis=0, keepdims=True)
        first = jnp.min(jnp.where(blk == m1, sub, float(EPG)), axis=0, keepdims=True)
        m2 = jnp.max(jnp.where(sub == first, neg, blk), axis=0, keepdims=True)
        blks.append(blk)
        sblks.append(scores[g * EPG:(g + 1) * EPG, :])
        gscore.append(m1 + m2)
    masked = []
    for g in range(NG):
        cnt = jnp.zeros((1, tm), F32)
        for g2 in range(NG):
            if g2 == g:
                continue
            beats = (gscore[g2] >= gscore[g]) if g2 < g else (gscore[g2] > gscore[g])
            cnt = cnt + jnp.where(beats, 1.0, 0.0)
        masked.append(jnp.where(cnt < TOPG, blks[g], neg))
    eidx = [sub + float(g * EPG) for g in range(NG)]
    chosen = [jnp.zeros((EPG, tm), F32) for _ in range(NG)]
    for _ in range(TOPK):
        best = masked[0]
        for g in range(1, NG):
            best = jnp.maximum(best, masked[g])
        best = jnp.max(best, axis=0, keepdims=True)
        first = jnp.where(masked[0] == best, eidx[0], float(NE))
        for g in range(1, NG):
            first = jnp.minimum(first, jnp.where(masked[g] == best, eidx[g], float(NE)))
        first = jnp.min(first, axis=0, keepdims=True)
        for g in range(NG):
            hit = eidx[g] == first
            chosen[g] = jnp.where(hit, 1.0, chosen[g])
            masked[g] = jnp.where(hit, neg, masked[g])
    wts = [jnp.where(chosen[g] > 0.0, sblks[g], 0.0) for g in range(NG)]
    tot = wts[0]
    for g in range(1, NG):
        tot = tot + wts[g]
    denom = jnp.sum(tot, axis=0, keepdims=True)
    comb_t = jnp.concatenate([w / denom * ROUTE_SCALE for w in wts] + [jnp.zeros((NE, tm), F32)], axis=0)
    return comb_t.T


def _post(y, x, mod_ref, ng_ref, wrt_ref, br_ref, xo_ref, h2_ref, comb_ref):
    xn = x + mod_ref[2] * _rms(y, ng_ref[1:2, :])
    xo_ref[...] = xn
    h2 = _rms(xn, ng_ref[2:3, :]) * (1.0 + mod_ref[4]) + mod_ref[3]
    h2_ref[...] = h2.astype(BF16)
    comb_ref[...] = _route(h2, wrt_ref, br_ref)


def _ab_pre_kernel(*refs, seq_mode, tm):
    it = iter(refs)
    x_ref, mod_ref, ng_ref, win_ref = next(it), next(it), next(it), next(it)
    cw_ref, mu_ref, w0_ref, wd_ref, a0_ref, wa_ref, wg_ref, kk_ref, ka_ref = (next(it) for _ in range(9))
    if not seq_mode:
        p2_ref, p1_ref, ps_ref = next(it), next(it), next(it)
    ya_ref, r_ref, lw_ref, k_ref, v_ref, kkn_ref, a_ref, g_ref, ctail_ref, stail_ref = (next(it) for _ in range(10))
    if seq_mode:
        cc_ref, sc_ref = next(it), next(it)

        @pl.when(pl.program_id(1) == 0)
        def _():
            cc_ref[...] = jnp.zeros_like(cc_ref)
            sc_ref[...] = jnp.zeros_like(sc_ref)

    x = x_ref[...]
    h = _rms(x, ng_ref[0:1, :]) * (1.0 + mod_ref[1]) + mod_ref[0]
    proj = _dot(h.astype(BF16), win_ref[...])
    a_h, a_b, a_c = proj[:, 0:DA], proj[:, DA:2 * DA], proj[:, 2 * DA:3 * DA]
    pb = proj[:, 3 * DA:]
    gated = a_c * a_h
    if seq_mode:
        rows = _iota((tm, 1), 0)
        g1 = jnp.where(rows == 0, cc_ref[7:8, :], pltpu.roll(gated, 1, 0))
        g2 = jnp.where(rows == 0, cc_ref[6:7, :], jnp.where(rows == 1, cc_ref[7:8, :], pltpu.roll(gated, 2, 0)))
        pprev = jnp.where(rows == 0, sc_ref[7:8, :], pltpu.roll(pb, 1, 0))
        cc_ref[...] = gated[tm - 8:, :]
        sc_ref[...] = pb[tm - 8:, :]
        ctail_ref[...] = gated[tm - 8:, :]
        stail_ref[...] = pb[tm - 8:, :]
    else:
        g1, g2, pprev = p1_ref[...], p2_ref[...], ps_ref[...]
        ctail_ref[...] = gated
        stail_ref[...] = pb
    conv = g2 * cw_ref[0:1, :] + g1 * cw_ref[1:2, :] + gated * cw_ref[2:3, :]
    ya_ref[...] = (a_b * conv).astype(BF16)

    xm = pb + (pprev - pb) * mu_ref[...]
    r, k, v = xm[:, 0:DB], xm[:, DB:2 * DB], xm[:, 2 * DB:3 * DB]
    lowrank = xm[:, 3 * DB:3 * DB + 128]
    dg = xm[:, 3 * DB + 128:]
    w_pre = w0_ref[...] + _dot(jnp.tanh(lowrank).astype(BF16), wd_ref[...])
    z = -w_pre
    softplus = jnp.maximum(z, 0.0) + jnp.log(1.0 + jnp.exp(-jnp.abs(z)))
    lw_ref[...] = -jnp.exp(-softplus - 0.5)
    a = jax.nn.sigmoid(a0_ref[...] + _dot(lowrank.astype(BF16), wa_ref[...]))
    g_ref[...] = _dot(jax.nn.sigmoid(dg).astype(BF16), wg_ref[...])
    kk = k * kk_ref[...]
    ss = _mm(kk * kk, _seg_ones(DB, HEAD), 2, 1)
    kkn_ref[...] = kk / jnp.maximum(jnp.sqrt(ss), 1e-12)
    r_ref[...] = r
    k_ref[...] = k * (1.0 + (a - 1.0) * ka_ref[...])
    v_ref[...] = v
    a_ref[...] = a


def _ab_pre(x, mods, ng, win, small, prev, *, nb, seq_len, tm, seq_mode):
    nt = seq_len // tm
    rows = nb * seq_len
    tail = 8 if seq_mode else tm
    full = lambda shape: pl.BlockSpec(shape, lambda b, t: tuple(0 for _ in shape))
    tok = lambda w: pl.BlockSpec((tm, w), lambda b, t: (b * nt + t, 0))
    in_specs = [tok(D),
                pl.BlockSpec((None,) + mods.shape[1:], lambda b, t: (b, 0, 0, 0)),
                full(ng.shape), full(win.shape)] + [full(s.shape) for s in small]
    args = [x, mods, ng, win] + list(small)
    if not seq_mode:
        in_specs += [tok(DA), tok(DA), tok(PB)]
        args += list(prev)
    out_shape = ([jax.ShapeDtypeStruct((rows, DA), BF16)] + [jax.ShapeDtypeStruct((rows, DB), F32)] * 7
                 + [jax.ShapeDtypeStruct((nb * tail, DA), F32), jax.ShapeDtypeStruct((nb * tail, PB), F32)])
    out_specs = ([tok(DA)] + [tok(DB)] * 7
                 + [pl.BlockSpec((tail, DA), lambda b, t: (b, 0)), pl.BlockSpec((tail, PB), lambda b, t: (b, 0))])
    scratch = [pltpu.VMEM((8, DA), F32), pltpu.VMEM((8, PB), F32)] if seq_mode else []
    return pl.pallas_call(
        functools.partial(_ab_pre_kernel, seq_mode=seq_mode, tm=tm),
        grid=(nb, nt), in_specs=in_specs, out_specs=out_specs, out_shape=out_shape,
        scratch_shapes=scratch,
        compiler_params=_cparams(("arbitrary", "arbitrary")),
        name="ab_pre_seq" if seq_mode else "ab_pre_step",
    )(*args)


def _split(x):
    hi = x.astype(BF16)
    return hi, (x - hi.astype(F32)).astype(BF16)


def _mm(a, b, pa=1, pb=1, nt=False):
    ah, al = _split(a) if pa == 2 else (a.astype(BF16), None)
    bh, bl = _split(b) if pb == 2 else (b.astype(BF16), None)
    bx = 1 if nt else 0
    if pa == 2 and pb == 2:
        ah, bh = jnp.concatenate([ah, al, ah], axis=1), jnp.concatenate([bh, bh, bl], axis=bx)
    elif pa == 2:
        ah, bh = jnp.concatenate([ah, al], axis=1), jnp.concatenate([bh, bh], axis=bx)
    elif pb == 2:
        ah, bh = jnp.concatenate([ah, ah], axis=1), jnp.concatenate([bh, bl], axis=bx)
    return _dot_nt(ah, bh) if nt else _dot(ah, bh)


def _gn_gate(o, r, k, v, g, rk, gng, gnb, seg):
    mean = _mm(o, seg, 2, 1) * (1.0 / HEAD)
    d = o - mean
    var = _mm(d * d, seg, 2, 1) * (1.0 / HEAD)
    out = d * lax.rsqrt(var + GN_EPS) * gng + gnb
    out = out + _mm(r * k * rk, seg, 2, 1) * v
    return out * g


def _wkv_pair(r, lw, k, v, kk, a, st, cst):
    ts, c = WKV_SLAB, WKV_CHUNK
    nchunk = ts // c
    strict, incl, cum_lhs, m0, m1, headdiag, eye, eye2, blks, chunk_cols = cst

    l1 = lw.astype(BF16)
    rem = lw - l1.astype(F32)
    l2 = rem.astype(BF16)
    l3 = (rem - l2.astype(F32)).astype(BF16)
    cums = _dot(cum_lhs, jnp.concatenate([l1, l2, l3], axis=0))
    yield
    cl, ctot = cums[:ts], cums[ts:]
    e_neg = jnp.exp(-cl)
    e_end = jnp.exp(ctot - cl)
    at = -kk * jnp.exp(cl - lw)
    rt = r * jnp.exp(cl)
    beta = kk * a
    bk = jnp.concatenate([beta * e_neg, k * e_neg], axis=0)
    bend, kend = beta * e_end, k * e_end

    at_st = jnp.concatenate([at * m0, at * m1], axis=0)
    a_all = _mm(jnp.concatenate([at_st, rt * m0, rt * m1], axis=0), bk, nt=True)
    yield
    a_st, a_out = a_all[:2 * ts], a_all[2 * ts:]
    zero = jnp.zeros((ts, ts), F32)

    def two_heads(p0, p1):
        return jnp.concatenate([jnp.concatenate([p0, zero], axis=1), jnp.concatenate([zero, p1], axis=1)], axis=0)

    lbd = two_heads(a_st[:ts, :ts] * strict, a_st[ts:, :ts] * strict)
    d1 = jnp.where(blks[0], lbd, 0.0)
    tinv = eye2 + d1
    d2 = _mm(d1, d1)
    yield
    tinv = tinv + _mm(tinv, d2)
    d4 = _mm(d2, d2)
    yield
    tinv = tinv + _mm(tinv, d4)
    yield
    for lvl in range(1, len(blks)):
        off = jnp.where(blks[lvl] & jnp.logical_not(blks[lvl - 1]), lbd, 0.0)
        half = _mm(tinv, off)
        yield
        tinv = tinv + _mm(half, tinv)
        yield

    v_st = jnp.concatenate([v * m0, v * m1], axis=0)
    y_st = _mm(two_heads(a_st[:ts, ts:] * strict, a_st[ts:, ts:] * strict), v_st)
    yield
    wu = _mm(tinv, jnp.concatenate([at_st, y_st], axis=1))
    yield
    w = wu[:ts, :2 * HEAD] + wu[ts:, :2 * HEAD]
    u = wu[:ts, 2 * HEAD:] + wu[ts:, 2 * HEAD:]
    rhs = jnp.concatenate([jnp.concatenate([w, u], axis=1),
                           jnp.concatenate([jnp.zeros_like(v), v], axis=1)], axis=0)
    bend_t, kend_t = bend.T, kend.T
    lhs = [jnp.concatenate([bend_t * cm, kend_t * cm], axis=1) for cm in chunk_cols]
    lhs.append(jnp.concatenate([a_out[:ts, :ts] * incl, a_out[:ts, ts:] * incl], axis=1))
    lhs.append(jnp.concatenate([a_out[ts:, :ts] * incl, a_out[ts:, ts:] * incl], axis=1))
    big = _mm(jnp.concatenate(lhs, axis=0), rhs)
    yield
    mn_all, qo = big[:nchunk * ts], big[nchunk * ts:]
    q = rt + qo[:ts, :2 * HEAD] * m0 + qo[ts:, :2 * HEAD] * m1
    olocal = qo[:ts, 2 * HEAD:] * m0 + qo[ts:, 2 * HEAD:] * m1

    outs = []
    for ch in range(nchunk):
        mn = mn_all[ch * ts:(ch + 1) * ts]
        trans = eye * jnp.exp(ctot[ch * c:ch * c + 1, :]) + mn[:, :2 * HEAD] * headdiag
        both = _mm(jnp.concatenate([q[ch * c:(ch + 1) * c], trans], axis=0), st)
        yield
        outs.append(both[:c] + olocal[ch * c:(ch + 1) * c])
        st = both[c:] + mn[:, 2 * HEAD:] * headdiag
    return jnp.concatenate(outs, axis=0), st


def _round_robin(gens):
    results = [None] * len(gens)
    active = list(enumerate(gens))
    while active:
        still = []
        for i, gen in active:
            try:
                next(gen)
                still.append((i, gen))
            except StopIteration as stop:
                results[i] = stop.value
        active = still
    return results


def _wkv_seq_kernel(r_ref, lw_ref, k_ref, v_ref, kk_ref, a_ref, g_ref, rk_ref, gng_ref, gnb_ref,
                    y_ref, sfin_ref, st_ref, *, npp):
    ts, c = WKV_SLAB, WKV_CHUNK

    @pl.when(pl.program_id(2) == 0)
    def _():
        st_ref[...] = jnp.zeros_like(st_ref)

    ri, ci = _iota((ts, ts), 0), _iota((ts, ts), 1)
    same = ri // c == ci // c
    strict = (same & (ci < ri)).astype(F32)
    incl = (same & (ci <= ri)).astype(F32)
    cum_lhs = jnp.concatenate([incl, same.astype(F32)], axis=0).astype(BF16)
    cum_lhs = jnp.concatenate([cum_lhs] * 3, axis=1)
    lane = _iota((1, 2 * HEAD), 1)
    m0 = (lane < HEAD).astype(F32)
    m1 = 1.0 - m0
    headdiag = _seg_ones(2 * HEAD, HEAD)
    eye = (_iota((2 * HEAD, 2 * HEAD), 0) == _iota((2 * HEAD, 2 * HEAD), 1)).astype(F32)
    r2, c2 = _iota((2 * ts, 2 * ts), 0), _iota((2 * ts, 2 * ts), 1)
    eye2 = (r2 == c2).astype(F32)
    blks, b = [], 8
    while b <= c:
        blks.append(r2 // b == c2 // b)
        b *= 2
    tcol = _iota((1, ts), 1) // c
    chunk_cols = [(tcol == ch).astype(F32) for ch in range(ts // c)]
    cst = (strict, incl, cum_lhs, m0, m1, headdiag, eye, eye2, blks, chunk_cols)

    sls = [slice(p * 2 * HEAD, (p + 1) * 2 * HEAD) for p in range(npp)]
    res = _round_robin([_wkv_pair(r_ref[:, sl], lw_ref[:, sl], k_ref[:, sl], v_ref[:, sl], kk_ref[:, sl],
                                  a_ref[:, sl], st_ref[p], cst) for p, sl in enumerate(sls)])
    for p, sl in enumerate(sls):
        o, st = res[p]
        st_ref[p] = st
        y_ref[:, sl] = _gn_gate(o, r_ref[:, sl], k_ref[:, sl], v_ref[:, sl], g_ref[:, sl], rk_ref[:, sl],
                                gng_ref[:, sl], gnb_ref[:, sl], headdiag).astype(BF16)

    @pl.when(pl.program_id(2) == pl.num_programs(2) - 1)
    def _():
        sfin_ref[...] = st_ref[...]


def _wkv_seq(r, lw, k, v, kk, a, g, rk, gng, gnb, *, nb, seq_len):
    ns = seq_len // WKV_SLAB
    npp = WKV_PAIRS_PER_STEP
    width = 2 * HEAD * npp
    tok = pl.BlockSpec((WKV_SLAB, width), lambda b, p, s: (b * ns + s, p))
    par = pl.BlockSpec((1, width), lambda b, p, s: (0, p))
    return pl.pallas_call(
        functools.partial(_wkv_seq_kernel, npp=npp),
        grid=(nb, DB // width, ns),
        in_specs=[tok] * 7 + [par] * 3,
        out_specs=[tok, pl.BlockSpec((None, npp, 2 * HEAD, 2 * HEAD), lambda b, p, s: (b, p, 0, 0))],
        out_shape=[jax.ShapeDtypeStruct((nb * seq_len, DB), BF16),
                   jax.ShapeDtypeStruct((nb, DB // (2 * HEAD), 2 * HEAD, 2 * HEAD), F32)],
        scratch_shapes=[pltpu.VMEM((npp, 2 * HEAD, 2 * HEAD), F32)],
        compiler_params=_cparams(("arbitrary", "arbitrary", "arbitrary")),
        name="wkv_seq",
    )(r, lw, k, v, kk, a, g, rk, gng, gnb)


def _wkv_step_kernel(s_ref, r_ref, lw_ref, k_ref, v_ref, kk_ref, a_ref, g_ref, rk_ref, gng_ref, gnb_ref,
                     y_ref, so_ref):
    s = s_ref[...]
    r, k, v, kk, a = r_ref[...], k_ref[...], v_ref[...], kk_ref[...], a_ref[...]
    w = jnp.exp(lw_ref[...])
    eye = (_iota((1, HEAD, HEAD), 1) == _iota((1, HEAD, HEAD), 2)).astype(F32)
    v_col = jnp.sum(eye * v, axis=2, keepdims=True)
    s_kk = jnp.sum(s * kk, axis=2, keepdims=True)
    s = s * w - s_kk * (kk * a) + v_col * k
    so_ref[...] = s
    o_col = jnp.sum(s * r, axis=2, keepdims=True)
    o = jnp.sum(eye * o_col, axis=1, keepdims=True)
    mean = jnp.mean(o, axis=2, keepdims=True)
    d = o - mean
    var = jnp.mean(d * d, axis=2, keepdims=True)
    out = d * lax.rsqrt(var + GN_EPS) * gng_ref[...] + gnb_ref[...]
    out = out + jnp.sum(r * k * rk_ref[...], axis=2, keepdims=True) * v
    y_ref[...] = out * g_ref[...]


def _wkv_step(state, vecs, params, *, nheads):
    n = state.shape[0]
    tb = 8 * nheads
    sspec = pl.BlockSpec((tb, HEAD, HEAD), lambda i: (i, 0, 0))
    vspec = pl.BlockSpec((tb, 1, HEAD), lambda i: (i, 0, 0))
    pspec = pl.BlockSpec((tb, 1, HEAD), lambda i: (0, 0, 0))
    reps = tb // nheads
    params = [jnp.tile(p, (reps, 1, 1)) for p in params]
    return pl.pallas_call(
        _wkv_step_kernel,
        grid=(n // tb,),
        in_specs=[sspec] + [vspec] * 7 + [pspec] * 3,
        out_specs=[vspec, sspec],
        out_shape=[jax.ShapeDtypeStruct((n, 1, HEAD), F32), jax.ShapeDtypeStruct((n, HEAD, HEAD), F32)],
        compiler_params=_cparams(("arbitrary",)),
        name="wkv_step",
    )(state, *vecs, *params)


def _ab_post_kernel(ya_ref, yb_ref, x_ref, mod_ref, ng_ref, wo_ref, wrt_ref, br_ref,
                    h2in_ref, combin_ref, xo_ref, h2_ref, comb_ref):
    del h2in_ref, combin_ref
    y = _dot(ya_ref[...], wo_ref[0:DA, :]) + _dot(yb_ref[...], wo_ref[DA:, :])
    _post(y, x_ref[...], mod_ref, ng_ref, wrt_ref, br_ref, xo_ref, h2_ref, comb_ref)


def _ab_post(ya, yb, x, mods, ng, wo, wrt, br, h2_all, comb_all, *, nb, seq_len, tm, row0):
    nt = seq_len // tm
    t0 = row0 // tm
    full = lambda shape: pl.BlockSpec(shape, lambda b, t: tuple(0 for _ in shape))
    tok = lambda w: pl.BlockSpec((tm, w), lambda b, t: (b * nt + t, 0))
    tok_all = lambda w: pl.BlockSpec((tm, w), lambda b, t: (t0 + b * nt + t, 0))
    anyspec = pl.BlockSpec(memory_space=pl.ANY)
    return pl.pallas_call(
        _ab_post_kernel,
        grid=(nb, nt),
        in_specs=[tok(DA), tok(DB), tok(D),
                  pl.BlockSpec((None,) + mods.shape[1:], lambda b, t: (b, 0, 0, 0)),
                  full(ng.shape), full(wo.shape), full(wrt.shape), full(br.shape), anyspec, anyspec],
        out_specs=[tok(D), tok_all(D), tok_all(128)],
        out_shape=[jax.ShapeDtypeStruct(x.shape, F32), jax.ShapeDtypeStruct(h2_all.shape, BF16),
                   jax.ShapeDtypeStruct(comb_all.shape, F32)],
        input_output_aliases={8: 1, 9: 2},
        compiler_params=_cparams(("arbitrary", "arbitrary")),
        name="ab_post",
    )(ya, yb, x, mods, ng, wo, wrt, br, h2_all, comb_all)


def _gmlp_kernel(*refs, seq_mode, tm):
    it = iter(refs)
    x_ref, f_ref, pmod_ref, png_ref, mod_ref, ng_ref = (next(it) for _ in range(6))
    win_ref, bin_ref, lng_ref, lnb_ref, ws_ref, bs_ref, wo_ref, wrt_ref, br_ref = (next(it) for _ in range(9))
    next(it), next(it)
    xo_ref, h2_ref, comb_ref = next(it), next(it), next(it)
    if not seq_mode:
        cv_ref = next(it)
    x = _residual_in(x_ref, f_ref, pmod_ref, png_ref)
    h = _rms(x, ng_ref[0:1, :]) * (1.0 + mod_ref[1]) + mod_ref[0]
    z = jax.nn.gelu(_dot(h.astype(BF16), win_ref[...]) + bin_ref[...])
    u, v = z[:, :D], z[:, D:]
    mu = jnp.mean(v, axis=-1, keepdims=True)
    var = jnp.mean(jnp.square(v - mu), axis=-1, keepdims=True)
    v = (v - mu) * lax.rsqrt(var + EPS) * lng_ref[...] + lnb_ref[...]
    if seq_mode:
        vb = v.astype(BF16)
        causal = _iota((CHUNK, CHUNK), 1) <= _iota((CHUNK, CHUNK), 0)
        cols = []
        for g in range(D // CHUNK):
            wsg = jnp.where(causal, ws_ref[g], 0.0).astype(BF16)
            bsg = bs_ref[:, g:g + 1]
            rows = [_dot(wsg, vb[c * CHUNK:(c + 1) * CHUNK, g * CHUNK:(g + 1) * CHUNK]) + bsg
                    for c in range(tm // CHUNK)]
            cols.append(jnp.concatenate(rows, axis=0) if len(rows) > 1 else rows[0])
        s = jnp.concatenate(cols, axis=1)
    else:
        cv_ref[...] = v
        s = v * ws_ref[...] + bs_ref[...]
    y = _dot((u * s).astype(BF16), wo_ref[...])
    _post(y, x, mod_ref, ng_ref, wrt_ref, br_ref, xo_ref, h2_ref, comb_ref)


def _gmlp(x, f_all, pmods, png, mods, ng, weights, h2_all, comb_all, *, nb, seq_len, tm, row0, seq_mode):
    nt = seq_len // tm
    t0 = row0 // tm
    full = lambda shape: pl.BlockSpec(shape, lambda b, t: tuple(0 for _ in shape))
    tok = lambda w: pl.BlockSpec((tm, w), lambda b, t: (b * nt + t, 0))
    tok_all = lambda w: pl.BlockSpec((tm, w), lambda b, t: (t0 + b * nt + t, 0))
    modspec = lambda m: pl.BlockSpec((None,) + m.shape[1:], lambda b, t: (b, 0, 0, 0))
    anyspec = pl.BlockSpec(memory_space=pl.ANY)
    in_specs = ([tok(D), tok_all(D), modspec(pmods), full(png.shape), modspec(mods), full(ng.shape)]
                + [full(w.shape) for w in weights] + [anyspec, anyspec])
    out_specs = [tok(D), tok_all(D), tok_all(128)]
    out_shape = [jax.ShapeDtypeStruct(x.shape, F32), jax.ShapeDtypeStruct(h2_all.shape, BF16),
                 jax.ShapeDtypeStruct(comb_all.shape, F32)]
    if not seq_mode:
        out_specs.append(tok(D))
        out_shape.append(jax.ShapeDtypeStruct(x.shape, F32))
    n_in = len(in_specs)
    return pl.pallas_call(
        functools.partial(_gmlp_kernel, seq_mode=seq_mode, tm=tm),
        grid=(nb, nt), in_specs=in_specs, out_specs=out_specs, out_shape=out_shape,
        input_output_aliases={n_in - 2: 1, n_in - 1: 2},
        compiler_params=_cparams(("arbitrary", "arbitrary")),
        name="gmlp_seq" if seq_mode else "gmlp_step",
    )(x, f_all, pmods, png, mods, ng, *weights, h2_all, comb_all)


def _strict_lower(n):
    return _iota((n, n), 1) < _iota((n, n), 0)


def _col_of_row(row):
    return jnp.broadcast_to(row, (128, 128)).T[:, 0:1]


def _plan_kernel(comb_ref, dst_ref, lo_ref, n_ref, te_ref, misc_ref, lof_ref, pieces_ref, cnt_ref, *, ntile, te_cols):
    i = pl.program_id(0)

    @pl.when(i == 0)
    def _():
        cnt_ref[...] = jnp.zeros_like(cnt_ref)

    sel = jnp.where(comb_ref[...] > 0.0, 1.0, 0.0)
    cnt_ref[pl.ds(i, 1), :] = jnp.sum(sel, axis=0, keepdims=True)

    @pl.when(i == ntile - 1)
    def _():
        c = cnt_ref[...]
        cpad = jnp.floor((c + (MOE_SEG - 1.0)) * (1.0 / MOE_SEG)) * MOE_SEG
        cb = cpad.astype(BF16)
        lower = jnp.where(_strict_lower(PLAN_ROWS), 1.0, 0.0).astype(BF16)
        upper = jnp.where(_iota((128, 128), 0) < _iota((128, 128), 1), 1.0, 0.0).astype(BF16)
        before = _dot(lower, cb)
        lo = _dot(cb, upper)
        tot = jnp.sum(cpad, axis=0, keepdims=True)
        rt = jnp.floor((tot + (MOE_TR - 1.0)) * (1.0 / MOE_TR))
        gt = _dot(jnp.broadcast_to(rt, (8, 128)).astype(BF16), upper)[0:1, :]
        dst_ref[...] = (gt * MOE_TR + before).astype(jnp.int32)
        lo_ref[...] = lo.astype(jnp.int32)
        lof_ref[...] = lo
        n_ref[...] = cpad.astype(jnp.int32)
        ends_col = _col_of_row(gt + rt)
        jrow = _iota((1, te_cols), 1).astype(F32)
        te = jnp.sum(jnp.where(ends_col <= jrow, 1.0, 0.0), axis=0, keepdims=True)
        te_ref[...] = jnp.broadcast_to(jnp.minimum(te, NE - 1.0), (8, te_cols)).astype(jnp.int32)
        nused = jnp.sum(rt, axis=1, keepdims=True)
        rowi = _iota((8, 128), 0)
        misc = jnp.where(rowi == 0, gt * MOE_TR + tot, jnp.where(rowi == 1, rt * MOE_TR - tot, nused))
        misc_ref[...] = misc.astype(jnp.int32)
        lane = _iota((1, 128), 1)
        pieces = jnp.zeros((PLAN_ROWS, 128), F32)
        taken = jnp.zeros_like(cpad)
        for k, size in enumerate(MOE_PIECES):
            cnt = jnp.floor((cpad - taken) * (1.0 / size))
            taken = taken + cnt * size
            pieces = pieces + jnp.where(lane == k, jnp.sum(cnt, axis=1, keepdims=True), 0.0)
        pieces_ref[...] = pieces.astype(jnp.int32)


def _moe_plan(comb_all, *, ntmax):
    ntile = comb_all.shape[0] // MOE_TD
    te_cols = -(-ntmax // 128) * 128
    full = lambda shape: pl.BlockSpec(shape, lambda i: tuple(0 for _ in shape))
    shapes = [((PLAN_ROWS, 128), jnp.int32)] * 3 + [((8, te_cols), jnp.int32), ((8, 128), jnp.int32),
                                                   ((PLAN_ROWS, 128), F32), ((PLAN_ROWS, 128), jnp.int32)]
    return pl.pallas_call(
        functools.partial(_plan_kernel, ntile=ntile, te_cols=te_cols),
        grid=(ntile,),
        in_specs=[pl.BlockSpec((MOE_TD, 128), lambda i: (i, 0))],
        out_specs=[full(s) for s, _ in shapes],
        out_shape=[jax.ShapeDtypeStruct(s, d) for s, d in shapes],
        scratch_shapes=[pltpu.VMEM((PLAN_ROWS, 128), F32)],
        compiler_params=_cparams(("arbitrary",)),
        name="moe_plan",
    )(comb_all)


def _rank_plus(comb):
    sel = jnp.where(comb > 0.0, 1.0, 0.0)
    lower = jnp.where(_strict_lower(MOE_TD), 1.0, 0.0).astype(BF16)
    return sel * (_dot(lower, sel.astype(BF16)) + 1.0)


def _for_each_piece(n, fn):
    off = 0
    for size in MOE_PIECES:
        cnt = (n - off) // size

        def body(p, carry, off=off, size=size):
            fn(off + p * size, size)
            return carry

        lax.fori_loop(0, cnt, body, 0)
        off = off + cnt * size


def _wait_pieces(pieces_s, tile, make_copy):
    for k, size in enumerate(MOE_PIECES):
        def wait_one(p, carry, size=size):
            make_copy(size).wait()
            return carry

        lax.fori_loop(0, pieces_s[tile * 128 + k], wait_one, 0)


def _dispatch_kernel(dst_s, lo_s, n_s, misc_s, pieces_s, h_ref, comb_ref, lof_ref, nf_ref, xs_hbm, buf, zbuf, sem,
                     tsem, *, ntile):
    i = pl.program_id(0)
    slot = lax.rem(i, 2)
    comb = comb_ref[...]
    lo_row = lof_ref[...]
    hi_row = lo_row + nf_ref[...]
    rp_hi, rp_lo = _split(_rank_plus(comb).T)
    rhs_rank = jnp.concatenate([rp_hi, rp_lo], axis=0)
    c_hi, c_lo = _split(comb)
    h_ext = jnp.concatenate([h_ref[...], c_hi, c_lo], axis=1)
    lane_lo = _iota((1, 128), 1) < 64
    for ch in range(MOE_LS // MOE_CH):
        srow = (_iota((MOE_CH, 1), 0) + ch * MOE_CH).astype(F32)
        e_t = jnp.where((srow >= lo_row) & (srow < hi_row), 1.0, 0.0)
        e_tb = e_t.astype(BF16)
        rank_of = _dot(jnp.concatenate([e_tb, e_tb], axis=1), rhs_rank)
        target = srow + 1.0 - jnp.sum(e_t * lo_row, axis=1, keepdims=True)
        perm = jnp.where(rank_of == target, 1.0, 0.0).astype(BF16)
        rows = pl.ds(ch * MOE_CH, MOE_CH)
        sorted_rows = _dot(perm, h_ext)
        buf[slot, rows, :D] = sorted_rows[:, :D].astype(BF16)
        w = jnp.sum((sorted_rows[:, D:D + 128] + sorted_rows[:, D + 128:]) * e_t, axis=1, keepdims=True)
        w_hi = w.astype(BF16).astype(F32)
        buf[slot, rows, D:] = jnp.where(lane_lo, w_hi, w - w_hi).astype(BF16)

    def copy_out(sl, src_row, dst_row, size):
        return pltpu.make_async_copy(buf.at[sl, pl.ds(pl.multiple_of(src_row, MOE_SEG), size)],
                                     xs_hbm.at[pl.ds(pl.multiple_of(dst_row, MOE_SEG), size)], sem.at[sl])

    def start_seg(e, carry):
        idx = i * 128 + e
        lo, dst = lo_s[idx], dst_s[idx]
        _for_each_piece(n_s[idx], lambda off, size: copy_out(slot, lo + off, dst + off, size).start())
        return carry

    lax.fori_loop(0, NE, start_seg, 0)

    @pl.when(i > 0)
    def _():
        _wait_pieces(pieces_s, i - 1, lambda size: copy_out(1 - slot, 0, 0, size))

    @pl.when(i == ntile - 1)
    def _():
        _wait_pieces(pieces_s, i, lambda size: copy_out(slot, 0, 0, size))
        zbuf[...] = jnp.zeros_like(zbuf)

        def tail_copy(e, off, size):
            return pltpu.make_async_copy(
                zbuf.at[pl.ds(0, size)], xs_hbm.at[pl.ds(pl.multiple_of(misc_s[e] + off, MOE_SEG), size)], tsem)

        def tail_start(e, carry):
            _for_each_piece(misc_s[128 + e], lambda off, size: tail_copy(e, off, size).start())
            return carry

        def tail_wait(e, carry):
            _for_each_piece(misc_s[128 + e], lambda off, size: tail_copy(e, off, size).wait())
            return carry

        lax.fori_loop(0, NE, tail_start, 0)
        lax.fori_loop(0, NE, tail_wait, 0)


def _moe_dispatch(plan, h_all, comb_all, *, ntmax):
    dst, lo, n, _, misc, lof, pieces = plan
    ntile = h_all.shape[0] // MOE_TD
    grid_spec = pltpu.PrefetchScalarGridSpec(
        num_scalar_prefetch=5, grid=(ntile,),
        in_specs=[pl.BlockSpec((MOE_TD, D), lambda i, *_: (i, 0)),
                  pl.BlockSpec((MOE_TD, 128), lambda i, *_: (i, 0)),
                  pl.BlockSpec((None, 1, 128), lambda i, *_: (i, 0, 0)),
                  pl.BlockSpec((None, 1, 128), lambda i, *_: (i, 0, 0))],
        out_specs=pl.BlockSpec(memory_space=pl.ANY),
        scratch_shapes=[pltpu.VMEM((2, MOE_LS, MOE_XW), BF16), pltpu.VMEM((MOE_PIECES[0], MOE_XW), BF16),
                        pltpu.SemaphoreType.DMA((2,)), pltpu.SemaphoreType.DMA(())])
    return pl.pallas_call(
        functools.partial(_dispatch_kernel, ntile=ntile),
        grid_spec=grid_spec,
        out_shape=jax.ShapeDtypeStruct((ntmax * MOE_TR, MOE_XW), BF16),
        compiler_params=_cparams(("arbitrary",)),
        name="moe_dispatch",
    )(dst.reshape(-1), lo.reshape(-1), n.reshape(-1), misc.reshape(-1), pieces.reshape(-1), h_all, comb_all,
      lof.reshape(PLAN_ROWS, 1, 128), n.astype(F32).reshape(PLAN_ROWS, 1, 128))


def _ffn_kernel(te_s, misc_s, xs_ref, wg_ref, wu_ref, wd_ref, ys_ref, wgb, wub, wdb):
    j = pl.program_id(0)

    @pl.when(j < misc_s[2 * 128])
    def _():
        @pl.when((j == 0) | (te_s[j] != te_s[jnp.maximum(j - 1, 0)]))
        def _():
            wgb[...] = wg_ref[...].astype(BF16)
            wub[...] = wu_ref[...].astype(BF16)
            wdb[...] = wd_ref[...].astype(BF16)

        x = xs_ref[:, :D]
        w = xs_ref[:, D:D + 1].astype(F32) + xs_ref[:, D + 64:D + 65].astype(F32)
        hw = _silu(_dot(x, wgb[...])) * _dot(x, wub[...]) * w
        ys_ref[...] = _dot(hw.astype(BF16), wdb[...]).astype(BF16)


def _moe_ffn(plan, xs, wg, wu, wd, *, layer, ntmax):
    te, misc = plan[3], plan[4]

    def tile(j, misc_s):
        return jnp.minimum(j, misc_s[2 * 128] - 1)

    wspec = lambda shape: pl.BlockSpec((None, None) + shape,
                                       lambda j, te_s, misc_s: (layer, te_s[tile(j, misc_s)], 0, 0))
    grid_spec = pltpu.PrefetchScalarGridSpec(
        num_scalar_prefetch=2, grid=(ntmax,),
        in_specs=[pl.BlockSpec((MOE_TR, MOE_XW), lambda j, te_s, misc_s: (tile(j, misc_s), 0)),
                  wspec((D, DE)), wspec((D, DE)), wspec((DE, D))],
        out_specs=pl.BlockSpec((MOE_TR, D), lambda j, te_s, misc_s: (tile(j, misc_s), 0)),
        scratch_shapes=[pltpu.VMEM((D, DE), BF16), pltpu.VMEM((D, DE), BF16), pltpu.VMEM((DE, D), BF16)])
    return pl.pallas_call(
        _ffn_kernel, grid_spec=grid_spec,
        out_shape=jax.ShapeDtypeStruct((ntmax * MOE_TR, D), BF16),
        compiler_params=_cparams(("arbitrary",)),
        name="moe_ffn",
    )(te[0], misc.reshape(-1), xs, wg, wu, wd)


def _combine_kernel(dst_s, lo_s, n_s, pieces_s, h_ref, comb_ref, lof_ref, nf_ref, sg_ref, su_ref, sd_ref, ys_hbm,
                    o_ref, ybuf, sem, *, ntile):
    i = pl.program_id(0)
    slot = lax.rem(i, 2)

    def copy_in(sl, src_row, dst_row, size):
        return pltpu.make_async_copy(ys_hbm.at[pl.ds(pl.multiple_of(src_row, MOE_SEG), size)],
                                     ybuf.at[sl, pl.ds(pl.multiple_of(dst_row, MOE_SEG), size)], sem.at[sl])

    def start_tile(tile, sl):
        def start_seg(e, carry):
            idx = tile * 128 + e
            lo, src = lo_s[idx], dst_s[idx]
            _for_each_piece(n_s[idx], lambda off, size: copy_in(sl, src + off, lo + off, size).start())
            return carry

        lax.fori_loop(0, NE, start_seg, 0)

    @pl.when(i == 0)
    def _():
        ybuf[...] = jnp.zeros_like(ybuf)
        start_tile(0, 0)

    @pl.when(i + 1 < ntile)
    def _():
        start_tile(i + 1, 1 - slot)

    lo_row = lof_ref[...]
    hi_row = lo_row + nf_ref[...]
    lo_col, hi_col = _col_of_row(lo_row), _col_of_row(hi_row)
    rp_hi, rp_lo = _split(_rank_plus(comb_ref[...]))
    l_hi, l_lo = _split(jnp.broadcast_to(lo_row * (1.0 / MOE_SEG), (8, 128)))
    lhs_meta = jnp.concatenate([jnp.concatenate([rp_hi, rp_lo], axis=1), jnp.concatenate([l_hi, l_lo], axis=1)], axis=0)

    h = h_ref[...]
    hs = _silu(_dot(h, sg_ref[...].astype(BF16))) * _dot(h, su_ref[...].astype(BF16))
    acc = _dot(hs.astype(BF16), sd_ref[...].astype(BF16))

    _wait_pieces(pieces_s, i, lambda size: copy_in(slot, 0, 0, size))
    for ch in range(MOE_LS // MOE_CH):
        scol = (_iota((1, MOE_CH), 1) + ch * MOE_CH).astype(F32)
        e_m = jnp.where((scol >= lo_col) & (scol < hi_col), 1.0, 0.0).astype(BF16)
        meta = _dot(lhs_meta, jnp.concatenate([e_m, e_m], axis=0))
        target = scol + 1.0 - meta[MOE_TD:MOE_TD + 1, :] * MOE_SEG
        perm_t = jnp.where(meta[:MOE_TD] == target, 1.0, 0.0).astype(BF16)
        acc = acc + _dot(perm_t, ybuf[slot, pl.ds(ch * MOE_CH, MOE_CH), :])
    o_ref[...] = acc


def _moe_combine(plan, ys, h_all, comb_all, sg, su, sd, *, layer):
    dst, lo, n, _, _, lof, pieces = plan
    ntile = h_all.shape[0] // MOE_TD
    shared = lambda w: pl.BlockSpec((None,) + w.shape[1:], lambda i, *_: (layer, 0, 0))
    grid_spec = pltpu.PrefetchScalarGridSpec(
        num_scalar_prefetch=4, grid=(ntile,),
        in_specs=[pl.BlockSpec((MOE_TD, D), lambda i, *_: (i, 0)),
                  pl.BlockSpec((MOE_TD, 128), lambda i, *_: (i, 0)),
                  pl.BlockSpec((None, 1, 128), lambda i, *_: (i, 0, 0)),
                  pl.BlockSpec((None, 1, 128), lambda i, *_: (i, 0, 0)),
                  shared(sg), shared(su), shared(sd),
                  pl.BlockSpec(memory_space=pl.ANY)],
        out_specs=pl.BlockSpec((MOE_TD, D), lambda i, *_: (i, 0)),
        scratch_shapes=[pltpu.VMEM((2, MOE_LS, D), BF16), pltpu.SemaphoreType.DMA((2,))])
    return pl.pallas_call(
        functools.partial(_combine_kernel, ntile=ntile),
        grid_spec=grid_spec,
        out_shape=jax.ShapeDtypeStruct((h_all.shape[0], D), F32),
        compiler_params=_cparams(("arbitrary",)),
        name="moe_combine",
    )(dst.reshape(-1), lo.reshape(-1), n.reshape(-1), pieces.reshape(-1), h_all, comb_all,
      lof.reshape(PLAN_ROWS, 1, 128), n.astype(F32).reshape(PLAN_ROWS, 1, 128), sg, su, sd, ys)


def _moe(h_all, comb_all, wg, wu, wd, sg, su, sd, *, layer, n_tokens):
    ntile = h_all.shape[0] // MOE_TD
    max_rows = n_tokens * TOPK + ntile * NE * (MOE_SEG - 1) + NE * (MOE_TR - 1)
    ntmax = -(-max_rows // MOE_TR)
    plan = _moe_plan(comb_all, ntmax=ntmax)
    xs = _moe_dispatch(plan, h_all, comb_all, ntmax=ntmax)
    ys = _moe_ffn(plan, xs, wg, wu, wd, layer=layer, ntmax=ntmax)
    return _moe_combine(plan, ys, h_all, comb_all, sg, su, sd, layer=layer)


def _final_kernel(x_ref, f_ref, pmod_ref, png_ref, o_ref):
    o_ref[...] = _residual_in(x_ref, f_ref, pmod_ref, png_ref)


def _final(x, f_all, pmods, png, *, nb, seq_len, tm, row0):
    nt = seq_len // tm
    t0 = row0 // tm
    return pl.pallas_call(
        _final_kernel,
        grid=(nb, nt),
        in_specs=[pl.BlockSpec((tm, D), lambda b, t: (b * nt + t, 0)),
                  pl.BlockSpec((tm, D), lambda b, t: (t0 + b * nt + t, 0)),
                  pl.BlockSpec((None,) + pmods.shape[1:], lambda b, t: (b, 0, 0, 0)),
                  pl.BlockSpec(png.shape, lambda b, t: (0, 0))],
        out_specs=pl.BlockSpec((tm, D), lambda b, t: (b * nt + t, 0)),
        out_shape=jax.ShapeDtypeStruct(x.shape, F32),
        compiler_params=_cparams(("arbitrary", "arbitrary")),
        name="final_residual",
    )(x, f_all, pmods, png)


def kernel(x_prompt, x_sample, c_prompt, c_sample, state_conv, state_shift, state_wkv, w_mod, b_mod, norm_g, ab_w_in, ab_conv_w, ab_mu, ab_w0, ab_w_decay_up, ab_a0, ab_a_up, ab_g_up, ab_k_k, ab_k_a, ab_r_k, ab_gn_g, ab_gn_b, ab_w_out, gm_w_in, gm_b_in, gm_ln_g, gm_ln_b, gm_w_s, gm_b_s, gm_w_out, moe_w_router, moe_b_router, moe_w_gate, moe_w_up, moe_w_down, moe_ws_gate, moe_ws_up, moe_ws_down):
    bsz, seq_len, _ = x_prompt.shape
    nsamp = x_sample.shape[0]
    nheads = DB // HEAD
    n_prompt = bsz * seq_len
    n_all = -(-(n_prompt + nsamp) // MOE_TD) * MOE_TD
    tm = 512

    xp = x_prompt.reshape(n_prompt, D)
    xs = x_sample.reshape(nsamp, D)

    m = _modulation(jnp.concatenate([c_prompt, c_sample], axis=0), w_mod, b_mod)
    mods_p = [m[l, :bsz].reshape(bsz, 6, 1, D) for l in range(2)]
    mods_s = [m[l, bsz:].reshape(nsamp, 6, D).transpose(1, 0, 2)[None] for l in range(2)]

    row = lambda t: t.reshape(1, -1)
    pad_rows = lambda t, lo, hi: jnp.pad(t, ((lo, hi), (0, 0)))
    small = [ab_conv_w[0], row(ab_mu[0]), row(ab_w0[0]),
             pad_rows(ab_w_decay_up[0], 0, 64).astype(BF16), row(ab_a0[0]),
             pad_rows(ab_a_up[0], 64, 0).astype(BF16), ab_g_up[0].astype(BF16),
             row(ab_k_k[0]), row(ab_k_a[0])]
    win0 = ab_w_in[0].astype(BF16)
    wo0 = ab_w_out[0].astype(BF16)
    rk, gng, gnb = row(ab_r_k[0]), row(ab_gn_g[0]), row(ab_gn_b[0])
    wrt = [moe_w_router[l].T for l in range(2)]
    br = [moe_b_router[l].reshape(NE, 1) for l in range(2)]

    h2_all = jnp.zeros((n_all, D), BF16)
    comb_all = jnp.zeros((n_all, 128), F32)

    ya, r, lw, k, v, kk, a, g, ctail, stail = _ab_pre(
        xp, mods_p[0], norm_g[0], win0, small, None, nb=bsz, seq_len=seq_len, tm=tm, seq_mode=True)
    conv_p = ctail.reshape(bsz, 8, DA)[:, 6:8][None]
    shift_p = stail.reshape(bsz, 8, PB)[:, 7][None]
    yb, sfin = _wkv_seq(r, lw, k, v, kk, a, g, rk, gng, gnb, nb=bsz, seq_len=seq_len)
    sfin = sfin.reshape(bsz, nheads // 2, 2, HEAD, 2, HEAD)
    wkv_p = jnp.stack([sfin[:, :, 0, :, 0, :], sfin[:, :, 1, :, 1, :]], axis=2)
    wkv_p = wkv_p.reshape(bsz, nheads, HEAD, HEAD).transpose(0, 1, 3, 2)[None]
    xp1, h2_all, comb_all = _ab_post(ya, yb, xp, mods_p[0], norm_g[0], wo0, wrt[0], br[0], h2_all, comb_all,
                                     nb=bsz, seq_len=seq_len, tm=tm, row0=0)

    prev = [state_conv[0, :, 0], state_conv[0, :, 1], state_shift[0]]
    ya_s, r, lw, k, v, kk, a, g, gated_s, pb_s = _ab_pre(
        xs, mods_s[0], norm_g[0], win0, small, prev, nb=1, seq_len=nsamp, tm=nsamp, seq_mode=False)
    conv_s = jnp.stack([state_conv[0, :, 1], gated_s], axis=1)[None]
    shift_s = pb_s[None]
    vecs = [t.reshape(nsamp * nheads, 1, HEAD) for t in (r, lw, k, v, kk, a, g)]
    params = [t.reshape(nheads, 1, HEAD) for t in (ab_r_k[0], ab_gn_g[0], ab_gn_b[0])]
    yb_s, snew = _wkv_step(state_wkv[0].reshape(nsamp * nheads, HEAD, HEAD), vecs, params, nheads=nheads)
    wkv_s = snew.reshape(1, nsamp, nheads, HEAD, HEAD)
    xs1, h2_all, comb_all = _ab_post(ya_s, yb_s.reshape(nsamp, DB).astype(BF16), xs, mods_s[0], norm_g[0], wo0,
                                     wrt[0], br[0], h2_all, comb_all, nb=1, seq_len=nsamp, tm=nsamp, row0=n_prompt)

    def moe_layer(l, h_all, c_all):
        return _moe(h_all, c_all, moe_w_gate, moe_w_up, moe_w_down,
                    moe_ws_gate, moe_ws_up, moe_ws_down, layer=l, n_tokens=n_prompt + nsamp)

    f0 = moe_layer(0, h2_all, comb_all)

    gw = [gm_w_in[0].astype(BF16), row(gm_b_in[0]), row(gm_ln_g[0]), row(gm_ln_b[0])]
    gw_seq = gw + [gm_w_s[0], gm_b_s[0].T, gm_w_out[0].astype(BF16), wrt[1], br[1]]
    gw_step = gw + [row(jnp.repeat(gm_w_s[0, :, 0, 0], CHUNK)), row(jnp.repeat(gm_b_s[0, :, 0], CHUNK)),
                    gm_w_out[0].astype(BF16), wrt[1], br[1]]
    tm1 = 512
    xp2, h2_all, comb_all = _gmlp(xp1, f0, mods_p[0], norm_g[0], mods_p[1], norm_g[1], gw_seq, h2_all, comb_all,
                                  nb=bsz, seq_len=seq_len, tm=tm1, row0=0, seq_mode=True)
    xs2, h2_all, comb_all, chunk_v = _gmlp(xs1, f0, mods_s[0], norm_g[0], mods_s[1], norm_g[1], gw_step, h2_all,
                                           comb_all, nb=1, seq_len=nsamp, tm=nsamp, row0=n_prompt, seq_mode=False)
    f1 = moe_layer(1, h2_all, comb_all)
    y_p = _final(xp2, f1, mods_p[1], norm_g[1], nb=bsz, seq_len=seq_len, tm=tm, row0=0)
    y_s = _final(xs2, f1, mods_s[1], norm_g[1], nb=1, seq_len=nsamp, tm=nsamp, row0=n_prompt)

    return (y_p.reshape(bsz, seq_len, D), y_s.reshape(nsamp, 1, D), conv_p, conv_s, shift_p, shift_s,
            wkv_p, wkv_s, chunk_v.reshape(1, nsamp, 1, D))
```

```python
import functools

import jax
import jax.numpy as jnp
from jax import lax
from jax.experimental import pallas as pl
from jax.experimental.pallas import tpu as pltpu

F32 = jnp.float32
BF16 = jnp.bfloat16

D = 1024
DA = 512
DB = 512
HEAD = 64
PB = 1792
PAB = 3 * DA + PB
NE = 64
NG = 8
EPG = NE // NG
TOPG = 4
TOPK = 8
DE = 256
ROUTE_SCALE = 2.5
EPS = 1e-6
GN_EPS = 64e-5
CHUNK = 128
WKV_SLAB = 128
WKV_CHUNK = 32
WKV_PAIRS_PER_STEP = 4
MOE_TD = 512
MOE_SEG = 16
MOE_LS = 5120
MOE_CH = 1024
MOE_PIECES = (64, 16)
MOE_TR = 1024
MOE_XW = D + 128
MOE_XBUF = 3
PLAN_ROWS = 128

VMEM_LIMIT = 56 * 1024 * 1024


def _cparams(sem):
    return pltpu.CompilerParams(dimension_semantics=sem, vmem_limit_bytes=VMEM_LIMIT)


def _dot(a, b, precision=None):
    return jnp.dot(a, b, preferred_element_type=F32, precision=precision)


def _dot_nt(a, b, precision=None):
    return lax.dot_general(a, b, (((1,), (1,)), ((), ())), preferred_element_type=F32, precision=precision)


def _iota(shape, dim):
    return lax.broadcasted_iota(jnp.int32, shape, dim)


def _rms(x, g):
    return x * lax.rsqrt(jnp.mean(x * x, axis=-1, keepdims=True) + EPS) * g


def _silu(x):
    return x * jax.nn.sigmoid(x)


def _seg_ones(n, seg):
    return (_iota((n, n), 0) // seg == _iota((n, n), 1) // seg).astype(F32)


def _mod_kernel(c_ref, w_ref, b_ref, o_ref):
    s = _silu(c_ref[...]).astype(BF16)
    o_ref[...] = _dot(s, w_ref[...].astype(BF16)) + b_ref[...]


def _modulation(c_all, w_mod, b_mod):
    depth, _, width = w_mod.shape
    rows = c_all.shape[0]
    bn = 512
    return pl.pallas_call(
        _mod_kernel,
        grid=(depth, width // bn),
        in_specs=[pl.BlockSpec((rows, D), lambda l, j: (0, 0)),
                  pl.BlockSpec((None, D, bn), lambda l, j: (l, 0, j)),
                  pl.BlockSpec((None, 1, bn), lambda l, j: (l, 0, j))],
        out_specs=pl.BlockSpec((None, rows, bn), lambda l, j: (l, 0, j)),
        out_shape=jax.ShapeDtypeStruct((depth, rows, width), F32),
        compiler_params=_cparams(("arbitrary", "arbitrary")),
        name="modulation",
    )(c_all, w_mod, b_mod.reshape(depth, 1, width))


def _residual_in(x_ref, f_ref, pmod_ref, png_ref):
    x = x_ref[...]
    if f_ref is None:
        return x
    return x + pmod_ref[5] * _rms(f_ref[...], png_ref[3:4, :])


def _route(h, wrt_ref, br_ref):
    tm = h.shape[0]
    logits = _mm(wrt_ref[...], h, 2, 2, nt=True)
    scores = jax.nn.sigmoid(logits)
    sel = scores + br_ref[...]
    sub = _iota((EPG, tm), 0).astype(F32)
    neg = jnp.float32(-jnp.inf)
    blks, sblks, gscore = [], [], []
    for g in range(NG):
        blk = sel[g * EPG:(g + 1) * EPG, :]
        m1 = jnp.max(blk, axis=0, keepdims=True)
        first = jnp.min(jnp.where(blk == m1, sub, float(EPG)), axis=0, keepdims=True)
        m2 = jnp.max(jnp.where(sub == first, neg, blk), axis=0, keepdims=True)
        blks.append(blk)
        sblks.append(scores[g * EPG:(g + 1) * EPG, :])
        gscore.append(m1 + m2)
    masked = []
    for g in range(NG):
        cnt = jnp.zeros((1, tm), F32)
        for g2 in range(NG):
            if g2 == g:
                continue
            beats = (gscore[g2] >= gscore[g]) if g2 < g else (gscore[g2] > gscore[g])
            cnt = cnt + jnp.where(beats, 1.0, 0.0)
        masked.append(jnp.where(cnt < TOPG, blks[g], neg))
    eidx = [sub + float(g * EPG) for g in range(NG)]
    chosen = [jnp.zeros((EPG, tm), F32) for _ in range(NG)]
    for _ in range(TOPK):
        best = masked[0]
        for g in range(1, NG):
            best = jnp.maximum(best, masked[g])
        best = jnp.max(best, axis=0, keepdims=True)
        first = jnp.where(masked[0] == best, eidx[0], float(NE))
        for g in range(1, NG):
            first = jnp.minimum(first, jnp.where(masked[g] == best, eidx[g], float(NE)))
        first = jnp.min(first, axis=0, keepdims=True)
        for g in range(NG):
            hit = eidx[g] == first
            chosen[g] = jnp.where(hit, 1.0, chosen[g])
            masked[g] = jnp.where(hit, neg, masked[g])
    wts = [jnp.where(chosen[g] > 0.0, sblks[g], 0.0) for g in range(NG)]
    tot = wts[0]
    for g in range(1, NG):
        tot = tot + wts[g]
    denom = jnp.sum(tot, axis=0, keepdims=True)
    comb_t = jnp.concatenate([w / denom * ROUTE_SCALE for w in wts] + [jnp.zeros((NE, tm), F32)], axis=0)
    return comb_t.T


def _post(y, x, mod_ref, ng_ref, wrt_ref, br_ref, xo_ref, h2_ref, comb_ref):
    xn = x + mod_ref[2] * _rms(y, ng_ref[1:2, :])
    xo_ref[...] = xn
    h2 = _rms(xn, ng_ref[2:3, :]) * (1.0 + mod_ref[4]) + mod_ref[3]
    h2_ref[...] = h2.astype(BF16)
    comb_ref[...] = _route(h2, wrt_ref, br_ref)


def _ab_pre_kernel(*refs, seq_mode, tm):
    it = iter(refs)
    x_ref, mod_ref, ng_ref, win_ref = next(it), next(it), next(it), next(it)
    cw_ref, mu_ref, w0_ref, wd_ref, a0_ref, wa_ref, wg_ref, kk_ref, ka_ref = (next(it) for _ in range(9))
    if not seq_mode:
        p2_ref, p1_ref, ps_ref = next(it), next(it), next(it)
    ya_ref, r_ref, lw_ref, k_ref, v_ref, kkn_ref, a_ref, g_ref, ctail_ref, stail_ref = (next(it) for _ in range(10))
    if seq_mode:
        cc_ref, sc_ref = next(it), next(it)

        @pl.when(pl.program_id(1) == 0)
        def _():
            cc_ref[...] = jnp.zeros_like(cc_ref)
            sc_ref[...] = jnp.zeros_like(sc_ref)

    x = x_ref[...]
    h = _rms(x, ng_ref[0:1, :]) * (1.0 + mod_ref[1]) + mod_ref[0]
    proj = _dot(h.astype(BF16), win_ref[...])
    a_h, a_b, a_c = proj[:, 0:DA], proj[:, DA:2 * DA], proj[:, 2 * DA:3 * DA]
    pb = proj[:, 3 * DA:]
    gated = a_c * a_h
    if seq_mode:
        rows = _iota((tm, 1), 0)
        g1 = jnp.where(rows == 0, cc_ref[7:8, :], pltpu.roll(gated, 1, 0))
        g2 = jnp.where(rows == 0, cc_ref[6:7, :], jnp.where(rows == 1, cc_ref[7:8, :], pltpu.roll(gated, 2, 0)))
        pprev = jnp.where(rows == 0, sc_ref[7:8, :], pltpu.roll(pb, 1, 0))
        cc_ref[...] = gated[tm - 8:, :]
        sc_ref[...] = pb[tm - 8:, :]
        ctail_ref[...] = gated[tm - 8:, :]
        stail_ref[...] = pb[tm - 8:, :]
    else:
        g1, g2, pprev = p1_ref[...], p2_ref[...], ps_ref[...]
        ctail_ref[...] = gated
        stail_ref[...] = pb
    conv = g2 * cw_ref[0:1, :] + g1 * cw_ref[1:2, :] + gated * cw_ref[2:3, :]
    ya_ref[...] = (a_b * conv).astype(BF16)

    xm = pb + (pprev - pb) * mu_ref[...]
    r, k, v = xm[:, 0:DB], xm[:, DB:2 * DB], xm[:, 2 * DB:3 * DB]
    lowrank = xm[:, 3 * DB:3 * DB + 128]
    dg = xm[:, 3 * DB + 128:]
    w_pre = w0_ref[...] + _dot(jnp.tanh(lowrank).astype(BF16), wd_ref[...])
    z = -w_pre
    softplus = jnp.maximum(z, 0.0) + jnp.log(1.0 + jnp.exp(-jnp.abs(z)))
    lw_ref[...] = -jnp.exp(-softplus - 0.5)
    a = jax.nn.sigmoid(a0_ref[...] + _dot(lowrank.astype(BF16), wa_ref[...]))
    g_ref[...] = _dot(jax.nn.sigmoid(dg).astype(BF16), wg_ref[...])
    kk = k * kk_ref[...]
    ss = _mm(kk * kk, _seg_ones(DB, HEAD), 2, 1)
    kkn_ref[...] = kk / jnp.maximum(jnp.sqrt(ss), 1e-12)
    r_ref[...] = r
    k_ref[...] = k * (1.0 + (a - 1.0) * ka_ref[...])
    v_ref[...] = v
    a_ref[...] = a


def _ab_pre(x, mods, ng, win, small, prev, *, nb, seq_len, tm, seq_mode):
    nt = seq_len // tm
    rows = nb * seq_len
    tail = 8 if seq_mode else tm
    full = lambda shape: pl.BlockSpec(shape, lambda b, t: tuple(0 for _ in shape))
    tok = lambda w: pl.BlockSpec((tm, w), lambda b, t: (b * nt + t, 0))
    in_specs = [tok(D),
                pl.BlockSpec((None,) + mods.shape[1:], lambda b, t: (b, 0, 0, 0)),
                full(ng.shape), full(win.shape)] + [full(s.shape) for s in small]
    args = [x, mods, ng, win] + list(small)
    if not seq_mode:
        in_specs += [tok(DA), tok(DA), tok(PB)]
        args += list(prev)
    out_shape = ([jax.ShapeDtypeStruct((rows, DA), BF16)] + [jax.ShapeDtypeStruct((rows, DB), F32)] * 7
                 + [jax.ShapeDtypeStruct((nb * tail, DA), F32), jax.ShapeDtypeStruct((nb * tail, PB), F32)])
    out_specs = ([tok(DA)] + [tok(DB)] * 7
                 + [pl.BlockSpec((tail, DA), lambda b, t: (b, 0)), pl.BlockSpec((tail, PB), lambda b, t: (b, 0))])
    scratch = [pltpu.VMEM((8, DA), F32), pltpu.VMEM((8, PB), F32)] if seq_mode else []
    return pl.pallas_call(
        functools.partial(_ab_pre_kernel, seq_mode=seq_mode, tm=tm),
        grid=(nb, nt), in_specs=in_specs, out_specs=out_specs, out_shape=out_shape,
        scratch_shapes=scratch,
        compiler_params=_cparams(("arbitrary", "arbitrary")),
        name="ab_pre_seq" if seq_mode else "ab_pre_step",
    )(*args)


def _split(x):
    hi = x.astype(BF16)
    return hi, (x - hi.astype(F32)).astype(BF16)


def _mm(a, b, pa=1, pb=1, nt=False):
    ah, al = _split(a) if pa == 2 else (a.astype(BF16), None)
    bh, bl = _split(b) if pb == 2 else (b.astype(BF16), None)
    bx = 1 if nt else 0
    if pa == 2 and pb == 2:
        ah, bh = jnp.concatenate([ah, al, ah], axis=1), jnp.concatenate([bh, bh, bl], axis=bx)
    elif pa == 2:
        ah, bh = jnp.concatenate([ah, al], axis=1), jnp.concatenate([bh, bh], axis=bx)
    elif pb == 2:
        ah, bh = jnp.concatenate([ah, ah], axis=1), jnp.concatenate([bh, bl], axis=bx)
    return _dot_nt(ah, bh) if nt else _dot(ah, bh)


def _gn_gate(o, r, k, v, g, rk, gng, gnb, seg):
    mean = _mm(o, seg, 2, 1) * (1.0 / HEAD)
    d = o - mean
    var = _mm(d * d, seg, 2, 1) * (1.0 / HEAD)
    out = d * lax.rsqrt(var + GN_EPS) * gng + gnb
    out = out + _mm(r * k * rk, seg, 2, 1) * v
    return out * g


def _wkv_pair(r, lw, k, v, kk, a, st, cst):
    ts, c = WKV_SLAB, WKV_CHUNK
    nchunk = ts // c
    strict, incl, cum_lhs, m0, m1, headdiag, eye, eye2, blks, chunk_cols = cst

    l1 = lw.astype(BF16)
    rem = lw - l1.astype(F32)
    l2 = rem.astype(BF16)
    l3 = (rem - l2.astype(F32)).astype(BF16)
    cums = _dot(cum_lhs, jnp.concatenate([l1, l2, l3], axis=0))
    yield
    cl, ctot = cums[:ts], cums[ts:]
    e_neg = jnp.exp(-cl)
    e_end = jnp.exp(ctot - cl)
    at = -kk * jnp.exp(cl - lw)
    rt = r * jnp.exp(cl)
    beta = kk * a
    bk = jnp.concatenate([beta * e_neg, k * e_neg], axis=0)
    bend, kend = beta * e_end, k * e_end

    at_st = jnp.concatenate([at * m0, at * m1], axis=0)
    a_all = _mm(jnp.concatenate([at_st, rt * m0, rt * m1], axis=0), bk, nt=True)
    yield
    a_st, a_out = a_all[:2 * ts], a_all[2 * ts:]
    zero = jnp.zeros((ts, ts), F32)

    def two_heads(p0, p1):
        return jnp.concatenate([jnp.concatenate([p0, zero], axis=1), jnp.concatenate([zero, p1], axis=1)], axis=0)

    lbd = two_heads(a_st[:ts, :ts] * strict, a_st[ts:, :ts] * strict)
    d1 = jnp.where(blks[0], lbd, 0.0)
    tinv = eye2 + d1
    d2 = _mm(d1, d1)
    yield
    tinv = tinv + _mm(tinv, d2)
    d4 = _mm(d2, d2)
    yield
    tinv = tinv + _mm(tinv, d4)
    yield
    for lvl in range(1, len(blks)):
        off = jnp.where(blks[lvl] & jnp.logical_not(blks[lvl - 1]), lbd, 0.0)
        half = _mm(tinv, off)
        yield
        tinv = tinv + _mm(half, tinv)
        yield

    v_st = jnp.concatenate([v * m0, v * m1], axis=0)
    y_st = _mm(two_heads(a_st[:ts, ts:] * strict, a_st[ts:, ts:] * strict), v_st)
    yield
    wu = _mm(tinv, jnp.concatenate([at_st, y_st], axis=1))
    yield
    w = wu[:ts, :2 * HEAD] + wu[ts:, :2 * HEAD]
    u = wu[:ts, 2 * HEAD:] + wu[ts:, 2 * HEAD:]
    rhs = jnp.concatenate([jnp.concatenate([w, u], axis=1),
                           jnp.concatenate([jnp.zeros_like(v), v], axis=1)], axis=0)
    bend_t, kend_t = bend.T, kend.T
    lhs = [jnp.concatenate([bend_t * cm, kend_t * cm], axis=1) for cm in chunk_cols]
    lhs.append(jnp.concatenate([a_out[:ts, :ts] * incl, a_out[:ts, ts:] * incl], axis=1))
    lhs.append(jnp.concatenate([a_out[ts:, :ts] * incl, a_out[ts:, ts:] * incl], axis=1))
    big = _mm(jnp.concatenate(lhs, axis=0), rhs)
    yield
    mn_all, qo = big[:nchunk * ts], big[nchunk * ts:]
    q = rt + qo[:ts, :2 * HEAD] * m0 + qo[ts:, :2 * HEAD] * m1
    olocal = qo[:ts, 2 * HEAD:] * m0 + qo[ts:, 2 * HEAD:] * m1

    outs = []
    for ch in range(nchunk):
        mn = mn_all[ch * ts:(ch + 1) * ts]
        trans = eye * jnp.exp(ctot[ch * c:ch * c + 1, :]) + mn[:, :2 * HEAD] * headdiag
        both = _mm(jnp.concatenate([q[ch * c:(ch + 1) * c], trans], axis=0), st)
        yield
        outs.append(both[:c] + olocal[ch * c:(ch + 1) * c])
        st = both[c:] + mn[:, 2 * HEAD:] * headdiag
    return jnp.concatenate(outs, axis=0), st


def _round_robin(gens):
    results = [None] * len(gens)
    active = list(enumerate(gens))
    while active:
        still = []
        for i, gen in active:
            try:
                next(gen)
                still.append((i, gen))
            except StopIteration as stop:
                results[i] = stop.value
        active = still
    return results


def _wkv_seq_kernel(r_ref, lw_ref, k_ref, v_ref, kk_ref, a_ref, g_ref, rk_ref, gng_ref, gnb_ref,
                    y_ref, sfin_ref, st_ref, *, npp):
    ts, c = WKV_SLAB, WKV_CHUNK

    @pl.when(pl.program_id(2) == 0)
    def _():
        st_ref[...] = jnp.zeros_like(st_ref)

    ri, ci = _iota((ts, ts), 0), _iota((ts, ts), 1)
    same = ri // c == ci // c
    strict = (same & (ci < ri)).astype(F32)
    incl = (same & (ci <= ri)).astype(F32)
    cum_lhs = jnp.concatenate([incl, same.astype(F32)], axis=0).astype(BF16)
    cum_lhs = jnp.concatenate([cum_lhs] * 3, axis=1)
    lane = _iota((1, 2 * HEAD), 1)
    m0 = (lane < HEAD).astype(F32)
    m1 = 1.0 - m0
    headdiag = _seg_ones(2 * HEAD, HEAD)
    eye = (_iota((2 * HEAD, 2 * HEAD), 0) == _iota((2 * HEAD, 2 * HEAD), 1)).astype(F32)
    r2, c2 = _iota((2 * ts, 2 * ts), 0), _iota((2 * ts, 2 * ts), 1)
    eye2 = (r2 == c2).astype(F32)
    blks, b = [], 8
    while b <= c:
        blks.append(r2 // b == c2 // b)
        b *= 2
    tcol = _iota((1, ts), 1) // c
    chunk_cols = [(tcol == ch).astype(F32) for ch in range(ts // c)]
    cst = (strict, incl, cum_lhs, m0, m1, headdiag, eye, eye2, blks, chunk_cols)

    sls = [slice(p * 2 * HEAD, (p + 1) * 2 * HEAD) for p in range(npp)]
    res = _round_robin([_wkv_pair(r_ref[:, sl], lw_ref[:, sl], k_ref[:, sl], v_ref[:, sl], kk_ref[:, sl],
                                  a_ref[:, sl], st_ref[p], cst) for p, sl in enumerate(sls)])
    for p, sl in enumerate(sls):
        o, st = res[p]
        st_ref[p] = st
        y_ref[:, sl] = _gn_gate(o, r_ref[:, sl], k_ref[:, sl], v_ref[:, sl], g_ref[:, sl], rk_ref[:, sl],
                                gng_ref[:, sl], gnb_ref[:, sl], headdiag).astype(BF16)

    @pl.when(pl.program_id(2) == pl.num_programs(2) - 1)
    def _():
        sfin_ref[...] = st_ref[...]


def _wkv_seq(r, lw, k, v, kk, a, g, rk, gng, gnb, *, nb, seq_len):
    ns = seq_len // WKV_SLAB
    npp = WKV_PAIRS_PER_STEP
    width = 2 * HEAD * npp
    tok = pl.BlockSpec((WKV_SLAB, width), lambda b, p, s: (b * ns + s, p))
    par = pl.BlockSpec((1, width), lambda b, p, s: (0, p))
    return pl.pallas_call(
        functools.partial(_wkv_seq_kernel, npp=npp),
        grid=(nb, DB // width, ns),
        in_specs=[tok] * 7 + [par] * 3,
        out_specs=[tok, pl.BlockSpec((None, npp, 2 * HEAD, 2 * HEAD), lambda b, p, s: (b, p, 0, 0))],
        out_shape=[jax.ShapeDtypeStruct((nb * seq_len, DB), BF16),
                   jax.ShapeDtypeStruct((nb, DB // (2 * HEAD), 2 * HEAD, 2 * HEAD), F32)],
        scratch_shapes=[pltpu.VMEM((npp, 2 * HEAD, 2 * HEAD), F32)],
        compiler_params=_cparams(("arbitrary", "arbitrary", "arbitrary")),
        name="wkv_seq",
    )(r, lw, k, v, kk, a, g, rk, gng, gnb)


def _wkv_step_kernel(s_ref, r_ref, lw_ref, k_ref, v_ref, kk_ref, a_ref, g_ref, rk_ref, gng_ref, gnb_ref,
                     y_ref, so_ref):
    s = s_ref[...]
    r, k, v, kk, a = r_ref[...], k_ref[...], v_ref[...], kk_ref[...], a_ref[...]
    w = jnp.exp(lw_ref[...])
    eye = (_iota((1, HEAD, HEAD), 1) == _iota((1, HEAD, HEAD), 2)).astype(F32)
    v_col = jnp.sum(eye * v, axis=2, keepdims=True)
    s_kk = jnp.sum(s * kk, axis=2, keepdims=True)
    s = s * w - s_kk * (kk * a) + v_col * k
    so_ref[...] = s
    o_col = jnp.sum(s * r, axis=2, keepdims=True)
    o = jnp.sum(eye * o_col, axis=1, keepdims=True)
    mean = jnp.mean(o, axis=2, keepdims=True)
    d = o - mean
    var = jnp.mean(d * d, axis=2, keepdims=True)
    out = d * lax.rsqrt(var + GN_EPS) * gng_ref[...] + gnb_ref[...]
    out = out + jnp.sum(r * k * rk_ref[...], axis=2, keepdims=True) * v
    y_ref[...] = out * g_ref[...]


def _wkv_step(state, vecs, params, *, nheads):
    n = state.shape[0]
    tb = 8 * nheads
    sspec = pl.BlockSpec((tb, HEAD, HEAD), lambda i: (i, 0, 0))
    vspec = pl.BlockSpec((tb, 1, HEAD), lambda i: (i, 0, 0))
    pspec = pl.BlockSpec((tb, 1, HEAD), lambda i: (0, 0, 0))
    reps = tb // nheads
    params = [jnp.tile(p, (reps, 1, 1)) for p in params]
    return pl.pallas_call(
        _wkv_step_kernel,
        grid=(n // tb,),
        in_specs=[sspec] + [vspec] * 7 + [pspec] * 3,
        out_specs=[vspec, sspec],
        out_shape=[jax.ShapeDtypeStruct((n, 1, HEAD), F32), jax.ShapeDtypeStruct((n, HEAD, HEAD), F32)],
        compiler_params=_cparams(("arbitrary",)),
        name="wkv_step",
    )(state, *vecs, *params)


def _ab_post_kernel(ya_ref, yb_ref, x_ref, mod_ref, ng_ref, wo_ref, wrt_ref, br_ref,
                    h2in_ref, combin_ref, xo_ref, h2_ref, comb_ref):
    del h2in_ref, combin_ref
    y = _dot(ya_ref[...], wo_ref[0:DA, :]) + _dot(yb_ref[...], wo_ref[DA:, :])
    _post(y, x_ref[...], mod_ref, ng_ref, wrt_ref, br_ref, xo_ref, h2_ref, comb_ref)


def _ab_post(ya, yb, x, mods, ng, wo, wrt, br, h2_all, comb_all, *, nb, seq_len, tm, row0):
    nt = seq_len // tm
    t0 = row0 // tm
    full = lambda shape: pl.BlockSpec(shape, lambda b, t: tuple(0 for _ in shape))
    tok = lambda w: pl.BlockSpec((tm, w), lambda b, t: (b * nt + t, 0))
    tok_all = lambda w: pl.BlockSpec((tm, w), lambda b, t: (t0 + b * nt + t, 0))
    anyspec = pl.BlockSpec(memory_space=pl.ANY)
    return pl.pallas_call(
        _ab_post_kernel,
        grid=(nb, nt),
        in_specs=[tok(DA), tok(DB), tok(D),
                  pl.BlockSpec((None,) + mods.shape[1:], lambda b, t: (b, 0, 0, 0)),
                  full(ng.shape), full(wo.shape), full(wrt.shape), full(br.shape), anyspec, anyspec],
        out_specs=[tok(D), tok_all(D), tok_all(128)],
        out_shape=[jax.ShapeDtypeStruct(x.shape, F32), jax.ShapeDtypeStruct(h2_all.shape, BF16),
                   jax.ShapeDtypeStruct(comb_all.shape, F32)],
        input_output_aliases={8: 1, 9: 2},
        compiler_params=_cparams(("arbitrary", "arbitrary")),
        name="ab_post",
    )(ya, yb, x, mods, ng, wo, wrt, br, h2_all, comb_all)


def _gmlp_kernel(*refs, seq_mode, tm):
    it = iter(refs)
    x_ref, f_ref, pmod_ref, png_ref, mod_ref, ng_ref = (next(it) for _ in range(6))
    win_ref, bin_ref, lng_ref, lnb_ref, ws_ref, bs_ref, wo_ref, wrt_ref, br_ref = (next(it) for _ in range(9))
    next(it), next(it)
    xo_ref, h2_ref, comb_ref = next(it), next(it), next(it)
    if not seq_mode:
        cv_ref = next(it)
    x = _residual_in(x_ref, f_ref, pmod_ref, png_ref)
    h = _rms(x, ng_ref[0:1, :]) * (1.0 + mod_ref[1]) + mod_ref[0]
    z = jax.nn.gelu(_dot(h.astype(BF16), win_ref[...]) + bin_ref[...])
    u, v = z[:, :D], z[:, D:]
    mu = jnp.mean(v, axis=-1, keepdims=True)
    var = jnp.mean(jnp.square(v - mu), axis=-1, keepdims=True)
    v = (v - mu) * lax.rsqrt(var + EPS) * lng_ref[...] + lnb_ref[...]
    if seq_mode:
        vb = v.astype(BF16)
        causal = _iota((CHUNK, CHUNK), 1) <= _iota((CHUNK, CHUNK), 0)
        cols = []
        for g in range(D // CHUNK):
            wsg = jnp.where(causal, ws_ref[g], 0.0).astype(BF16)
            bsg = bs_ref[:, g:g + 1]
            rows = [_dot(wsg, vb[c * CHUNK:(c + 1) * CHUNK, g * CHUNK:(g + 1) * CHUNK]) + bsg
                    for c in range(tm // CHUNK)]
            cols.append(jnp.concatenate(rows, axis=0) if len(rows) > 1 else rows[0])
        s = jnp.concatenate(cols, axis=1)
    else:
        cv_ref[...] = v
        s = v * ws_ref[...] + bs_ref[...]
    y = _dot((u * s).astype(BF16), wo_ref[...])
    _post(y, x, mod_ref, ng_ref, wrt_ref, br_ref, xo_ref, h2_ref, comb_ref)


def _gmlp(x, f_all, pmods, png, mods, ng, weights, h2_all, comb_all, *, nb, seq_len, tm, row0, seq_mode):
    nt = seq_len // tm
    t0 = row0 // tm
    full = lambda shape: pl.BlockSpec(shape, lambda b, t: tuple(0 for _ in shape))
    tok = lambda w: pl.BlockSpec((tm, w), lambda b, t: (b * nt + t, 0))
    tok_all = lambda w: pl.BlockSpec((tm, w), lambda b, t: (t0 + b * nt + t, 0))
    modspec = lambda m: pl.BlockSpec((None,) + m.shape[1:], lambda b, t: (b, 0, 0, 0))
    anyspec = pl.BlockSpec(memory_space=pl.ANY)
    in_specs = ([tok(D), tok_all(D), modspec(pmods), full(png.shape), modspec(mods), full(ng.shape)]
                + [full(w.shape) for w in weights] + [anyspec, anyspec])
    out_specs = [tok(D), tok_all(D), tok_all(128)]
    out_shape = [jax.ShapeDtypeStruct(x.shape, F32), jax.ShapeDtypeStruct(h2_all.shape, BF16),
                 jax.ShapeDtypeStruct(comb_all.shape, F32)]
    if not seq_mode:
        out_specs.append(tok(D))
        out_shape.append(jax.ShapeDtypeStruct(x.shape, F32))
    n_in = len(in_specs)
    return pl.pallas_call(
        functools.partial(_gmlp_kernel, seq_mode=seq_mode, tm=tm),
        grid=(nb, nt), in_specs=in_specs, out_specs=out_specs, out_shape=out_shape,
        input_output_aliases={n_in - 2: 1, n_in - 1: 2},
        compiler_params=_cparams(("arbitrary", "arbitrary")),
        name="gmlp_seq" if seq_mode else "gmlp_step",
    )(x, f_all, pmods, png, mods, ng, *weights, h2_all, comb_all)


def _strict_lower(n):
    return _iota((n, n), 1) < _iota((n, n), 0)


def _col_of_row(row):
    return jnp.broadcast_to(row, (128, 128)).T[:, 0:1]


def _plan_kernel(comb_ref, dst_ref, lo_ref, n_ref, te_ref, misc_ref, lof_ref, pieces_ref, cnt_ref, *, ntile, te_cols):
    i = pl.program_id(0)

    @pl.when(i == 0)
    def _():
        cnt_ref[...] = jnp.zeros_like(cnt_ref)

    sel = jnp.where(comb_ref[...] > 0.0, 1.0, 0.0)
    cnt_ref[pl.ds(i, 1), :] = jnp.sum(sel, axis=0, keepdims=True)

    @pl.when(i == ntile - 1)
    def _():
        c = cnt_ref[...]
        cpad = jnp.floor((c + (MOE_SEG - 1.0)) * (1.0 / MOE_SEG)) * MOE_SEG
        cb = cpad.astype(BF16)
        lower = jnp.where(_strict_lower(PLAN_ROWS), 1.0, 0.0).astype(BF16)
        upper = jnp.where(_iota((128, 128), 0) < _iota((128, 128), 1), 1.0, 0.0).astype(BF16)
        before = _dot(lower, cb)
        lo = _dot(cb, upper)
        tot = jnp.sum(cpad, axis=0, keepdims=True)
        rt = jnp.floor((tot + (MOE_TR - 1.0)) * (1.0 / MOE_TR))
        gt = _dot(jnp.broadcast_to(rt, (8, 128)).astype(BF16), upper)[0:1, :]
        dst_ref[...] = (gt * MOE_TR + before).astype(jnp.int32)
        lo_ref[...] = lo.astype(jnp.int32)
        lof_ref[...] = lo
        n_ref[...] = cpad.astype(jnp.int32)
        ends_col = _col_of_row(gt + rt)
        jrow = _iota((1, te_cols), 1).astype(F32)
        te = jnp.sum(jnp.where(ends_col <= jrow, 1.0, 0.0), axis=0, keepdims=True)
        te_ref[...] = jnp.broadcast_to(jnp.minimum(te, NE - 1.0), (8, te_cols)).astype(jnp.int32)
        nused = jnp.sum(rt, axis=1, keepdims=True)
        rowi = _iota((8, 128), 0)
        misc = jnp.where(rowi == 0, gt * MOE_TR + tot, jnp.where(rowi == 1, rt * MOE_TR - tot, nused))
        misc_ref[...] = misc.astype(jnp.int32)
        lane = _iota((1, 128), 1)
        pieces = jnp.zeros((PLAN_ROWS, 128), F32)
        taken = jnp.zeros_like(cpad)
        for k, size in enumerate(MOE_PIECES):
            cnt = jnp.floor((cpad - taken) * (1.0 / size))
            taken = taken + cnt * size
            pieces = pieces + jnp.where(lane == k, jnp.sum(cnt, axis=1, keepdims=True), 0.0)
        pieces_ref[...] = pieces.astype(jnp.int32)


def _moe_plan(comb_all, *, ntmax):
    ntile = comb_all.shape[0] // MOE_TD
    te_cols = -(-ntmax // 128) * 128
    full = lambda shape: pl.BlockSpec(shape, lambda i: tuple(0 for _ in shape))
    shapes = [((PLAN_ROWS, 128), jnp.int32)] * 3 + [((8, te_cols), jnp.int32), ((8, 128), jnp.int32),
                                                   ((PLAN_ROWS, 128), F32), ((PLAN_ROWS, 128), jnp.int32)]
    return pl.pallas_call(
        functools.partial(_plan_kernel, ntile=ntile, te_cols=te_cols),
        grid=(ntile,),
        in_specs=[pl.BlockSpec((MOE_TD, 128), lambda i: (i, 0))],
        out_specs=[full(s) for s, _ in shapes],
        out_shape=[jax.ShapeDtypeStruct(s, d) for s, d in shapes],
        scratch_shapes=[pltpu.VMEM((PLAN_ROWS, 128), F32)],
        compiler_params=_cparams(("arbitrary",)),
        name="moe_plan",
    )(comb_all)


def _rank_plus(comb):
    sel = jnp.where(comb > 0.0, 1.0, 0.0)
    lower = jnp.where(_strict_lower(MOE_TD), 1.0, 0.0).astype(BF16)
    return sel * (_dot(lower, sel.astype(BF16)) + 1.0)


def _for_each_piece(n, fn):
    off = 0
    for size in MOE_PIECES:
        cnt = (n - off) // size

        def body(p, carry, off=off, size=size):
            fn(off + p * size, size)
            return carry

        lax.fori_loop(0, cnt, body, 0)
        off = off + cnt * size


def _wait_pieces(pieces_s, tile, make_copy):
    for k, size in enumerate(MOE_PIECES):
        def wait_one(p, carry, size=size):
            make_copy(size).wait()
            return carry

        lax.fori_loop(0, pieces_s[tile * 128 + k], wait_one, 0)


def _dispatch_kernel(dst_s, lo_s, n_s, misc_s, pieces_s, h_ref, comb_ref, lof_ref, nf_ref, xs_hbm, buf, zbuf, sem,
                     tsem, *, ntile):
    i = pl.program_id(0)
    slot = lax.rem(i, 2)
    comb = comb_ref[...]
    lo_row = lof_ref[...]
    hi_row = lo_row + nf_ref[...]
    rp_hi, rp_lo = _split(_rank_plus(comb).T)
    rhs_rank = jnp.concatenate([rp_hi, rp_lo], axis=0)
    c_hi, c_lo = _split(comb)
    h_ext = jnp.concatenate([h_ref[...], c_hi, c_lo], axis=1)
    lane_lo = _iota((1, 128), 1) < 64
    for ch in range(MOE_LS // MOE_CH):
        srow = (_iota((MOE_CH, 1), 0) + ch * MOE_CH).astype(F32)
        e_t = jnp.where((srow >= lo_row) & (srow < hi_row), 1.0, 0.0)
        e_tb = e_t.astype(BF16)
        rank_of = _dot(jnp.concatenate([e_tb, e_tb], axis=1), rhs_rank)
        target = srow + 1.0 - jnp.sum(e_t * lo_row, axis=1, keepdims=True)
        perm = jnp.where(rank_of == target, 1.0, 0.0).astype(BF16)
        rows = pl.ds(ch * MOE_CH, MOE_CH)
        sorted_rows = _dot(perm, h_ext)
        buf[slot, rows, :D] = sorted_rows[:, :D].astype(BF16)
        w = jnp.sum((sorted_rows[:, D:D + 128] + sorted_rows[:, D + 128:]) * e_t, axis=1, keepdims=True)
        w_hi = w.astype(BF16).astype(F32)
        buf[slot, rows, D:] = jnp.where(lane_lo, w_hi, w - w_hi).astype(BF16)

    def copy_out(sl, src_row, dst_row, size):
        return pltpu.make_async_copy(buf.at[sl, pl.ds(pl.multiple_of(src_row, MOE_SEG), size)],
                                     xs_hbm.at[pl.ds(pl.multiple_of(dst_row, MOE_SEG), size)], sem.at[sl])

    def start_seg(e, carry):
        idx = i * 128 + e
        lo, dst = lo_s[idx], dst_s[idx]
        _for_each_piece(n_s[idx], lambda off, size: copy_out(slot, lo + off, dst + off, size).start())
        return carry

    lax.fori_loop(0, NE, start_seg, 0)

    @pl.when(i > 0)
    def _():
        _wait_pieces(pieces_s, i - 1, lambda size: copy_out(1 - slot, 0, 0, size))

    @pl.when(i == ntile - 1)
    def _():
        _wait_pieces(pieces_s, i, lambda size: copy_out(slot, 0, 0, size))
        zbuf[...] = jnp.zeros_like(zbuf)

        def tail_copy(e, off, size):
            return pltpu.make_async_copy(
                zbuf.at[pl.ds(0, size)], xs_hbm.at[pl.ds(pl.multiple_of(misc_s[e] + off, MOE_SEG), size)], tsem)

        def tail_start(e, carry):
            _for_each_piece(misc_s[128 + e], lambda off, size: tail_copy(e, off, size).start())
            return carry

        def tail_wait(e, carry):
            _for_each_piece(misc_s[128 + e], lambda off, size: tail_copy(e, off, size).wait())
            return carry

        lax.fori_loop(0, NE, tail_start, 0)
        lax.fori_loop(0, NE, tail_wait, 0)


def _moe_dispatch(plan, h_all, comb_all, *, ntmax):
    dst, lo, n, _, misc, lof, pieces = plan
    ntile = h_all.shape[0] // MOE_TD
    grid_spec = pltpu.PrefetchScalarGridSpec(
        num_scalar_prefetch=5, grid=(ntile,),
        in_specs=[pl.BlockSpec((MOE_TD, D), lambda i, *_: (i, 0)),
                  pl.BlockSpec((MOE_TD, 128), lambda i, *_: (i, 0)),
                  pl.BlockSpec((None, 1, 128), lambda i, *_: (i, 0, 0)),
                  pl.BlockSpec((None, 1, 128), lambda i, *_: (i, 0, 0))],
        out_specs=pl.BlockSpec(memory_space=pl.ANY),
        scratch_shapes=[pltpu.VMEM((2, MOE_LS, MOE_XW), BF16), pltpu.VMEM((MOE_PIECES[0], MOE_XW), BF16),
                        pltpu.SemaphoreType.DMA((2,)), pltpu.SemaphoreType.DMA(())])
    return pl.pallas_call(
        functools.partial(_dispatch_kernel, ntile=ntile),
        grid_spec=grid_spec,
        out_shape=jax.ShapeDtypeStruct((ntmax * MOE_TR, MOE_XW), BF16),
        compiler_params=_cparams(("arbitrary",)),
        name="moe_dispatch",
    )(dst.reshape(-1), lo.reshape(-1), n.reshape(-1), misc.reshape(-1), pieces.reshape(-1), h_all, comb_all,
      lof.reshape(PLAN_ROWS, 1, 128), n.astype(F32).reshape(PLAN_ROWS, 1, 128))


def _ffn_kernel(te_s, misc_s, xs_hbm, wg_ref, wu_ref, wd_ref, ys_ref, wgb, wub, wdb, xbuf, xsem):
    j = pl.program_id(0)
    nused = misc_s[2 * 128]

    def fetch(t, slot):
        return pltpu.make_async_copy(xs_hbm.at[pl.ds(pl.multiple_of(t * MOE_TR, MOE_TR), MOE_TR)],
                                     xbuf.at[slot], xsem.at[slot])

    @pl.when(j == 0)
    def _():
        for t in range(MOE_XBUF - 1):
            @pl.when(t < nused)
            def _(t=t):
                fetch(t, t).start()

    @pl.when(j < nused)
    def _():
        ahead = j + (MOE_XBUF - 1)

        @pl.when(ahead < nused)
        def _():
            fetch(ahead, lax.rem(ahead, MOE_XBUF)).start()

        @pl.when((j == 0) | (te_s[j] != te_s[jnp.maximum(j - 1, 0)]))
        def _():
            wgb[...] = wg_ref[...].astype(BF16)
            wub[...] = wu_ref[...].astype(BF16)
            wdb[...] = wd_ref[...].astype(BF16)

        slot = lax.rem(j, MOE_XBUF)
        fetch(j, slot).wait()
        xs_ref = xbuf.at[slot]
        x = xs_ref[:, :D]
        w = xs_ref[:, D:D + 1].astype(F32) + xs_ref[:, D + 64:D + 65].astype(F32)
        hw = _silu(_dot(x, wgb[...])) * _dot(x, wub[...]) * w
        ys_ref[...] = _dot(hw.astype(BF16), wdb[...]).astype(BF16)


def _moe_ffn(plan, xs, wg, wu, wd, *, layer, ntmax):
    te, misc = plan[3], plan[4]

    def tile(j, misc_s):
        return jnp.minimum(j, misc_s[2 * 128] - 1)

    wspec = lambda shape: pl.BlockSpec((None, None) + shape,
                                       lambda j, te_s, misc_s: (layer, te_s[tile(j, misc_s)], 0, 0))
    grid_spec = pltpu.PrefetchScalarGridSpec(
        num_scalar_prefetch=2, grid=(ntmax,),
        in_specs=[pl.BlockSpec(memory_space=pl.ANY), wspec((D, DE)), wspec((D, DE)), wspec((DE, D))],
        out_specs=pl.BlockSpec((MOE_TR, D), lambda j, te_s, misc_s: (tile(j, misc_s), 0)),
        scratch_shapes=[pltpu.VMEM((D, DE), BF16), pltpu.VMEM((D, DE), BF16), pltpu.VMEM((DE, D), BF16),
                        pltpu.VMEM((MOE_XBUF, MOE_TR, MOE_XW), BF16), pltpu.SemaphoreType.DMA((MOE_XBUF,))])
    return pl.pallas_call(
        _ffn_kernel, grid_spec=grid_spec,
        out_shape=jax.ShapeDtypeStruct((ntmax * MOE_TR, D), BF16),
        compiler_params=_cparams(("arbitrary",)),
        name="moe_ffn",
    )(te[0], misc.reshape(-1), xs, wg, wu, wd)


def _combine_kernel(dst_s, lo_s, n_s, pieces_s, h_ref, comb_ref, lof_ref, nf_ref, sg_ref, su_ref, sd_ref, ys_hbm,
                    o_ref, ybuf, sem, *, ntile):
    i = pl.program_id(0)
    slot = lax.rem(i, 2)

    def copy_in(sl, src_row, dst_row, size):
        return pltpu.make_async_copy(ys_hbm.at[pl.ds(pl.multiple_of(src_row, MOE_SEG), size)],
                                     ybuf.at[sl, pl.ds(pl.multiple_of(dst_row, MOE_SEG), size)], sem.at[sl])

    def start_tile(tile, sl):
        def start_seg(e, carry):
            idx = tile * 128 + e
            lo, src = lo_s[idx], dst_s[idx]
            _for_each_piece(n_s[idx], lambda off, size: copy_in(sl, src + off, lo + off, size).start())
            return carry

        lax.fori_loop(0, NE, start_seg, 0)

    @pl.when(i == 0)
    def _():
        ybuf[...] = jnp.zeros_like(ybuf)
        start_tile(0, 0)

    @pl.when(i + 1 < ntile)
    def _():
        start_tile(i + 1, 1 - slot)

    lo_row = lof_ref[...]
    hi_row = lo_row + nf_ref[...]
    lo_col, hi_col = _col_of_row(lo_row), _col_of_row(hi_row)
    rp_hi, rp_lo = _split(_rank_plus(comb_ref[...]))
    l_hi, l_lo = _split(jnp.broadcast_to(lo_row * (1.0 / MOE_SEG), (8, 128)))
    lhs_meta = jnp.concatenate([jnp.concatenate([rp_hi, rp_lo], axis=1), jnp.concatenate([l_hi, l_lo], axis=1)], axis=0)

    h = h_ref[...]
    hs = _silu(_dot(h, sg_ref[...].astype(BF16))) * _dot(h, su_ref[...].astype(BF16))
    acc = _dot(hs.astype(BF16), sd_ref[...].astype(BF16))

    _wait_pieces(pieces_s, i, lambda size: copy_in(slot, 0, 0, size))
    for ch in range(MOE_LS // MOE_CH):
        scol = (_iota((1, MOE_CH), 1) + ch * MOE_CH).astype(F32)
        e_m = jnp.where((scol >= lo_col) & (scol < hi_col), 1.0, 0.0).astype(BF16)
        meta = _dot(lhs_meta, jnp.concatenate([e_m, e_m], axis=0))
        target = scol + 1.0 - meta[MOE_TD:MOE_TD + 1, :] * MOE_SEG
        perm_t = jnp.where(meta[:MOE_TD] == target, 1.0, 0.0).astype(BF16)
        acc = acc + _dot(perm_t, ybuf[slot, pl.ds(ch * MOE_CH, MOE_CH), :])
    o_ref[...] = acc


def _moe_combine(plan, ys, h_all, comb_all, sg, su, sd, *, layer):
    dst, lo, n, _, _, lof, pieces = plan
    ntile = h_all.shape[0] // MOE_TD
    shared = lambda w: pl.BlockSpec((None,) + w.shape[1:], lambda i, *_: (layer, 0, 0))
    grid_spec = pltpu.PrefetchScalarGridSpec(
        num_scalar_prefetch=4, grid=(ntile,),
        in_specs=[pl.BlockSpec((MOE_TD, D), lambda i, *_: (i, 0)),
                  pl.BlockSpec((MOE_TD, 128), lambda i, *_: (i, 0)),
                  pl.BlockSpec((None, 1, 128), lambda i, *_: (i, 0, 0)),
                  pl.BlockSpec((None, 1, 128), lambda i, *_: (i, 0, 0)),
                  shared(sg), shared(su), shared(sd),
                  pl.BlockSpec(memory_space=pl.ANY)],
        out_specs=pl.BlockSpec((MOE_TD, D), lambda i, *_: (i, 0)),
        scratch_shapes=[pltpu.VMEM((2, MOE_LS, D), BF16), pltpu.SemaphoreType.DMA((2,))])
    return pl.pallas_call(
        functools.partial(_combine_kernel, ntile=ntile),
        grid_spec=grid_spec,
        out_shape=jax.ShapeDtypeStruct((h_all.shape[0], D), F32),
        compiler_params=_cparams(("arbitrary",)),
        name="moe_combine",
    )(dst.reshape(-1), lo.reshape(-1), n.reshape(-1), pieces.reshape(-1), h_all, comb_all,
      lof.reshape(PLAN_ROWS, 1, 128), n.astype(F32).reshape(PLAN_ROWS, 1, 128), sg, su, sd, ys)


def _moe(h_all, comb_all, wg, wu, wd, sg, su, sd, *, layer, n_tokens):
    ntile = h_all.shape[0] // MOE_TD
    max_rows = n_tokens * TOPK + ntile * NE * (MOE_SEG - 1) + NE * (MOE_TR - 1)
    ntmax = -(-max_rows // MOE_TR)
    plan = _moe_plan(comb_all, ntmax=ntmax)
    xs = _moe_dispatch(plan, h_all, comb_all, ntmax=ntmax)
    ys = _moe_ffn(plan, xs, wg, wu, wd, layer=layer, ntmax=ntmax)
    return _moe_combine(plan, ys, h_all, comb_all, sg, su, sd, layer=layer)


def _final_kernel(x_ref, f_ref, pmod_ref, png_ref, o_ref):
    o_ref[...] = _residual_in(x_ref, f_ref, pmod_ref, png_ref)


def _final(x, f_all, pmods, png, *, nb, seq_len, tm, row0):
    nt = seq_len // tm
    t0 = row0 // tm
    return pl.pallas_call(
        _final_kernel,
        grid=(nb, nt),
        in_specs=[pl.BlockSpec((tm, D), lambda b, t: (b * nt + t, 0)),
                  pl.BlockSpec((tm, D), lambda b, t: (t0 + b * nt + t, 0)),
                  pl.BlockSpec((None,) + pmods.shape[1:], lambda b, t: (b, 0, 0, 0)),
                  pl.BlockSpec(png.shape, lambda b, t: (0, 0))],
        out_specs=pl.BlockSpec((tm, D), lambda b, t: (b * nt + t, 0)),
        out_shape=jax.ShapeDtypeStruct(x.shape, F32),
        compiler_params=_cparams(("arbitrary", "arbitrary")),
        name="final_residual",
    )(x, f_all, pmods, png)


def kernel(x_prompt, x_sample, c_prompt, c_sample, state_conv, state_shift, state_wkv, w_mod, b_mod, norm_g, ab_w_in, ab_conv_w, ab_mu, ab_w0, ab_w_decay_up, ab_a0, ab_a_up, ab_g_up, ab_k_k, ab_k_a, ab_r_k, ab_gn_g, ab_gn_b, ab_w_out, gm_w_in, gm_b_in, gm_ln_g, gm_ln_b, gm_w_s, gm_b_s, gm_w_out, moe_w_router, moe_b_router, moe_w_gate, moe_w_up, moe_w_down, moe_ws_gate, moe_ws_up, moe_ws_down):
    bsz, seq_len, _ = x_prompt.shape
    nsamp = x_sample.shape[0]
    nheads = DB // HEAD
    n_prompt = bsz * seq_len
    n_all = -(-(n_prompt + nsamp) // MOE_TD) * MOE_TD
    tm = 512

    xp = x_prompt.reshape(n_prompt, D)
    xs = x_sample.reshape(nsamp, D)

    m = _modulation(jnp.concatenate([c_prompt, c_sample], axis=0), w_mod, b_mod)
    mods_p = [m[l, :bsz].reshape(bsz, 6, 1, D) for l in range(2)]
    mods_s = [m[l, bsz:].reshape(nsamp, 6, D).transpose(1, 0, 2)[None] for l in range(2)]

    row = lambda t: t.reshape(1, -1)
    pad_rows = lambda t, lo, hi: jnp.pad(t, ((lo, hi), (0, 0)))
    small = [ab_conv_w[0], row(ab_mu[0]), row(ab_w0[0]),
             pad_rows(ab_w_decay_up[0], 0, 64).astype(BF16), row(ab_a0[0]),
             pad_rows(ab_a_up[0], 64, 0).astype(BF16), ab_g_up[0].astype(BF16),
             row(ab_k_k[0]), row(ab_k_a[0])]
    win0 = ab_w_in[0].astype(BF16)
    wo0 = ab_w_out[0].astype(BF16)
    rk, gng, gnb = row(ab_r_k[0]), row(ab_gn_g[0]), row(ab_gn_b[0])
    wrt = [moe_w_router[l].T for l in range(2)]
    br = [moe_b_router[l].reshape(NE, 1) for l in range(2)]

    h2_all = jnp.zeros((n_all, D), BF16)
    comb_all = jnp.zeros((n_all, 128), F32)

    ya, r, lw, k, v, kk, a, g, ctail, stail = _ab_pre(
        xp, mods_p[0], norm_g[0], win0, small, None, nb=bsz, seq_len=seq_len, tm=tm, seq_mode=True)
    conv_p = ctail.reshape(bsz, 8, DA)[:, 6:8][None]
    shift_p = stail.reshape(bsz, 8, PB)[:, 7][None]
    yb, sfin = _wkv_seq(r, lw, k, v, kk, a, g, rk, gng, gnb, nb=bsz, seq_len=seq_len)
    sfin = sfin.reshape(bsz, nheads // 2, 2, HEAD, 2, HEAD)
    wkv_p = jnp.stack([sfin[:, :, 0, :, 0, :], sfin[:, :, 1, :, 1, :]], axis=2)
    wkv_p = wkv_p.reshape(bsz, nheads, HEAD, HEAD).transpose(0, 1, 3, 2)[None]
    xp1, h2_all, comb_all = _ab_post(ya, yb, xp, mods_p[0], norm_g[0], wo0, wrt[0], br[0], h2_all, comb_all,
                                     nb=bsz, seq_len=seq_len, tm=tm, row0=0)

    prev = [state_conv[0, :, 0], state_conv[0, :, 1], state_shift[0]]
    ya_s, r, lw, k, v, kk, a, g, gated_s, pb_s = _ab_pre(
        xs, mods_s[0], norm_g[0], win0, small, prev, nb=1, seq_len=nsamp, tm=nsamp, seq_mode=False)
    conv_s = jnp.stack([state_conv[0, :, 1], gated_s], axis=1)[None]
    shift_s = pb_s[None]
    vecs = [t.reshape(nsamp * nheads, 1, HEAD) for t in (r, lw, k, v, kk, a, g)]
    params = [t.reshape(nheads, 1, HEAD) for t in (ab_r_k[0], ab_gn_g[0], ab_gn_b[0])]
    yb_s, snew = _wkv_step(state_wkv[0].reshape(nsamp * nheads, HEAD, HEAD), vecs, params, nheads=nheads)
    wkv_s = snew.reshape(1, nsamp, nheads, HEAD, HEAD)
    xs1, h2_all, comb_all = _ab_post(ya_s, yb_s.reshape(nsamp, DB).astype(BF16), xs, mods_s[0], norm_g[0], wo0,
                                     wrt[0], br[0], h2_all, comb_all, nb=1, seq_len=nsamp, tm=nsamp, row0=n_prompt)

    def moe_layer(l, h_all, c_all):
        return _moe(h_all, c_all, moe_w_gate, moe_w_up, moe_w_down,
                    moe_ws_gate, moe_ws_up, moe_ws_down, layer=l, n_tokens=n_prompt + nsamp)

    f0 = moe_layer(0, h2_all, comb_all)

    gw = [gm_w_in[0].astype(BF16), row(gm_b_in[0]), row(gm_ln_g[0]), row(gm_ln_b[0])]
    gw_seq = gw + [gm_w_s[0], gm_b_s[0].T, gm_w_out[0].astype(BF16), wrt[1], br[1]]
    gw_step = gw + [row(jnp.repeat(gm_w_s[0, :, 0, 0], CHUNK)), row(jnp.repeat(gm_b_s[0, :, 0], CHUNK)),
                    gm_w_out[0].astype(BF16), wrt[1], br[1]]
    tm1 = 512
    xp2, h2_all, comb_all = _gmlp(xp1, f0, mods_p[0], norm_g[0], mods_p[1], norm_g[1], gw_seq, h2_all, comb_all,
                                  nb=bsz, seq_len=seq_len, tm=tm1, row0=0, seq_mode=True)
    xs2, h2_all, comb_all, chunk_v = _gmlp(xs1, f0, mods_s[0], norm_g[0], mods_s[1], norm_g[1], gw_step, h2_all,
                                           comb_all, nb=1, seq_len=nsamp, tm=nsamp, row0=n_prompt, seq_mode=False)
    f1 = moe_layer(1, h2_all, comb_all)
    y_p = _final(xp2, f1, mods_p[1], norm_g[1], nb=bsz, seq_len=seq_len, tm=tm, row0=0)
    y_s = _final(xs2, f1, mods_s[1], norm_g[1], nb=1, seq_len=nsamp, tm=nsamp, row0=n_prompt)

    return (y_p.reshape(bsz, seq_len, D), y_s.reshape(nsamp, 1, D), conv_p, conv_s, shift_p, shift_s,
            wkv_p, wkv_s, chunk_v.reshape(1, nsamp, 1, D))
```

```python
import functools

import jax
import jax.numpy as jnp
from jax import lax
from jax.experimental import pallas as pl
from jax.experimental.pallas import tpu as pltpu

F32 = jnp.float32
BF16 = jnp.bfloat16

D = 1024
DA = 512
DB = 512
HEAD = 64
PB = 1792
PAB = 3 * DA + PB
NE = 64
NG = 8
EPG = NE // NG
TOPG = 4
TOPK = 8
DE = 256
ROUTE_SCALE = 2.5
EPS = 1e-6
GN_EPS = 64e-5
CHUNK = 128
WKV_SLAB = 128
WKV_CHUNK = 32
WKV_PAIRS_PER_STEP = 4
MOE_TD = 512
MOE_SEG = 16
MOE_LS = 5120
MOE_CH = 1024
MOE_PIECES = (64, 16)
MOE_TR = 1024
MOE_XW = D + 128
MOE_XBUF = 3
PLAN_ROWS = 128

VMEM_LIMIT = 56 * 1024 * 1024


def _cparams(sem):
    return pltpu.CompilerParams(dimension_semantics=sem, vmem_limit_bytes=VMEM_LIMIT)


def _dot(a, b, precision=None):
    return jnp.dot(a, b, preferred_element_type=F32, precision=precision)


def _dot_nt(a, b, precision=None):
    return lax.dot_general(a, b, (((1,), (1,)), ((), ())), preferred_element_type=F32, precision=precision)


def _iota(shape, dim):
    return lax.broadcasted_iota(jnp.int32, shape, dim)


def _rms(x, g):
    return x * lax.rsqrt(jnp.mean(x * x, axis=-1, keepdims=True) + EPS) * g


def _silu(x):
    return x * jax.nn.sigmoid(x)


def _seg_ones(n, seg):
    return (_iota((n, n), 0) // seg == _iota((n, n), 1) // seg).astype(F32)


def _mod_kernel(c_ref, w_ref, b_ref, o_ref):
    s = _silu(c_ref[...]).astype(BF16)
    o_ref[...] = _dot(s, w_ref[...].astype(BF16)) + b_ref[...]


def _modulation(c_all, w_mod, b_mod):
    depth, _, width = w_mod.shape
    rows = c_all.shape[0]
    bn = 512
    return pl.pallas_call(
        _mod_kernel,
        grid=(depth, width // bn),
        in_specs=[pl.BlockSpec((rows, D), lambda l, j: (0, 0)),
                  pl.BlockSpec((None, D, bn), lambda l, j: (l, 0, j)),
                  pl.BlockSpec((None, 1, bn), lambda l, j: (l, 0, j))],
        out_specs=pl.BlockSpec((None, rows, bn), lambda l, j: (l, 0, j)),
        out_shape=jax.ShapeDtypeStruct((depth, rows, width), F32),
        compiler_params=_cparams(("arbitrary", "arbitrary")),
        name="modulation",
    )(c_all, w_mod, b_mod.reshape(depth, 1, width))


def _residual_in(x_ref, f_ref, pmod_ref, png_ref):
    x = x_ref[...]
    if f_ref is None:
        return x
    return x + pmod_ref[5] * _rms(f_ref[...], png_ref[3:4, :])


def _route(h, wrt_ref, br_ref):
    tm = h.shape[0]
    logits = _mm(wrt_ref[...], h, 2, 2, nt=True)
    scores = jax.nn.sigmoid(logits)
    sel = scores + br_ref[...]
    sub = _iota((EPG, tm), 0).astype(F32)
    neg = jnp.float32(-jnp.inf)
    blks, sblks, gscore = [], [], []
    for g in range(NG):
        blk = sel[g * EPG:(g + 1) * EPG, :]
        m1 = jnp.max(blk, axis=0, keepdims=True)
        first = jnp.min(jnp.where(blk == m1, sub, float(EPG)), axis=0, keepdims=True)
        m2 = jnp.max(jnp.where(sub == first, neg, blk), axis=0, keepdims=True)
        blks.append(blk)
        sblks.append(scores[g * EPG:(g + 1) * EPG, :])
        gscore.append(m1 + m2)
    masked = []
    for g in range(NG):
        cnt = jnp.zeros((1, tm), F32)
        for g2 in range(NG):
            if g2 == g:
                continue
            beats = (gscore[g2] >= gscore[g]) if g2 < g else (gscore[g2] > gscore[g])
            cnt = cnt + jnp.where(beats, 1.0, 0.0)
        masked.append(jnp.where(cnt < TOPG, blks[g], neg))
    eidx = [sub + float(g * EPG) for g in range(NG)]
    chosen = [jnp.zeros((EPG, tm), F32) for _ in range(NG)]
    for _ in range(TOPK):
        best = masked[0]
        for g in range(1, NG):
            best = jnp.maximum(best, masked[g])
        best = jnp.max(best, axis=0, keepdims=True)
        first = jnp.where(masked[0] == best, eidx[0], float(NE))
        for g in range(1, NG):
            first = jnp.minimum(first, jnp.where(masked[g] == best, eidx[g], float(NE)))
        first = jnp.min(first, axis=0, keepdims=True)
        for g in range(NG):
            hit = eidx[g] == first
            chosen[g] = jnp.where(hit, 1.0, chosen[g])
            masked[g] = jnp.where(hit, neg, masked[g])
    wts = [jnp.where(chosen[g] > 0.0, sblks[g], 0.0) for g in range(NG)]
    tot = wts[0]
    for g in range(1, NG):
        tot = tot + wts[g]
    denom = jnp.sum(tot, axis=0, keepdims=True)
    comb_t = jnp.concatenate([w / denom * ROUTE_SCALE for w in wts] + [jnp.zeros((NE, tm), F32)], axis=0)
    return comb_t.T


def _post(y, x, mod_ref, ng_ref, wrt_ref, br_ref, xo_ref, h2_ref, comb_ref):
    xn = x + mod_ref[2] * _rms(y, ng_ref[1:2, :])
    xo_ref[...] = xn
    h2 = _rms(xn, ng_ref[2:3, :]) * (1.0 + mod_ref[4]) + mod_ref[3]
    h2_ref[...] = h2.astype(BF16)
    comb_ref[...] = _route(h2, wrt_ref, br_ref)


def _ab_pre_kernel(*refs, seq_mode, tm):
    it = iter(refs)
    x_ref, mod_ref, ng_ref, win_ref = next(it), next(it), next(it), next(it)
    cw_ref, mu_ref, w0_ref, wd_ref, a0_ref, wa_ref, wg_ref, kk_ref, ka_ref = (next(it) for _ in range(9))
    if not seq_mode:
        p2_ref, p1_ref, ps_ref = next(it), next(it), next(it)
    ya_ref, r_ref, lw_ref, k_ref, v_ref, kkn_ref, a_ref, g_ref, ctail_ref, stail_ref = (next(it) for _ in range(10))
    if seq_mode:
        cc_ref, sc_ref = next(it), next(it)

        @pl.when(pl.program_id(1) == 0)
        def _():
            cc_ref[...] = jnp.zeros_like(cc_ref)
            sc_ref[...] = jnp.zeros_like(sc_ref)

    x = x_ref[...]
    h = _rms(x, ng_ref[0:1, :]) * (1.0 + mod_ref[1]) + mod_ref[0]
    proj = _dot(h.astype(BF16), win_ref[...])
    a_h, a_b, a_c = proj[:, 0:DA], proj[:, DA:2 * DA], proj[:, 2 * DA:3 * DA]
    pb = proj[:, 3 * DA:]
    gated = a_c * a_h
    if seq_mode:
        rows = _iota((tm, 1), 0)
        g1 = jnp.where(rows == 0, cc_ref[7:8, :], pltpu.roll(gated, 1, 0))
        g2 = jnp.where(rows == 0, cc_ref[6:7, :], jnp.where(rows == 1, cc_ref[7:8, :], pltpu.roll(gated, 2, 0)))
        pprev = jnp.where(rows == 0, sc_ref[7:8, :], pltpu.roll(pb, 1, 0))
        cc_ref[...] = gated[tm - 8:, :]
        sc_ref[...] = pb[tm - 8:, :]
        ctail_ref[...] = gated[tm - 8:, :]
        stail_ref[...] = pb[tm - 8:, :]
    else:
        g1, g2, pprev = p1_ref[...], p2_ref[...], ps_ref[...]
        ctail_ref[...] = gated
        stail_ref[...] = pb
    conv = g2 * cw_ref[0:1, :] + g1 * cw_ref[1:2, :] + gated * cw_ref[2:3, :]
    ya_ref[...] = (a_b * conv).astype(BF16)

    xm = pb + (pprev - pb) * mu_ref[...]
    r, k, v = xm[:, 0:DB], xm[:, DB:2 * DB], xm[:, 2 * DB:3 * DB]
    lowrank = xm[:, 3 * DB:3 * DB + 128]
    dg = xm[:, 3 * DB + 128:]
    w_pre = w0_ref[...] + _dot(jnp.tanh(lowrank).astype(BF16), wd_ref[...])
    z = -w_pre
    softplus = jnp.maximum(z, 0.0) + jnp.log(1.0 + jnp.exp(-jnp.abs(z)))
    lw_ref[...] = -jnp.exp(-softplus - 0.5)
    a = jax.nn.sigmoid(a0_ref[...] + _dot(lowrank.astype(BF16), wa_ref[...]))
    g_ref[...] = _dot(jax.nn.sigmoid(dg).astype(BF16), wg_ref[...])
    kk = k * kk_ref[...]
    ss = _mm(kk * kk, _seg_ones(DB, HEAD), 2, 1)
    kkn_ref[...] = kk / jnp.maximum(jnp.sqrt(ss), 1e-12)
    r_ref[...] = r
    k_ref[...] = k * (1.0 + (a - 1.0) * ka_ref[...])
    v_ref[...] = v
    a_ref[...] = a


def _ab_pre(x, mods, ng, win, small, prev, *, nb, seq_len, tm, seq_mode):
    nt = seq_len // tm
    rows = nb * seq_len
    tail = 8 if seq_mode else tm
    full = lambda shape: pl.BlockSpec(shape, lambda b, t: tuple(0 for _ in shape))
    tok = lambda w: pl.BlockSpec((tm, w), lambda b, t: (b * nt + t, 0))
    in_specs = [tok(D),
                pl.BlockSpec((None,) + mods.shape[1:], lambda b, t: (b, 0, 0, 0)),
                full(ng.shape), full(win.shape)] + [full(s.shape) for s in small]
    args = [x, mods, ng, win] + list(small)
    if not seq_mode:
        in_specs += [tok(DA), tok(DA), tok(PB)]
        args += list(prev)
    out_shape = ([jax.ShapeDtypeStruct((rows, DA), BF16)] + [jax.ShapeDtypeStruct((rows, DB), F32)] * 7
                 + [jax.ShapeDtypeStruct((nb * tail, DA), F32), jax.ShapeDtypeStruct((nb * tail, PB), F32)])
    out_specs = ([tok(DA)] + [tok(DB)] * 7
                 + [pl.BlockSpec((tail, DA), lambda b, t: (b, 0)), pl.BlockSpec((tail, PB), lambda b, t: (b, 0))])
    scratch = [pltpu.VMEM((8, DA), F32), pltpu.VMEM((8, PB), F32)] if seq_mode else []
    return pl.pallas_call(
        functools.partial(_ab_pre_kernel, seq_mode=seq_mode, tm=tm),
        grid=(nb, nt), in_specs=in_specs, out_specs=out_specs, out_shape=out_shape,
        scratch_shapes=scratch,
        compiler_params=_cparams(("arbitrary", "arbitrary")),
        name="ab_pre_seq" if seq_mode else "ab_pre_step",
    )(*args)


def _split(x):
    hi = x.astype(BF16)
    return hi, (x - hi.astype(F32)).astype(BF16)


def _mm(a, b, pa=1, pb=1, nt=False):
    ah, al = _split(a) if pa == 2 else (a.astype(BF16), None)
    bh, bl = _split(b) if pb == 2 else (b.astype(BF16), None)
    bx = 1 if nt else 0
    if pa == 2 and pb == 2:
        ah, bh = jnp.concatenate([ah, al, ah], axis=1), jnp.concatenate([bh, bh, bl], axis=bx)
    elif pa == 2:
        ah, bh = jnp.concatenate([ah, al], axis=1), jnp.concatenate([bh, bh], axis=bx)
    elif pb == 2:
        ah, bh = jnp.concatenate([ah, ah], axis=1), jnp.concatenate([bh, bl], axis=bx)
    return _dot_nt(ah, bh) if nt else _dot(ah, bh)


def _gn_gate(o, r, k, v, g, rk, gng, gnb, seg):
    mean = _mm(o, seg, 2, 1) * (1.0 / HEAD)
    d = o - mean
    var = _mm(d * d, seg, 2, 1) * (1.0 / HEAD)
    out = d * lax.rsqrt(var + GN_EPS) * gng + gnb
    out = out + _mm(r * k * rk, seg, 2, 1) * v
    return out * g


def _wkv_pair(r, lw, k, v, kk, a, st, cst):
    ts, c = WKV_SLAB, WKV_CHUNK
    nchunk = ts // c
    strict, incl, cum_lhs, m0, m1, headdiag, eye, eye2, blks, chunk_cols = cst

    l1 = lw.astype(BF16)
    rem = lw - l1.astype(F32)
    l2 = rem.astype(BF16)
    l3 = (rem - l2.astype(F32)).astype(BF16)
    cums = _dot(cum_lhs, jnp.concatenate([l1, l2, l3], axis=0))
    yield
    cl, ctot = cums[:ts], cums[ts:]
    e_neg = jnp.exp(-cl)
    e_end = jnp.exp(ctot - cl)
    at = -kk * jnp.exp(cl - lw)
    rt = r * jnp.exp(cl)
    beta = kk * a
    bk = jnp.concatenate([beta * e_neg, k * e_neg], axis=0)
    bend, kend = beta * e_end, k * e_end

    at_st = jnp.concatenate([at * m0, at * m1], axis=0)
    a_all = _mm(jnp.concatenate([at_st, rt * m0, rt * m1], axis=0), bk, nt=True)
    yield
    a_st, a_out = a_all[:2 * ts], a_all[2 * ts:]
    zero = jnp.zeros((ts, ts), F32)

    def two_heads(p0, p1):
        return jnp.concatenate([jnp.concatenate([p0, zero], axis=1), jnp.concatenate([zero, p1], axis=1)], axis=0)

    lbd = two_heads(a_st[:ts, :ts] * strict, a_st[ts:, :ts] * strict)
    d1 = jnp.where(blks[0], lbd, 0.0)
    tinv = eye2 + d1
    d2 = _mm(d1, d1)
    yield
    tinv = tinv + _mm(tinv, d2)
    d4 = _mm(d2, d2)
    yield
    tinv = tinv + _mm(tinv, d4)
    yield
    for lvl in range(1, len(blks)):
        off = jnp.where(blks[lvl] & jnp.logical_not(blks[lvl - 1]), lbd, 0.0)
        half = _mm(tinv, off)
        yield
        tinv = tinv + _mm(half, tinv)
        yield

    v_st = jnp.concatenate([v * m0, v * m1], axis=0)
    y_st = _mm(two_heads(a_st[:ts, ts:] * strict, a_st[ts:, ts:] * strict), v_st)
    yield
    wu = _mm(tinv, jnp.concatenate([at_st, y_st], axis=1))
    yield
    w = wu[:ts, :2 * HEAD] + wu[ts:, :2 * HEAD]
    u = wu[:ts, 2 * HEAD:] + wu[ts:, 2 * HEAD:]
    rhs = jnp.concatenate([jnp.concatenate([w, u], axis=1),
                           jnp.concatenate([jnp.zeros_like(v), v], axis=1)], axis=0)
    bend_t, kend_t = bend.T, kend.T
    lhs = [jnp.concatenate([bend_t * cm, kend_t * cm], axis=1) for cm in chunk_cols]
    lhs.append(jnp.concatenate([a_out[:ts, :ts] * incl, a_out[:ts, ts:] * incl], axis=1))
    lhs.append(jnp.concatenate([a_out[ts:, :ts] * incl, a_out[ts:, ts:] * incl], axis=1))
    big = _mm(jnp.concatenate(lhs, axis=0), rhs)
    yield
    mn_all, qo = big[:nchunk * ts], big[nchunk * ts:]
    q = rt + qo[:ts, :2 * HEAD] * m0 + qo[ts:, :2 * HEAD] * m1
    olocal = qo[:ts, 2 * HEAD:] * m0 + qo[ts:, 2 * HEAD:] * m1

    outs = []
    for ch in range(nchunk):
        mn = mn_all[ch * ts:(ch + 1) * ts]
        trans = eye * jnp.exp(ctot[ch * c:ch * c + 1, :]) + mn[:, :2 * HEAD] * headdiag
        both = _mm(jnp.concatenate([q[ch * c:(ch + 1) * c], trans], axis=0), st)
        yield
        outs.append(both[:c] + olocal[ch * c:(ch + 1) * c])
        st = both[c:] + mn[:, 2 * HEAD:] * headdiag
    return jnp.concatenate(outs, axis=0), st


def _round_robin(gens):
    results = [None] * len(gens)
    active = list(enumerate(gens))
    while active:
        still = []
        for i, gen in active:
            try:
                next(gen)
                still.append((i, gen))
            except StopIteration as stop:
                results[i] = stop.value
        active = still
    return results


def _wkv_seq_kernel(r_ref, lw_ref, k_ref, v_ref, kk_ref, a_ref, g_ref, rk_ref, gng_ref, gnb_ref,
                    y_ref, sfin_ref, st_ref, *, npp):
    ts, c = WKV_SLAB, WKV_CHUNK

    @pl.when(pl.program_id(2) == 0)
    def _():
        st_ref[...] = jnp.zeros_like(st_ref)

    ri, ci = _iota((ts, ts), 0), _iota((ts, ts), 1)
    same = ri // c == ci // c
    strict = (same & (ci < ri)).astype(F32)
    incl = (same & (ci <= ri)).astype(F32)
    cum_lhs = jnp.concatenate([incl, same.astype(F32)], axis=0).astype(BF16)
    cum_lhs = jnp.concatenate([cum_lhs] * 3, axis=1)
    lane = _iota((1, 2 * HEAD), 1)
    m0 = (lane < HEAD).astype(F32)
    m1 = 1.0 - m0
    headdiag = _seg_ones(2 * HEAD, HEAD)
    eye = (_iota((2 * HEAD, 2 * HEAD), 0) == _iota((2 * HEAD, 2 * HEAD), 1)).astype(F32)
    r2, c2 = _iota((2 * ts, 2 * ts), 0), _iota((2 * ts, 2 * ts), 1)
    eye2 = (r2 == c2).astype(F32)
    blks, b = [], 8
    while b <= c:
        blks.append(r2 // b == c2 // b)
        b *= 2
    tcol = _iota((1, ts), 1) // c
    chunk_cols = [(tcol == ch).astype(F32) for ch in range(ts // c)]
    cst = (strict, incl, cum_lhs, m0, m1, headdiag, eye, eye2, blks, chunk_cols)

    sls = [slice(p * 2 * HEAD, (p + 1) * 2 * HEAD) for p in range(npp)]
    res = _round_robin([_wkv_pair(r_ref[:, sl], lw_ref[:, sl], k_ref[:, sl], v_ref[:, sl], kk_ref[:, sl],
                                  a_ref[:, sl], st_ref[p], cst) for p, sl in enumerate(sls)])
    for p, sl in enumerate(sls):
        o, st = res[p]
        st_ref[p] = st
        y_ref[:, sl] = _gn_gate(o, r_ref[:, sl], k_ref[:, sl], v_ref[:, sl], g_ref[:, sl], rk_ref[:, sl],
                                gng_ref[:, sl], gnb_ref[:, sl], headdiag).astype(BF16)

    @pl.when(pl.program_id(2) == pl.num_programs(2) - 1)
    def _():
        sfin_ref[...] = st_ref[...]


def _wkv_seq(r, lw, k, v, kk, a, g, rk, gng, gnb, *, nb, seq_len):
    ns = seq_len // WKV_SLAB
    npp = WKV_PAIRS_PER_STEP
    width = 2 * HEAD * npp
    tok = pl.BlockSpec((WKV_SLAB, width), lambda b, p, s: (b * ns + s, p))
    par = pl.BlockSpec((1, width), lambda b, p, s: (0, p))
    return pl.pallas_call(
        functools.partial(_wkv_seq_kernel, npp=npp),
        grid=(nb, DB // width, ns),
        in_specs=[tok] * 7 + [par] * 3,
        out_specs=[tok, pl.BlockSpec((None, npp, 2 * HEAD, 2 * HEAD), lambda b, p, s: (b, p, 0, 0))],
        out_shape=[jax.ShapeDtypeStruct((nb * seq_len, DB), BF16),
                   jax.ShapeDtypeStruct((nb, DB // (2 * HEAD), 2 * HEAD, 2 * HEAD), F32)],
        scratch_shapes=[pltpu.VMEM((npp, 2 * HEAD, 2 * HEAD), F32)],
        compiler_params=_cparams(("arbitrary", "arbitrary", "arbitrary")),
        name="wkv_seq",
    )(r, lw, k, v, kk, a, g, rk, gng, gnb)


def _wkv_step_kernel(s_ref, r_ref, lw_ref, k_ref, v_ref, kk_ref, a_ref, g_ref, rk_ref, gng_ref, gnb_ref,
                     y_ref, so_ref):
    s = s_ref[...]
    r, k, v, kk, a = r_ref[...], k_ref[...], v_ref[...], kk_ref[...], a_ref[...]
    w = jnp.exp(lw_ref[...])
    eye = (_iota((1, HEAD, HEAD), 1) == _iota((1, HEAD, HEAD), 2)).astype(F32)
    v_col = jnp.sum(eye * v, axis=2, keepdims=True)
    s_kk = jnp.sum(s * kk, axis=2, keepdims=True)
    s = s * w - s_kk * (kk * a) + v_col * k
    so_ref[...] = s
    o_col = jnp.sum(s * r, axis=2, keepdims=True)
    o = jnp.sum(eye * o_col, axis=1, keepdims=True)
    mean = jnp.mean(o, axis=2, keepdims=True)
    d = o - mean
    var = jnp.mean(d * d, axis=2, keepdims=True)
    out = d * lax.rsqrt(var + GN_EPS) * gng_ref[...] + gnb_ref[...]
    out = out + jnp.sum(r * k * rk_ref[...], axis=2, keepdims=True) * v
    y_ref[...] = out * g_ref[...]


def _wkv_step(state, vecs, params, *, nheads):
    n = state.shape[0]
    tb = 8 * nheads
    sspec = pl.BlockSpec((tb, HEAD, HEAD), lambda i: (i, 0, 0))
    vspec = pl.BlockSpec((tb, 1, HEAD), lambda i: (i, 0, 0))
    pspec = pl.BlockSpec((tb, 1, HEAD), lambda i: (0, 0, 0))
    reps = tb // nheads
    params = [jnp.tile(p, (reps, 1, 1)) for p in params]
    return pl.pallas_call(
        _wkv_step_kernel,
        grid=(n // tb,),
        in_specs=[sspec] + [vspec] * 7 + [pspec] * 3,
        out_specs=[vspec, sspec],
        out_shape=[jax.ShapeDtypeStruct((n, 1, HEAD), F32), jax.ShapeDtypeStruct((n, HEAD, HEAD), F32)],
        compiler_params=_cparams(("arbitrary",)),
        name="wkv_step",
    )(state, *vecs, *params)


def _ab_post_kernel(ya_ref, yb_ref, x_ref, mod_ref, ng_ref, wo_ref, wrt_ref, br_ref,
                    h2in_ref, combin_ref, xo_ref, h2_ref, comb_ref):
    del h2in_ref, combin_ref
    y = _dot(ya_ref[...], wo_ref[0:DA, :]) + _dot(yb_ref[...], wo_ref[DA:, :])
    _post(y, x_ref[...], mod_ref, ng_ref, wrt_ref, br_ref, xo_ref, h2_ref, comb_ref)


def _ab_post(ya, yb, x, mods, ng, wo, wrt, br, h2_all, comb_all, *, nb, seq_len, tm, row0):
    nt = seq_len // tm
    t0 = row0 // tm
    full = lambda shape: pl.BlockSpec(shape, lambda b, t: tuple(0 for _ in shape))
    tok = lambda w: pl.BlockSpec((tm, w), lambda b, t: (b * nt + t, 0))
    tok_all = lambda w: pl.BlockSpec((tm, w), lambda b, t: (t0 + b * nt + t, 0))
    anyspec = pl.BlockSpec(memory_space=pl.ANY)
    return pl.pallas_call(
        _ab_post_kernel,
        grid=(nb, nt),
        in_specs=[tok(DA), tok(DB), tok(D),
                  pl.BlockSpec((None,) + mods.shape[1:], lambda b, t: (b, 0, 0, 0)),
                  full(ng.shape), full(wo.shape), full(wrt.shape), full(br.shape), anyspec, anyspec],
        out_specs=[tok(D), tok_all(D), tok_all(128)],
        out_shape=[jax.ShapeDtypeStruct(x.shape, F32), jax.ShapeDtypeStruct(h2_all.shape, BF16),
                   jax.ShapeDtypeStruct(comb_all.shape, F32)],
        input_output_aliases={8: 1, 9: 2},
        compiler_params=_cparams(("arbitrary", "arbitrary")),
        name="ab_post",
    )(ya, yb, x, mods, ng, wo, wrt, br, h2_all, comb_all)


def _gmlp_kernel(*refs, seq_mode, tm):
    it = iter(refs)
    x_ref, f_ref, pmod_ref, png_ref, mod_ref, ng_ref = (next(it) for _ in range(6))
    win_ref, bin_ref, lng_ref, lnb_ref, ws_ref, bs_ref, wo_ref, wrt_ref, br_ref = (next(it) for _ in range(9))
    next(it), next(it)
    xo_ref, h2_ref, comb_ref = next(it), next(it), next(it)
    if not seq_mode:
        cv_ref = next(it)
    x = _residual_in(x_ref, f_ref, pmod_ref, png_ref)
    h = _rms(x, ng_ref[0:1, :]) * (1.0 + mod_ref[1]) + mod_ref[0]
    z = jax.nn.gelu(_dot(h.astype(BF16), win_ref[...]) + bin_ref[...])
    u, v = z[:, :D], z[:, D:]
    mu = jnp.mean(v, axis=-1, keepdims=True)
    var = jnp.mean(jnp.square(v - mu), axis=-1, keepdims=True)
    v = (v - mu) * lax.rsqrt(var + EPS) * lng_ref[...] + lnb_ref[...]
    if seq_mode:
        vb = v.astype(BF16)
        causal = _iota((CHUNK, CHUNK), 1) <= _iota((CHUNK, CHUNK), 0)
        cols = []
        for g in range(D // CHUNK):
            wsg = jnp.where(causal, ws_ref[g], 0.0).astype(BF16)
            bsg = bs_ref[:, g:g + 1]
            rows = [_dot(wsg, vb[c * CHUNK:(c + 1) * CHUNK, g * CHUNK:(g + 1) * CHUNK]) + bsg
                    for c in range(tm // CHUNK)]
            cols.append(jnp.concatenate(rows, axis=0) if len(rows) > 1 else rows[0])
        s = jnp.concatenate(cols, axis=1)
    else:
        cv_ref[...] = v
        s = v * ws_ref[...] + bs_ref[...]
    y = _dot((u * s).astype(BF16), wo_ref[...])
    _post(y, x, mod_ref, ng_ref, wrt_ref, br_ref, xo_ref, h2_ref, comb_ref)


def _gmlp(x, f_all, pmods, png, mods, ng, weights, h2_all, comb_all, *, nb, seq_len, tm, row0, seq_mode):
    nt = seq_len // tm
    t0 = row0 // tm
    full = lambda shape: pl.BlockSpec(shape, lambda b, t: tuple(0 for _ in shape))
    tok = lambda w: pl.BlockSpec((tm, w), lambda b, t: (b * nt + t, 0))
    tok_all = lambda w: pl.BlockSpec((tm, w), lambda b, t: (t0 + b * nt + t, 0))
    modspec = lambda m: pl.BlockSpec((None,) + m.shape[1:], lambda b, t: (b, 0, 0, 0))
    anyspec = pl.BlockSpec(memory_space=pl.ANY)
    in_specs = ([tok(D), tok_all(D), modspec(pmods), full(png.shape), modspec(mods), full(ng.shape)]
                + [full(w.shape) for w in weights] + [anyspec, anyspec])
    out_specs = [tok(D), tok_all(D), tok_all(128)]
    out_shape = [jax.ShapeDtypeStruct(x.shape, F32), jax.ShapeDtypeStruct(h2_all.shape, BF16),
                 jax.ShapeDtypeStruct(comb_all.shape, F32)]
    if not seq_mode:
        out_specs.append(tok(D))
        out_shape.append(jax.ShapeDtypeStruct(x.shape, F32))
    n_in = len(in_specs)
    return pl.pallas_call(
        functools.partial(_gmlp_kernel, seq_mode=seq_mode, tm=tm),
        grid=(nb, nt), in_specs=in_specs, out_specs=out_specs, out_shape=out_shape,
        input_output_aliases={n_in - 2: 1, n_in - 1: 2},
        compiler_params=_cparams(("arbitrary", "arbitrary")),
        name="gmlp_seq" if seq_mode else "gmlp_step",
    )(x, f_all, pmods, png, mods, ng, *weights, h2_all, comb_all)


def _strict_lower(n):
    return _iota((n, n), 1) < _iota((n, n), 0)


def _col_of_row(row):
    return jnp.broadcast_to(row, (128, 128)).T[:, 0:1]


def _plan_kernel(comb_ref, dst_ref, lo_ref, n_ref, te_ref, misc_ref, lof_ref, pieces_ref, cnt_ref, *, ntile, te_cols):
    i = pl.program_id(0)

    @pl.when(i == 0)
    def _():
        cnt_ref[...] = jnp.zeros_like(cnt_ref)

    sel = jnp.where(comb_ref[...] > 0.0, 1.0, 0.0)
    cnt_ref[pl.ds(i, 1), :] = jnp.sum(sel, axis=0, keepdims=True)

    @pl.when(i == ntile - 1)
    def _():
        c = cnt_ref[...]
        cpad = jnp.floor((c + (MOE_SEG - 1.0)) * (1.0 / MOE_SEG)) * MOE_SEG
        cb = cpad.astype(BF16)
        lower = jnp.where(_strict_lower(PLAN_ROWS), 1.0, 0.0).astype(BF16)
        upper = jnp.where(_iota((128, 128), 0) < _iota((128, 128), 1), 1.0, 0.0).astype(BF16)
        before = _dot(lower, cb)
        lo = _dot(cb, upper)
        tot = jnp.sum(cpad, axis=0, keepdims=True)
        rt = jnp.floor((tot + (MOE_TR - 1.0)) * (1.0 / MOE_TR))
        gt = _dot(jnp.broadcast_to(rt, (8, 128)).astype(BF16), upper)[0:1, :]
        dst_ref[...] = (gt * MOE_TR + before).astype(jnp.int32)
        lo_ref[...] = lo.astype(jnp.int32)
        lof_ref[...] = lo
        n_ref[...] = cpad.astype(jnp.int32)
        ends_col = _col_of_row(gt + rt)
        jrow = _iota((1, te_cols), 1).astype(F32)
        te = jnp.sum(jnp.where(ends_col <= jrow, 1.0, 0.0), axis=0, keepdims=True)
        te_ref[...] = jnp.broadcast_to(jnp.minimum(te, NE - 1.0), (8, te_cols)).astype(jnp.int32)
        nused = jnp.sum(rt, axis=1, keepdims=True)
        rowi = _iota((8, 128), 0)
        misc = jnp.where(rowi == 0, gt * MOE_TR + tot, jnp.where(rowi == 1, rt * MOE_TR - tot, nused))
        misc_ref[...] = misc.astype(jnp.int32)
        lane = _iota((1, 128), 1)
        pieces = jnp.zeros((PLAN_ROWS, 128), F32)
        taken = jnp.zeros_like(cpad)
        for k, size in enumerate(MOE_PIECES):
            cnt = jnp.floor((cpad - taken) * (1.0 / size))
            taken = taken + cnt * size
            pieces = pieces + jnp.where(lane == k, jnp.sum(cnt, axis=1, keepdims=True), 0.0)
        pieces_ref[...] = pieces.astype(jnp.int32)


def _moe_plan(comb_all, *, ntmax):
    ntile = comb_all.shape[0] // MOE_TD
    te_cols = -(-ntmax // 128) * 128
    full = lambda shape: pl.BlockSpec(shape, lambda i: tuple(0 for _ in shape))
    shapes = [((PLAN_ROWS, 128), jnp.int32)] * 3 + [((8, te_cols), jnp.int32), ((8, 128), jnp.int32),
                                                   ((PLAN_ROWS, 128), F32), ((PLAN_ROWS, 128), jnp.int32)]
    return pl.pallas_call(
        functools.partial(_plan_kernel, ntile=ntile, te_cols=te_cols),
        grid=(ntile,),
        in_specs=[pl.BlockSpec((MOE_TD, 128), lambda i: (i, 0))],
        out_specs=[full(s) for s, _ in shapes],
        out_shape=[jax.ShapeDtypeStruct(s, d) for s, d in shapes],
        scratch_shapes=[pltpu.VMEM((PLAN_ROWS, 128), F32)],
        compiler_params=_cparams(("arbitrary",)),
        name="moe_plan",
    )(comb_all)


def _rank_plus(comb):
    sel = jnp.where(comb > 0.0, 1.0, 0.0)
    lower = jnp.where(_strict_lower(MOE_TD), 1.0, 0.0).astype(BF16)
    return sel * (_dot(lower, sel.astype(BF16)) + 1.0)


def _for_each_piece(n, fn):
    off = 0
    for size in MOE_PIECES:
        cnt = lax.shift_right_logical(n - off, size.bit_length() - 1)

        def body(p, carry, off=off, size=size):
            fn(off + p * size, size)
            return carry

        lax.fori_loop(0, cnt, body, 0)
        off = off + cnt * size


def _wait_pieces(pieces_s, tile, make_copy):
    for k, size in enumerate(MOE_PIECES):
        def wait_one(p, carry, size=size):
            make_copy(size).wait()
            return carry

        lax.fori_loop(0, pieces_s[tile * 128 + k], wait_one, 0)


def _dispatch_kernel(dst_s, lo_s, n_s, misc_s, pieces_s, h_ref, comb_ref, lof_ref, nf_ref, xs_hbm, buf, zbuf, sem,
                     tsem, *, ntile):
    i = pl.program_id(0)
    slot = lax.rem(i, 2)
    comb = comb_ref[...]
    lo_row = lof_ref[...]
    hi_row = lo_row + nf_ref[...]
    rp_hi, rp_lo = _split(_rank_plus(comb).T)
    rhs_rank = jnp.concatenate([rp_hi, rp_lo], axis=0)
    c_hi, c_lo = _split(comb)
    h_ext = jnp.concatenate([h_ref[...], c_hi, c_lo], axis=1)
    lane_lo = _iota((1, 128), 1) < 64
    for ch in range(MOE_LS // MOE_CH):
        srow = (_iota((MOE_CH, 1), 0) + ch * MOE_CH).astype(F32)
        e_t = jnp.where((srow >= lo_row) & (srow < hi_row), 1.0, 0.0)
        e_tb = e_t.astype(BF16)
        rank_of = _dot(jnp.concatenate([e_tb, e_tb], axis=1), rhs_rank)
        target = srow + 1.0 - jnp.sum(e_t * lo_row, axis=1, keepdims=True)
        perm = jnp.where(rank_of == target, 1.0, 0.0).astype(BF16)
        rows = pl.ds(ch * MOE_CH, MOE_CH)
        sorted_rows = _dot(perm, h_ext)
        buf[slot, rows, :D] = sorted_rows[:, :D].astype(BF16)
        w = jnp.sum((sorted_rows[:, D:D + 128] + sorted_rows[:, D + 128:]) * e_t, axis=1, keepdims=True)
        w_hi = w.astype(BF16).astype(F32)
        buf[slot, rows, D:] = jnp.where(lane_lo, w_hi, w - w_hi).astype(BF16)

    def copy_out(sl, src_row, dst_row, size):
        return pltpu.make_async_copy(buf.at[sl, pl.ds(pl.multiple_of(src_row, MOE_SEG), size)],
                                     xs_hbm.at[pl.ds(pl.multiple_of(dst_row, MOE_SEG), size)], sem.at[sl])

    def start_seg(e, carry):
        idx = i * 128 + e
        lo, dst = lo_s[idx], dst_s[idx]
        _for_each_piece(n_s[idx], lambda off, size: copy_out(slot, lo + off, dst + off, size).start())
        return carry

    lax.fori_loop(0, NE, start_seg, 0)

    @pl.when(i > 0)
    def _():
        _wait_pieces(pieces_s, i - 1, lambda size: copy_out(1 - slot, 0, 0, size))

    @pl.when(i == ntile - 1)
    def _():
        _wait_pieces(pieces_s, i, lambda size: copy_out(slot, 0, 0, size))
        zbuf[...] = jnp.zeros_like(zbuf)

        def tail_copy(e, off, size):
            return pltpu.make_async_copy(
                zbuf.at[pl.ds(0, size)], xs_hbm.at[pl.ds(pl.multiple_of(misc_s[e] + off, MOE_SEG), size)], tsem)

        def tail_start(e, carry):
            _for_each_piece(misc_s[128 + e], lambda off, size: tail_copy(e, off, size).start())
            return carry

        def tail_wait(e, carry):
            _for_each_piece(misc_s[128 + e], lambda off, size: tail_copy(e, off, size).wait())
            return carry

        lax.fori_loop(0, NE, tail_start, 0)
        lax.fori_loop(0, NE, tail_wait, 0)


def _moe_dispatch(plan, h_all, comb_all, *, ntmax):
    dst, lo, n, _, misc, lof, pieces = plan
    ntile = h_all.shape[0] // MOE_TD
    grid_spec = pltpu.PrefetchScalarGridSpec(
        num_scalar_prefetch=5, grid=(ntile,),
        in_specs=[pl.BlockSpec((MOE_TD, D), lambda i, *_: (i, 0)),
                  pl.BlockSpec((MOE_TD, 128), lambda i, *_: (i, 0)),
                  pl.BlockSpec((None, 1, 128), lambda i, *_: (i, 0, 0)),
                  pl.BlockSpec((None, 1, 128), lambda i, *_: (i, 0, 0))],
        out_specs=pl.BlockSpec(memory_space=pl.ANY),
        scratch_shapes=[pltpu.VMEM((2, MOE_LS, MOE_XW), BF16), pltpu.VMEM((MOE_PIECES[0], MOE_XW), BF16),
                        pltpu.SemaphoreType.DMA((2,)), pltpu.SemaphoreType.DMA(())])
    return pl.pallas_call(
        functools.partial(_dispatch_kernel, ntile=ntile),
        grid_spec=grid_spec,
        out_shape=jax.ShapeDtypeStruct((ntmax * MOE_TR, MOE_XW), BF16),
        compiler_params=_cparams(("arbitrary",)),
        name="moe_dispatch",
    )(dst.reshape(-1), lo.reshape(-1), n.reshape(-1), misc.reshape(-1), pieces.reshape(-1), h_all, comb_all,
      lof.reshape(PLAN_ROWS, 1, 128), n.astype(F32).reshape(PLAN_ROWS, 1, 128))


def _ffn_kernel(te_s, misc_s, xs_hbm, wg_ref, wu_ref, wd_ref, ys_ref, wgb, wub, wdb, xbuf, xsem):
    j = pl.program_id(0)
    nused = misc_s[2 * 128]

    def fetch(t, slot):
        return pltpu.make_async_copy(xs_hbm.at[pl.ds(pl.multiple_of(t * MOE_TR, MOE_TR), MOE_TR)],
                                     xbuf.at[slot], xsem.at[slot])

    @pl.when(j == 0)
    def _():
        for t in range(MOE_XBUF - 1):
            @pl.when(t < nused)
            def _(t=t):
                fetch(t, t).start()

    @pl.when(j < nused)
    def _():
        ahead = j + (MOE_XBUF - 1)

        @pl.when(ahead < nused)
        def _():
            fetch(ahead, lax.rem(ahead, MOE_XBUF)).start()

        @pl.when((j == 0) | (te_s[j] != te_s[jnp.maximum(j - 1, 0)]))
        def _():
            wgb[...] = wg_ref[...].astype(BF16)
            wub[...] = wu_ref[...].astype(BF16)
            wdb[...] = wd_ref[...].astype(BF16)

        slot = lax.rem(j, MOE_XBUF)
        fetch(j, slot).wait()
        xs_ref = xbuf.at[slot]
        x = xs_ref[:, :D]
        w = xs_ref[:, D:D + 1].astype(F32) + xs_ref[:, D + 64:D + 65].astype(F32)
        hw = _silu(_dot(x, wgb[...])) * _dot(x, wub[...]) * w
        ys_ref[...] = _dot(hw.astype(BF16), wdb[...]).astype(BF16)


def _moe_ffn(plan, xs, wg, wu, wd, *, layer, ntmax):
    te, misc = plan[3], plan[4]

    def tile(j, misc_s):
        return jnp.minimum(j, misc_s[2 * 128] - 1)

    wspec = lambda shape: pl.BlockSpec((None, None) + shape,
                                       lambda j, te_s, misc_s: (layer, te_s[tile(j, misc_s)], 0, 0))
    grid_spec = pltpu.PrefetchScalarGridSpec(
        num_scalar_prefetch=2, grid=(ntmax,),
        in_specs=[pl.BlockSpec(memory_space=pl.ANY), wspec((D, DE)), wspec((D, DE)), wspec((DE, D))],
        out_specs=pl.BlockSpec((MOE_TR, D), lambda j, te_s, misc_s: (tile(j, misc_s), 0)),
        scratch_shapes=[pltpu.VMEM((D, DE), BF16), pltpu.VMEM((D, DE), BF16), pltpu.VMEM((DE, D), BF16),
                        pltpu.VMEM((MOE_XBUF, MOE_TR, MOE_XW), BF16), pltpu.SemaphoreType.DMA((MOE_XBUF,))])
    return pl.pallas_call(
        _ffn_kernel, grid_spec=grid_spec,
        out_shape=jax.ShapeDtypeStruct((ntmax * MOE_TR, D), BF16),
        compiler_params=_cparams(("arbitrary",)),
        name="moe_ffn",
    )(te[0], misc.reshape(-1), xs, wg, wu, wd)


def _combine_kernel(dst_s, lo_s, n_s, pieces_s, h_ref, comb_ref, lof_ref, nf_ref, sg_ref, su_ref, sd_ref, ys_hbm,
                    o_ref, ybuf, sem, *, ntile):
    i = pl.program_id(0)
    slot = lax.rem(i, 2)

    def copy_in(sl, src_row, dst_row, size):
        return pltpu.make_async_copy(ys_hbm.at[pl.ds(pl.multiple_of(src_row, MOE_SEG), size)],
                                     ybuf.at[sl, pl.ds(pl.multiple_of(dst_row, MOE_SEG), size)], sem.at[sl])

    def start_tile(tile, sl):
        def start_seg(e, carry):
            idx = tile * 128 + e
            lo, src = lo_s[idx], dst_s[idx]
            _for_each_piece(n_s[idx], lambda off, size: copy_in(sl, src + off, lo + off, size).start())
            return carry

        lax.fori_loop(0, NE, start_seg, 0)

    @pl.when(i == 0)
    def _():
        ybuf[...] = jnp.zeros_like(ybuf)
        start_tile(0, 0)

    @pl.when(i + 1 < ntile)
    def _():
        start_tile(i + 1, 1 - slot)

    lo_row = lof_ref[...]
    hi_row = lo_row + nf_ref[...]
    lo_col, hi_col = _col_of_row(lo_row), _col_of_row(hi_row)
    rp_hi, rp_lo = _split(_rank_plus(comb_ref[...]))
    l_hi, l_lo = _split(jnp.broadcast_to(lo_row * (1.0 / MOE_SEG), (8, 128)))
    lhs_meta = jnp.concatenate([jnp.concatenate([rp_hi, rp_lo], axis=1), jnp.concatenate([l_hi, l_lo], axis=1)], axis=0)

    h = h_ref[...]
    hs = _silu(_dot(h, sg_ref[...].astype(BF16))) * _dot(h, su_ref[...].astype(BF16))
    acc = _dot(hs.astype(BF16), sd_ref[...].astype(BF16))

    _wait_pieces(pieces_s, i, lambda size: copy_in(slot, 0, 0, size))
    for ch in range(MOE_LS // MOE_CH):
        scol = (_iota((1, MOE_CH), 1) + ch * MOE_CH).astype(F32)
        e_m = jnp.where((scol >= lo_col) & (scol < hi_col), 1.0, 0.0).astype(BF16)
        meta = _dot(lhs_meta, jnp.concatenate([e_m, e_m], axis=0))
        target = scol + 1.0 - meta[MOE_TD:MOE_TD + 1, :] * MOE_SEG
        perm_t = jnp.where(meta[:MOE_TD] == target, 1.0, 0.0).astype(BF16)
        acc = acc + _dot(perm_t, ybuf[slot, pl.ds(ch * MOE_CH, MOE_CH), :])
    o_ref[...] = acc


def _moe_combine(plan, ys, h_all, comb_all, sg, su, sd, *, layer):
    dst, lo, n, _, _, lof, pieces = plan
    ntile = h_all.shape[0] // MOE_TD
    shared = lambda w: pl.BlockSpec((None,) + w.shape[1:], lambda i, *_: (layer, 0, 0))
    grid_spec = pltpu.PrefetchScalarGridSpec(
        num_scalar_prefetch=4, grid=(ntile,),
        in_specs=[pl.BlockSpec((MOE_TD, D), lambda i, *_: (i, 0)),
                  pl.BlockSpec((MOE_TD, 128), lambda i, *_: (i, 0)),
                  pl.BlockSpec((None, 1, 128), lambda i, *_: (i, 0, 0)),
                  pl.BlockSpec((None, 1, 128), lambda i, *_: (i, 0, 0)),
                  shared(sg), shared(su), shared(sd),
                  pl.BlockSpec(memory_space=pl.ANY)],
        out_specs=pl.BlockSpec((MOE_TD, D), lambda i, *_: (i, 0)),
        scratch_shapes=[pltpu.VMEM((2, MOE_LS, D), BF16), pltpu.SemaphoreType.DMA((2,))])
    return pl.pallas_call(
        functools.partial(_combine_kernel, ntile=ntile),
        grid_spec=grid_spec,
        out_shape=jax.ShapeDtypeStruct((h_all.shape[0], D), F32),
        compiler_params=_cparams(("arbitrary",)),
        name="moe_combine",
    )(dst.reshape(-1), lo.reshape(-1), n.reshape(-1), pieces.reshape(-1), h_all, comb_all,
      lof.reshape(PLAN_ROWS, 1, 128), n.astype(F32).reshape(PLAN_ROWS, 1, 128), sg, su, sd, ys)


def _moe(h_all, comb_all, wg, wu, wd, sg, su, sd, *, layer, n_tokens):
    ntile = h_all.shape[0] // MOE_TD
    max_rows = n_tokens * TOPK + ntile * NE * (MOE_SEG - 1) + NE * (MOE_TR - 1)
    ntmax = -(-max_rows // MOE_TR)
    plan = _moe_plan(comb_all, ntmax=ntmax)
    xs = _moe_dispatch(plan, h_all, comb_all, ntmax=ntmax)
    ys = _moe_ffn(plan, xs, wg, wu, wd, layer=layer, ntmax=ntmax)
    return _moe_combine(plan, ys, h_all, comb_all, sg, su, sd, layer=layer)


def _final_kernel(x_ref, f_ref, pmod_ref, png_ref, o_ref):
    o_ref[...] = _residual_in(x_ref, f_ref, pmod_ref, png_ref)


def _final(x, f_all, pmods, png, *, nb, seq_len, tm, row0):
    nt = seq_len // tm
    t0 = row0 // tm
    return pl.pallas_call(
        _final_kernel,
        grid=(nb, nt),
        in_specs=[pl.BlockSpec((tm, D), lambda b, t: (b * nt + t, 0)),
                  pl.BlockSpec((tm, D), lambda b, t: (t0 + b * nt + t, 0)),
                  pl.BlockSpec((None,) + pmods.shape[1:], lambda b, t: (b, 0, 0, 0)),
                  pl.BlockSpec(png.shape, lambda b, t: (0, 0))],
        out_specs=pl.BlockSpec((tm, D), lambda b, t: (b * nt + t, 0)),
        out_shape=jax.ShapeDtypeStruct(x.shape, F32),
        compiler_params=_cparams(("arbitrary", "arbitrary")),
        name="final_residual",
    )(x, f_all, pmods, png)


def kernel(x_prompt, x_sample, c_prompt, c_sample, state_conv, state_shift, state_wkv, w_mod, b_mod, norm_g, ab_w_in, ab_conv_w, ab_mu, ab_w0, ab_w_decay_up, ab_a0, ab_a_up, ab_g_up, ab_k_k, ab_k_a, ab_r_k, ab_gn_g, ab_gn_b, ab_w_out, gm_w_in, gm_b_in, gm_ln_g, gm_ln_b, gm_w_s, gm_b_s, gm_w_out, moe_w_router, moe_b_router, moe_w_gate, moe_w_up, moe_w_down, moe_ws_gate, moe_ws_up, moe_ws_down):
    bsz, seq_len, _ = x_prompt.shape
    nsamp = x_sample.shape[0]
    nheads = DB // HEAD
    n_prompt = bsz * seq_len
    n_all = -(-(n_prompt + nsamp) // MOE_TD) * MOE_TD
    tm = 512

    xp = x_prompt.reshape(n_prompt, D)
    xs = x_sample.reshape(nsamp, D)

    m = _modulation(jnp.concatenate([c_prompt, c_sample], axis=0), w_mod, b_mod)
    mods_p = [m[l, :bsz].reshape(bsz, 6, 1, D) for l in range(2)]
    mods_s = [m[l, bsz:].reshape(nsamp, 6, D).transpose(1, 0, 2)[None] for l in range(2)]

    row = lambda t: t.reshape(1, -1)
    pad_rows = lambda t, lo, hi: jnp.pad(t, ((lo, hi), (0, 0)))
    small = [ab_conv_w[0], row(ab_mu[0]), row(ab_w0[0]),
             pad_rows(ab_w_decay_up[0], 0, 64).astype(BF16), row(ab_a0[0]),
             pad_rows(ab_a_up[0], 64, 0).astype(BF16), ab_g_up[0].astype(BF16),
             row(ab_k_k[0]), row(ab_k_a[0])]
    win0 = ab_w_in[0].astype(BF16)
    wo0 = ab_w_out[0].astype(BF16)
    rk, gng, gnb = row(ab_r_k[0]), row(ab_gn_g[0]), row(ab_gn_b[0])
    wrt = [moe_w_router[l].T for l in range(2)]
    br = [moe_b_router[l].reshape(NE, 1) for l in range(2)]

    h2_all = jnp.zeros((n_all, D), BF16)
    comb_all = jnp.zeros((n_all, 128), F32)

    ya, r, lw, k, v, kk, a, g, ctail, stail = _ab_pre(
        xp, mods_p[0], norm_g[0], win0, small, None, nb=bsz, seq_len=seq_len, tm=tm, seq_mode=True)
    conv_p = ctail.reshape(bsz, 8, DA)[:, 6:8][None]
    shift_p = stail.reshape(bsz, 8, PB)[:, 7][None]
    yb, sfin = _wkv_seq(r, lw, k, v, kk, a, g, rk, gng, gnb, nb=bsz, seq_len=seq_len)
    sfin = sfin.reshape(bsz, nheads // 2, 2, HEAD, 2, HEAD)
    wkv_p = jnp.stack([sfin[:, :, 0, :, 0, :], sfin[:, :, 1, :, 1, :]], axis=2)
    wkv_p = wkv_p.reshape(bsz, nheads, HEAD, HEAD).transpose(0, 1, 3, 2)[None]
    xp1, h2_all, comb_all = _ab_post(ya, yb, xp, mods_p[0], norm_g[0], wo0, wrt[0], br[0], h2_all, comb_all,
                                     nb=bsz, seq_len=seq_len, tm=tm, row0=0)

    prev = [state_conv[0, :, 0], state_conv[0, :, 1], state_shift[0]]
    ya_s, r, lw, k, v, kk, a, g, gated_s, pb_s = _ab_pre(
        xs, mods_s[0], norm_g[0], win0, small, prev, nb=1, seq_len=nsamp, tm=nsamp, seq_mode=False)
    conv_s = jnp.stack([state_conv[0, :, 1], gated_s], axis=1)[None]
    shift_s = pb_s[None]
    vecs = [t.reshape(nsamp * nheads, 1, HEAD) for t in (r, lw, k, v, kk, a, g)]
    params = [t.reshape(nheads, 1, HEAD) for t in (ab_r_k[0], ab_gn_g[0], ab_gn_b[0])]
    yb_s, snew = _wkv_step(state_wkv[0].reshape(nsamp * nheads, HEAD, HEAD), vecs, params, nheads=nheads)
    wkv_s = snew.reshape(1, nsamp, nheads, HEAD, HEAD)
    xs1, h2_all, comb_all = _ab_post(ya_s, yb_s.reshape(nsamp, DB).astype(BF16), xs, mods_s[0], norm_g[0], wo0,
                                     wrt[0], br[0], h2_all, comb_all, nb=1, seq_len=nsamp, tm=nsamp, row0=n_prompt)

    def moe_layer(l, h_all, c_all):
        return _moe(h_all, c_all, moe_w_gate, moe_w_up, moe_w_down,
                    moe_ws_gate, moe_ws_up, moe_ws_down, layer=l, n_tokens=n_prompt + nsamp)

    f0 = moe_layer(0, h2_all, comb_all)

    gw = [gm_w_in[0].astype(BF16), row(gm_b_in[0]), row(gm_ln_g[0]), row(gm_ln_b[0])]
    gw_seq = gw + [gm_w_s[0], gm_b_s[0].T, gm_w_out[0].astype(BF16), wrt[1], br[1]]
    gw_step = gw + [row(jnp.repeat(gm_w_s[0, :, 0, 0], CHUNK)), row(jnp.repeat(gm_b_s[0, :, 0], CHUNK)),
                    gm_w_out[0].astype(BF16), wrt[1], br[1]]
    tm1 = 512
    xp2, h2_all, comb_all = _gmlp(xp1, f0, mods_p[0], norm_g[0], mods_p[1], norm_g[1], gw_seq, h2_all, comb_all,
                                  nb=bsz, seq_len=seq_len, tm=tm1, row0=0, seq_mode=True)
    xs2, h2_all, comb_all, chunk_v = _gmlp(xs1, f0, mods_s[0], norm_g[0], mods_s[1], norm_g[1], gw_step, h2_all,
                                           comb_all, nb=1, seq_len=nsamp, tm=nsamp, row0=n_prompt, seq_mode=False)
    f1 = moe_layer(1, h2_all, comb_all)
    y_p = _final(xp2, f1, mods_p[1], norm_g[1], nb=bsz, seq_len=seq_len, tm=tm, row0=0)
    y_s = _final(xs2, f1, mods_s[1], norm_g[1], nb=1, seq_len=nsamp, tm=nsamp, row0=n_prompt)

    return (y_p.reshape(bsz, seq_len, D), y_s.reshape(nsamp, 1, D), conv_p, conv_s, shift_p, shift_s,
            wkv_p, wkv_s, chunk_v.reshape(1, nsamp, 1, D))
```

```python
import functools

import jax
import jax.numpy as jnp
from jax import lax
from jax.experimental import pallas as pl
from jax.experimental.pallas import tpu as pltpu

F32 = jnp.float32
BF16 = jnp.bfloat16

D = 1024
DA = 512
DB = 512
HEAD = 64
PB = 1792
PAB = 3 * DA + PB
NE = 64
NG = 8
EPG = NE // NG
TOPG = 4
TOPK = 8
DE = 256
ROUTE_SCALE = 2.5
EPS = 1e-6
GN_EPS = 64e-5
CHUNK = 128
WKV_SLAB = 128
WKV_CHUNK = 32
WKV_PAIRS_PER_STEP = 4
MOE_TD = 512
MOE_SEG = 16
MOE_LS = 5120
MOE_CH = 1024
MOE_PIECES = (64, 16)
MOE_TR = 1024
MOE_XW = D + 128
MOE_XBUF = 3
PLAN_ROWS = 128

VMEM_LIMIT = 56 * 1024 * 1024


def _cparams(sem):
    return pltpu.CompilerParams(dimension_semantics=sem, vmem_limit_bytes=VMEM_LIMIT)


def _dot(a, b, precision=None):
    return jnp.dot(a, b, preferred_element_type=F32, precision=precision)


def _dot_nt(a, b, precision=None):
    return lax.dot_general(a, b, (((1,), (1,)), ((), ())), preferred_element_type=F32, precision=precision)


def _iota(shape, dim):
    return lax.broadcasted_iota(jnp.int32, shape, dim)


def _rms(x, g):
    return x * lax.rsqrt(jnp.mean(x * x, axis=-1, keepdims=True) + EPS) * g


def _silu(x):
    return x * jax.nn.sigmoid(x)


def _seg_ones(n, seg):
    return (_iota((n, n), 0) // seg == _iota((n, n), 1) // seg).astype(F32)


def _mod_kernel(c_ref, w_ref, b_ref, o_ref):
    s = _silu(c_ref[...]).astype(BF16)
    o_ref[...] = _dot(s, w_ref[...].astype(BF16)) + b_ref[...]


def _modulation(c_all, w_mod, b_mod):
    depth, _, width = w_mod.shape
    rows = c_all.shape[0]
    bn = 512
    return pl.pallas_call(
        _mod_kernel,
        grid=(depth, width // bn),
        in_specs=[pl.BlockSpec((rows, D), lambda l, j: (0, 0)),
                  pl.BlockSpec((None, D, bn), lambda l, j: (l, 0, j)),
                  pl.BlockSpec((None, 1, bn), lambda l, j: (l, 0, j))],
        out_specs=pl.BlockSpec((None, rows, bn), lambda l, j: (l, 0, j)),
        out_shape=jax.ShapeDtypeStruct((depth, rows, width), F32),
        compiler_params=_cparams(("arbitrary", "arbitrary")),
        name="modulation",
    )(c_all, w_mod, b_mod.reshape(depth, 1, width))


def _residual_in(x_ref, f_ref, pmod_ref, png_ref):
    x = x_ref[...]
    if f_ref is None:
        return x
    return x + pmod_ref[5] * _rms(f_ref[...], png_ref[3:4, :])


def _route(h, wrt_ref, br_ref):
    tm = h.shape[0]
    logits = _mm(wrt_ref[...], h, 2, 2, nt=True)
    scores = jax.nn.sigmoid(logits)
    sel = scores + br_ref[...]
    sub = _iota((EPG, tm), 0).astype(F32)
    neg = jnp.float32(-jnp.inf)
    blks, sblks, gscore = [], [], []
    for g in range(NG):
        blk = sel[g * EPG:(g + 1) * EPG, :]
        m1 = jnp.max(blk, axis=0, keepdims=True)
        first = jnp.min(jnp.where(blk == m1, sub, float(EPG)), axis=0, keepdims=True)
        m2 = jnp.max(jnp.where(sub == first, neg, blk), axis=0, keepdims=True)
        blks.append(blk)
        sblks.append(scores[g * EPG:(g + 1) * EPG, :])
        gscore.append(m1 + m2)
    masked = []
    for g in range(NG):
        cnt = jnp.zeros((1, tm), F32)
        for g2 in range(NG):
            if g2 == g:
                continue
            beats = (gscore[g2] >= gscore[g]) if g2 < g else (gscore[g2] > gscore[g])
            cnt = cnt + jnp.where(beats, 1.0, 0.0)
        masked.append(jnp.where(cnt < TOPG, blks[g], neg))
    eidx = [sub + float(g * EPG) for g in range(NG)]
    chosen = [jnp.zeros((EPG, tm), F32) for _ in range(NG)]
    for _ in range(TOPK):
        best = masked[0]
        for g in range(1, NG):
            best = jnp.maximum(best, masked[g])
        best = jnp.max(best, axis=0, keepdims=True)
        first = jnp.where(masked[0] == best, eidx[0], float(NE))
        for g in range(1, NG):
            first = jnp.minimum(first, jnp.where(masked[g] == best, eidx[g], float(NE)))
        first = jnp.min(first, axis=0, keepdims=True)
        for g in range(NG):
            hit = eidx[g] == first
            chosen[g] = jnp.where(hit, 1.0, chosen[g])
            masked[g] = jnp.where(hit, neg, masked[g])
    wts = [jnp.where(chosen[g] > 0.0, sblks[g], 0.0) for g in range(NG)]
    tot = wts[0]
    for g in range(1, NG):
        tot = tot + wts[g]
    denom = jnp.sum(tot, axis=0, keepdims=True)
    comb_t = jnp.concatenate([w / denom * ROUTE_SCALE for w in wts] + [jnp.zeros((NE, tm), F32)], axis=0)
    return comb_t.T


def _post(y, x, mod_ref, ng_ref, wrt_ref, br_ref, xo_ref, h2_ref, comb_ref):
    xn = x + mod_ref[2] * _rms(y, ng_ref[1:2, :])
    xo_ref[...] = xn
    h2 = _rms(xn, ng_ref[2:3, :]) * (1.0 + mod_ref[4]) + mod_ref[3]
    h2_ref[...] = h2.astype(BF16)
    comb_ref[...] = _route(h2, wrt_ref, br_ref)


def _ab_pre_kernel(*refs, seq_mode, tm):
    it = iter(refs)
    x_ref, mod_ref, ng_ref, win_ref = next(it), next(it), next(it), next(it)
    cw_ref, mu_ref, w0_ref, wd_ref, a0_ref, wa_ref, wg_ref, kk_ref, ka_ref = (next(it) for _ in range(9))
    if not seq_mode:
        p2_ref, p1_ref, ps_ref = next(it), next(it), next(it)
    ya_ref, r_ref, lw_ref, k_ref, v_ref, kkn_ref, a_ref, g_ref, ctail_ref, stail_ref = (next(it) for _ in range(10))
    if seq_mode:
        cc_ref, sc_ref = next(it), next(it)

        @pl.when(pl.program_id(1) == 0)
        def _():
            cc_ref[...] = jnp.zeros_like(cc_ref)
            sc_ref[...] = jnp.zeros_like(sc_ref)

    x = x_ref[...]
    h = _rms(x, ng_ref[0:1, :]) * (1.0 + mod_ref[1]) + mod_ref[0]
    proj = _dot(h.astype(BF16), win_ref[...])
    a_h, a_b, a_c = proj[:, 0:DA], proj[:, DA:2 * DA], proj[:, 2 * DA:3 * DA]
    pb = proj[:, 3 * DA:]
    gated = a_c * a_h
    if seq_mode:
        rows = _iota((tm, 1), 0)
        g1 = jnp.where(rows == 0, cc_ref[7:8, :], pltpu.roll(gated, 1, 0))
        g2 = jnp.where(rows == 0, cc_ref[6:7, :], jnp.where(rows == 1, cc_ref[7:8, :], pltpu.roll(gated, 2, 0)))
        pprev = jnp.where(rows == 0, sc_ref[7:8, :], pltpu.roll(pb, 1, 0))
        cc_ref[...] = gated[tm - 8:, :]
        sc_ref[...] = pb[tm - 8:, :]
        ctail_ref[...] = gated[tm - 8:, :]
        stail_ref[...] = pb[tm - 8:, :]
    else:
        g1, g2, pprev = p1_ref[...], p2_ref[...], ps_ref[...]
        ctail_ref[...] = gated
        stail_ref[...] = pb
    conv = g2 * cw_ref[0:1, :] + g1 * cw_ref[1:2, :] + gated * cw_ref[2:3, :]
    ya_ref[...] = (a_b * conv).astype(BF16)

    xm = pb + (pprev - pb) * mu_ref[...]
    r, k, v = xm[:, 0:DB], xm[:, DB:2 * DB], xm[:, 2 * DB:3 * DB]
    lowrank = xm[:, 3 * DB:3 * DB + 128]
    dg = xm[:, 3 * DB + 128:]
    w_pre = w0_ref[...] + _dot(jnp.tanh(lowrank).astype(BF16), wd_ref[...])
    z = -w_pre
    softplus = jnp.maximum(z, 0.0) + jnp.log(1.0 + jnp.exp(-jnp.abs(z)))
    lw_ref[...] = -jnp.exp(-softplus - 0.5)
    a = jax.nn.sigmoid(a0_ref[...] + _dot(lowrank.astype(BF16), wa_ref[...]))
    g_ref[...] = _dot(jax.nn.sigmoid(dg).astype(BF16), wg_ref[...])
    kk = k * kk_ref[...]
    ss = _mm(kk * kk, _seg_ones(DB, HEAD), 2, 1)
    kkn_ref[...] = kk / jnp.maximum(jnp.sqrt(ss), 1e-12)
    r_ref[...] = r
    k_ref[...] = k * (1.0 + (a - 1.0) * ka_ref[...])
    v_ref[...] = v
    a_ref[...] = a


def _ab_pre(x, mods, ng, win, small, prev, *, nb, seq_len, tm, seq_mode):
    nt = seq_len // tm
    rows = nb * seq_len
    tail = 8 if seq_mode else tm
    full = lambda shape: pl.BlockSpec(shape, lambda b, t: tuple(0 for _ in shape))
    tok = lambda w: pl.BlockSpec((tm, w), lambda b, t: (b * nt + t, 0))
    in_specs = [tok(D),
                pl.BlockSpec((None,) + mods.shape[1:], lambda b, t: (b, 0, 0, 0)),
                full(ng.shape), full(win.shape)] + [full(s.shape) for s in small]
    args = [x, mods, ng, win] + list(small)
    if not seq_mode:
        in_specs += [tok(DA), tok(DA), tok(PB)]
        args += list(prev)
    out_shape = ([jax.ShapeDtypeStruct((rows, DA), BF16)] + [jax.ShapeDtypeStruct((rows, DB), F32)] * 7
                 + [jax.ShapeDtypeStruct((nb * tail, DA), F32), jax.ShapeDtypeStruct((nb * tail, PB), F32)])
    out_specs = ([tok(DA)] + [tok(DB)] * 7
                 + [pl.BlockSpec((tail, DA), lambda b, t: (b, 0)), pl.BlockSpec((tail, PB), lambda b, t: (b, 0))])
    scratch = [pltpu.VMEM((8, DA), F32), pltpu.VMEM((8, PB), F32)] if seq_mode else []
    return pl.pallas_call(
        functools.partial(_ab_pre_kernel, seq_mode=seq_mode, tm=tm),
        grid=(nb, nt), in_specs=in_specs, out_specs=out_specs, out_shape=out_shape,
        scratch_shapes=scratch,
        compiler_params=_cparams(("arbitrary", "arbitrary")),
        name="ab_pre_seq" if seq_mode else "ab_pre_step",
    )(*args)


def _split(x):
    hi = x.astype(BF16)
    return hi, (x - hi.astype(F32)).astype(BF16)


def _mm(a, b, pa=1, pb=1, nt=False):
    ah, al = _split(a) if pa == 2 else (a.astype(BF16), None)
    bh, bl = _split(b) if pb == 2 else (b.astype(BF16), None)
    bx = 1 if nt else 0
    if pa == 2 and pb == 2:
        ah, bh = jnp.concatenate([ah, al, ah], axis=1), jnp.concatenate([bh, bh, bl], axis=bx)
    elif pa == 2:
        ah, bh = jnp.concatenate([ah, al], axis=1), jnp.concatenate([bh, bh], axis=bx)
    elif pb == 2:
        ah, bh = jnp.concatenate([ah, ah], axis=1), jnp.concatenate([bh, bl], axis=bx)
    return _dot_nt(ah, bh) if nt else _dot(ah, bh)


def _gn_gate(o, r, k, v, g, rk, gng, gnb, seg):
    mean = _mm(o, seg, 2, 1) * (1.0 / HEAD)
    d = o - mean
    var = _mm(d * d, seg, 2, 1) * (1.0 / HEAD)
    out = d * lax.rsqrt(var + GN_EPS) * gng + gnb
    out = out + _mm(r * k * rk, seg, 2, 1) * v
    return out * g


def _wkv_pair(r, lw, k, v, kk, a, st, cst):
    ts, c = WKV_SLAB, WKV_CHUNK
    nchunk = ts // c
    strict, incl, cum_lhs, m0, m1, headdiag, eye, eye2, blks, chunk_cols = cst

    l1 = lw.astype(BF16)
    rem = lw - l1.astype(F32)
    l2 = rem.astype(BF16)
    l3 = (rem - l2.astype(F32)).astype(BF16)
    cums = _dot(cum_lhs, jnp.concatenate([l1, l2, l3], axis=0))
    yield
    cl, ctot = cums[:ts], cums[ts:]
    e_neg = jnp.exp(-cl)
    e_end = jnp.exp(ctot - cl)
    at = -kk * jnp.exp(cl - lw)
    rt = r * jnp.exp(cl)
    beta = kk * a
    bk = jnp.concatenate([beta * e_neg, k * e_neg], axis=0)
    bend, kend = beta * e_end, k * e_end

    at_st = jnp.concatenate([at * m0, at * m1], axis=0)
    a_all = _mm(jnp.concatenate([at_st, rt * m0, rt * m1], axis=0), bk, nt=True)
    yield
    a_st, a_out = a_all[:2 * ts], a_all[2 * ts:]
    zero = jnp.zeros((ts, ts), F32)

    def two_heads(p0, p1):
        return jnp.concatenate([jnp.concatenate([p0, zero], axis=1), jnp.concatenate([zero, p1], axis=1)], axis=0)

    lbd = two_heads(a_st[:ts, :ts] * strict, a_st[ts:, :ts] * strict)
    d1 = jnp.where(blks[0], lbd, 0.0)
    tinv = eye2 + d1
    d2 = _mm(d1, d1)
    yield
    tinv = tinv + _mm(tinv, d2)
    d4 = _mm(d2, d2)
    yield
    tinv = tinv + _mm(tinv, d4)
    yield
    for lvl in range(1, len(blks)):
        off = jnp.where(blks[lvl] & jnp.logical_not(blks[lvl - 1]), lbd, 0.0)
        half = _mm(tinv, off)
        yield
        tinv = tinv + _mm(half, tinv)
        yield

    v_st = jnp.concatenate([v * m0, v * m1], axis=0)
    y_st = _mm(two_heads(a_st[:ts, ts:] * strict, a_st[ts:, ts:] * strict), v_st)
    yield
    wu = _mm(tinv, jnp.concatenate([at_st, y_st], axis=1))
    yield
    w = wu[:ts, :2 * HEAD] + wu[ts:, :2 * HEAD]
    u = wu[:ts, 2 * HEAD:] + wu[ts:, 2 * HEAD:]
    rhs = jnp.concatenate([jnp.concatenate([w, u], axis=1),
                           jnp.concatenate([jnp.zeros_like(v), v], axis=1)], axis=0)
    bend_t, kend_t = bend.T, kend.T
    lhs = [jnp.concatenate([bend_t * cm, kend_t * cm], axis=1) for cm in chunk_cols]
    lhs.append(jnp.concatenate([a_out[:ts, :ts] * incl, a_out[:ts, ts:] * incl], axis=1))
    lhs.append(jnp.concatenate([a_out[ts:, :ts] * incl, a_out[ts:, ts:] * incl], axis=1))
    big = _mm(jnp.concatenate(lhs, axis=0), rhs)
    yield
    mn_all, qo = big[:nchunk * ts], big[nchunk * ts:]
    q = rt + qo[:ts, :2 * HEAD] * m0 + qo[ts:, :2 * HEAD] * m1
    olocal = qo[:ts, 2 * HEAD:] * m0 + qo[ts:, 2 * HEAD:] * m1

    outs = []
    for ch in range(nchunk):
        mn = mn_all[ch * ts:(ch + 1) * ts]
        trans = eye * jnp.exp(ctot[ch * c:ch * c + 1, :]) + mn[:, :2 * HEAD] * headdiag
        both = _mm(jnp.concatenate([q[ch * c:(ch + 1) * c], trans], axis=0), st)
        yield
        outs.append(both[:c] + olocal[ch * c:(ch + 1) * c])
        st = both[c:] + mn[:, 2 * HEAD:] * headdiag
    return jnp.concatenate(outs, axis=0), st


def _round_robin(gens):
    results = [None] * len(gens)
    active = list(enumerate(gens))
    while active:
        still = []
        for i, gen in active:
            try:
                next(gen)
                still.append((i, gen))
            except StopIteration as stop:
                results[i] = stop.value
        active = still
    return results


def _wkv_seq_kernel(r_ref, lw_ref, k_ref, v_ref, kk_ref, a_ref, g_ref, rk_ref, gng_ref, gnb_ref,
                    y_ref, sfin_ref, st_ref, *, npp):
    ts, c = WKV_SLAB, WKV_CHUNK

    @pl.when(pl.program_id(2) == 0)
    def _():
        st_ref[...] = jnp.zeros_like(st_ref)

    ri, ci = _iota((ts, ts), 0), _iota((ts, ts), 1)
    same = ri // c == ci // c
    strict = (same & (ci < ri)).astype(F32)
    incl = (same & (ci <= ri)).astype(F32)
    cum_lhs = jnp.concatenate([incl, same.astype(F32)], axis=0).astype(BF16)
    cum_lhs = jnp.concatenate([cum_lhs] * 3, axis=1)
    lane = _iota((1, 2 * HEAD), 1)
    m0 = (lane < HEAD).astype(F32)
    m1 = 1.0 - m0
    headdiag = _seg_ones(2 * HEAD, HEAD)
    eye = (_iota((2 * HEAD, 2 * HEAD), 0) == _iota((2 * HEAD, 2 * HEAD), 1)).astype(F32)
    r2, c2 = _iota((2 * ts, 2 * ts), 0), _iota((2 * ts, 2 * ts), 1)
    eye2 = (r2 == c2).astype(F32)
    blks, b = [], 8
    while b <= c:
        blks.append(r2 // b == c2 // b)
        b *= 2
    tcol = _iota((1, ts), 1) // c
    chunk_cols = [(tcol == ch).astype(F32) for ch in range(ts // c)]
    cst = (strict, incl, cum_lhs, m0, m1, headdiag, eye, eye2, blks, chunk_cols)

    sls = [slice(p * 2 * HEAD, (p + 1) * 2 * HEAD) for p in range(npp)]
    res = _round_robin([_wkv_pair(r_ref[:, sl], lw_ref[:, sl], k_ref[:, sl], v_ref[:, sl], kk_ref[:, sl],
                                  a_ref[:, sl], st_ref[p], cst) for p, sl in enumerate(sls)])
    for p, sl in enumerate(sls):
        o, st = res[p]
        st_ref[p] = st
        y_ref[:, sl] = _gn_gate(o, r_ref[:, sl], k_ref[:, sl], v_ref[:, sl], g_ref[:, sl], rk_ref[:, sl],
                                gng_ref[:, sl], gnb_ref[:, sl], headdiag).astype(BF16)

    @pl.when(pl.program_id(2) == pl.num_programs(2) - 1)
    def _():
        sfin_ref[...] = st_ref[...]


def _wkv_seq(r, lw, k, v, kk, a, g, rk, gng, gnb, *, nb, seq_len):
    ns = seq_len // WKV_SLAB
    npp = WKV_PAIRS_PER_STEP
    width = 2 * HEAD * npp
    tok = pl.BlockSpec((WKV_SLAB, width), lambda b, p, s: (b * ns + s, p))
    par = pl.BlockSpec((1, width), lambda b, p, s: (0, p))
    return pl.pallas_call(
        functools.partial(_wkv_seq_kernel, npp=npp),
        grid=(nb, DB // width, ns),
        in_specs=[tok] * 7 + [par] * 3,
        out_specs=[tok, pl.BlockSpec((None, npp, 2 * HEAD, 2 * HEAD), lambda b, p, s: (b, p, 0, 0))],
        out_shape=[jax.ShapeDtypeStruct((nb * seq_len, DB), BF16),
                   jax.ShapeDtypeStruct((nb, DB // (2 * HEAD), 2 * HEAD, 2 * HEAD), F32)],
        scratch_shapes=[pltpu.VMEM((npp, 2 * HEAD, 2 * HEAD), F32)],
        compiler_params=_cparams(("arbitrary", "arbitrary", "arbitrary")),
        name="wkv_seq",
    )(r, lw, k, v, kk, a, g, rk, gng, gnb)


def _wkv_step_kernel(s_ref, r_ref, lw_ref, k_ref, v_ref, kk_ref, a_ref, g_ref, rk_ref, gng_ref, gnb_ref,
                     y_ref, so_ref):
    s = s_ref[...]
    r, k, v, kk, a = r_ref[...], k_ref[...], v_ref[...], kk_ref[...], a_ref[...]
    w = jnp.exp(lw_ref[...])
    eye = (_iota((1, HEAD, HEAD), 1) == _iota((1, HEAD, HEAD), 2)).astype(F32)
    v_col = jnp.sum(eye * v, axis=2, keepdims=True)
    s_kk = jnp.sum(s * kk, axis=2, keepdims=True)
    s = s * w - s_kk * (kk * a) + v_col * k
    so_ref[...] = s
    o_col = jnp.sum(s * r, axis=2, keepdims=True)
    o = jnp.sum(eye * o_col, axis=1, keepdims=True)
    mean = jnp.mean(o, axis=2, keepdims=True)
    d = o - mean
    var = jnp.mean(d * d, axis=2, keepdims=True)
    out = d * lax.rsqrt(var + GN_EPS) * gng_ref[...] + gnb_ref[...]
    out = out + jnp.sum(r * k * rk_ref[...], axis=2, keepdims=True) * v
    y_ref[...] = out * g_ref[...]


def _wkv_step(state, vecs, params, *, nheads):
    n = state.shape[0]
    tb = 8 * nheads
    sspec = pl.BlockSpec((tb, HEAD, HEAD), lambda i: (i, 0, 0))
    vspec = pl.BlockSpec((tb, 1, HEAD), lambda i: (i, 0, 0))
    pspec = pl.BlockSpec((tb, 1, HEAD), lambda i: (0, 0, 0))
    reps = tb // nheads
    params = [jnp.tile(p, (reps, 1, 1)) for p in params]
    return pl.pallas_call(
        _wkv_step_kernel,
        grid=(n // tb,),
        in_specs=[sspec] + [vspec] * 7 + [pspec] * 3,
        out_specs=[vspec, sspec],
        out_shape=[jax.ShapeDtypeStruct((n, 1, HEAD), F32), jax.ShapeDtypeStruct((n, HEAD, HEAD), F32)],
        compiler_params=_cparams(("arbitrary",)),
        name="wkv_step",
    )(state, *vecs, *params)


def _ab_post_kernel(ya_ref, yb_ref, x_ref, mod_ref, ng_ref, wo_ref, wrt_ref, br_ref,
                    h2in_ref, combin_ref, xo_ref, h2_ref, comb_ref):
    del h2in_ref, combin_ref
    y = _dot(ya_ref[...], wo_ref[0:DA, :]) + _dot(yb_ref[...], wo_ref[DA:, :])
    _post(y, x_ref[...], mod_ref, ng_ref, wrt_ref, br_ref, xo_ref, h2_ref, comb_ref)


def _ab_post(ya, yb, x, mods, ng, wo, wrt, br, h2_all, comb_all, *, nb, seq_len, tm, row0):
    nt = seq_len // tm
    t0 = row0 // tm
    full = lambda shape: pl.BlockSpec(shape, lambda b, t: tuple(0 for _ in shape))
    tok = lambda w: pl.BlockSpec((tm, w), lambda b, t: (b * nt + t, 0))
    tok_all = lambda w: pl.BlockSpec((tm, w), lambda b, t: (t0 + b * nt + t, 0))
    anyspec = pl.BlockSpec(memory_space=pl.ANY)
    return pl.pallas_call(
        _ab_post_kernel,
        grid=(nb, nt),
        in_specs=[tok(DA), tok(DB), tok(D),
                  pl.BlockSpec((None,) + mods.shape[1:], lambda b, t: (b, 0, 0, 0)),
                  full(ng.shape), full(wo.shape), full(wrt.shape), full(br.shape), anyspec, anyspec],
        out_specs=[tok(D), tok_all(D), tok_all(128)],
        out_shape=[jax.ShapeDtypeStruct(x.shape, F32), jax.ShapeDtypeStruct(h2_all.shape, BF16),
                   jax.ShapeDtypeStruct(comb_all.shape, F32)],
        input_output_aliases={8: 1, 9: 2},
        compiler_params=_cparams(("arbitrary", "arbitrary")),
        name="ab_post",
    )(ya, yb, x, mods, ng, wo, wrt, br, h2_all, comb_all)


def _gmlp_kernel(*refs, seq_mode, tm):
    it = iter(refs)
    x_ref, f_ref, pmod_ref, png_ref, mod_ref, ng_ref = (next(it) for _ in range(6))
    win_ref, bin_ref, lng_ref, lnb_ref, ws_ref, bs_ref, wo_ref, wrt_ref, br_ref = (next(it) for _ in range(9))
    next(it), next(it)
    xo_ref, h2_ref, comb_ref = next(it), next(it), next(it)
    if not seq_mode:
        cv_ref = next(it)
    x = _residual_in(x_ref, f_ref, pmod_ref, png_ref)
    h = _rms(x, ng_ref[0:1, :]) * (1.0 + mod_ref[1]) + mod_ref[0]
    z = jax.nn.gelu(_dot(h.astype(BF16), win_ref[...]) + bin_ref[...])
    u, v = z[:, :D], z[:, D:]
    mu = jnp.mean(v, axis=-1, keepdims=True)
    var = jnp.mean(jnp.square(v - mu), axis=-1, keepdims=True)
    v = (v - mu) * lax.rsqrt(var + EPS) * lng_ref[...] + lnb_ref[...]
    if seq_mode:
        vb = v.astype(BF16)
        causal = _iota((CHUNK, CHUNK), 1) <= _iota((CHUNK, CHUNK), 0)
        cols = []
        for g in range(D // CHUNK):
            wsg = jnp.where(causal, ws_ref[g], 0.0).astype(BF16)
            bsg = bs_ref[:, g:g + 1]
            rows = [_dot(wsg, vb[c * CHUNK:(c + 1) * CHUNK, g * CHUNK:(g + 1) * CHUNK]) + bsg
                    for c in range(tm // CHUNK)]
            cols.append(jnp.concatenate(rows, axis=0) if len(rows) > 1 else rows[0])
        s = jnp.concatenate(cols, axis=1)
    else:
        cv_ref[...] = v
        s = v * ws_ref[...] + bs_ref[...]
    y = _dot((u * s).astype(BF16), wo_ref[...])
    _post(y, x, mod_ref, ng_ref, wrt_ref, br_ref, xo_ref, h2_ref, comb_ref)


def _gmlp(x, f_all, pmods, png, mods, ng, weights, h2_all, comb_all, *, nb, seq_len, tm, row0, seq_mode):
    nt = seq_len // tm
    t0 = row0 // tm
    full = lambda shape: pl.BlockSpec(shape, lambda b, t: tuple(0 for _ in shape))
    tok = lambda w: pl.BlockSpec((tm, w), lambda b, t: (b * nt + t, 0))
    tok_all = lambda w: pl.BlockSpec((tm, w), lambda b, t: (t0 + b * nt + t, 0))
    modspec = lambda m: pl.BlockSpec((None,) + m.shape[1:], lambda b, t: (b, 0, 0, 0))
    anyspec = pl.BlockSpec(memory_space=pl.ANY)
    in_specs = ([tok(D), tok_all(D), modspec(pmods), full(png.shape), modspec(mods), full(ng.shape)]
                + [full(w.shape) for w in weights] + [anyspec, anyspec])
    out_specs = [tok(D), tok_all(D), tok_all(128)]
    out_shape = [jax.ShapeDtypeStruct(x.shape, F32), jax.ShapeDtypeStruct(h2_all.shape, BF16),
                 jax.ShapeDtypeStruct(comb_all.shape, F32)]
    if not seq_mode:
        out_specs.append(tok(D))
        out_shape.append(jax.ShapeDtypeStruct(x.shape, F32))
    n_in = len(in_specs)
    return pl.pallas_call(
        functools.partial(_gmlp_kernel, seq_mode=seq_mode, tm=tm),
        grid=(nb, nt), in_specs=in_specs, out_specs=out_specs, out_shape=out_shape,
        input_output_aliases={n_in - 2: 1, n_in - 1: 2},
        compiler_params=_cparams(("arbitrary", "arbitrary")),
        name="gmlp_seq" if seq_mode else "gmlp_step",
    )(x, f_all, pmods, png, mods, ng, *weights, h2_all, comb_all)


def _strict_lower(n):
    return _iota((n, n), 1) < _iota((n, n), 0)


def _col_of_row(row):
    return jnp.broadcast_to(row, (128, 128)).T[:, 0:1]


def _plan_kernel(comb_ref, dst_ref, lo_ref, n_ref, te_ref, misc_ref, lof_ref, pieces_ref, cnt_ref, *, ntile, te_cols):
    i = pl.program_id(0)

    @pl.when(i == 0)
    def _():
        cnt_ref[...] = jnp.zeros_like(cnt_ref)

    sel = jnp.where(comb_ref[...] > 0.0, 1.0, 0.0)
    cnt_ref[pl.ds(i, 1), :] = jnp.sum(sel, axis=0, keepdims=True)

    @pl.when(i == ntile - 1)
    def _():
        c = cnt_ref[...]
        cpad = jnp.floor((c + (MOE_SEG - 1.0)) * (1.0 / MOE_SEG)) * MOE_SEG
        cb = cpad.astype(BF16)
        lower = jnp.where(_strict_lower(PLAN_ROWS), 1.0, 0.0).astype(BF16)
        upper = jnp.where(_iota((128, 128), 0) < _iota((128, 128), 1), 1.0, 0.0).astype(BF16)
        before = _dot(lower, cb)
        lo = _dot(cb, upper)
        tot = jnp.sum(cpad, axis=0, keepdims=True)
        rt = jnp.floor((tot + (MOE_TR - 1.0)) * (1.0 / MOE_TR))
        gt = _dot(jnp.broadcast_to(rt, (8, 128)).astype(BF16), upper)[0:1, :]
        dst_ref[...] = (gt * MOE_TR + before).astype(jnp.int32)
        lo_ref[...] = lo.astype(jnp.int32)
        lof_ref[...] = lo
        n_ref[...] = cpad.astype(jnp.int32)
        ends_col = _col_of_row(gt + rt)
        jrow = _iota((1, te_cols), 1).astype(F32)
        te = jnp.sum(jnp.where(ends_col <= jrow, 1.0, 0.0), axis=0, keepdims=True)
        te_ref[...] = jnp.broadcast_to(jnp.minimum(te, NE - 1.0), (8, te_cols)).astype(jnp.int32)
        nused = jnp.sum(rt, axis=1, keepdims=True)
        rowi = _iota((8, 128), 0)
        misc = jnp.where(rowi == 0, gt * MOE_TR + tot, jnp.where(rowi == 1, rt * MOE_TR - tot, nused))
        misc_ref[...] = misc.astype(jnp.int32)
        lane = _iota((1, 128), 1)
        pieces = jnp.zeros((PLAN_ROWS, 128), F32)
        taken = jnp.zeros_like(cpad)
        for k, size in enumerate(MOE_PIECES):
            cnt = jnp.floor((cpad - taken) * (1.0 / size))
            taken = taken + cnt * size
            pieces = pieces + jnp.where(lane == k, jnp.sum(cnt, axis=1, keepdims=True), 0.0)
        pieces_ref[...] = pieces.astype(jnp.int32)


def _moe_plan(comb_all, *, ntmax):
    ntile = comb_all.shape[0] // MOE_TD
    te_cols = -(-ntmax // 128) * 128
    full = lambda shape: pl.BlockSpec(shape, lambda i: tuple(0 for _ in shape))
    shapes = [((PLAN_ROWS, 128), jnp.int32)] * 3 + [((8, te_cols), jnp.int32), ((8, 128), jnp.int32),
                                                   ((PLAN_ROWS, 128), F32), ((PLAN_ROWS, 128), jnp.int32)]
    return pl.pallas_call(
        functools.partial(_plan_kernel, ntile=ntile, te_cols=te_cols),
        grid=(ntile,),
        in_specs=[pl.BlockSpec((MOE_TD, 128), lambda i: (i, 0))],
        out_specs=[full(s) for s, _ in shapes],
        out_shape=[jax.ShapeDtypeStruct(s, d) for s, d in shapes],
        scratch_shapes=[pltpu.VMEM((PLAN_ROWS, 128), F32)],
        compiler_params=_cparams(("arbitrary",)),
        name="moe_plan",
    )(comb_all)


def _rank_plus(comb):
    sel = jnp.where(comb > 0.0, 1.0, 0.0)
    lower = jnp.where(_strict_lower(MOE_TD), 1.0, 0.0).astype(BF16)
    return sel * (_dot(lower, sel.astype(BF16)) + 1.0)


def _for_each_piece(n, fn):
    off = 0
    for size in MOE_PIECES:
        cnt = lax.shift_right_logical(n - off, size.bit_length() - 1)

        def body(p, carry, off=off, size=size):
            fn(off + p * size, size)
            return carry

        lax.fori_loop(0, cnt, body, 0)
        off = off + cnt * size


def _wait_pieces(pieces_s, tile, make_copy):
    for k, size in enumerate(MOE_PIECES):
        def wait_one(p, carry, size=size):
            make_copy(size).wait()
            return carry

        lax.fori_loop(0, pieces_s[tile * 128 + k], wait_one, 0)


def _dispatch_kernel(dst_s, lo_s, n_s, misc_s, pieces_s, h_ref, comb_ref, lof_ref, nf_ref, xs_hbm, buf, zbuf, sem,
                     tsem, *, ntile):
    i = pl.program_id(0)
    slot = lax.rem(i, 2)
    comb = comb_ref[...]
    lo_row = lof_ref[...]
    hi_row = lo_row + nf_ref[...]
    rp_hi, rp_lo = _split(_rank_plus(comb).T)
    rhs_rank = jnp.concatenate([rp_hi, rp_lo], axis=0)
    c_hi, c_lo = _split(comb)
    h_ext = jnp.concatenate([h_ref[...], c_hi, c_lo], axis=1)
    lane_lo = _iota((1, 128), 1) < 64
    for ch in range(MOE_LS // MOE_CH):
        srow = (_iota((MOE_CH, 1), 0) + ch * MOE_CH).astype(F32)
        e_t = jnp.where((srow >= lo_row) & (srow < hi_row), 1.0, 0.0)
        e_tb = e_t.astype(BF16)
        rank_of = _dot(jnp.concatenate([e_tb, e_tb], axis=1), rhs_rank)
        target = srow + 1.0 - jnp.sum(e_t * lo_row, axis=1, keepdims=True)
        perm = jnp.where(rank_of == target, 1.0, 0.0).astype(BF16)
        rows = pl.ds(ch * MOE_CH, MOE_CH)
        sorted_rows = _dot(perm, h_ext)
        buf[slot, rows, :D] = sorted_rows[:, :D].astype(BF16)
        w = jnp.sum((sorted_rows[:, D:D + 128] + sorted_rows[:, D + 128:]) * e_t, axis=1, keepdims=True)
        w_hi = w.astype(BF16).astype(F32)
        buf[slot, rows, D:] = jnp.where(lane_lo, w_hi, w - w_hi).astype(BF16)

    def copy_out(sl, src_row, dst_row, size):
        return pltpu.make_async_copy(buf.at[sl, pl.ds(pl.multiple_of(src_row, MOE_SEG), size)],
                                     xs_hbm.at[pl.ds(pl.multiple_of(dst_row, MOE_SEG), size)], sem.at[sl])

    def start_seg(e, carry):
        idx = i * 128 + e
        lo, dst = lo_s[idx], dst_s[idx]
        _for_each_piece(n_s[idx], lambda off, size: copy_out(slot, lo + off, dst + off, size).start())
        return carry

    lax.fori_loop(0, NE, start_seg, 0)

    @pl.when(i > 0)
    def _():
        _wait_pieces(pieces_s, i - 1, lambda size: copy_out(1 - slot, 0, 0, size))

    @pl.when(i == ntile - 1)
    def _():
        _wait_pieces(pieces_s, i, lambda size: copy_out(slot, 0, 0, size))

    experts_per_step = -(-NE // ntile)

    def tail_copy(e, off, size):
        return pltpu.make_async_copy(
            zbuf.at[pl.ds(0, size)], xs_hbm.at[pl.ds(pl.multiple_of(misc_s[e] + off, MOE_SEG), size)], tsem)

    def tails(step, start):
        for k in range(experts_per_step):
            e = step * experts_per_step + k

            @pl.when(e < NE)
            def _(e=e):
                n = misc_s[128 + jnp.minimum(e, NE - 1)]
                if start:
                    _for_each_piece(n, lambda off, size: tail_copy(e, off, size).start())
                else:
                    _for_each_piece(n, lambda off, size: tail_copy(e, off, size).wait())

    @pl.when(i == 0)
    def _():
        zbuf[...] = jnp.zeros_like(zbuf)

    @pl.when(i > 0)
    def _():
        tails(i - 1, start=False)

    tails(i, start=True)

    @pl.when(i == ntile - 1)
    def _():
        tails(i, start=False)


def _moe_dispatch(plan, h_all, comb_all, *, ntmax):
    dst, lo, n, _, misc, lof, pieces = plan
    ntile = h_all.shape[0] // MOE_TD
    grid_spec = pltpu.PrefetchScalarGridSpec(
        num_scalar_prefetch=5, grid=(ntile,),
        in_specs=[pl.BlockSpec((MOE_TD, D), lambda i, *_: (i, 0)),
                  pl.BlockSpec((MOE_TD, 128), lambda i, *_: (i, 0)),
                  pl.BlockSpec((None, 1, 128), lambda i, *_: (i, 0, 0)),
                  pl.BlockSpec((None, 1, 128), lambda i, *_: (i, 0, 0))],
        out_specs=pl.BlockSpec(memory_space=pl.ANY),
        scratch_shapes=[pltpu.VMEM((2, MOE_LS, MOE_XW), BF16), pltpu.VMEM((MOE_PIECES[0], MOE_XW), BF16),
                        pltpu.SemaphoreType.DMA((2,)), pltpu.SemaphoreType.DMA(())])
    return pl.pallas_call(
        functools.partial(_dispatch_kernel, ntile=ntile),
        grid_spec=grid_spec,
        out_shape=jax.ShapeDtypeStruct((ntmax * MOE_TR, MOE_XW), BF16),
        compiler_params=_cparams(("arbitrary",)),
        name="moe_dispatch",
    )(dst.reshape(-1), lo.reshape(-1), n.reshape(-1), misc.reshape(-1), pieces.reshape(-1), h_all, comb_all,
      lof.reshape(PLAN_ROWS, 1, 128), n.astype(F32).reshape(PLAN_ROWS, 1, 128))


def _ffn_kernel(te_s, misc_s, xs_hbm, wg_ref, wu_ref, wd_ref, ys_ref, wgb, wub, wdb, xbuf, xsem):
    j = pl.program_id(0)
    nused = misc_s[2 * 128]

    def fetch(t, slot):
        return pltpu.make_async_copy(xs_hbm.at[pl.ds(pl.multiple_of(t * MOE_TR, MOE_TR), MOE_TR)],
                                     xbuf.at[slot], xsem.at[slot])

    @pl.when(j == 0)
    def _():
        for t in range(MOE_XBUF - 1):
            @pl.when(t < nused)
            def _(t=t):
                fetch(t, t).start()

    @pl.when(j < nused)
    def _():
        ahead = j + (MOE_XBUF - 1)

        @pl.when(ahead < nused)
        def _():
            fetch(ahead, lax.rem(ahead, MOE_XBUF)).start()

        @pl.when((j == 0) | (te_s[j] != te_s[jnp.maximum(j - 1, 0)]))
        def _():
            wgb[...] = wg_ref[...].astype(BF16)
            wub[...] = wu_ref[...].astype(BF16)
            wdb[...] = wd_ref[...].astype(BF16)

        slot = lax.rem(j, MOE_XBUF)
        fetch(j, slot).wait()
        xs_ref = xbuf.at[slot]
        x = xs_ref[:, :D]
        w = xs_ref[:, D:D + 1].astype(F32) + xs_ref[:, D + 64:D + 65].astype(F32)
        hw = _silu(_dot(x, wgb[...])) * _dot(x, wub[...]) * w
        ys_ref[...] = _dot(hw.astype(BF16), wdb[...]).astype(BF16)


def _moe_ffn(plan, xs, wg, wu, wd, *, layer, ntmax):
    te, misc = plan[3], plan[4]

    def tile(j, misc_s):
        return jnp.minimum(j, misc_s[2 * 128] - 1)

    wspec = lambda shape: pl.BlockSpec((None, None) + shape,
                                       lambda j, te_s, misc_s: (layer, te_s[tile(j, misc_s)], 0, 0))
    grid_spec = pltpu.PrefetchScalarGridSpec(
        num_scalar_prefetch=2, grid=(ntmax,),
        in_specs=[pl.BlockSpec(memory_space=pl.ANY), wspec((D, DE)), wspec((D, DE)), wspec((DE, D))],
        out_specs=pl.BlockSpec((MOE_TR, D), lambda j, te_s, misc_s: (tile(j, misc_s), 0)),
        scratch_shapes=[pltpu.VMEM((D, DE), BF16), pltpu.VMEM((D, DE), BF16), pltpu.VMEM((DE, D), BF16),
                        pltpu.VMEM((MOE_XBUF, MOE_TR, MOE_XW), BF16), pltpu.SemaphoreType.DMA((MOE_XBUF,))])
    return pl.pallas_call(
        _ffn_kernel, grid_spec=grid_spec,
        out_shape=jax.ShapeDtypeStruct((ntmax * MOE_TR, D), BF16),
        compiler_params=_cparams(("arbitrary",)),
        name="moe_ffn",
    )(te[0], misc.reshape(-1), xs, wg, wu, wd)


def _combine_kernel(dst_s, lo_s, n_s, pieces_s, h_ref, comb_ref, lof_ref, nf_ref, sg_ref, su_ref, sd_ref, ys_hbm,
                    o_ref, ybuf, sem, *, ntile):
    i = pl.program_id(0)
    slot = lax.rem(i, 2)

    def copy_in(sl, src_row, dst_row, size):
        return pltpu.make_async_copy(ys_hbm.at[pl.ds(pl.multiple_of(src_row, MOE_SEG), size)],
                                     ybuf.at[sl, pl.ds(pl.multiple_of(dst_row, MOE_SEG), size)], sem.at[sl])

    def start_tile(tile, sl):
        def start_seg(e, carry):
            idx = tile * 128 + e
            lo, src = lo_s[idx], dst_s[idx]
            _for_each_piece(n_s[idx], lambda off, size: copy_in(sl, src + off, lo + off, size).start())
            return carry

        lax.fori_loop(0, NE, start_seg, 0)

    @pl.when(i == 0)
    def _():
        ybuf[...] = jnp.zeros_like(ybuf)
        start_tile(0, 0)

    @pl.when(i + 1 < ntile)
    def _():
        start_tile(i + 1, 1 - slot)

    lo_row = lof_ref[...]
    hi_row = lo_row + nf_ref[...]
    lo_col, hi_col = _col_of_row(lo_row), _col_of_row(hi_row)
    rp_hi, rp_lo = _split(_rank_plus(comb_ref[...]))
    l_hi, l_lo = _split(jnp.broadcast_to(lo_row * (1.0 / MOE_SEG), (8, 128)))
    lhs_meta = jnp.concatenate([jnp.concatenate([rp_hi, rp_lo], axis=1), jnp.concatenate([l_hi, l_lo], axis=1)], axis=0)

    h = h_ref[...]
    hs = _silu(_dot(h, sg_ref[...].astype(BF16))) * _dot(h, su_ref[...].astype(BF16))
    acc = _dot(hs.astype(BF16), sd_ref[...].astype(BF16))

    _wait_pieces(pieces_s, i, lambda size: copy_in(slot, 0, 0, size))
    for ch in range(MOE_LS // MOE_CH):
        scol = (_iota((1, MOE_CH), 1) + ch * MOE_CH).astype(F32)
        e_m = jnp.where((scol >= lo_col) & (scol < hi_col), 1.0, 0.0).astype(BF16)
        meta = _dot(lhs_meta, jnp.concatenate([e_m, e_m], axis=0))
        target = scol + 1.0 - meta[MOE_TD:MOE_TD + 1, :] * MOE_SEG
        perm_t = jnp.where(meta[:MOE_TD] == target, 1.0, 0.0).astype(BF16)
        acc = acc + _dot(perm_t, ybuf[slot, pl.ds(ch * MOE_CH, MOE_CH), :])
    o_ref[...] = acc


def _moe_combine(plan, ys, h_all, comb_all, sg, su, sd, *, layer):
    dst, lo, n, _, _, lof, pieces = plan
    ntile = h_all.shape[0] // MOE_TD
    shared = lambda w: pl.BlockSpec((None,) + w.shape[1:], lambda i, *_: (layer, 0, 0))
    grid_spec = pltpu.PrefetchScalarGridSpec(
        num_scalar_prefetch=4, grid=(ntile,),
        in_specs=[pl.BlockSpec((MOE_TD, D), lambda i, *_: (i, 0)),
                  pl.BlockSpec((MOE_TD, 128), lambda i, *_: (i, 0)),
                  pl.BlockSpec((None, 1, 128), lambda i, *_: (i, 0, 0)),
                  pl.BlockSpec((None, 1, 128), lambda i, *_: (i, 0, 0)),
                  shared(sg), shared(su), shared(sd),
                  pl.BlockSpec(memory_space=pl.ANY)],
        out_specs=pl.BlockSpec((MOE_TD, D), lambda i, *_: (i, 0)),
        scratch_shapes=[pltpu.VMEM((2, MOE_LS, D), BF16), pltpu.SemaphoreType.DMA((2,))])
    return pl.pallas_call(
        functools.partial(_combine_kernel, ntile=ntile),
        grid_spec=grid_spec,
        out_shape=jax.ShapeDtypeStruct((h_all.shape[0], D), F32),
        compiler_params=_cparams(("arbitrary",)),
        name="moe_combine",
    )(dst.reshape(-1), lo.reshape(-1), n.reshape(-1), pieces.reshape(-1), h_all, comb_all,
      lof.reshape(PLAN_ROWS, 1, 128), n.astype(F32).reshape(PLAN_ROWS, 1, 128), sg, su, sd, ys)


def _moe(h_all, comb_all, wg, wu, wd, sg, su, sd, *, layer, n_tokens):
    ntile = h_all.shape[0] // MOE_TD
    max_rows = n_tokens * TOPK + ntile * NE * (MOE_SEG - 1) + NE * (MOE_TR - 1)
    ntmax = -(-max_rows // MOE_TR)
    plan = _moe_plan(comb_all, ntmax=ntmax)
    xs = _moe_dispatch(plan, h_all, comb_all, ntmax=ntmax)
    ys = _moe_ffn(plan, xs, wg, wu, wd, layer=layer, ntmax=ntmax)
    return _moe_combine(plan, ys, h_all, comb_all, sg, su, sd, layer=layer)


def _final_kernel(x_ref, f_ref, pmod_ref, png_ref, o_ref):
    o_ref[...] = _residual_in(x_ref, f_ref, pmod_ref, png_ref)


def _final(x, f_all, pmods, png, *, nb, seq_len, tm, row0):
    nt = seq_len // tm
    t0 = row0 // tm
    return pl.pallas_call(
        _final_kernel,
        grid=(nb, nt),
        in_specs=[pl.BlockSpec((tm, D), lambda b, t: (b * nt + t, 0)),
                  pl.BlockSpec((tm, D), lambda b, t: (t0 + b * nt + t, 0)),
                  pl.BlockSpec((None,) + pmods.shape[1:], lambda b, t: (b, 0, 0, 0)),
                  pl.BlockSpec(png.shape, lambda b, t: (0, 0))],
        out_specs=pl.BlockSpec((tm, D), lambda b, t: (b * nt + t, 0)),
        out_shape=jax.ShapeDtypeStruct(x.shape, F32),
        compiler_params=_cparams(("arbitrary", "arbitrary")),
        name="final_residual",
    )(x, f_all, pmods, png)


def kernel(x_prompt, x_sample, c_prompt, c_sample, state_conv, state_shift, state_wkv, w_mod, b_mod, norm_g, ab_w_in, ab_conv_w, ab_mu, ab_w0, ab_w_decay_up, ab_a0, ab_a_up, ab_g_up, ab_k_k, ab_k_a, ab_r_k, ab_gn_g, ab_gn_b, ab_w_out, gm_w_in, gm_b_in, gm_ln_g, gm_ln_b, gm_w_s, gm_b_s, gm_w_out, moe_w_router, moe_b_router, moe_w_gate, moe_w_up, moe_w_down, moe_ws_gate, moe_ws_up, moe_ws_down):
    bsz, seq_len, _ = x_prompt.shape
    nsamp = x_sample.shape[0]
    nheads = DB // HEAD
    n_prompt = bsz * seq_len
    n_all = -(-(n_prompt + nsamp) // MOE_TD) * MOE_TD
    tm = 512

    xp = x_prompt.reshape(n_prompt, D)
    xs = x_sample.reshape(nsamp, D)

    m = _modulation(jnp.concatenate([c_prompt, c_sample], axis=0), w_mod, b_mod)
    mods_p = [m[l, :bsz].reshape(bsz, 6, 1, D) for l in range(2)]
    mods_s = [m[l, bsz:].reshape(nsamp, 6, D).transpose(1, 0, 2)[None] for l in range(2)]

    row = lambda t: t.reshape(1, -1)
    pad_rows = lambda t, lo, hi: jnp.pad(t, ((lo, hi), (0, 0)))
    small = [ab_conv_w[0], row(ab_mu[0]), row(ab_w0[0]),
             pad_rows(ab_w_decay_up[0], 0, 64).astype(BF16), row(ab_a0[0]),
             pad_rows(ab_a_up[0], 64, 0).astype(BF16), ab_g_up[0].astype(BF16),
             row(ab_k_k[0]), row(ab_k_a[0])]
    win0 = ab_w_in[0].astype(BF16)
    wo0 = ab_w_out[0].astype(BF16)
    rk, gng, gnb = row(ab_r_k[0]), row(ab_gn_g[0]), row(ab_gn_b[0])
    wrt = [moe_w_router[l].T for l in range(2)]
    br = [moe_b_router[l].reshape(NE, 1) for l in range(2)]

    h2_all = jnp.zeros((n_all, D), BF16)
    comb_all = jnp.zeros((n_all, 128), F32)

    ya, r, lw, k, v, kk, a, g, ctail, stail = _ab_pre(
        xp, mods_p[0], norm_g[0], win0, small, None, nb=bsz, seq_len=seq_len, tm=tm, seq_mode=True)
    conv_p = ctail.reshape(bsz, 8, DA)[:, 6:8][None]
    shift_p = stail.reshape(bsz, 8, PB)[:, 7][None]
    yb, sfin = _wkv_seq(r, lw, k, v, kk, a, g, rk, gng, gnb, nb=bsz, seq_len=seq_len)
    sfin = sfin.reshape(bsz, nheads // 2, 2, HEAD, 2, HEAD)
    wkv_p = jnp.stack([sfin[:, :, 0, :, 0, :], sfin[:, :, 1, :, 1, :]], axis=2)
    wkv_p = wkv_p.reshape(bsz, nheads, HEAD, HEAD).transpose(0, 1, 3, 2)[None]
    xp1, h2_all, comb_all = _ab_post(ya, yb, xp, mods_p[0], norm_g[0], wo0, wrt[0], br[0], h2_all, comb_all,
                                     nb=bsz, seq_len=seq_len, tm=tm, row0=0)

    prev = [state_conv[0, :, 0], state_conv[0, :, 1], state_shift[0]]
    ya_s, r, lw, k, v, kk, a, g, gated_s, pb_s = _ab_pre(
        xs, mods_s[0], norm_g[0], win0, small, prev, nb=1, seq_len=nsamp, tm=nsamp, seq_mode=False)
    conv_s = jnp.stack([state_conv[0, :, 1], gated_s], axis=1)[None]
    shift_s = pb_s[None]
    vecs = [t.reshape(nsamp * nheads, 1, HEAD) for t in (r, lw, k, v, kk, a, g)]
    params = [t.reshape(nheads, 1, HEAD) for t in (ab_r_k[0], ab_gn_g[0], ab_gn_b[0])]
    yb_s, snew = _wkv_step(state_wkv[0].reshape(nsamp * nheads, HEAD, HEAD), vecs, params, nheads=nheads)
    wkv_s = snew.reshape(1, nsamp, nheads, HEAD, HEAD)
    xs1, h2_all, comb_all = _ab_post(ya_s, yb_s.reshape(nsamp, DB).astype(BF16), xs, mods_s[0], norm_g[0], wo0,
                                     wrt[0], br[0], h2_all, comb_all, nb=1, seq_len=nsamp, tm=nsamp, row0=n_prompt)

    def moe_layer(l, h_all, c_all):
        return _moe(h_all, c_all, moe_w_gate, moe_w_up, moe_w_down,
                    moe_ws_gate, moe_ws_up, moe_ws_down, layer=l, n_tokens=n_prompt + nsamp)

    f0 = moe_layer(0, h2_all, comb_all)

    gw = [gm_w_in[0].astype(BF16), row(gm_b_in[0]), row(gm_ln_g[0]), row(gm_ln_b[0])]
    gw_seq = gw + [gm_w_s[0], gm_b_s[0].T, gm_w_out[0].astype(BF16), wrt[1], br[1]]
    gw_step = gw + [row(jnp.repeat(gm_w_s[0, :, 0, 0], CHUNK)), row(jnp.repeat(gm_b_s[0, :, 0], CHUNK)),
                    gm_w_out[0].astype(BF16), wrt[1], br[1]]
    tm1 = 512
    xp2, h2_all, comb_all = _gmlp(xp1, f0, mods_p[0], norm_g[0], mods_p[1], norm_g[1], gw_seq, h2_all, comb_all,
                                  nb=bsz, seq_len=seq_len, tm=tm1, row0=0, seq_mode=True)
    xs2, h2_all, comb_all, chunk_v = _gmlp(xs1, f0, mods_s[0], norm_g[0], mods_s[1], norm_g[1], gw_step, h2_all,
                                           comb_all, nb=1, seq_len=nsamp, tm=nsamp, row0=n_prompt, seq_mode=False)
    f1 = moe_layer(1, h2_all, comb_all)
    y_p = _final(xp2, f1, mods_p[1], norm_g[1], nb=bsz, seq_len=seq_len, tm=tm, row0=0)
    y_s = _final(xs2, f1, mods_s[1], norm_g[1], nb=1, seq_len=nsamp, tm=nsamp, row0=n_prompt)

    return (y_p.reshape(bsz, seq_len, D), y_s.reshape(nsamp, 1, D), conv_p, conv_s, shift_p, shift_s,
            wkv_p, wkv_s, chunk_v.reshape(1, nsamp, 1, D))
```

```python
import functools

import jax
import jax.numpy as jnp
from jax import lax
from jax.experimental import pallas as pl
from jax.experimental.pallas import tpu as pltpu

F32 = jnp.float32
BF16 = jnp.bfloat16

D = 1024
DA = 512
DB = 512
HEAD = 64
PB = 1792
PAB = 3 * DA + PB
NE = 64
NG = 8
EPG = NE // NG
TOPG = 4
TOPK = 8
DE = 256
ROUTE_SCALE = 2.5
EPS = 1e-6
GN_EPS = 64e-5
CHUNK = 128
WKV_SLAB = 128
WKV_CHUNK = 32
WKV_PAIRS_PER_STEP = 4
MOE_TD = 512
MOE_SEG = 16
MOE_LS = 5120
MOE_CH = 1024
MOE_PIECES = (64, 16)
MOE_TR = 1024
MOE_XW = D + 128
MOE_XBUF = 4
PLAN_ROWS = 128

VMEM_LIMIT = 56 * 1024 * 1024


def _cparams(sem):
    return pltpu.CompilerParams(dimension_semantics=sem, vmem_limit_bytes=VMEM_LIMIT)


def _dot(a, b, precision=None):
    return jnp.dot(a, b, preferred_element_type=F32, precision=precision)


def _dot_nt(a, b, precision=None):
    return lax.dot_general(a, b, (((1,), (1,)), ((), ())), preferred_element_type=F32, precision=precision)


def _iota(shape, dim):
    return lax.broadcasted_iota(jnp.int32, shape, dim)


def _rms(x, g):
    return x * lax.rsqrt(jnp.mean(x * x, axis=-1, keepdims=True) + EPS) * g


def _silu(x):
    return x * jax.nn.sigmoid(x)


def _seg_ones(n, seg):
    return (_iota((n, n), 0) // seg == _iota((n, n), 1) // seg).astype(F32)


def _mod_kernel(c_ref, w_ref, b_ref, o_ref):
    s = _silu(c_ref[...]).astype(BF16)
    o_ref[...] = _dot(s, w_ref[...].astype(BF16)) + b_ref[...]


def _modulation(c_all, w_mod, b_mod):
    depth, _, width = w_mod.shape
    rows = c_all.shape[0]
    bn = 512
    return pl.pallas_call(
        _mod_kernel,
        grid=(depth, width // bn),
        in_specs=[pl.BlockSpec((rows, D), lambda l, j: (0, 0)),
                  pl.BlockSpec((None, D, bn), lambda l, j: (l, 0, j)),
                  pl.BlockSpec((None, 1, bn), lambda l, j: (l, 0, j))],
        out_specs=pl.BlockSpec((None, rows, bn), lambda l, j: (l, 0, j)),
        out_shape=jax.ShapeDtypeStruct((depth, rows, width), F32),
        compiler_params=_cparams(("arbitrary", "arbitrary")),
        name="modulation",
    )(c_all, w_mod, b_mod.reshape(depth, 1, width))


def _residual_in(x_ref, f_ref, pmod_ref, png_ref):
    x = x_ref[...]
    if f_ref is None:
        return x
    return x + pmod_ref[5] * _rms(f_ref[...], png_ref[3:4, :])


def _route(h, wrt_ref, br_ref):
    tm = h.shape[0]
    logits = _mm(wrt_ref[...], h, 2, 2, nt=True)
    scores = jax.nn.sigmoid(logits)
    sel = scores + br_ref[...]
    sub = _iota((EPG, tm), 0).astype(F32)
    neg = jnp.float32(-jnp.inf)
    blks, sblks, gscore = [], [], []
    for g in range(NG):
        blk = sel[g * EPG:(g + 1) * EPG, :]
        m1 = jnp.max(blk, axis=0, keepdims=True)
        first = jnp.min(jnp.where(blk == m1, sub, float(EPG)), axis=0, keepdims=True)
        m2 = jnp.max(jnp.where(sub == first, neg, blk), axis=0, keepdims=True)
        blks.append(blk)
        sblks.append(scores[g * EPG:(g + 1) * EPG, :])
        gscore.append(m1 + m2)
    masked = []
    for g in range(NG):
        cnt = jnp.zeros((1, tm), F32)
        for g2 in range(NG):
            if g2 == g:
                continue
            beats = (gscore[g2] >= gscore[g]) if g2 < g else (gscore[g2] > gscore[g])
            cnt = cnt + jnp.where(beats, 1.0, 0.0)
        masked.append(jnp.where(cnt < TOPG, blks[g], neg))
    eidx = [sub + float(g * EPG) for g in range(NG)]
    chosen = [jnp.zeros((EPG, tm), F32) for _ in range(NG)]
    for _ in range(TOPK):
        best = masked[0]
        for g in range(1, NG):
            best = jnp.maximum(best, masked[g])
        best = jnp.max(best, axis=0, keepdims=True)
        first = jnp.where(masked[0] == best, eidx[0], float(NE))
        for g in range(1, NG):
            first = jnp.minimum(first, jnp.where(masked[g] == best, eidx[g], float(NE)))
        first = jnp.min(first, axis=0, keepdims=True)
        for g in range(NG):
            hit = eidx[g] == first
            chosen[g] = jnp.where(hit, 1.0, chosen[g])
            masked[g] = jnp.where(hit, neg, masked[g])
    wts = [jnp.where(chosen[g] > 0.0, sblks[g], 0.0) for g in range(NG)]
    tot = wts[0]
    for g in range(1, NG):
        tot = tot + wts[g]
    denom = jnp.sum(tot, axis=0, keepdims=True)
    comb_t = jnp.concatenate([w / denom * ROUTE_SCALE for w in wts] + [jnp.zeros((NE, tm), F32)], axis=0)
    return comb_t.T


def _post(y, x, mod_ref, ng_ref, wrt_ref, br_ref, xo_ref, h2_ref, comb_ref):
    xn = x + mod_ref[2] * _rms(y, ng_ref[1:2, :])
    xo_ref[...] = xn
    h2 = _rms(xn, ng_ref[2:3, :]) * (1.0 + mod_ref[4]) + mod_ref[3]
    h2_ref[...] = h2.astype(BF16)
    comb_ref[...] = _route(h2, wrt_ref, br_ref)


def _ab_pre_kernel(*refs, seq_mode, tm):
    it = iter(refs)
    x_ref, mod_ref, ng_ref, win_ref = next(it), next(it), next(it), next(it)
    cw_ref, mu_ref, w0_ref, wd_ref, a0_ref, wa_ref, wg_ref, kk_ref, ka_ref = (next(it) for _ in range(9))
    if not seq_mode:
        p2_ref, p1_ref, ps_ref = next(it), next(it), next(it)
    ya_ref, r_ref, lw_ref, k_ref, v_ref, kkn_ref, a_ref, g_ref, ctail_ref, stail_ref = (next(it) for _ in range(10))
    if seq_mode:
        cc_ref, sc_ref = next(it), next(it)

        @pl.when(pl.program_id(1) == 0)
        def _():
            cc_ref[...] = jnp.zeros_like(cc_ref)
            sc_ref[...] = jnp.zeros_like(sc_ref)

    x = x_ref[...]
    h = _rms(x, ng_ref[0:1, :]) * (1.0 + mod_ref[1]) + mod_ref[0]
    proj = _dot(h.astype(BF16), win_ref[...])
    a_h, a_b, a_c = proj[:, 0:DA], proj[:, DA:2 * DA], proj[:, 2 * DA:3 * DA]
    pb = proj[:, 3 * DA:]
    gated = a_c * a_h
    if seq_mode:
        rows = _iota((tm, 1), 0)
        g1 = jnp.where(rows == 0, cc_ref[7:8, :], pltpu.roll(gated, 1, 0))
        g2 = jnp.where(rows == 0, cc_ref[6:7, :], jnp.where(rows == 1, cc_ref[7:8, :], pltpu.roll(gated, 2, 0)))
        pprev = jnp.where(rows == 0, sc_ref[7:8, :], pltpu.roll(pb, 1, 0))
        cc_ref[...] = gated[tm - 8:, :]
        sc_ref[...] = pb[tm - 8:, :]
        ctail_ref[...] = gated[tm - 8:, :]
        stail_ref[...] = pb[tm - 8:, :]
    else:
        g1, g2, pprev = p1_ref[...], p2_ref[...], ps_ref[...]
        ctail_ref[...] = gated
        stail_ref[...] = pb
    conv = g2 * cw_ref[0:1, :] + g1 * cw_ref[1:2, :] + gated * cw_ref[2:3, :]
    ya_ref[...] = (a_b * conv).astype(BF16)

    xm = pb + (pprev - pb) * mu_ref[...]
    r, k, v = xm[:, 0:DB], xm[:, DB:2 * DB], xm[:, 2 * DB:3 * DB]
    lowrank = xm[:, 3 * DB:3 * DB + 128]
    dg = xm[:, 3 * DB + 128:]
    w_pre = w0_ref[...] + _dot(jnp.tanh(lowrank).astype(BF16), wd_ref[...])
    z = -w_pre
    softplus = jnp.maximum(z, 0.0) + jnp.log(1.0 + jnp.exp(-jnp.abs(z)))
    lw_ref[...] = -jnp.exp(-softplus - 0.5)
    a = jax.nn.sigmoid(a0_ref[...] + _dot(lowrank.astype(BF16), wa_ref[...]))
    g_ref[...] = _dot(jax.nn.sigmoid(dg).astype(BF16), wg_ref[...])
    kk = k * kk_ref[...]
    ss = _mm(kk * kk, _seg_ones(DB, HEAD), 2, 1)
    kkn_ref[...] = kk / jnp.maximum(jnp.sqrt(ss), 1e-12)
    r_ref[...] = r
    k_ref[...] = k * (1.0 + (a - 1.0) * ka_ref[...])
    v_ref[...] = v
    a_ref[...] = a


def _ab_pre(x, mods, ng, win, small, prev, *, nb, seq_len, tm, seq_mode):
    nt = seq_len // tm
    rows = nb * seq_len
    tail = 8 if seq_mode else tm
    full = lambda shape: pl.BlockSpec(shape, lambda b, t: tuple(0 for _ in shape))
    tok = lambda w: pl.BlockSpec((tm, w), lambda b, t: (b * nt + t, 0))
    in_specs = [tok(D),
                pl.BlockSpec((None,) + mods.shape[1:], lambda b, t: (b, 0, 0, 0)),
                full(ng.shape), full(win.shape)] + [full(s.shape) for s in small]
    args = [x, mods, ng, win] + list(small)
    if not seq_mode:
        in_specs += [tok(DA), tok(DA), tok(PB)]
        args += list(prev)
    out_shape = ([jax.ShapeDtypeStruct((rows, DA), BF16)] + [jax.ShapeDtypeStruct((rows, DB), F32)] * 7
                 + [jax.ShapeDtypeStruct((nb * tail, DA), F32), jax.ShapeDtypeStruct((nb * tail, PB), F32)])
    out_specs = ([tok(DA)] + [tok(DB)] * 7
                 + [pl.BlockSpec((tail, DA), lambda b, t: (b, 0)), pl.BlockSpec((tail, PB), lambda b, t: (b, 0))])
    scratch = [pltpu.VMEM((8, DA), F32), pltpu.VMEM((8, PB), F32)] if seq_mode else []
    return pl.pallas_call(
        functools.partial(_ab_pre_kernel, seq_mode=seq_mode, tm=tm),
        grid=(nb, nt), in_specs=in_specs, out_specs=out_specs, out_shape=out_shape,
        scratch_shapes=scratch,
        compiler_params=_cparams(("arbitrary", "arbitrary")),
        name="ab_pre_seq" if seq_mode else "ab_pre_step",
    )(*args)


def _split(x):
    hi = x.astype(BF16)
    return hi, (x - hi.astype(F32)).astype(BF16)


def _mm(a, b, pa=1, pb=1, nt=False):
    ah, al = _split(a) if pa == 2 else (a.astype(BF16), None)
    bh, bl = _split(b) if pb == 2 else (b.astype(BF16), None)
    bx = 1 if nt else 0
    if pa == 2 and pb == 2:
        ah, bh = jnp.concatenate([ah, al, ah], axis=1), jnp.concatenate([bh, bh, bl], axis=bx)
    elif pa == 2:
        ah, bh = jnp.concatenate([ah, al], axis=1), jnp.concatenate([bh, bh], axis=bx)
    elif pb == 2:
        ah, bh = jnp.concatenate([ah, ah], axis=1), jnp.concatenate([bh, bl], axis=bx)
    return _dot_nt(ah, bh) if nt else _dot(ah, bh)


def _gn_gate(o, r, k, v, g, rk, gng, gnb, seg):
    mean = _mm(o, seg, 2, 1) * (1.0 / HEAD)
    d = o - mean
    var = _mm(d * d, seg, 2, 1) * (1.0 / HEAD)
    out = d * lax.rsqrt(var + GN_EPS) * gng + gnb
    out = out + _mm(r * k * rk, seg, 2, 1) * v
    return out * g


def _wkv_pair(r, lw, k, v, kk, a, st, cst):
    ts, c = WKV_SLAB, WKV_CHUNK
    nchunk = ts // c
    strict, incl, cum_lhs, m0, m1, headdiag, eye, eye2, blks, chunk_cols = cst

    l1 = lw.astype(BF16)
    rem = lw - l1.astype(F32)
    l2 = rem.astype(BF16)
    l3 = (rem - l2.astype(F32)).astype(BF16)
    cums = _dot(cum_lhs, jnp.concatenate([l1, l2, l3], axis=0))
    yield
    cl, ctot = cums[:ts], cums[ts:]
    e_neg = jnp.exp(-cl)
    e_end = jnp.exp(ctot - cl)
    at = -kk * jnp.exp(cl - lw)
    rt = r * jnp.exp(cl)
    beta = kk * a
    bk = jnp.concatenate([beta * e_neg, k * e_neg], axis=0)
    bend, kend = beta * e_end, k * e_end

    at_st = jnp.concatenate([at * m0, at * m1], axis=0)
    a_all = _mm(jnp.concatenate([at_st, rt * m0, rt * m1], axis=0), bk, nt=True)
    yield
    a_st, a_out = a_all[:2 * ts], a_all[2 * ts:]
    zero = jnp.zeros((ts, ts), F32)

    def two_heads(p0, p1):
        return jnp.concatenate([jnp.concatenate([p0, zero], axis=1), jnp.concatenate([zero, p1], axis=1)], axis=0)

    lbd = two_heads(a_st[:ts, :ts] * strict, a_st[ts:, :ts] * strict)
    d1 = jnp.where(blks[0], lbd, 0.0)
    tinv = eye2 + d1
    d2 = _mm(d1, d1)
    yield
    tinv = tinv + _mm(tinv, d2)
    d4 = _mm(d2, d2)
    yield
    tinv = tinv + _mm(tinv, d4)
    yield
    for lvl in range(1, len(blks)):
        off = jnp.where(blks[lvl] & jnp.logical_not(blks[lvl - 1]), lbd, 0.0)
        half = _mm(tinv, off)
        yield
        tinv = tinv + _mm(half, tinv)
        yield

    v_st = jnp.concatenate([v * m0, v * m1], axis=0)
    y_st = _mm(two_heads(a_st[:ts, ts:] * strict, a_st[ts:, ts:] * strict), v_st)
    yield
    wu = _mm(tinv, jnp.concatenate([at_st, y_st], axis=1))
    yield
    w = wu[:ts, :2 * HEAD] + wu[ts:, :2 * HEAD]
    u = wu[:ts, 2 * HEAD:] + wu[ts:, 2 * HEAD:]
    rhs = jnp.concatenate([jnp.concatenate([w, u], axis=1),
                           jnp.concatenate([jnp.zeros_like(v), v], axis=1)], axis=0)
    bend_t, kend_t = bend.T, kend.T
    lhs = [jnp.concatenate([bend_t * cm, kend_t * cm], axis=1) for cm in chunk_cols]
    lhs.append(jnp.concatenate([a_out[:ts, :ts] * incl, a_out[:ts, ts:] * incl], axis=1))
    lhs.append(jnp.concatenate([a_out[ts:, :ts] * incl, a_out[ts:, ts:] * incl], axis=1))
    big = _mm(jnp.concatenate(lhs, axis=0), rhs)
    yield
    mn_all, qo = big[:nchunk * ts], big[nchunk * ts:]
    q = rt + qo[:ts, :2 * HEAD] * m0 + qo[ts:, :2 * HEAD] * m1
    olocal = qo[:ts, 2 * HEAD:] * m0 + qo[ts:, 2 * HEAD:] * m1

    outs = []
    for ch in range(nchunk):
        mn = mn_all[ch * ts:(ch + 1) * ts]
        trans = eye * jnp.exp(ctot[ch * c:ch * c + 1, :]) + mn[:, :2 * HEAD] * headdiag
        both = _mm(jnp.concatenate([q[ch * c:(ch + 1) * c], trans], axis=0), st)
        yield
        outs.append(both[:c] + olocal[ch * c:(ch + 1) * c])
        st = both[c:] + mn[:, 2 * HEAD:] * headdiag
    return jnp.concatenate(outs, axis=0), st


def _round_robin(gens):
    results = [None] * len(gens)
    active = list(enumerate(gens))
    while active:
        still = []
        for i, gen in active:
            try:
                next(gen)
                still.append((i, gen))
            except StopIteration as stop:
                results[i] = stop.value
        active = still
    return results


def _wkv_seq_kernel(r_ref, lw_ref, k_ref, v_ref, kk_ref, a_ref, g_ref, rk_ref, gng_ref, gnb_ref,
                    y_ref, sfin_ref, st_ref, *, npp):
    ts, c = WKV_SLAB, WKV_CHUNK

    @pl.when(pl.program_id(2) == 0)
    def _():
        st_ref[...] = jnp.zeros_like(st_ref)

    ri, ci = _iota((ts, ts), 0), _iota((ts, ts), 1)
    same = ri // c == ci // c
    strict = (same & (ci < ri)).astype(F32)
    incl = (same & (ci <= ri)).astype(F32)
    cum_lhs = jnp.concatenate([incl, same.astype(F32)], axis=0).astype(BF16)
    cum_lhs = jnp.concatenate([cum_lhs] * 3, axis=1)
    lane = _iota((1, 2 * HEAD), 1)
    m0 = (lane < HEAD).astype(F32)
    m1 = 1.0 - m0
    headdiag = _seg_ones(2 * HEAD, HEAD)
    eye = (_iota((2 * HEAD, 2 * HEAD), 0) == _iota((2 * HEAD, 2 * HEAD), 1)).astype(F32)
    r2, c2 = _iota((2 * ts, 2 * ts), 0), _iota((2 * ts, 2 * ts), 1)
    eye2 = (r2 == c2).astype(F32)
    blks, b = [], 8
    while b <= c:
        blks.append(r2 // b == c2 // b)
        b *= 2
    tcol = _iota((1, ts), 1) // c
    chunk_cols = [(tcol == ch).astype(F32) for ch in range(ts // c)]
    cst = (strict, incl, cum_lhs, m0, m1, headdiag, eye, eye2, blks, chunk_cols)

    sls = [slice(p * 2 * HEAD, (p + 1) * 2 * HEAD) for p in range(npp)]
    res = _round_robin([_wkv_pair(r_ref[:, sl], lw_ref[:, sl], k_ref[:, sl], v_ref[:, sl], kk_ref[:, sl],
                                  a_ref[:, sl], st_ref[p], cst) for p, sl in enumerate(sls)])
    for p, sl in enumerate(sls):
        o, st = res[p]
        st_ref[p] = st
        y_ref[:, sl] = _gn_gate(o, r_ref[:, sl], k_ref[:, sl], v_ref[:, sl], g_ref[:, sl], rk_ref[:, sl],
                                gng_ref[:, sl], gnb_ref[:, sl], headdiag).astype(BF16)

    @pl.when(pl.program_id(2) == pl.num_programs(2) - 1)
    def _():
        sfin_ref[...] = st_ref[...]


def _wkv_seq(r, lw, k, v, kk, a, g, rk, gng, gnb, *, nb, seq_len):
    ns = seq_len // WKV_SLAB
    npp = WKV_PAIRS_PER_STEP
    width = 2 * HEAD * npp
    tok = pl.BlockSpec((WKV_SLAB, width), lambda b, p, s: (b * ns + s, p))
    par = pl.BlockSpec((1, width), lambda b, p, s: (0, p))
    return pl.pallas_call(
        functools.partial(_wkv_seq_kernel, npp=npp),
        grid=(nb, DB // width, ns),
        in_specs=[tok] * 7 + [par] * 3,
        out_specs=[tok, pl.BlockSpec((None, npp, 2 * HEAD, 2 * HEAD), lambda b, p, s: (b, p, 0, 0))],
        out_shape=[jax.ShapeDtypeStruct((nb * seq_len, DB), BF16),
                   jax.ShapeDtypeStruct((nb, DB // (2 * HEAD), 2 * HEAD, 2 * HEAD), F32)],
        scratch_shapes=[pltpu.VMEM((npp, 2 * HEAD, 2 * HEAD), F32)],
        compiler_params=_cparams(("arbitrary", "arbitrary", "arbitrary")),
        name="wkv_seq",
    )(r, lw, k, v, kk, a, g, rk, gng, gnb)


def _wkv_step_kernel(s_ref, r_ref, lw_ref, k_ref, v_ref, kk_ref, a_ref, g_ref, rk_ref, gng_ref, gnb_ref,
                     y_ref, so_ref):
    s = s_ref[...]
    r, k, v, kk, a = r_ref[...], k_ref[...], v_ref[...], kk_ref[...], a_ref[...]
    w = jnp.exp(lw_ref[...])
    eye = (_iota((1, HEAD, HEAD), 1) == _iota((1, HEAD, HEAD), 2)).astype(F32)
    v_col = jnp.sum(eye * v, axis=2, keepdims=True)
    s_kk = jnp.sum(s * kk, axis=2, keepdims=True)
    s = s * w - s_kk * (kk * a) + v_col * k
    so_ref[...] = s
    o_col = jnp.sum(s * r, axis=2, keepdims=True)
    o = jnp.sum(eye * o_col, axis=1, keepdims=True)
    mean = jnp.mean(o, axis=2, keepdims=True)
    d = o - mean
    var = jnp.mean(d * d, axis=2, keepdims=True)
    out = d * lax.rsqrt(var + GN_EPS) * gng_ref[...] + gnb_ref[...]
    out = out + jnp.sum(r * k * rk_ref[...], axis=2, keepdims=True) * v
    y_ref[...] = out * g_ref[...]


def _wkv_step(state, vecs, params, *, nheads):
    n = state.shape[0]
    tb = 8 * nheads
    sspec = pl.BlockSpec((tb, HEAD, HEAD), lambda i: (i, 0, 0))
    vspec = pl.BlockSpec((tb, 1, HEAD), lambda i: (i, 0, 0))
    pspec = pl.BlockSpec((tb, 1, HEAD), lambda i: (0, 0, 0))
    reps = tb // nheads
    params = [jnp.tile(p, (reps, 1, 1)) for p in params]
    return pl.pallas_call(
        _wkv_step_kernel,
        grid=(n // tb,),
        in_specs=[sspec] + [vspec] * 7 + [pspec] * 3,
        out_specs=[vspec, sspec],
        out_shape=[jax.ShapeDtypeStruct((n, 1, HEAD), F32), jax.ShapeDtypeStruct((n, HEAD, HEAD), F32)],
        compiler_params=_cparams(("arbitrary",)),
        name="wkv_step",
    )(state, *vecs, *params)


def _ab_post_kernel(ya_ref, yb_ref, x_ref, mod_ref, ng_ref, wo_ref, wrt_ref, br_ref,
                    h2in_ref, combin_ref, xo_ref, h2_ref, comb_ref):
    del h2in_ref, combin_ref
    y = _dot(ya_ref[...], wo_ref[0:DA, :]) + _dot(yb_ref[...], wo_ref[DA:, :])
    _post(y, x_ref[...], mod_ref, ng_ref, wrt_ref, br_ref, xo_ref, h2_ref, comb_ref)


def _ab_post(ya, yb, x, mods, ng, wo, wrt, br, h2_all, comb_all, *, nb, seq_len, tm, row0):
    nt = seq_len // tm
    t0 = row0 // tm
    full = lambda shape: pl.BlockSpec(shape, lambda b, t: tuple(0 for _ in shape))
    tok = lambda w: pl.BlockSpec((tm, w), lambda b, t: (b * nt + t, 0))
    tok_all = lambda w: pl.BlockSpec((tm, w), lambda b, t: (t0 + b * nt + t, 0))
    anyspec = pl.BlockSpec(memory_space=pl.ANY)
    return pl.pallas_call(
        _ab_post_kernel,
        grid=(nb, nt),
        in_specs=[tok(DA), tok(DB), tok(D),
                  pl.BlockSpec((None,) + mods.shape[1:], lambda b, t: (b, 0, 0, 0)),
                  full(ng.shape), full(wo.shape), full(wrt.shape), full(br.shape), anyspec, anyspec],
        out_specs=[tok(D), tok_all(D), tok_all(128)],
        out_shape=[jax.ShapeDtypeStruct(x.shape, F32), jax.ShapeDtypeStruct(h2_all.shape, BF16),
                   jax.ShapeDtypeStruct(comb_all.shape, F32)],
        input_output_aliases={8: 1, 9: 2},
        compiler_params=_cparams(("arbitrary", "arbitrary")),
        name="ab_post",
    )(ya, yb, x, mods, ng, wo, wrt, br, h2_all, comb_all)


def _gmlp_kernel(*refs, seq_mode, tm):
    it = iter(refs)
    x_ref, f_ref, pmod_ref, png_ref, mod_ref, ng_ref = (next(it) for _ in range(6))
    win_ref, bin_ref, lng_ref, lnb_ref, ws_ref, bs_ref, wo_ref, wrt_ref, br_ref = (next(it) for _ in range(9))
    next(it), next(it)
    xo_ref, h2_ref, comb_ref = next(it), next(it), next(it)
    if not seq_mode:
        cv_ref = next(it)
    x = _residual_in(x_ref, f_ref, pmod_ref, png_ref)
    h = _rms(x, ng_ref[0:1, :]) * (1.0 + mod_ref[1]) + mod_ref[0]
    z = jax.nn.gelu(_dot(h.astype(BF16), win_ref[...]) + bin_ref[...])
    u, v = z[:, :D], z[:, D:]
    mu = jnp.mean(v, axis=-1, keepdims=True)
    var = jnp.mean(jnp.square(v - mu), axis=-1, keepdims=True)
    v = (v - mu) * lax.rsqrt(var + EPS) * lng_ref[...] + lnb_ref[...]
    if seq_mode:
        vb = v.astype(BF16)
        causal = _iota((CHUNK, CHUNK), 1) <= _iota((CHUNK, CHUNK), 0)
        cols = []
        for g in range(D // CHUNK):
            wsg = jnp.where(causal, ws_ref[g], 0.0).astype(BF16)
            bsg = bs_ref[:, g:g + 1]
            rows = [_dot(wsg, vb[c * CHUNK:(c + 1) * CHUNK, g * CHUNK:(g + 1) * CHUNK]) + bsg
                    for c in range(tm // CHUNK)]
            cols.append(jnp.concatenate(rows, axis=0) if len(rows) > 1 else rows[0])
        s = jnp.concatenate(cols, axis=1)
    else:
        cv_ref[...] = v
        s = v * ws_ref[...] + bs_ref[...]
    y = _dot((u * s).astype(BF16), wo_ref[...])
    _post(y, x, mod_ref, ng_ref, wrt_ref, br_ref, xo_ref, h2_ref, comb_ref)


def _gmlp(x, f_all, pmods, png, mods, ng, weights, h2_all, comb_all, *, nb, seq_len, tm, row0, seq_mode):
    nt = seq_len // tm
    t0 = row0 // tm
    full = lambda shape: pl.BlockSpec(shape, lambda b, t: tuple(0 for _ in shape))
    tok = lambda w: pl.BlockSpec((tm, w), lambda b, t: (b * nt + t, 0))
    tok_all = lambda w: pl.BlockSpec((tm, w), lambda b, t: (t0 + b * nt + t, 0))
    modspec = lambda m: pl.BlockSpec((None,) + m.shape[1:], lambda b, t: (b, 0, 0, 0))
    anyspec = pl.BlockSpec(memory_space=pl.ANY)
    in_specs = ([tok(D), tok_all(D), modspec(pmods), full(png.shape), modspec(mods), full(ng.shape)]
                + [full(w.shape) for w in weights] + [anyspec, anyspec])
    out_specs = [tok(D), tok_all(D), tok_all(128)]
    out_shape = [jax.ShapeDtypeStruct(x.shape, F32), jax.ShapeDtypeStruct(h2_all.shape, BF16),
                 jax.ShapeDtypeStruct(comb_all.shape, F32)]
    if not seq_mode:
        out_specs.append(tok(D))
        out_shape.append(jax.ShapeDtypeStruct(x.shape, F32))
    n_in = len(in_specs)
    return pl.pallas_call(
        functools.partial(_gmlp_kernel, seq_mode=seq_mode, tm=tm),
        grid=(nb, nt), in_specs=in_specs, out_specs=out_specs, out_shape=out_shape,
        input_output_aliases={n_in - 2: 1, n_in - 1: 2},
        compiler_params=_cparams(("arbitrary", "arbitrary")),
        name="gmlp_seq" if seq_mode else "gmlp_step",
    )(x, f_all, pmods, png, mods, ng, *weights, h2_all, comb_all)


def _strict_lower(n):
    return _iota((n, n), 1) < _iota((n, n), 0)


def _col_of_row(row):
    return jnp.broadcast_to(row, (128, 128)).T[:, 0:1]


def _plan_kernel(comb_ref, dst_ref, lo_ref, n_ref, te_ref, misc_ref, lof_ref, pieces_ref, cnt_ref, *, ntile, te_cols):
    i = pl.program_id(0)

    @pl.when(i == 0)
    def _():
        cnt_ref[...] = jnp.zeros_like(cnt_ref)

    sel = jnp.where(comb_ref[...] > 0.0, 1.0, 0.0)
    cnt_ref[pl.ds(i, 1), :] = jnp.sum(sel, axis=0, keepdims=True)

    @pl.when(i == ntile - 1)
    def _():
        c = cnt_ref[...]
        cpad = jnp.floor((c + (MOE_SEG - 1.0)) * (1.0 / MOE_SEG)) * MOE_SEG
        cb = cpad.astype(BF16)
        lower = jnp.where(_strict_lower(PLAN_ROWS), 1.0, 0.0).astype(BF16)
        upper = jnp.where(_iota((128, 128), 0) < _iota((128, 128), 1), 1.0, 0.0).astype(BF16)
        before = _dot(lower, cb)
        lo = _dot(cb, upper)
        tot = jnp.sum(cpad, axis=0, keepdims=True)
        rt = jnp.floor((tot + (MOE_TR - 1.0)) * (1.0 / MOE_TR))
        gt = _dot(jnp.broadcast_to(rt, (8, 128)).astype(BF16), upper)[0:1, :]
        dst_ref[...] = (gt * MOE_TR + before).astype(jnp.int32)
        lo_ref[...] = lo.astype(jnp.int32)
        lof_ref[...] = lo
        n_ref[...] = cpad.astype(jnp.int32)
        ends_col = _col_of_row(gt + rt)
        jrow = _iota((1, te_cols), 1).astype(F32)
        te = jnp.sum(jnp.where(ends_col <= jrow, 1.0, 0.0), axis=0, keepdims=True)
        te_ref[...] = jnp.broadcast_to(jnp.minimum(te, NE - 1.0), (8, te_cols)).astype(jnp.int32)
        nused = jnp.sum(rt, axis=1, keepdims=True)
        rowi = _iota((8, 128), 0)
        misc = jnp.where(rowi == 0, gt * MOE_TR + tot, jnp.where(rowi == 1, rt * MOE_TR - tot, nused))
        misc_ref[...] = misc.astype(jnp.int32)
        lane = _iota((1, 128), 1)
        pieces = jnp.zeros((PLAN_ROWS, 128), F32)
        taken = jnp.zeros_like(cpad)
        for k, size in enumerate(MOE_PIECES):
            cnt = jnp.floor((cpad - taken) * (1.0 / size))
            taken = taken + cnt * size
            pieces = pieces + jnp.where(lane == k, jnp.sum(cnt, axis=1, keepdims=True), 0.0)
        pieces_ref[...] = pieces.astype(jnp.int32)


def _moe_plan(comb_all, *, ntmax):
    ntile = comb_all.shape[0] // MOE_TD
    te_cols = -(-ntmax // 128) * 128
    full = lambda shape: pl.BlockSpec(shape, lambda i: tuple(0 for _ in shape))
    shapes = [((PLAN_ROWS, 128), jnp.int32)] * 3 + [((8, te_cols), jnp.int32), ((8, 128), jnp.int32),
                                                   ((PLAN_ROWS, 128), F32), ((PLAN_ROWS, 128), jnp.int32)]
    return pl.pallas_call(
        functools.partial(_plan_kernel, ntile=ntile, te_cols=te_cols),
        grid=(ntile,),
        in_specs=[pl.BlockSpec((MOE_TD, 128), lambda i: (i, 0))],
        out_specs=[full(s) for s, _ in shapes],
        out_shape=[jax.ShapeDtypeStruct(s, d) for s, d in shapes],
        scratch_shapes=[pltpu.VMEM((PLAN_ROWS, 128), F32)],
        compiler_params=_cparams(("arbitrary",)),
        name="moe_plan",
    )(comb_all)


def _rank_plus(comb):
    sel = jnp.where(comb > 0.0, 1.0, 0.0)
    lower = jnp.where(_strict_lower(MOE_TD), 1.0, 0.0).astype(BF16)
    return sel * (_dot(lower, sel.astype(BF16)) + 1.0)


def _for_each_piece(n, fn):
    off = 0
    for size in MOE_PIECES:
        cnt = lax.shift_right_logical(n - off, size.bit_length() - 1)

        def body(p, carry, off=off, size=size):
            fn(off + p * size, size)
            return carry

        lax.fori_loop(0, cnt, body, 0)
        off = off + cnt * size


def _wait_pieces(pieces_s, tile, make_copy):
    for k, size in enumerate(MOE_PIECES):
        def wait_one(p, carry, size=size):
            make_copy(size).wait()
            return carry

        lax.fori_loop(0, pieces_s[tile * 128 + k], wait_one, 0)


def _dispatch_kernel(dst_s, lo_s, n_s, misc_s, pieces_s, h_ref, comb_ref, lof_ref, nf_ref, xs_hbm, buf, zbuf, sem,
                     tsem, *, ntile):
    i = pl.program_id(0)
    slot = lax.rem(i, 2)
    comb = comb_ref[...]
    lo_row = lof_ref[...]
    hi_row = lo_row + nf_ref[...]
    rp_hi, rp_lo = _split(_rank_plus(comb).T)
    rhs_rank = jnp.concatenate([rp_hi, rp_lo], axis=0)
    c_hi, c_lo = _split(comb)
    h_ext = jnp.concatenate([h_ref[...], c_hi, c_lo], axis=1)
    lane_lo = _iota((1, 128), 1) < 64
    for ch in range(MOE_LS // MOE_CH):
        srow = (_iota((MOE_CH, 1), 0) + ch * MOE_CH).astype(F32)
        e_t = jnp.where((srow >= lo_row) & (srow < hi_row), 1.0, 0.0)
        e_tb = e_t.astype(BF16)
        rank_of = _dot(jnp.concatenate([e_tb, e_tb], axis=1), rhs_rank)
        target = srow + 1.0 - jnp.sum(e_t * lo_row, axis=1, keepdims=True)
        perm = jnp.where(rank_of == target, 1.0, 0.0).astype(BF16)
        rows = pl.ds(ch * MOE_CH, MOE_CH)
        sorted_rows = _dot(perm, h_ext)
        buf[slot, rows, :D] = sorted_rows[:, :D].astype(BF16)
        w = jnp.sum((sorted_rows[:, D:D + 128] + sorted_rows[:, D + 128:]) * e_t, axis=1, keepdims=True)
        w_hi = w.astype(BF16).astype(F32)
        buf[slot, rows, D:] = jnp.where(lane_lo, w_hi, w - w_hi).astype(BF16)

    def copy_out(sl, src_row, dst_row, size):
        return pltpu.make_async_copy(buf.at[sl, pl.ds(pl.multiple_of(src_row, MOE_SEG), size)],
                                     xs_hbm.at[pl.ds(pl.multiple_of(dst_row, MOE_SEG), size)], sem.at[sl])

    def start_seg(e, carry):
        idx = i * 128 + e
        lo, dst = lo_s[idx], dst_s[idx]
        _for_each_piece(n_s[idx], lambda off, size: copy_out(slot, lo + off, dst + off, size).start())
        return carry

    lax.fori_loop(0, NE, start_seg, 0)

    @pl.when(i > 0)
    def _():
        _wait_pieces(pieces_s, i - 1, lambda size: copy_out(1 - slot, 0, 0, size))

    @pl.when(i == ntile - 1)
    def _():
        _wait_pieces(pieces_s, i, lambda size: copy_out(slot, 0, 0, size))

    experts_per_step = -(-NE // ntile)

    def tail_copy(e, off, size):
        return pltpu.make_async_copy(
            zbuf.at[pl.ds(0, size)], xs_hbm.at[pl.ds(pl.multiple_of(misc_s[e] + off, MOE_SEG), size)], tsem)

    def tails(step, start):
        for k in range(experts_per_step):
            e = step * experts_per_step + k

            @pl.when(e < NE)
            def _(e=e):
                n = misc_s[128 + jnp.minimum(e, NE - 1)]
                if start:
                    _for_each_piece(n, lambda off, size: tail_copy(e, off, size).start())
                else:
                    _for_each_piece(n, lambda off, size: tail_copy(e, off, size).wait())

    @pl.when(i == 0)
    def _():
        zbuf[...] = jnp.zeros_like(zbuf)

    @pl.when(i > 0)
    def _():
        tails(i - 1, start=False)

    tails(i, start=True)

    @pl.when(i == ntile - 1)
    def _():
        tails(i, start=False)


def _moe_dispatch(plan, h_all, comb_all, *, ntmax):
    dst, lo, n, _, misc, lof, pieces = plan
    ntile = h_all.shape[0] // MOE_TD
    grid_spec = pltpu.PrefetchScalarGridSpec(
        num_scalar_prefetch=5, grid=(ntile,),
        in_specs=[pl.BlockSpec((MOE_TD, D), lambda i, *_: (i, 0)),
                  pl.BlockSpec((MOE_TD, 128), lambda i, *_: (i, 0)),
                  pl.BlockSpec((None, 1, 128), lambda i, *_: (i, 0, 0)),
                  pl.BlockSpec((None, 1, 128), lambda i, *_: (i, 0, 0))],
        out_specs=pl.BlockSpec(memory_space=pl.ANY),
        scratch_shapes=[pltpu.VMEM((2, MOE_LS, MOE_XW), BF16), pltpu.VMEM((MOE_PIECES[0], MOE_XW), BF16),
                        pltpu.SemaphoreType.DMA((2,)), pltpu.SemaphoreType.DMA(())])
    return pl.pallas_call(
        functools.partial(_dispatch_kernel, ntile=ntile),
        grid_spec=grid_spec,
        out_shape=jax.ShapeDtypeStruct((ntmax * MOE_TR, MOE_XW), BF16),
        compiler_params=_cparams(("arbitrary",)),
        name="moe_dispatch",
    )(dst.reshape(-1), lo.reshape(-1), n.reshape(-1), misc.reshape(-1), pieces.reshape(-1), h_all, comb_all,
      lof.reshape(PLAN_ROWS, 1, 128), n.astype(F32).reshape(PLAN_ROWS, 1, 128))


def _ffn_kernel(te_s, misc_s, xs_hbm, wg_ref, wu_ref, wd_ref, ys_ref, wgb, wub, wdb, xbuf, xsem):
    j = pl.program_id(0)
    nused = misc_s[2 * 128]

    def fetch(t, slot):
        return pltpu.make_async_copy(xs_hbm.at[pl.ds(pl.multiple_of(t * MOE_TR, MOE_TR), MOE_TR)],
                                     xbuf.at[slot], xsem.at[slot])

    @pl.when(j == 0)
    def _():
        for t in range(MOE_XBUF - 1):
            @pl.when(t < nused)
            def _(t=t):
                fetch(t, t).start()

    @pl.when(j < nused)
    def _():
        ahead = j + (MOE_XBUF - 1)

        @pl.when(ahead < nused)
        def _():
            fetch(ahead, lax.rem(ahead, MOE_XBUF)).start()

        @pl.when((j == 0) | (te_s[j] != te_s[jnp.maximum(j - 1, 0)]))
        def _():
            wgb[...] = wg_ref[...].astype(BF16)
            wub[...] = wu_ref[...].astype(BF16)
            wdb[...] = wd_ref[...].astype(BF16)

        slot = lax.rem(j, MOE_XBUF)
        fetch(j, slot).wait()
        xs_ref = xbuf.at[slot]
        x = xs_ref[:, :D]
        w = xs_ref[:, D:D + 1].astype(F32) + xs_ref[:, D + 64:D + 65].astype(F32)
        hw = _silu(_dot(x, wgb[...])) * _dot(x, wub[...]) * w
        ys_ref[...] = _dot(hw.astype(BF16), wdb[...]).astype(BF16)


def _moe_ffn(plan, xs, wg, wu, wd, *, layer, ntmax):
    te, misc = plan[3], plan[4]

    def tile(j, misc_s):
        return jnp.minimum(j, misc_s[2 * 128] - 1)

    wspec = lambda shape: pl.BlockSpec((None, None) + shape,
                                       lambda j, te_s, misc_s: (layer, te_s[tile(j, misc_s)], 0, 0))
    grid_spec = pltpu.PrefetchScalarGridSpec(
        num_scalar_prefetch=2, grid=(ntmax,),
        in_specs=[pl.BlockSpec(memory_space=pl.ANY), wspec((D, DE)), wspec((D, DE)), wspec((DE, D))],
        out_specs=pl.BlockSpec((MOE_TR, D), lambda j, te_s, misc_s: (tile(j, misc_s), 0)),
        scratch_shapes=[pltpu.VMEM((D, DE), BF16), pltpu.VMEM((D, DE), BF16), pltpu.VMEM((DE, D), BF16),
                        pltpu.VMEM((MOE_XBUF, MOE_TR, MOE_XW), BF16), pltpu.SemaphoreType.DMA((MOE_XBUF,))])
    return pl.pallas_call(
        _ffn_kernel, grid_spec=grid_spec,
        out_shape=jax.ShapeDtypeStruct((ntmax * MOE_TR, D), BF16),
        compiler_params=_cparams(("arbitrary",)),
        name="moe_ffn",
    )(te[0], misc.reshape(-1), xs, wg, wu, wd)


def _combine_kernel(dst_s, lo_s, n_s, pieces_s, h_ref, comb_ref, lof_ref, nf_ref, sg_ref, su_ref, sd_ref, ys_hbm,
                    o_ref, ybuf, sem, *, ntile):
    i = pl.program_id(0)
    slot = lax.rem(i, 2)

    def copy_in(sl, src_row, dst_row, size):
        return pltpu.make_async_copy(ys_hbm.at[pl.ds(pl.multiple_of(src_row, MOE_SEG), size)],
                                     ybuf.at[sl, pl.ds(pl.multiple_of(dst_row, MOE_SEG), size)], sem.at[sl])

    def start_tile(tile, sl):
        def start_seg(e, carry):
            idx = tile * 128 + e
            lo, src = lo_s[idx], dst_s[idx]
            _for_each_piece(n_s[idx], lambda off, size: copy_in(sl, src + off, lo + off, size).start())
            return carry

        lax.fori_loop(0, NE, start_seg, 0)

    @pl.when(i == 0)
    def _():
        ybuf[...] = jnp.zeros_like(ybuf)
        start_tile(0, 0)

    @pl.when(i + 1 < ntile)
    def _():
        start_tile(i + 1, 1 - slot)

    lo_row = lof_ref[...]
    hi_row = lo_row + nf_ref[...]
    lo_col, hi_col = _col_of_row(lo_row), _col_of_row(hi_row)
    rp_hi, rp_lo = _split(_rank_plus(comb_ref[...]))
    l_hi, l_lo = _split(jnp.broadcast_to(lo_row * (1.0 / MOE_SEG), (8, 128)))
    lhs_meta = jnp.concatenate([jnp.concatenate([rp_hi, rp_lo], axis=1), jnp.concatenate([l_hi, l_lo], axis=1)], axis=0)

    h = h_ref[...]
    hs = _silu(_dot(h, sg_ref[...].astype(BF16))) * _dot(h, su_ref[...].astype(BF16))
    acc = _dot(hs.astype(BF16), sd_ref[...].astype(BF16))

    _wait_pieces(pieces_s, i, lambda size: copy_in(slot, 0, 0, size))
    for ch in range(MOE_LS // MOE_CH):
        scol = (_iota((1, MOE_CH), 1) + ch * MOE_CH).astype(F32)
        e_m = jnp.where((scol >= lo_col) & (scol < hi_col), 1.0, 0.0).astype(BF16)
        meta = _dot(lhs_meta, jnp.concatenate([e_m, e_m], axis=0))
        target = scol + 1.0 - meta[MOE_TD:MOE_TD + 1, :] * MOE_SEG
        perm_t = jnp.where(meta[:MOE_TD] == target, 1.0, 0.0).astype(BF16)
        acc = acc + _dot(perm_t, ybuf[slot, pl.ds(ch * MOE_CH, MOE_CH), :])
    o_ref[...] = acc


def _moe_combine(plan, ys, h_all, comb_all, sg, su, sd, *, layer):
    dst, lo, n, _, _, lof, pieces = plan
    ntile = h_all.shape[0] // MOE_TD
    shared = lambda w: pl.BlockSpec((None,) + w.shape[1:], lambda i, *_: (layer, 0, 0))
    grid_spec = pltpu.PrefetchScalarGridSpec(
        num_scalar_prefetch=4, grid=(ntile,),
        in_specs=[pl.BlockSpec((MOE_TD, D), lambda i, *_: (i, 0)),
                  pl.BlockSpec((MOE_TD, 128), lambda i, *_: (i, 0)),
                  pl.BlockSpec((None, 1, 128), lambda i, *_: (i, 0, 0)),
                  pl.BlockSpec((None, 1, 128), lambda i, *_: (i, 0, 0)),
                  shared(sg), shared(su), shared(sd),
                  pl.BlockSpec(memory_space=pl.ANY)],
        out_specs=pl.BlockSpec((MOE_TD, D), lambda i, *_: (i, 0)),
        scratch_shapes=[pltpu.VMEM((2, MOE_LS, D), BF16), pltpu.SemaphoreType.DMA((2,))])
    return pl.pallas_call(
        functools.partial(_combine_kernel, ntile=ntile),
        grid_spec=grid_spec,
        out_shape=jax.ShapeDtypeStruct((h_all.shape[0], D), F32),
        compiler_params=_cparams(("arbitrary",)),
        name="moe_combine",
    )(dst.reshape(-1), lo.reshape(-1), n.reshape(-1), pieces.reshape(-1), h_all, comb_all,
      lof.reshape(PLAN_ROWS, 1, 128), n.astype(F32).reshape(PLAN_ROWS, 1, 128), sg, su, sd, ys)


def _moe(h_all, comb_all, wg, wu, wd, sg, su, sd, *, layer, n_tokens):
    ntile = h_all.shape[0] // MOE_TD
    max_rows = n_tokens * TOPK + ntile * NE * (MOE_SEG - 1) + NE * (MOE_TR - 1)
    ntmax = -(-max_rows // MOE_TR)
    plan = _moe_plan(comb_all, ntmax=ntmax)
    xs = _moe_dispatch(plan, h_all, comb_all, ntmax=ntmax)
    ys = _moe_ffn(plan, xs, wg, wu, wd, layer=layer, ntmax=ntmax)
    return _moe_combine(plan, ys, h_all, comb_all, sg, su, sd, layer=layer)


def _final_kernel(x_ref, f_ref, pmod_ref, png_ref, o_ref):
    o_ref[...] = _residual_in(x_ref, f_ref, pmod_ref, png_ref)


def _final(x, f_all, pmods, png, *, nb, seq_len, tm, row0):
    nt = seq_len // tm
    t0 = row0 // tm
    return pl.pallas_call(
        _final_kernel,
        grid=(nb, nt),
        in_specs=[pl.BlockSpec((tm, D), lambda b, t: (b * nt + t, 0)),
                  pl.BlockSpec((tm, D), lambda b, t: (t0 + b * nt + t, 0)),
                  pl.BlockSpec((None,) + pmods.shape[1:], lambda b, t: (b, 0, 0, 0)),
                  pl.BlockSpec(png.shape, lambda b, t: (0, 0))],
        out_specs=pl.BlockSpec((tm, D), lambda b, t: (b * nt + t, 0)),
        out_shape=jax.ShapeDtypeStruct(x.shape, F32),
        compiler_params=_cparams(("arbitrary", "arbitrary")),
        name="final_residual",
    )(x, f_all, pmods, png)


def kernel(x_prompt, x_sample, c_prompt, c_sample, state_conv, state_shift, state_wkv, w_mod, b_mod, norm_g, ab_w_in, ab_conv_w, ab_mu, ab_w0, ab_w_decay_up, ab_a0, ab_a_up, ab_g_up, ab_k_k, ab_k_a, ab_r_k, ab_gn_g, ab_gn_b, ab_w_out, gm_w_in, gm_b_in, gm_ln_g, gm_ln_b, gm_w_s, gm_b_s, gm_w_out, moe_w_router, moe_b_router, moe_w_gate, moe_w_up, moe_w_down, moe_ws_gate, moe_ws_up, moe_ws_down):
    bsz, seq_len, _ = x_prompt.shape
    nsamp = x_sample.shape[0]
    nheads = DB // HEAD
    n_prompt = bsz * seq_len
    n_all = -(-(n_prompt + nsamp) // MOE_TD) * MOE_TD
    tm = 512

    xp = x_prompt.reshape(n_prompt, D)
    xs = x_sample.reshape(nsamp, D)

    m = _modulation(jnp.concatenate([c_prompt, c_sample], axis=0), w_mod, b_mod)
    mods_p = [m[l, :bsz].reshape(bsz, 6, 1, D) for l in range(2)]
    mods_s = [m[l, bsz:].reshape(nsamp, 6, D).transpose(1, 0, 2)[None] for l in range(2)]

    row = lambda t: t.reshape(1, -1)
    pad_rows = lambda t, lo, hi: jnp.pad(t, ((lo, hi), (0, 0)))
    small = [ab_conv_w[0], row(ab_mu[0]), row(ab_w0[0]),
             pad_rows(ab_w_decay_up[0], 0, 64).astype(BF16), row(ab_a0[0]),
             pad_rows(ab_a_up[0], 64, 0).astype(BF16), ab_g_up[0].astype(BF16),
             row(ab_k_k[0]), row(ab_k_a[0])]
    win0 = ab_w_in[0].astype(BF16)
    wo0 = ab_w_out[0].astype(BF16)
    rk, gng, gnb = row(ab_r_k[0]), row(ab_gn_g[0]), row(ab_gn_b[0])
    wrt = [moe_w_router[l].T for l in range(2)]
    br = [moe_b_router[l].reshape(NE, 1) for l in range(2)]

    h2_all = jnp.zeros((n_all, D), BF16)
    comb_all = jnp.zeros((n_all, 128), F32)

    ya, r, lw, k, v, kk, a, g, ctail, stail = _ab_pre(
        xp, mods_p[0], norm_g[0], win0, small, None, nb=bsz, seq_len=seq_len, tm=tm, seq_mode=True)
    conv_p = ctail.reshape(bsz, 8, DA)[:, 6:8][None]
    shift_p = stail.reshape(bsz, 8, PB)[:, 7][None]
    yb, sfin = _wkv_seq(r, lw, k, v, kk, a, g, rk, gng, gnb, nb=bsz, seq_len=seq_len)
    sfin = sfin.reshape(bsz, nheads // 2, 2, HEAD, 2, HEAD)
    wkv_p = jnp.stack([sfin[:, :, 0, :, 0, :], sfin[:, :, 1, :, 1, :]], axis=2)
    wkv_p = wkv_p.reshape(bsz, nheads, HEAD, HEAD).transpose(0, 1, 3, 2)[None]
    xp1, h2_all, comb_all = _ab_post(ya, yb, xp, mods_p[0], norm_g[0], wo0, wrt[0], br[0], h2_all, comb_all,
                                     nb=bsz, seq_len=seq_len, tm=tm, row0=0)

    prev = [state_conv[0, :, 0], state_conv[0, :, 1], state_shift[0]]
    ya_s, r, lw, k, v, kk, a, g, gated_s, pb_s = _ab_pre(
        xs, mods_s[0], norm_g[0], win0, small, prev, nb=1, seq_len=nsamp, tm=nsamp, seq_mode=False)
    conv_s = jnp.stack([state_conv[0, :, 1], gated_s], axis=1)[None]
    shift_s = pb_s[None]
    vecs = [t.reshape(nsamp * nheads, 1, HEAD) for t in (r, lw, k, v, kk, a, g)]
    params = [t.reshape(nheads, 1, HEAD) for t in (ab_r_k[0], ab_gn_g[0], ab_gn_b[0])]
    yb_s, snew = _wkv_step(state_wkv[0].reshape(nsamp * nheads, HEAD, HEAD), vecs, params, nheads=nheads)
    wkv_s = snew.reshape(1, nsamp, nheads, HEAD, HEAD)
    xs1, h2_all, comb_all = _ab_post(ya_s, yb_s.reshape(nsamp, DB).astype(BF16), xs, mods_s[0], norm_g[0], wo0,
                                     wrt[0], br[0], h2_all, comb_all, nb=1, seq_len=nsamp, tm=nsamp, row0=n_prompt)

    def moe_layer(l, h_all, c_all):
        return _moe(h_all, c_all, moe_w_gate, moe_w_up, moe_w_down,
                    moe_ws_gate, moe_ws_up, moe_ws_down, layer=l, n_tokens=n_prompt + nsamp)

    f0 = moe_layer(0, h2_all, comb_all)

    gw = [gm_w_in[0].astype(BF16), row(gm_b_in[0]), row(gm_ln_g[0]), row(gm_ln_b[0])]
    gw_seq = gw + [gm_w_s[0], gm_b_s[0].T, gm_w_out[0].astype(BF16), wrt[1], br[1]]
    gw_step = gw + [row(jnp.repeat(gm_w_s[0, :, 0, 0], CHUNK)), row(jnp.repeat(gm_b_s[0, :, 0], CHUNK)),
                    gm_w_out[0].astype(BF16), wrt[1], br[1]]
    tm1 = 512
    xp2, h2_all, comb_all = _gmlp(xp1, f0, mods_p[0], norm_g[0], mods_p[1], norm_g[1], gw_seq, h2_all, comb_all,
                                  nb=bsz, seq_len=seq_len, tm=tm1, row0=0, seq_mode=True)
    xs2, h2_all, comb_all, chunk_v = _gmlp(xs1, f0, mods_s[0], norm_g[0], mods_s[1], norm_g[1], gw_step, h2_all,
                                           comb_all, nb=1, seq_len=nsamp, tm=nsamp, row0=n_prompt, seq_mode=False)
    f1 = moe_layer(1, h2_all, comb_all)
    y_p = _final(xp2, f1, mods_p[1], norm_g[1], nb=bsz, seq_len=seq_len, tm=tm, row0=0)
    y_s = _final(xs2, f1, mods_s[1], norm_g[1], nb=1, seq_len=nsamp, tm=nsamp, row0=n_prompt)

    return (y_p.reshape(bsz, seq_len, D), y_s.reshape(nsamp, 1, D), conv_p, conv_s, shift_p, shift_s,
            wkv_p, wkv_s, chunk_v.reshape(1, nsamp, 1, D))
```

```python
import functools

import jax
import jax.numpy as jnp
from jax import lax
from jax.experimental import pallas as pl
from jax.experimental.pallas import tpu as pltpu

F32 = jnp.float32
BF16 = jnp.bfloat16

D = 1024
DA = 512
DB = 512
HEAD = 64
PB = 1792
PAB = 3 * DA + PB
NE = 64
NG = 8
EPG = NE // NG
TOPG = 4
TOPK = 8
DE = 256
ROUTE_SCALE = 2.5
EPS = 1e-6
GN_EPS = 64e-5
CHUNK = 128
WKV_SLAB = 128
WKV_CHUNK = 32
WKV_PAIRS_PER_STEP = 4
MOE_TD = 512
MOE_SEG = 16
MOE_LS = 5120
MOE_CH = 1024
MOE_PIECES = (64, 16)
MOE_TR = 1024
MOE_XW = D + 128
MOE_XBUF = 3
PLAN_ROWS = 128

VMEM_LIMIT = 56 * 1024 * 1024


def _cparams(sem):
    return pltpu.CompilerParams(dimension_semantics=sem, vmem_limit_bytes=VMEM_LIMIT)


def _dot(a, b, precision=None):
    return jnp.dot(a, b, preferred_element_type=F32, precision=precision)


def _dot_nt(a, b, precision=None):
    return lax.dot_general(a, b, (((1,), (1,)), ((), ())), preferred_element_type=F32, precision=precision)


def _iota(shape, dim):
    return lax.broadcasted_iota(jnp.int32, shape, dim)


def _rms(x, g):
    return x * lax.rsqrt(jnp.mean(x * x, axis=-1, keepdims=True) + EPS) * g


def _silu(x):
    return x * jax.nn.sigmoid(x)


def _seg_ones(n, seg):
    return (_iota((n, n), 0) // seg == _iota((n, n), 1) // seg).astype(F32)


def _mod_kernel(c_ref, w_ref, b_ref, o_ref):
    s = _silu(c_ref[...]).astype(BF16)
    o_ref[...] = _dot(s, w_ref[...].astype(BF16)) + b_ref[...]


def _modulation(c_all, w_mod, b_mod):
    depth, _, width = w_mod.shape
    rows = c_all.shape[0]
    bn = 512
    return pl.pallas_call(
        _mod_kernel,
        grid=(depth, width // bn),
        in_specs=[pl.BlockSpec((rows, D), lambda l, j: (0, 0)),
                  pl.BlockSpec((None, D, bn), lambda l, j: (l, 0, j)),
                  pl.BlockSpec((None, 1, bn), lambda l, j: (l, 0, j))],
        out_specs=pl.BlockSpec((None, rows, bn), lambda l, j: (l, 0, j)),
        out_shape=jax.ShapeDtypeStruct((depth, rows, width), F32),
        compiler_params=_cparams(("arbitrary", "arbitrary")),
        name="modulation",
    )(c_all, w_mod, b_mod.reshape(depth, 1, width))


def _residual_in(x_ref, f_ref, pmod_ref, png_ref):
    x = x_ref[...]
    if f_ref is None:
        return x
    return x + pmod_ref[5] * _rms(f_ref[...], png_ref[3:4, :])


def _route(h, wrt_ref, br_ref):
    tm = h.shape[0]
    logits = _mm(wrt_ref[...], h, 2, 2, nt=True)
    scores = jax.nn.sigmoid(logits)
    sel = scores + br_ref[...]
    sub = _iota((EPG, tm), 0).astype(F32)
    neg = jnp.float32(-jnp.inf)
    blks, sblks, gscore = [], [], []
    for g in range(NG):
        blk = sel[g * EPG:(g + 1) * EPG, :]
        m1 = jnp.max(blk, axis=0, keepdims=True)
        first = jnp.min(jnp.where(blk == m1, sub, float(EPG)), axis=0, keepdims=True)
        m2 = jnp.max(jnp.where(sub == first, neg, blk), axis=0, keepdims=True)
        blks.append(blk)
        sblks.append(scores[g * EPG:(g + 1) * EPG, :])
        gscore.append(m1 + m2)
    masked = []
    for g in range(NG):
        cnt = jnp.zeros((1, tm), F32)
        for g2 in range(NG):
            if g2 == g:
                continue
            beats = (gscore[g2] >= gscore[g]) if g2 < g else (gscore[g2] > gscore[g])
            cnt = cnt + jnp.where(beats, 1.0, 0.0)
        masked.append(jnp.where(cnt < TOPG, blks[g], neg))
    eidx = [sub + float(g * EPG) for g in range(NG)]
    chosen = [jnp.zeros((EPG, tm), F32) for _ in range(NG)]
    for _ in range(TOPK):
        best = masked[0]
        for g in range(1, NG):
            best = jnp.maximum(best, masked[g])
        best = jnp.max(best, axis=0, keepdims=True)
        first = jnp.where(masked[0] == best, eidx[0], float(NE))
        for g in range(1, NG):
            first = jnp.minimum(first, jnp.where(masked[g] == best, eidx[g], float(NE)))
        first = jnp.min(first, axis=0, keepdims=True)
        for g in range(NG):
            hit = eidx[g] == first
            chosen[g] = jnp.where(hit, 1.0, chosen[g])
            masked[g] = jnp.where(hit, neg, masked[g])
    wts = [jnp.where(chosen[g] > 0.0, sblks[g], 0.0) for g in range(NG)]
    tot = wts[0]
    for g in range(1, NG):
        tot = tot + wts[g]
    denom = jnp.sum(tot, axis=0, keepdims=True)
    comb_t = jnp.concatenate([w / denom * ROUTE_SCALE for w in wts] + [jnp.zeros((NE, tm), F32)], axis=0)
    return comb_t.T


def _post(y, x, mod_ref, ng_ref, wrt_ref, br_ref, xo_ref, h2_ref, comb_ref):
    xn = x + mod_ref[2] * _rms(y, ng_ref[1:2, :])
    xo_ref[...] = xn
    h2 = _rms(xn, ng_ref[2:3, :]) * (1.0 + mod_ref[4]) + mod_ref[3]
    h2_ref[...] = h2.astype(BF16)
    comb_ref[...] = _route(h2, wrt_ref, br_ref)


def _ab_pre_kernel(*refs, seq_mode, tm):
    it = iter(refs)
    x_ref, mod_ref, ng_ref, win_ref = next(it), next(it), next(it), next(it)
    cw_ref, mu_ref, w0_ref, wd_ref, a0_ref, wa_ref, wg_ref, kk_ref, ka_ref = (next(it) for _ in range(9))
    if not seq_mode:
        p2_ref, p1_ref, ps_ref = next(it), next(it), next(it)
    ya_ref, r_ref, lw_ref, k_ref, v_ref, kkn_ref, a_ref, g_ref, ctail_ref, stail_ref = (next(it) for _ in range(10))
    if seq_mode:
        cc_ref, sc_ref = next(it), next(it)

        @pl.when(pl.program_id(1) == 0)
        def _():
            cc_ref[...] = jnp.zeros_like(cc_ref)
            sc_ref[...] = jnp.zeros_like(sc_ref)

    x = x_ref[...]
    h = _rms(x, ng_ref[0:1, :]) * (1.0 + mod_ref[1]) + mod_ref[0]
    proj = _dot(h.astype(BF16), win_ref[...])
    a_h, a_b, a_c = proj[:, 0:DA], proj[:, DA:2 * DA], proj[:, 2 * DA:3 * DA]
    pb = proj[:, 3 * DA:]
    gated = a_c * a_h
    if seq_mode:
        rows = _iota((tm, 1), 0)
        g1 = jnp.where(rows == 0, cc_ref[7:8, :], pltpu.roll(gated, 1, 0))
        g2 = jnp.where(rows == 0, cc_ref[6:7, :], jnp.where(rows == 1, cc_ref[7:8, :], pltpu.roll(gated, 2, 0)))
        pprev = jnp.where(rows == 0, sc_ref[7:8, :], pltpu.roll(pb, 1, 0))
        cc_ref[...] = gated[tm - 8:, :]
        sc_ref[...] = pb[tm - 8:, :]
        ctail_ref[...] = gated[tm - 8:, :]
        stail_ref[...] = pb[tm - 8:, :]
    else:
        g1, g2, pprev = p1_ref[...], p2_ref[...], ps_ref[...]
        ctail_ref[...] = gated
        stail_ref[...] = pb
    conv = g2 * cw_ref[0:1, :] + g1 * cw_ref[1:2, :] + gated * cw_ref[2:3, :]
    ya_ref[...] = (a_b * conv).astype(BF16)

    xm = pb + (pprev - pb) * mu_ref[...]
    r, k, v = xm[:, 0:DB], xm[:, DB:2 * DB], xm[:, 2 * DB:3 * DB]
    lowrank = xm[:, 3 * DB:3 * DB + 128]
    dg = xm[:, 3 * DB + 128:]
    w_pre = w0_ref[...] + _dot(jnp.tanh(lowrank).astype(BF16), wd_ref[...])
    z = -w_pre
    softplus = jnp.maximum(z, 0.0) + jnp.log(1.0 + jnp.exp(-jnp.abs(z)))
    lw_ref[...] = -jnp.exp(-softplus - 0.5)
    a = jax.nn.sigmoid(a0_ref[...] + _dot(lowrank.astype(BF16), wa_ref[...]))
    g_ref[...] = _dot(jax.nn.sigmoid(dg).astype(BF16), wg_ref[...])
    kk = k * kk_ref[...]
    ss = _mm(kk * kk, _seg_ones(DB, HEAD), 2, 1)
    kkn_ref[...] = kk / jnp.maximum(jnp.sqrt(ss), 1e-12)
    r_ref[...] = r
    k_ref[...] = k * (1.0 + (a - 1.0) * ka_ref[...])
    v_ref[...] = v
    a_ref[...] = a


def _ab_pre(x, mods, ng, win, small, prev, *, nb, seq_len, tm, seq_mode):
    nt = seq_len // tm
    rows = nb * seq_len
    tail = 8 if seq_mode else tm
    full = lambda shape: pl.BlockSpec(shape, lambda b, t: tuple(0 for _ in shape))
    tok = lambda w: pl.BlockSpec((tm, w), lambda b, t: (b * nt + t, 0))
    in_specs = [tok(D),
                pl.BlockSpec((None,) + mods.shape[1:], lambda b, t: (b, 0, 0, 0)),
                full(ng.shape), full(win.shape)] + [full(s.shape) for s in small]
    args = [x, mods, ng, win] + list(small)
    if not seq_mode:
        in_specs += [tok(DA), tok(DA), tok(PB)]
        args += list(prev)
    out_shape = ([jax.ShapeDtypeStruct((rows, DA), BF16)] + [jax.ShapeDtypeStruct((rows, DB), F32)] * 7
                 + [jax.ShapeDtypeStruct((nb * tail, DA), F32), jax.ShapeDtypeStruct((nb * tail, PB), F32)])
    out_specs = ([tok(DA)] + [tok(DB)] * 7
                 + [pl.BlockSpec((tail, DA), lambda b, t: (b, 0)), pl.BlockSpec((tail, PB), lambda b, t: (b, 0))])
    scratch = [pltpu.VMEM((8, DA), F32), pltpu.VMEM((8, PB), F32)] if seq_mode else []
    return pl.pallas_call(
        functools.partial(_ab_pre_kernel, seq_mode=seq_mode, tm=tm),
        grid=(nb, nt), in_specs=in_specs, out_specs=out_specs, out_shape=out_shape,
        scratch_shapes=scratch,
        compiler_params=_cparams(("arbitrary", "arbitrary")),
        name="ab_pre_seq" if seq_mode else "ab_pre_step",
    )(*args)


def _split(x):
    hi = x.astype(BF16)
    return hi, (x - hi.astype(F32)).astype(BF16)


def _mm(a, b, pa=1, pb=1, nt=False):
    ah, al = _split(a) if pa == 2 else (a.astype(BF16), None)
    bh, bl = _split(b) if pb == 2 else (b.astype(BF16), None)
    bx = 1 if nt else 0
    if pa == 2 and pb == 2:
        ah, bh = jnp.concatenate([ah, al, ah], axis=1), jnp.concatenate([bh, bh, bl], axis=bx)
    elif pa == 2:
        ah, bh = jnp.concatenate([ah, al], axis=1), jnp.concatenate([bh, bh], axis=bx)
    elif pb == 2:
        ah, bh = jnp.concatenate([ah, ah], axis=1), jnp.concatenate([bh, bl], axis=bx)
    return _dot_nt(ah, bh) if nt else _dot(ah, bh)


def _gn_gate(o, r, k, v, g, rk, gng, gnb, seg):
    mean = _mm(o, seg, 2, 1) * (1.0 / HEAD)
    d = o - mean
    var = _mm(d * d, seg, 2, 1) * (1.0 / HEAD)
    out = d * lax.rsqrt(var + GN_EPS) * gng + gnb
    out = out + _mm(r * k * rk, seg, 2, 1) * v
    return out * g


def _wkv_pair(r, lw, k, v, kk, a, st, cst):
    ts, c = WKV_SLAB, WKV_CHUNK
    nchunk = ts // c
    strict, incl, cum_lhs, m0, m1, headdiag, eye, eye2, blks, chunk_cols = cst

    l1 = lw.astype(BF16)
    rem = lw - l1.astype(F32)
    l2 = rem.astype(BF16)
    l3 = (rem - l2.astype(F32)).astype(BF16)
    cums = _dot(cum_lhs, jnp.concatenate([l1, l2, l3], axis=0))
    yield
    cl, ctot = cums[:ts], cums[ts:]
    e_neg = jnp.exp(-cl)
    e_end = jnp.exp(ctot - cl)
    at = -kk * jnp.exp(cl - lw)
    rt = r * jnp.exp(cl)
    beta = kk * a
    bk = jnp.concatenate([beta * e_neg, k * e_neg], axis=0)
    bend, kend = beta * e_end, k * e_end

    at_st = jnp.concatenate([at * m0, at * m1], axis=0)
    a_all = _mm(jnp.concatenate([at_st, rt * m0, rt * m1], axis=0), bk, nt=True)
    yield
    a_st, a_out = a_all[:2 * ts], a_all[2 * ts:]
    zero = jnp.zeros((ts, ts), F32)

    def two_heads(p0, p1):
        return jnp.concatenate([jnp.concatenate([p0, zero], axis=1), jnp.concatenate([zero, p1], axis=1)], axis=0)

    lbd = two_heads(a_st[:ts, :ts] * strict, a_st[ts:, :ts] * strict)
    d1 = jnp.where(blks[0], lbd, 0.0)
    tinv = eye2 + d1
    d2 = _mm(d1, d1)
    yield
    tinv = tinv + _mm(tinv, d2)
    d4 = _mm(d2, d2)
    yield
    tinv = tinv + _mm(tinv, d4)
    yield
    for lvl in range(1, len(blks)):
        off = jnp.where(blks[lvl] & jnp.logical_not(blks[lvl - 1]), lbd, 0.0)
        half = _mm(tinv, off)
        yield
        tinv = tinv + _mm(half, tinv)
        yield

    v_st = jnp.concatenate([v * m0, v * m1], axis=0)
    y_st = _mm(two_heads(a_st[:ts, ts:] * strict, a_st[ts:, ts:] * strict), v_st)
    yield
    wu = _mm(tinv, jnp.concatenate([at_st, y_st], axis=1))
    yield
    w = wu[:ts, :2 * HEAD] + wu[ts:, :2 * HEAD]
    u = wu[:ts, 2 * HEAD:] + wu[ts:, 2 * HEAD:]
    rhs = jnp.concatenate([jnp.concatenate([w, u], axis=1),
                           jnp.concatenate([jnp.zeros_like(v), v], axis=1)], axis=0)
    bend_t, kend_t = bend.T, kend.T
    lhs = [jnp.concatenate([bend_t * cm, kend_t * cm], axis=1) for cm in chunk_cols]
    lhs.append(jnp.concatenate([a_out[:ts, :ts] * incl, a_out[:ts, ts:] * incl], axis=1))
    lhs.append(jnp.concatenate([a_out[ts:, :ts] * incl, a_out[ts:, ts:] * incl], axis=1))
    big = _mm(jnp.concatenate(lhs, axis=0), rhs)
    yield
    mn_all, qo = big[:nchunk * ts], big[nchunk * ts:]
    q = rt + qo[:ts, :2 * HEAD] * m0 + qo[ts:, :2 * HEAD] * m1
    olocal = qo[:ts, 2 * HEAD:] * m0 + qo[ts:, 2 * HEAD:] * m1

    outs = []
    for ch in range(nchunk):
        mn = mn_all[ch * ts:(ch + 1) * ts]
        trans = eye * jnp.exp(ctot[ch * c:ch * c + 1, :]) + mn[:, :2 * HEAD] * headdiag
        both = _mm(jnp.concatenate([q[ch * c:(ch + 1) * c], trans], axis=0), st)
        yield
        outs.append(both[:c] + olocal[ch * c:(ch + 1) * c])
        st = both[c:] + mn[:, 2 * HEAD:] * headdiag
    return jnp.concatenate(outs, axis=0), st


def _round_robin(gens):
    results = [None] * len(gens)
    active = list(enumerate(gens))
    while active:
        still = []
        for i, gen in active:
            try:
                next(gen)
                still.append((i, gen))
            except StopIteration as stop:
                results[i] = stop.value
        active = still
    return results


def _wkv_seq_kernel(r_ref, lw_ref, k_ref, v_ref, kk_ref, a_ref, g_ref, rk_ref, gng_ref, gnb_ref,
                    y_ref, sfin_ref, st_ref, *, npp):
    ts, c = WKV_SLAB, WKV_CHUNK

    @pl.when(pl.program_id(2) == 0)
    def _():
        st_ref[...] = jnp.zeros_like(st_ref)

    ri, ci = _iota((ts, ts), 0), _iota((ts, ts), 1)
    same = ri // c == ci // c
    strict = (same & (ci < ri)).astype(F32)
    incl = (same & (ci <= ri)).astype(F32)
    cum_lhs = jnp.concatenate([incl, same.astype(F32)], axis=0).astype(BF16)
    cum_lhs = jnp.concatenate([cum_lhs] * 3, axis=1)
    lane = _iota((1, 2 * HEAD), 1)
    m0 = (lane < HEAD).astype(F32)
    m1 = 1.0 - m0
    headdiag = _seg_ones(2 * HEAD, HEAD)
    eye = (_iota((2 * HEAD, 2 * HEAD), 0) == _iota((2 * HEAD, 2 * HEAD), 1)).astype(F32)
    r2, c2 = _iota((2 * ts, 2 * ts), 0), _iota((2 * ts, 2 * ts), 1)
    eye2 = (r2 == c2).astype(F32)
    blks, b = [], 8
    while b <= c:
        blks.append(r2 // b == c2 // b)
        b *= 2
    tcol = _iota((1, ts), 1) // c
    chunk_cols = [(tcol == ch).astype(F32) for ch in range(ts // c)]
    cst = (strict, incl, cum_lhs, m0, m1, headdiag, eye, eye2, blks, chunk_cols)

    sls = [slice(p * 2 * HEAD, (p + 1) * 2 * HEAD) for p in range(npp)]
    res = _round_robin([_wkv_pair(r_ref[:, sl], lw_ref[:, sl], k_ref[:, sl], v_ref[:, sl], kk_ref[:, sl],
                                  a_ref[:, sl], st_ref[p], cst) for p, sl in enumerate(sls)])
    for p, sl in enumerate(sls):
        o, st = res[p]
        st_ref[p] = st
        y_ref[:, sl] = _gn_gate(o, r_ref[:, sl], k_ref[:, sl], v_ref[:, sl], g_ref[:, sl], rk_ref[:, sl],
                                gng_ref[:, sl], gnb_ref[:, sl], headdiag).astype(BF16)

    @pl.when(pl.program_id(2) == pl.num_programs(2) - 1)
    def _():
        sfin_ref[...] = st_ref[...]


def _wkv_seq(r, lw, k, v, kk, a, g, rk, gng, gnb, *, nb, seq_len):
    ns = seq_len // WKV_SLAB
    npp = WKV_PAIRS_PER_STEP
    width = 2 * HEAD * npp
    tok = pl.BlockSpec((WKV_SLAB, width), lambda b, p, s: (b * ns + s, p))
    par = pl.BlockSpec((1, width), lambda b, p, s: (0, p))
    return pl.pallas_call(
        functools.partial(_wkv_seq_kernel, npp=npp),
        grid=(nb, DB // width, ns),
        in_specs=[tok] * 7 + [par] * 3,
        out_specs=[tok, pl.BlockSpec((None, npp, 2 * HEAD, 2 * HEAD), lambda b, p, s: (b, p, 0, 0))],
        out_shape=[jax.ShapeDtypeStruct((nb * seq_len, DB), BF16),
                   jax.ShapeDtypeStruct((nb, DB // (2 * HEAD), 2 * HEAD, 2 * HEAD), F32)],
        scratch_shapes=[pltpu.VMEM((npp, 2 * HEAD, 2 * HEAD), F32)],
        compiler_params=_cparams(("arbitrary", "arbitrary", "arbitrary")),
        name="wkv_seq",
    )(r, lw, k, v, kk, a, g, rk, gng, gnb)


def _wkv_step_kernel(s_ref, r_ref, lw_ref, k_ref, v_ref, kk_ref, a_ref, g_ref, rk_ref, gng_ref, gnb_ref,
                     y_ref, so_ref):
    s = s_ref[...]
    r, k, v, kk, a = r_ref[...], k_ref[...], v_ref[...], kk_ref[...], a_ref[...]
    w = jnp.exp(lw_ref[...])
    eye = (_iota((1, HEAD, HEAD), 1) == _iota((1, HEAD, HEAD), 2)).astype(F32)
    v_col = jnp.sum(eye * v, axis=2, keepdims=True)
    s_kk = jnp.sum(s * kk, axis=2, keepdims=True)
    s = s * w - s_kk * (kk * a) + v_col * k
    so_ref[...] = s
    o_col = jnp.sum(s * r, axis=2, keepdims=True)
    o = jnp.sum(eye * o_col, axis=1, keepdims=True)
    mean = jnp.mean(o, axis=2, keepdims=True)
    d = o - mean
    var = jnp.mean(d * d, axis=2, keepdims=True)
    out = d * lax.rsqrt(var + GN_EPS) * gng_ref[...] + gnb_ref[...]
    out = out + jnp.sum(r * k * rk_ref[...], axis=2, keepdims=True) * v
    y_ref[...] = out * g_ref[...]


def _wkv_step(state, vecs, params, *, nheads):
    n = state.shape[0]
    tb = 8 * nheads
    sspec = pl.BlockSpec((tb, HEAD, HEAD), lambda i: (i, 0, 0))
    vspec = pl.BlockSpec((tb, 1, HEAD), lambda i: (i, 0, 0))
    pspec = pl.BlockSpec((tb, 1, HEAD), lambda i: (0, 0, 0))
    reps = tb // nheads
    params = [jnp.tile(p, (reps, 1, 1)) for p in params]
    return pl.pallas_call(
        _wkv_step_kernel,
        grid=(n // tb,),
        in_specs=[sspec] + [vspec] * 7 + [pspec] * 3,
        out_specs=[vspec, sspec],
        out_shape=[jax.ShapeDtypeStruct((n, 1, HEAD), F32), jax.ShapeDtypeStruct((n, HEAD, HEAD), F32)],
        compiler_params=_cparams(("arbitrary",)),
        name="wkv_step",
    )(state, *vecs, *params)


def _ab_post_kernel(ya_ref, yb_ref, x_ref, mod_ref, ng_ref, wo_ref, wrt_ref, br_ref,
                    h2in_ref, combin_ref, xo_ref, h2_ref, comb_ref):
    del h2in_ref, combin_ref
    y = _dot(ya_ref[...], wo_ref[0:DA, :]) + _dot(yb_ref[...], wo_ref[DA:, :])
    _post(y, x_ref[...], mod_ref, ng_ref, wrt_ref, br_ref, xo_ref, h2_ref, comb_ref)


def _ab_post(ya, yb, x, mods, ng, wo, wrt, br, h2_all, comb_all, *, nb, seq_len, tm, row0):
    nt = seq_len // tm
    t0 = row0 // tm
    full = lambda shape: pl.BlockSpec(shape, lambda b, t: tuple(0 for _ in shape))
    tok = lambda w: pl.BlockSpec((tm, w), lambda b, t: (b * nt + t, 0))
    tok_all = lambda w: pl.BlockSpec((tm, w), lambda b, t: (t0 + b * nt + t, 0))
    anyspec = pl.BlockSpec(memory_space=pl.ANY)
    return pl.pallas_call(
        _ab_post_kernel,
        grid=(nb, nt),
        in_specs=[tok(DA), tok(DB), tok(D),
                  pl.BlockSpec((None,) + mods.shape[1:], lambda b, t: (b, 0, 0, 0)),
                  full(ng.shape), full(wo.shape), full(wrt.shape), full(br.shape), anyspec, anyspec],
        out_specs=[tok(D), tok_all(D), tok_all(128)],
        out_shape=[jax.ShapeDtypeStruct(x.shape, F32), jax.ShapeDtypeStruct(h2_all.shape, BF16),
                   jax.ShapeDtypeStruct(comb_all.shape, F32)],
        input_output_aliases={8: 1, 9: 2},
        compiler_params=_cparams(("arbitrary", "arbitrary")),
        name="ab_post",
    )(ya, yb, x, mods, ng, wo, wrt, br, h2_all, comb_all)


def _gmlp_kernel(*refs, seq_mode, tm):
    it = iter(refs)
    x_ref, f_ref, pmod_ref, png_ref, mod_ref, ng_ref = (next(it) for _ in range(6))
    win_ref, bin_ref, lng_ref, lnb_ref, ws_ref, bs_ref, wo_ref, wrt_ref, br_ref = (next(it) for _ in range(9))
    next(it), next(it)
    xo_ref, h2_ref, comb_ref = next(it), next(it), next(it)
    if not seq_mode:
        cv_ref = next(it)
    x = _residual_in(x_ref, f_ref, pmod_ref, png_ref)
    h = _rms(x, ng_ref[0:1, :]) * (1.0 + mod_ref[1]) + mod_ref[0]
    z = jax.nn.gelu(_dot(h.astype(BF16), win_ref[...]) + bin_ref[...])
    u, v = z[:, :D], z[:, D:]
    mu = jnp.mean(v, axis=-1, keepdims=True)
    var = jnp.mean(jnp.square(v - mu), axis=-1, keepdims=True)
    v = (v - mu) * lax.rsqrt(var + EPS) * lng_ref[...] + lnb_ref[...]
    if seq_mode:
        vb = v.astype(BF16)
        causal = _iota((CHUNK, CHUNK), 1) <= _iota((CHUNK, CHUNK), 0)
        cols = []
        for g in range(D // CHUNK):
            wsg = jnp.where(causal, ws_ref[g], 0.0).astype(BF16)
            bsg = bs_ref[:, g:g + 1]
            rows = [_dot(wsg, vb[c * CHUNK:(c + 1) * CHUNK, g * CHUNK:(g + 1) * CHUNK]) + bsg
                    for c in range(tm // CHUNK)]
            cols.append(jnp.concatenate(rows, axis=0) if len(rows) > 1 else rows[0])
        s = jnp.concatenate(cols, axis=1)
    else:
        cv_ref[...] = v
        s = v * ws_ref[...] + bs_ref[...]
    y = _dot((u * s).astype(BF16), wo_ref[...])
    _post(y, x, mod_ref, ng_ref, wrt_ref, br_ref, xo_ref, h2_ref, comb_ref)


def _gmlp(x, f_all, pmods, png, mods, ng, weights, h2_all, comb_all, *, nb, seq_len, tm, row0, seq_mode):
    nt = seq_len // tm
    t0 = row0 // tm
    full = lambda shape: pl.BlockSpec(shape, lambda b, t: tuple(0 for _ in shape))
    tok = lambda w: pl.BlockSpec((tm, w), lambda b, t: (b * nt + t, 0))
    tok_all = lambda w: pl.BlockSpec((tm, w), lambda b, t: (t0 + b * nt + t, 0))
    modspec = lambda m: pl.BlockSpec((None,) + m.shape[1:], lambda b, t: (b, 0, 0, 0))
    anyspec = pl.BlockSpec(memory_space=pl.ANY)
    in_specs = ([tok(D), tok_all(D), modspec(pmods), full(png.shape), modspec(mods), full(ng.shape)]
                + [full(w.shape) for w in weights] + [anyspec, anyspec])
    out_specs = [tok(D), tok_all(D), tok_all(128)]
    out_shape = [jax.ShapeDtypeStruct(x.shape, F32), jax.ShapeDtypeStruct(h2_all.shape, BF16),
                 jax.ShapeDtypeStruct(comb_all.shape, F32)]
    if not seq_mode:
        out_specs.append(tok(D))
        out_shape.append(jax.ShapeDtypeStruct(x.shape, F32))
    n_in = len(in_specs)
    return pl.pallas_call(
        functools.partial(_gmlp_kernel, seq_mode=seq_mode, tm=tm),
        grid=(nb, nt), in_specs=in_specs, out_specs=out_specs, out_shape=out_shape,
        input_output_aliases={n_in - 2: 1, n_in - 1: 2},
        compiler_params=_cparams(("arbitrary", "arbitrary")),
        name="gmlp_seq" if seq_mode else "gmlp_step",
    )(x, f_all, pmods, png, mods, ng, *weights, h2_all, comb_all)


def _strict_lower(n):
    return _iota((n, n), 1) < _iota((n, n), 0)


def _col_of_row(row):
    return jnp.broadcast_to(row, (128, 128)).T[:, 0:1]


def _plan_kernel(comb_ref, dst_ref, lo_ref, n_ref, te_ref, misc_ref, lof_ref, pieces_ref, cnt_ref, *, ntile, te_cols):
    i = pl.program_id(0)

    @pl.when(i == 0)
    def _():
        cnt_ref[...] = jnp.zeros_like(cnt_ref)

    sel = jnp.where(comb_ref[...] > 0.0, 1.0, 0.0)
    cnt_ref[pl.ds(i, 1), :] = jnp.sum(sel, axis=0, keepdims=True)

    @pl.when(i == ntile - 1)
    def _():
        c = cnt_ref[...]
        cpad = jnp.floor((c + (MOE_SEG - 1.0)) * (1.0 / MOE_SEG)) * MOE_SEG
        cb = cpad.astype(BF16)
        lower = jnp.where(_strict_lower(PLAN_ROWS), 1.0, 0.0).astype(BF16)
        upper = jnp.where(_iota((128, 128), 0) < _iota((128, 128), 1), 1.0, 0.0).astype(BF16)
        before = _dot(lower, cb)
        lo = _dot(cb, upper)
        tot = jnp.sum(cpad, axis=0, keepdims=True)
        rt = jnp.floor((tot + (MOE_TR - 1.0)) * (1.0 / MOE_TR))
        gt = _dot(jnp.broadcast_to(rt, (8, 128)).astype(BF16), upper)[0:1, :]
        dst_ref[...] = (gt * MOE_TR + before).astype(jnp.int32)
        lo_ref[...] = lo.astype(jnp.int32)
        lof_ref[...] = lo
        n_ref[...] = cpad.astype(jnp.int32)
        ends_col = _col_of_row(gt + rt)
        jrow = _iota((1, te_cols), 1).astype(F32)
        te = jnp.sum(jnp.where(ends_col <= jrow, 1.0, 0.0), axis=0, keepdims=True)
        te_ref[...] = jnp.broadcast_to(jnp.minimum(te, NE - 1.0), (8, te_cols)).astype(jnp.int32)
        nused = jnp.sum(rt, axis=1, keepdims=True)
        rowi = _iota((8, 128), 0)
        misc = jnp.where(rowi == 0, gt * MOE_TR + tot, jnp.where(rowi == 1, rt * MOE_TR - tot, nused))
        misc_ref[...] = misc.astype(jnp.int32)
        lane = _iota((1, 128), 1)
        pieces = jnp.zeros((PLAN_ROWS, 128), F32)
        taken = jnp.zeros_like(cpad)
        for k, size in enumerate(MOE_PIECES):
            cnt = jnp.floor((cpad - taken) * (1.0 / size))
            taken = taken + cnt * size
            pieces = pieces + jnp.where(lane == k, jnp.sum(cnt, axis=1, keepdims=True), 0.0)
        pieces_ref[...] = pieces.astype(jnp.int32)


def _moe_plan(comb_all, *, ntmax):
    ntile = comb_all.shape[0] // MOE_TD
    te_cols = -(-ntmax // 128) * 128
    full = lambda shape: pl.BlockSpec(shape, lambda i: tuple(0 for _ in shape))
    shapes = [((PLAN_ROWS, 128), jnp.int32)] * 3 + [((8, te_cols), jnp.int32), ((8, 128), jnp.int32),
                                                   ((PLAN_ROWS, 128), F32), ((PLAN_ROWS, 128), jnp.int32)]
    return pl.pallas_call(
        functools.partial(_plan_kernel, ntile=ntile, te_cols=te_cols),
        grid=(ntile,),
        in_specs=[pl.BlockSpec((MOE_TD, 128), lambda i: (i, 0))],
        out_specs=[full(s) for s, _ in shapes],
        out_shape=[jax.ShapeDtypeStruct(s, d) for s, d in shapes],
        scratch_shapes=[pltpu.VMEM((PLAN_ROWS, 128), F32)],
        compiler_params=_cparams(("arbitrary",)),
        name="moe_plan",
    )(comb_all)


def _rank_plus(comb):
    sel = jnp.where(comb > 0.0, 1.0, 0.0)
    lower = jnp.where(_strict_lower(MOE_TD), 1.0, 0.0).astype(BF16)
    return sel * (_dot(lower, sel.astype(BF16)) + 1.0)


def _for_each_piece(n, fn):
    off = 0
    for size in MOE_PIECES:
        cnt = lax.shift_right_logical(n - off, size.bit_length() - 1)

        def body(p, carry, off=off, size=size):
            fn(off + p * size, size)
            return carry

        lax.fori_loop(0, cnt, body, 0)
        off = off + cnt * size


def _wait_pieces(pieces_s, tile, make_copy):
    for k, size in enumerate(MOE_PIECES):
        def wait_one(p, carry, size=size):
            make_copy(size).wait()
            return carry

        lax.fori_loop(0, pieces_s[tile * 128 + k], wait_one, 0)


def _dispatch_kernel(dst_s, lo_s, n_s, misc_s, pieces_s, h_ref, comb_ref, lof_ref, nf_ref, xs_hbm, buf, zbuf, sem,
                     tsem, *, ntile):
    i = pl.program_id(0)
    slot = lax.rem(i, 2)
    comb = comb_ref[...]
    lo_row = lof_ref[...]
    hi_row = lo_row + nf_ref[...]
    rp_hi, rp_lo = _split(_rank_plus(comb).T)
    rhs_rank = jnp.concatenate([rp_hi, rp_lo], axis=0)
    c_hi, c_lo = _split(comb)
    h_ext = jnp.concatenate([h_ref[...], c_hi, c_lo], axis=1)
    lane_lo = _iota((1, 128), 1) < 64
    for ch in range(MOE_LS // MOE_CH):
        srow = (_iota((MOE_CH, 1), 0) + ch * MOE_CH).astype(F32)
        e_t = jnp.where((srow >= lo_row) & (srow < hi_row), 1.0, 0.0)
        e_tb = e_t.astype(BF16)
        rank_of = _dot(jnp.concatenate([e_tb, e_tb], axis=1), rhs_rank)
        target = srow + 1.0 - jnp.sum(e_t * lo_row, axis=1, keepdims=True)
        perm = jnp.where(rank_of == target, 1.0, 0.0).astype(BF16)
        rows = pl.ds(ch * MOE_CH, MOE_CH)
        sorted_rows = _dot(perm, h_ext)
        buf[slot, rows, :D] = sorted_rows[:, :D].astype(BF16)
        w = jnp.sum((sorted_rows[:, D:D + 128] + sorted_rows[:, D + 128:]) * e_t, axis=1, keepdims=True)
        w_hi = w.astype(BF16).astype(F32)
        buf[slot, rows, D:] = jnp.where(lane_lo, w_hi, w - w_hi).astype(BF16)

    def copy_out(sl, src_row, dst_row, size):
        return pltpu.make_async_copy(buf.at[sl, pl.ds(pl.multiple_of(src_row, MOE_SEG), size)],
                                     xs_hbm.at[pl.ds(pl.multiple_of(dst_row, MOE_SEG), size)], sem.at[sl])

    def start_seg(e, carry):
        idx = i * 128 + e
        lo, dst = lo_s[idx], dst_s[idx]
        _for_each_piece(n_s[idx], lambda off, size: copy_out(slot, lo + off, dst + off, size).start(
            priority=MOE_PIECES.index(size) % 2))
        return carry

    lax.fori_loop(0, NE, start_seg, 0)

    @pl.when(i > 0)
    def _():
        _wait_pieces(pieces_s, i - 1, lambda size: copy_out(1 - slot, 0, 0, size))

    @pl.when(i == ntile - 1)
    def _():
        _wait_pieces(pieces_s, i, lambda size: copy_out(slot, 0, 0, size))

    experts_per_step = -(-NE // ntile)

    def tail_copy(e, off, size):
        return pltpu.make_async_copy(
            zbuf.at[pl.ds(0, size)], xs_hbm.at[pl.ds(pl.multiple_of(misc_s[e] + off, MOE_SEG), size)], tsem)

    def tails(step, start):
        for k in range(experts_per_step):
            e = step * experts_per_step + k

            @pl.when(e < NE)
            def _(e=e):
                n = misc_s[128 + jnp.minimum(e, NE - 1)]
                if start:
                    _for_each_piece(n, lambda off, size: tail_copy(e, off, size).start())
                else:
                    _for_each_piece(n, lambda off, size: tail_copy(e, off, size).wait())

    @pl.when(i == 0)
    def _():
        zbuf[...] = jnp.zeros_like(zbuf)

    @pl.when(i > 0)
    def _():
        tails(i - 1, start=False)

    tails(i, start=True)

    @pl.when(i == ntile - 1)
    def _():
        tails(i, start=False)


def _moe_dispatch(plan, h_all, comb_all, *, ntmax):
    dst, lo, n, _, misc, lof, pieces = plan
    ntile = h_all.shape[0] // MOE_TD
    grid_spec = pltpu.PrefetchScalarGridSpec(
        num_scalar_prefetch=5, grid=(ntile,),
        in_specs=[pl.BlockSpec((MOE_TD, D), lambda i, *_: (i, 0)),
                  pl.BlockSpec((MOE_TD, 128), lambda i, *_: (i, 0)),
                  pl.BlockSpec((None, 1, 128), lambda i, *_: (i, 0, 0)),
                  pl.BlockSpec((None, 1, 128), lambda i, *_: (i, 0, 0))],
        out_specs=pl.BlockSpec(memory_space=pl.ANY),
        scratch_shapes=[pltpu.VMEM((2, MOE_LS, MOE_XW), BF16), pltpu.VMEM((MOE_PIECES[0], MOE_XW), BF16),
                        pltpu.SemaphoreType.DMA((2,)), pltpu.SemaphoreType.DMA(())])
    return pl.pallas_call(
        functools.partial(_dispatch_kernel, ntile=ntile),
        grid_spec=grid_spec,
        out_shape=jax.ShapeDtypeStruct((ntmax * MOE_TR, MOE_XW), BF16),
        compiler_params=_cparams(("arbitrary",)),
        name="moe_dispatch",
    )(dst.reshape(-1), lo.reshape(-1), n.reshape(-1), misc.reshape(-1), pieces.reshape(-1), h_all, comb_all,
      lof.reshape(PLAN_ROWS, 1, 128), n.astype(F32).reshape(PLAN_ROWS, 1, 128))


def _ffn_kernel(te_s, misc_s, xs_hbm, wg_ref, wu_ref, wd_ref, ys_ref, wgb, wub, wdb, xbuf, xsem):
    j = pl.program_id(0)
    nused = misc_s[2 * 128]

    def fetch(t, slot):
        return pltpu.make_async_copy(xs_hbm.at[pl.ds(pl.multiple_of(t * MOE_TR, MOE_TR), MOE_TR)],
                                     xbuf.at[slot], xsem.at[slot])

    @pl.when(j == 0)
    def _():
        for t in range(MOE_XBUF - 1):
            @pl.when(t < nused)
            def _(t=t):
                fetch(t, t).start()

    @pl.when(j < nused)
    def _():
        ahead = j + (MOE_XBUF - 1)

        @pl.when(ahead < nused)
        def _():
            fetch(ahead, lax.rem(ahead, MOE_XBUF)).start()

        @pl.when((j == 0) | (te_s[j] != te_s[jnp.maximum(j - 1, 0)]))
        def _():
            wgb[...] = wg_ref[...].astype(BF16)
            wub[...] = wu_ref[...].astype(BF16)
            wdb[...] = wd_ref[...].astype(BF16)

        slot = lax.rem(j, MOE_XBUF)
        fetch(j, slot).wait()
        xs_ref = xbuf.at[slot]
        x = xs_ref[:, :D]
        w = xs_ref[:, D:D + 1].astype(F32) + xs_ref[:, D + 64:D + 65].astype(F32)
        hw = _silu(_dot(x, wgb[...])) * _dot(x, wub[...]) * w
        ys_ref[...] = _dot(hw.astype(BF16), wdb[...]).astype(BF16)


def _moe_ffn(plan, xs, wg, wu, wd, *, layer, ntmax):
    te, misc = plan[3], plan[4]

    def tile(j, misc_s):
        return jnp.minimum(j, misc_s[2 * 128] - 1)

    wspec = lambda shape: pl.BlockSpec((None, None) + shape,
                                       lambda j, te_s, misc_s: (layer, te_s[tile(j, misc_s)], 0, 0))
    grid_spec = pltpu.PrefetchScalarGridSpec(
        num_scalar_prefetch=2, grid=(ntmax,),
        in_specs=[pl.BlockSpec(memory_space=pl.ANY), wspec((D, DE)), wspec((D, DE)), wspec((DE, D))],
        out_specs=pl.BlockSpec((MOE_TR, D), lambda j, te_s, misc_s: (tile(j, misc_s), 0)),
        scratch_shapes=[pltpu.VMEM((D, DE), BF16), pltpu.VMEM((D, DE), BF16), pltpu.VMEM((DE, D), BF16),
                        pltpu.VMEM((MOE_XBUF, MOE_TR, MOE_XW), BF16), pltpu.SemaphoreType.DMA((MOE_XBUF,))])
    return pl.pallas_call(
        _ffn_kernel, grid_spec=grid_spec,
        out_shape=jax.ShapeDtypeStruct((ntmax * MOE_TR, D), BF16),
        compiler_params=_cparams(("arbitrary",)),
        name="moe_ffn",
    )(te[0], misc.reshape(-1), xs, wg, wu, wd)


def _combine_kernel(dst_s, lo_s, n_s, pieces_s, h_ref, comb_ref, lof_ref, nf_ref, sg_ref, su_ref, sd_ref, ys_hbm,
                    o_ref, ybuf, sem, *, ntile):
    i = pl.program_id(0)
    slot = lax.rem(i, 2)

    def copy_in(sl, src_row, dst_row, size):
        return pltpu.make_async_copy(ys_hbm.at[pl.ds(pl.multiple_of(src_row, MOE_SEG), size)],
                                     ybuf.at[sl, pl.ds(pl.multiple_of(dst_row, MOE_SEG), size)], sem.at[sl])

    def start_tile(tile, sl):
        def start_seg(e, carry):
            idx = tile * 128 + e
            lo, src = lo_s[idx], dst_s[idx]
            _for_each_piece(n_s[idx], lambda off, size: copy_in(sl, src + off, lo + off, size).start(
                priority=MOE_PIECES.index(size) % 2))
            return carry

        lax.fori_loop(0, NE, start_seg, 0)

    @pl.when(i == 0)
    def _():
        ybuf[...] = jnp.zeros_like(ybuf)
        start_tile(0, 0)

    @pl.when(i + 1 < ntile)
    def _():
        start_tile(i + 1, 1 - slot)

    lo_row = lof_ref[...]
    hi_row = lo_row + nf_ref[...]
    lo_col, hi_col = _col_of_row(lo_row), _col_of_row(hi_row)
    rp_hi, rp_lo = _split(_rank_plus(comb_ref[...]))
    l_hi, l_lo = _split(jnp.broadcast_to(lo_row * (1.0 / MOE_SEG), (8, 128)))
    lhs_meta = jnp.concatenate([jnp.concatenate([rp_hi, rp_lo], axis=1), jnp.concatenate([l_hi, l_lo], axis=1)], axis=0)

    h = h_ref[...]
    hs = _silu(_dot(h, sg_ref[...].astype(BF16))) * _dot(h, su_ref[...].astype(BF16))
    acc = _dot(hs.astype(BF16), sd_ref[...].astype(BF16))

    _wait_pieces(pieces_s, i, lambda size: copy_in(slot, 0, 0, size))
    for ch in range(MOE_LS // MOE_CH):
        scol = (_iota((1, MOE_CH), 1) + ch * MOE_CH).astype(F32)
        e_m = jnp.where((scol >= lo_col) & (scol < hi_col), 1.0, 0.0).astype(BF16)
        meta = _dot(lhs_meta, jnp.concatenate([e_m, e_m], axis=0))
        target = scol + 1.0 - meta[MOE_TD:MOE_TD + 1, :] * MOE_SEG
        perm_t = jnp.where(meta[:MOE_TD] == target, 1.0, 0.0).astype(BF16)
        acc = acc + _dot(perm_t, ybuf[slot, pl.ds(ch * MOE_CH, MOE_CH), :])
    o_ref[...] = acc


def _moe_combine(plan, ys, h_all, comb_all, sg, su, sd, *, layer):
    dst, lo, n, _, _, lof, pieces = plan
    ntile = h_all.shape[0] // MOE_TD
    shared = lambda w: pl.BlockSpec((None,) + w.shape[1:], lambda i, *_: (layer, 0, 0))
    grid_spec = pltpu.PrefetchScalarGridSpec(
        num_scalar_prefetch=4, grid=(ntile,),
        in_specs=[pl.BlockSpec((MOE_TD, D), lambda i, *_: (i, 0)),
                  pl.BlockSpec((MOE_TD, 128), lambda i, *_: (i, 0)),
                  pl.BlockSpec((None, 1, 128), lambda i, *_: (i, 0, 0)),
                  pl.BlockSpec((None, 1, 128), lambda i, *_: (i, 0, 0)),
                  shared(sg), shared(su), shared(sd),
                  pl.BlockSpec(memory_space=pl.ANY)],
        out_specs=pl.BlockSpec((MOE_TD, D), lambda i, *_: (i, 0)),
        scratch_shapes=[pltpu.VMEM((2, MOE_LS, D), BF16), pltpu.SemaphoreType.DMA((2,))])
    return pl.pallas_call(
        functools.partial(_combine_kernel, ntile=ntile),
        grid_spec=grid_spec,
        out_shape=jax.ShapeDtypeStruct((h_all.shape[0], D), F32),
        compiler_params=_cparams(("arbitrary",)),
        name="moe_combine",
    )(dst.reshape(-1), lo.reshape(-1), n.reshape(-1), pieces.reshape(-1), h_all, comb_all,
      lof.reshape(PLAN_ROWS, 1, 128), n.astype(F32).reshape(PLAN_ROWS, 1, 128), sg, su, sd, ys)


def _moe(h_all, comb_all, wg, wu, wd, sg, su, sd, *, layer, n_tokens):
    ntile = h_all.shape[0] // MOE_TD
    max_rows = n_tokens * TOPK + ntile * NE * (MOE_SEG - 1) + NE * (MOE_TR - 1)
    ntmax = -(-max_rows // MOE_TR)
    plan = _moe_plan(comb_all, ntmax=ntmax)
    xs = _moe_dispatch(plan, h_all, comb_all, ntmax=ntmax)
    ys = _moe_ffn(plan, xs, wg, wu, wd, layer=layer, ntmax=ntmax)
    return _moe_combine(plan, ys, h_all, comb_all, sg, su, sd, layer=layer)


def _final_kernel(x_ref, f_ref, pmod_ref, png_ref, o_ref):
    o_ref[...] = _residual_in(x_ref, f_ref, pmod_ref, png_ref)


def _final(x, f_all, pmods, png, *, nb, seq_len, tm, row0):
    nt = seq_len // tm
    t0 = row0 // tm
    return pl.pallas_call(
        _final_kernel,
        grid=(nb, nt),
        in_specs=[pl.BlockSpec((tm, D), lambda b, t: (b * nt + t, 0)),
                  pl.BlockSpec((tm, D), lambda b, t: (t0 + b * nt + t, 0)),
                  pl.BlockSpec((None,) + pmods.shape[1:], lambda b, t: (b, 0, 0, 0)),
                  pl.BlockSpec(png.shape, lambda b, t: (0, 0))],
        out_specs=pl.BlockSpec((tm, D), lambda b, t: (b * nt + t, 0)),
        out_shape=jax.ShapeDtypeStruct(x.shape, F32),
        compiler_params=_cparams(("arbitrary", "arbitrary")),
        name="final_residual",
    )(x, f_all, pmods, png)


def kernel(x_prompt, x_sample, c_prompt, c_sample, state_conv, state_shift, state_wkv, w_mod, b_mod, norm_g, ab_w_in, ab_conv_w, ab_mu, ab_w0, ab_w_decay_up, ab_a0, ab_a_up, ab_g_up, ab_k_k, ab_k_a, ab_r_k, ab_gn_g, ab_gn_b, ab_w_out, gm_w_in, gm_b_in, gm_ln_g, gm_ln_b, gm_w_s, gm_b_s, gm_w_out, moe_w_router, moe_b_router, moe_w_gate, moe_w_up, moe_w_down, moe_ws_gate, moe_ws_up, moe_ws_down):
    bsz, seq_len, _ = x_prompt.shape
    nsamp = x_sample.shape[0]
    nheads = DB // HEAD
    n_prompt = bsz * seq_len
    n_all = -(-(n_prompt + nsamp) // MOE_TD) * MOE_TD
    tm = 512

    xp = x_prompt.reshape(n_prompt, D)
    xs = x_sample.reshape(nsamp, D)

    m = _modulation(jnp.concatenate([c_prompt, c_sample], axis=0), w_mod, b_mod)
    mods_p = [m[l, :bsz].reshape(bsz, 6, 1, D) for l in range(2)]
    mods_s = [m[l, bsz:].reshape(nsamp, 6, D).transpose(1, 0, 2)[None] for l in range(2)]

    row = lambda t: t.reshape(1, -1)
    pad_rows = lambda t, lo, hi: jnp.pad(t, ((lo, hi), (0, 0)))
    small = [ab_conv_w[0], row(ab_mu[0]), row(ab_w0[0]),
             pad_rows(ab_w_decay_up[0], 0, 64).astype(BF16), row(ab_a0[0]),
             pad_rows(ab_a_up[0], 64, 0).astype(BF16), ab_g_up[0].astype(BF16),
             row(ab_k_k[0]), row(ab_k_a[0])]
    win0 = ab_w_in[0].astype(BF16)
    wo0 = ab_w_out[0].astype(BF16)
    rk, gng, gnb = row(ab_r_k[0]), row(ab_gn_g[0]), row(ab_gn_b[0])
    wrt = [moe_w_router[l].T for l in range(2)]
    br = [moe_b_router[l].reshape(NE, 1) for l in range(2)]

    h2_all = jnp.zeros((n_all, D), BF16)
    comb_all = jnp.zeros((n_all, 128), F32)

    ya, r, lw, k, v, kk, a, g, ctail, stail = _ab_pre(
        xp, mods_p[0], norm_g[0], win0, small, None, nb=bsz, seq_len=seq_len, tm=tm, seq_mode=True)
    conv_p = ctail.reshape(bsz, 8, DA)[:, 6:8][None]
    shift_p = stail.reshape(bsz, 8, PB)[:, 7][None]
    yb, sfin = _wkv_seq(r, lw, k, v, kk, a, g, rk, gng, gnb, nb=bsz, seq_len=seq_len)
    sfin = sfin.reshape(bsz, nheads // 2, 2, HEAD, 2, HEAD)
    wkv_p = jnp.stack([sfin[:, :, 0, :, 0, :], sfin[:, :, 1, :, 1, :]], axis=2)
    wkv_p = wkv_p.reshape(bsz, nheads, HEAD, HEAD).transpose(0, 1, 3, 2)[None]
    xp1, h2_all, comb_all = _ab_post(ya, yb, xp, mods_p[0], norm_g[0], wo0, wrt[0], br[0], h2_all, comb_all,
                                     nb=bsz, seq_len=seq_len, tm=tm, row0=0)

    prev = [state_conv[0, :, 0], state_conv[0, :, 1], state_shift[0]]
    ya_s, r, lw, k, v, kk, a, g, gated_s, pb_s = _ab_pre(
        xs, mods_s[0], norm_g[0], win0, small, prev, nb=1, seq_len=nsamp, tm=nsamp, seq_mode=False)
    conv_s = jnp.stack([state_conv[0, :, 1], gated_s], axis=1)[None]
    shift_s = pb_s[None]
    vecs = [t.reshape(nsamp * nheads, 1, HEAD) for t in (r, lw, k, v, kk, a, g)]
    params = [t.reshape(nheads, 1, HEAD) for t in (ab_r_k[0], ab_gn_g[0], ab_gn_b[0])]
    yb_s, snew = _wkv_step(state_wkv[0].reshape(nsamp * nheads, HEAD, HEAD), vecs, params, nheads=nheads)
    wkv_s = snew.reshape(1, nsamp, nheads, HEAD, HEAD)
    xs1, h2_all, comb_all = _ab_post(ya_s, yb_s.reshape(nsamp, DB).astype(BF16), xs, mods_s[0], norm_g[0], wo0,
                                     wrt[0], br[0], h2_all, comb_all, nb=1, seq_len=nsamp, tm=nsamp, row0=n_prompt)

    def moe_layer(l, h_all, c_all):
        return _moe(h_all, c_all, moe_w_gate, moe_w_up, moe_w_down,
                    moe_ws_gate, moe_ws_up, moe_ws_down, layer=l, n_tokens=n_prompt + nsamp)

    f0 = moe_layer(0, h2_all, comb_all)

    gw = [gm_w_in[0].astype(BF16), row(gm_b_in[0]), row(gm_ln_g[0]), row(gm_ln_b[0])]
    gw_seq = gw + [gm_w_s[0], gm_b_s[0].T, gm_w_out[0].astype(BF16), wrt[1], br[1]]
    gw_step = gw + [row(jnp.repeat(gm_w_s[0, :, 0, 0], CHUNK)), row(jnp.repeat(gm_b_s[0, :, 0], CHUNK)),
                    gm_w_out[0].astype(BF16), wrt[1], br[1]]
    tm1 = 512
    xp2, h2_all, comb_all = _gmlp(xp1, f0, mods_p[0], norm_g[0], mods_p[1], norm_g[1], gw_seq, h2_all, comb_all,
                                  nb=bsz, seq_len=seq_len, tm=tm1, row0=0, seq_mode=True)
    xs2, h2_all, comb_all, chunk_v = _gmlp(xs1, f0, mods_s[0], norm_g[0], mods_s[1], norm_g[1], gw_step, h2_all,
                                           comb_all, nb=1, seq_len=nsamp, tm=nsamp, row0=n_prompt, seq_mode=False)
    f1 = moe_layer(1, h2_all, comb_all)
    y_p = _final(xp2, f1, mods_p[1], norm_g[1], nb=bsz, seq_len=seq_len, tm=tm, row0=0)
    y_s = _final(xs2, f1, mods_s[1], norm_g[1], nb=1, seq_len=nsamp, tm=nsamp, row0=n_prompt)

    return (y_p.reshape(bsz, seq_len, D), y_s.reshape(nsamp, 1, D), conv_p, conv_s, shift_p, shift_s,
            wkv_p, wkv_s, chunk_v.reshape(1, nsamp, 1, D))
```
